```python
import jax
import jax.numpy as jnp
from jax import lax
import numpy as np

D_MODEL = 1024
BATCH = 16
SEQ = 256
DEPTH = 4
DEC_BATCH = 2
DEC_SEQ = 1024
PAST_LEN = 512

GRID_W = 64
N_AB = (DEPTH + 1) // 2
N_C = DEPTH // 2
D_FF = 4 * D_MODEL
EPS = 1e-6
ROPE_BASE = 10000.0
Q_BLOCK = 128
H_A = 4
DK_A = D_MODEL // 16
DV_A = D_MODEL // 8
GK_RANK = 16
GATE_NORM = 16.0
GLA_CHUNK = 64
H_B = 8
KV_B = 2
G_B = H_B // KV_B
D_B = 64
WINDOW = 128
WIN_BLOCK = 128
AB_SPLITS = (H_A * DK_A, H_A * DK_A, H_A * DV_A, H_A * DV_A, 2 * GK_RANK, H_B * D_B, KV_B * D_B, KV_B * D_B)
AB_IN = 2 * H_A * DK_A + 2 * H_A * DV_A + 2 * GK_RANK + H_B * D_B + 2 * KV_B * D_B
MIX_AB = H_A * DV_A + H_B * D_B
H_C = 16
NOPE_C = 64
ROPE_C = 32
V_C = 64
Q_LORA = 384
KV_LORA = 256
DOWN_C = Q_LORA + KV_LORA + ROPE_C

kernel_name = 'bidir_hybrid_gla_swa_mla_dit_step'


def rmsnorm(x, g):
    xf = x.astype(jnp.float32)
    y = xf * lax.rsqrt(jnp.mean(xf * xf, axis=-1, keepdims=True) + EPS)
    return (y * g.astype(jnp.float32)).astype(x.dtype)


def split_last(x, sizes):
    out, o = [], 0
    for s in sizes:
        out.append(x[..., o:o + s])
        o += s
    return out


def rope_2d(x, rows, cols):
    hd = x.shape[-1]
    nf = hd // 4
    inv = ROPE_BASE ** (-jnp.arange(nf, dtype=jnp.float32) / nf)
    ang = jnp.stack([rows[:, None] * inv, cols[:, None] * inv], axis=1)
    ang = ang.reshape((ang.shape[0],) + (1,) * (x.ndim - 3) + (2, nf))
    cos, sin = jnp.cos(ang), jnp.sin(ang)
    xr = x.astype(jnp.float32).reshape(x.shape[:-1] + (2, 2, nf))
    x1, x2 = xr[..., 0, :], xr[..., 1, :]
    out = jnp.stack([x1 * cos - x2 * sin, x1 * sin + x2 * cos], axis=-2)
    return out.reshape(x.shape).astype(x.dtype)


def adaln(cvec, w_mod, b_mod):
    m = jax.nn.silu(cvec) @ w_mod + b_mod
    m = m.reshape(cvec.shape[0], 1, 6, D_MODEL)
    return [m[:, :, i] for i in range(6)]


def pre(x, g, shift, scale):
    return rmsnorm(x, g) * (1 + scale) + shift


def post(x, out, g, gate):
    return x + gate * rmsnorm(out, g)


def sq_relu_mlp(h, w1, w2):
    return jnp.square(jax.nn.relu(h @ w1)) @ w2


def gla_chunked(q, k, v, g, s0):
    B, L, H, _ = q.shape
    dv = v.shape[-1]
    nc = L // GLA_CHUNK

    def chunks(a):
        return a.astype(jnp.float32).reshape(B, nc, GLA_CHUNK, H, a.shape[-1]).transpose(1, 0, 3, 2, 4)

    causal = jnp.tril(jnp.ones((GLA_CHUNK, GLA_CHUNK), dtype=bool))

    def step(s, inp):
        qc, kc, vc, gc = inp
        b = jnp.cumsum(gc, axis=2)
        o_inter = jnp.einsum('bhtd,bhde->bhte', qc * jnp.exp(b), s)
        diff = b[:, :, :, None, :] - b[:, :, None, :, :]
        decay = jnp.exp(jnp.where(causal[:, :, None], diff, -jnp.inf))
        att = jnp.einsum('bhtsd,bhsd->bhts', qc[:, :, :, None, :] * decay, kc)
        o_intra = jnp.einsum('bhts,bhse->bhte', att, vc)
        b_last = b[:, :, -1:, :]
        s_new = jnp.exp(b_last[:, :, 0, :])[..., None] * s + jnp.einsum('bhsd,bhse->bhde', kc * jnp.exp(b_last - b), vc)
        return s_new, o_inter + o_intra

    s_fin, o = lax.scan(step, s0.astype(jnp.float32), (chunks(q), chunks(k), chunks(v), chunks(g)))
    o = o.transpose(1, 0, 3, 2, 4).reshape(B, L, H, dv)
    return o, s_fin


def dense_attn(q, k, v, scale, sink=None):
    B, Lq, KV, G, d = q.shape
    nb = Lq // Q_BLOCK
    qb = q.reshape(B, nb, Q_BLOCK, KV, G, d).transpose(1, 0, 2, 3, 4, 5)

    def block(qblk):
        s = jnp.einsum('bqkgd,bskd->bkgqs', qblk, k).astype(jnp.float32) * scale
        if sink is not None:
            sk = jnp.broadcast_to(sink.astype(jnp.float32).reshape(1, KV, G, 1, 1), s.shape[:-1] + (1,))
            p = jax.nn.softmax(jnp.concatenate([sk, s], axis=-1), axis=-1)[..., 1:]
        else:
            p = jax.nn.softmax(s, axis=-1)
        return jnp.einsum('bkgqs,bskd->bqkgd', p.astype(v.dtype), v)

    o = lax.map(block, qb)
    return o.transpose(1, 0, 2, 3, 4, 5).reshape(B, Lq, KV, G, v.shape[-1])


def banded_attn(q, k, v, k_ctx, v_ctx, scale, sink):
    B, L, KV, G, d = q.shape
    W = WIN_BLOCK
    nb = L // W
    Lc = k_ctx.shape[1]
    qb = q.reshape(B, nb, W, KV, G, d)

    def band(a):
        ap = jnp.pad(a, ((0, 0), (W, W), (0, 0), (0, 0))).reshape(B, nb + 2, W, KV, a.shape[-1])
        return jnp.concatenate([ap[:, :-2], ap[:, 1:-1], ap[:, 2:]], axis=2)

    kb, vb = band(k), band(v)
    qi = jnp.arange(L).reshape(nb, W)
    ki = jnp.arange(nb)[:, None] * W - W + jnp.arange(3 * W)[None, :]
    mask = (ki[:, None, :] >= 0) & (ki[:, None, :] < L) & (jnp.abs(qi[:, :, None] - ki[:, None, :]) <= WINDOW)
    s_loc = jnp.einsum('bnqkgd,bnskd->bnkgqs', qb, kb).astype(jnp.float32) * scale
    s_loc = jnp.where(mask[None, :, None, None], s_loc, -jnp.inf)
    s_ctx = jnp.einsum('bnqkgd,bckd->bnkgqc', qb, k_ctx).astype(jnp.float32) * scale
    s_sink = jnp.broadcast_to(sink.astype(jnp.float32).reshape(1, 1, KV, G, 1, 1), s_ctx.shape[:-1] + (1,))
    p = jax.nn.softmax(jnp.concatenate([s_sink, s_ctx, s_loc], axis=-1), axis=-1)
    p_ctx = p[..., 1:1 + Lc].astype(v.dtype)
    p_loc = p[..., 1 + Lc:].astype(v.dtype)
    o = jnp.einsum('bnkgqc,bckd->bnqkgd', p_ctx, v_ctx) + jnp.einsum('bnkgqs,bnskd->bnqkgd', p_loc, vb)
    return o.reshape(B, L, KV, G, v.shape[-1])


def mixer_ab(h, w_in, w_gk_f, b_gk_f, w_gk_b, b_gk_b, g_gla, sink, w_out, ctx=None, pos=None):
    B, L, _ = h.shape
    q_a, k_a, v_a, g_a, gk_lo, q_b, k_b, v_b = split_last(h @ w_in, AB_SPLITS)
    q_a = q_a.reshape(B, L, H_A, DK_A) * (DK_A ** -0.5)
    k_a = k_a.reshape(B, L, H_A, DK_A)
    v_a = v_a.reshape(B, L, H_A, DV_A)

    def log_decay(lo, w, b):
        return (jax.nn.log_sigmoid((lo @ w + b).astype(jnp.float32)) / GATE_NORM).reshape(B, L, H_A, DK_A)

    ld_f = log_decay(gk_lo[..., :GK_RANK], w_gk_f, b_gk_f)
    ld_b = log_decay(gk_lo[..., GK_RANK:], w_gk_b, b_gk_b)
    if ctx is None:
        s0_f = jnp.zeros((B, H_A, DK_A, DV_A), jnp.float32)
        s0_b = jnp.zeros((B, H_A, DK_A, DV_A), jnp.float32)
    else:
        s0_f, s0_b, k_ctx, v_ctx = ctx
    rev = lambda a: jnp.flip(a, axis=1)
    o_fwd, s_fwd = gla_chunked(q_a, k_a, v_a, ld_f, s0_f)
    o_bwd, s_bwd = gla_chunked(rev(q_a), rev(k_a), rev(v_a), rev(ld_b), s0_b)
    o_gla = (o_fwd + rev(o_bwd)).astype(h.dtype)
    o_gla = rmsnorm(o_gla, g_gla) * jax.nn.silu(g_a.reshape(B, L, H_A, DV_A))

    q_b = q_b.reshape(B, L, H_B, D_B)
    k_b = k_b.reshape(B, L, KV_B, D_B)
    v_b = v_b.reshape(B, L, KV_B, D_B)
    scale = D_B ** -0.5
    sink_g = sink.reshape(KV_B, G_B)
    if ctx is None:
        o_swa = dense_attn(q_b.reshape(B, L, KV_B, G_B, D_B), k_b, v_b, scale, sink_g)
        new = (s_fwd.astype(h.dtype), s_bwd.astype(h.dtype), k_b, v_b)
    else:
        rows, cols = pos
        q_b = rope_2d(q_b, rows, cols)
        k_b = rope_2d(k_b, rows, cols)
        o_swa = banded_attn(q_b.reshape(B, L, KV_B, G_B, D_B), k_b, v_b, k_ctx, v_ctx, scale, sink_g)
        new = None
    out = jnp.concatenate([o_gla.reshape(B, L, H_A * DV_A), o_swa.reshape(B, L, H_B * D_B)], axis=-1) @ w_out
    return out, new


def mla_expand(c_kv, k_rope, w_ukv):
    B, L, _ = c_kv.shape
    kv = (c_kv @ w_ukv).reshape(B, L, H_C, NOPE_C + V_C)
    k = jnp.concatenate([kv[..., :NOPE_C], jnp.broadcast_to(k_rope[:, :, None, :], (B, L, H_C, ROPE_C))], axis=-1)
    return k, kv[..., NOPE_C:]


def mixer_c(h, w_down, g_q, g_kv, w_uq, w_ukv, w_o, ctx=None, pos=None):
    B, L, _ = h.shape
    c_q, c_kv, k_rope = split_last(h @ w_down, (Q_LORA, KV_LORA, ROPE_C))
    q = (rmsnorm(c_q, g_q) @ w_uq).reshape(B, L, H_C, NOPE_C + ROPE_C)
    c_kv = rmsnorm(c_kv, g_kv)
    if ctx is None:
        k, v = mla_expand(c_kv, k_rope, w_ukv)
        new = (c_kv, k_rope)
    else:
        rows, cols = pos
        q = jnp.concatenate([q[..., :NOPE_C], rope_2d(q[..., NOPE_C:], rows, cols)], axis=-1)
        k, v = mla_expand(c_kv, rope_2d(k_rope, rows, cols), w_ukv)
        ckv_ctx, kr_ctx = ctx
        k_c, v_c = mla_expand(ckv_ctx, kr_ctx, w_ukv)
        k = jnp.concatenate([k_c, k], axis=1)
        v = jnp.concatenate([v_c, v], axis=1)
        new = None
    o = dense_attn(q[:, :, :, None, :], k, v, (NOPE_C + ROPE_C) ** -0.5)
    return o.reshape(B, L, H_C * V_C) @ w_o, new


def setup_inputs(seed: int = 0) -> dict:
    key = jax.random.key(seed)
    ks = jax.random.split(key, 32)

    def nrm(k, shape, scale):
        return jax.random.normal(k, shape, jnp.float32) * scale

    return {
        'x_prompt': nrm(ks[0], (BATCH, SEQ, D_MODEL), 1.0),
        'x_sample': nrm(ks[1], (DEC_BATCH, DEC_SEQ, D_MODEL), 1.0),
        'state_gla_fwd': nrm(ks[2], (DEC_BATCH, N_AB, H_A, DK_A, DV_A), 0.5),
        'state_gla_bwd': nrm(ks[3], (DEC_BATCH, N_AB, H_A, DK_A, DV_A), 0.5),
        'cache_swa_k': nrm(ks[4], (DEC_BATCH, N_AB, PAST_LEN, KV_B, D_B), 1.0),
        'cache_swa_v': nrm(ks[5], (DEC_BATCH, N_AB, PAST_LEN, KV_B, D_B), 1.0),
        'cache_mla_ckv': nrm(ks[6], (DEC_BATCH, N_C, PAST_LEN, KV_LORA), 1.0),
        'cache_mla_kr': nrm(ks[7], (DEC_BATCH, N_C, PAST_LEN, ROPE_C), 1.0),
        'c': nrm(ks[8], (DEC_BATCH, D_MODEL), 1.0),
        'c_ctx': nrm(ks[9], (D_MODEL,), 1.0),
        'w_mod': nrm(ks[10], (DEPTH, D_MODEL, 6 * D_MODEL), 0.5 * D_MODEL ** -0.5),
        'b_mod': nrm(ks[11], (DEPTH, 6 * D_MODEL), 0.02),
        'g_norm': 1.0 + nrm(ks[12], (DEPTH, 4, D_MODEL), 0.05),
        'w_ff1': nrm(ks[13], (DEPTH, D_MODEL, D_FF), D_MODEL ** -0.5),
        'w_ff2': nrm(ks[14], (DEPTH, D_FF, D_MODEL), D_FF ** -0.5),
        'w_in_ab': nrm(ks[15], (N_AB, D_MODEL, AB_IN), D_MODEL ** -0.5),
        'w_gk_f': nrm(ks[16], (N_AB, GK_RANK, H_A * DK_A), GK_RANK ** -0.5),
        'b_gk_f': nrm(ks[17], (N_AB, H_A * DK_A), 0.1),
        'w_gk_b': nrm(ks[18], (N_AB, GK_RANK, H_A * DK_A), GK_RANK ** -0.5),
        'b_gk_b': nrm(ks[19], (N_AB, H_A * DK_A), 0.1),
        'g_gla': 1.0 + nrm(ks[20], (N_AB, DV_A), 0.05),
        'swa_sink': nrm(ks[21], (N_AB, H_B), 0.5),
        'w_out_ab': nrm(ks[22], (N_AB, MIX_AB, D_MODEL), MIX_AB ** -0.5),
        'w_mla_down': nrm(ks[23], (N_C, D_MODEL, DOWN_C), D_MODEL ** -0.5),
        'g_mla_q': 1.0 + nrm(ks[24], (N_C, Q_LORA), 0.05),
        'g_mla_kv': 1.0 + nrm(ks[25], (N_C, KV_LORA), 0.05),
        'w_mla_uq': nrm(ks[26], (N_C, Q_LORA, H_C * (NOPE_C + ROPE_C)), Q_LORA ** -0.5),
        'w_mla_ukv': nrm(ks[27], (N_C, KV_LORA, H_C * (NOPE_C + V_C)), KV_LORA ** -0.5),
        'w_mla_o': nrm(ks[28], (N_C, H_C * V_C, D_MODEL), (H_C * V_C) ** -0.5),
    }


def reference(x_prompt, x_sample, state_gla_fwd, state_gla_bwd, cache_swa_k, cache_swa_v, cache_mla_ckv,
              cache_mla_kr, c, c_ctx, w_mod, b_mod, g_norm, w_ff1, w_ff2, w_in_ab, w_gk_f, b_gk_f, w_gk_b,
              b_gk_b, g_gla, swa_sink, w_out_ab, w_mla_down, g_mla_q, g_mla_kv, w_mla_uq, w_mla_ukv, w_mla_o):
    n_rows = x_sample.shape[1] // GRID_W
    rows = jnp.repeat(jnp.arange(n_rows, dtype=jnp.float32), GRID_W)
    cols = jnp.tile(jnp.arange(GRID_W, dtype=jnp.float32), n_rows)
    pos = (rows, cols)

    xp, xs = x_prompt, x_sample
    st_f, st_b, sk, sv, ckv, ckr = [], [], [], [], [], []
    for l in range(DEPTH):
        mp = adaln(c_ctx[None, :], w_mod[l], b_mod[l])
        ms = adaln(c, w_mod[l], b_mod[l])
        g = g_norm[l]
        hp = pre(xp, g[0], mp[0], mp[1])
        hs = pre(xs, g[0], ms[0], ms[1])
        i = l // 2
        if l % 2 == 0:
            wts = (w_in_ab[i], w_gk_f[i], b_gk_f[i], w_gk_b[i], b_gk_b[i], g_gla[i], swa_sink[i], w_out_ab[i])
            op, (s_f, s_b, k_c, v_c) = mixer_ab(hp, *wts)
            os_, _ = mixer_ab(hs, *wts, ctx=(state_gla_fwd[:, i], state_gla_bwd[:, i], cache_swa_k[:, i], cache_swa_v[:, i]), pos=pos)
            st_f.append(s_f)
            st_b.append(s_b)
            sk.append(k_c)
            sv.append(v_c)
        else:
            wts = (w_mla_down[i], g_mla_q[i], g_mla_kv[i], w_mla_uq[i], w_mla_ukv[i], w_mla_o[i])
            op, (c_kv, k_r) = mixer_c(hp, *wts)
            os_, _ = mixer_c(hs, *wts, ctx=(cache_mla_ckv[:, i], cache_mla_kr[:, i]), pos=pos)
            ckv.append(c_kv)
            ckr.append(k_r)
        xp = post(xp, op, g[1], mp[2])
        xs = post(xs, os_, g[1], ms[2])
        xp = post(xp, sq_relu_mlp(pre(xp, g[2], mp[3], mp[4]), w_ff1[l], w_ff2[l]), g[3], mp[5])
        xs = post(xs, sq_relu_mlp(pre(xs, g[2], ms[3], ms[4]), w_ff1[l], w_ff2[l]), g[3], ms[5])

    new_state_gla_fwd = jnp.stack(st_f, axis=1)
    new_state_gla_bwd = jnp.stack(st_b, axis=1)
    new_cache_swa_k = jnp.stack(sk, axis=1)
    new_cache_swa_v = jnp.stack(sv, axis=1)
    new_cache_mla_ckv = jnp.stack(ckv, axis=1)
    new_cache_mla_kr = jnp.stack(ckr, axis=1)
    return (xp, xs, new_state_gla_fwd, new_state_gla_bwd, new_cache_swa_k, new_cache_swa_v, new_cache_mla_ckv, new_cache_mla_kr)
```

```python
import functools

import numpy as np
import jax
import jax.numpy as jnp
from jax import lax
from jax.experimental import pallas as pl
from jax.experimental.pallas import tpu as pltpu

F32 = jnp.float32
BF16 = jnp.bfloat16

D_MODEL = 1024
BATCH = 16
SEQ = 256
DEPTH = 4
DEC_BATCH = 2
DEC_SEQ = 1024
PAST_LEN = 512
GRID_W = 64
D_FF = 4 * D_MODEL
EPS = 1e-6
ROPE_BASE = 10000.0
H_A = 4
DK_A = 64
DV_A = 128
GK_RANK = 16
GATE_NORM = 16.0
GLA_CHUNK = 64
H_B = 8
KV_B = 2
G_B = H_B // KV_B
D_B = 64
WINDOW = 128
H_C = 16
NOPE_C = 64
ROPE_C = 32
V_C = 64
Q_LORA = 384
KV_LORA = 256

N_PROMPT = BATCH * SEQ
N_SAMPLE = DEC_BATCH * DEC_SEQ
N_TOK = N_PROMPT + N_SAMPLE
N_MOD_ROWS = 8
QA_W = H_A * DK_A
VA_W = H_A * DV_A
QB_W = H_B * D_B
KB_W = KV_B * D_B
LANES = 128
HEAD_PAD_C = 128
QC_W = H_C * HEAD_PAD_C
VC_W = H_C * V_C
DOWN_W = Q_LORA + KV_LORA + LANES
KR_OFF = NOPE_C

TM_IN = 512
TM_OUT = 1024
TK_FF = 512
TN_MOD = 1536
GLA_GROUP = 256
SWA_QB = 128
SWA_WIN = 3 * SWA_QB
MLA_QB = 256
VMEM_LIMIT = 56 * 1024 * 1024
NEG_BIG = -1e30


def _cparams(sem):
    return pltpu.CompilerParams(dimension_semantics=sem, vmem_limit_bytes=VMEM_LIMIT)


def _dot(a, b):
    return jnp.dot(a, b, preferred_element_type=F32)


def _dot_nt(a, b):
    return lax.dot_general(a, b, (((1,), (1,)), ((), ())), preferred_element_type=F32)


def _dot_tn(a, b):
    return lax.dot_general(a, b, (((0,), (0,)), ((), ())), preferred_element_type=F32)


def _rms(x, g):
    return x * lax.rsqrt(jnp.mean(x * x, axis=-1, keepdims=True) + EPS) * g


def _silu(x):
    return x / (1.0 + jnp.exp(-x))


def _rope(x, cos, sin, half):
    lane = lax.broadcasted_iota(jnp.int32, x.shape, 1)
    first = (lane % (2 * half)) < half
    partner = jnp.where(first, pltpu.roll(x, LANES - half, 1), pltpu.roll(x, half, 1))
    return x * cos + partner * sin


def _split3(x):
    x1 = x.astype(BF16)
    r1 = x - x1.astype(F32)
    x2 = r1.astype(BF16)
    x3 = (r1 - x2.astype(F32)).astype(BF16)
    return x1, x2, x3


def _dot3(t, parts):
    return _dot(t, parts[2]) + _dot(t, parts[1]) + _dot(t, parts[0])


def _ones_where(cond):
    return jnp.where(cond, 1.0, 0.0).astype(BF16)


def _mod_row(tile, tm):
    n_prompt_tiles = N_PROMPT // tm
    tiles_per_seq = DEC_SEQ // tm
    return jnp.where(tile < n_prompt_tiles, 0, 1 + (tile - n_prompt_tiles) // tiles_per_seq)


def _mod_kernel(c_ref, w_ref, b_ref, o_ref):
    s = _silu(c_ref[...])
    o_ref[...] = _dot(s.astype(BF16), w_ref[...].astype(BF16)) + b_ref[...]


def _modulation(cvecs, w_mod, b_mod):
    return pl.pallas_call(
        _mod_kernel,
        grid=(DEPTH, 6 * D_MODEL // TN_MOD),
        in_specs=[
            pl.BlockSpec((N_MOD_ROWS, D_MODEL), lambda l, j: (0, 0)),
            pl.BlockSpec((None, D_MODEL, TN_MOD), lambda l, j: (l, 0, j)),
            pl.BlockSpec((None, 1, TN_MOD), lambda l, j: (l, 0, j)),
        ],
        out_specs=pl.BlockSpec((None, N_MOD_ROWS, TN_MOD), lambda l, j: (l, 0, j)),
        out_shape=jax.ShapeDtypeStruct((DEPTH, N_MOD_ROWS, 6 * D_MODEL), F32),
        compiler_params=_cparams(("parallel", "parallel")),
        name="adaln_mod",
    )(cvecs, w_mod, b_mod.reshape(DEPTH, 1, 6 * D_MODEL))


_C_QA, _C_KA, _C_VA, _C_GA = 0, QA_W, 2 * QA_W, 2 * QA_W + VA_W
_C_QB = _C_GA + VA_W
_C_KB = _C_QB + QB_W
_C_VB = _C_KB + KB_W
_C_LO = _C_VB + KB_W
AB_IN = _C_LO + 2 * GK_RANK


def _in_even_kernel(x_ref, mod_ref, g_ref, w_ref, wgk_ref, bgk_ref, cos_ref, sin_ref,
                    qa_ref, ka_ref, va_ref, ga_ref, ld_ref, qb_ref, kb_ref, vb_ref):
    h = _rms(x_ref[...], g_ref[0:1, :]) * (1.0 + mod_ref[1:2, :]) + mod_ref[0:1, :]
    hb = h.astype(BF16)
    qa_ref[...] = _dot(hb, w_ref[:, _C_QA:_C_KA]) * (DK_A ** -0.5)
    ka_ref[...] = _dot(hb, w_ref[:, _C_KA:_C_VA])
    va_ref[...] = _dot(hb, w_ref[:, _C_VA:_C_GA]).astype(BF16)
    ga_ref[...] = _dot(hb, w_ref[:, _C_GA:_C_QB])
    cos = cos_ref[...]
    sin = sin_ref[...]
    for j in range(QB_W // LANES):
        qj = _dot(hb, w_ref[:, _C_QB + j * LANES:_C_QB + (j + 1) * LANES])
        qb_ref[:, j * LANES:(j + 1) * LANES] = (_rope(qj, cos, sin, D_B // 4) * (D_B ** -0.5)).astype(BF16)
    kb_ref[...] = _rope(_dot(hb, w_ref[:, _C_KB:_C_VB]), cos, sin, D_B // 4)
    vb_ref[...] = _dot(hb, w_ref[:, _C_VB:_C_LO])
    lo = _dot(hb, w_ref[:, _C_LO:AB_IN]).astype(BF16)
    z = _dot(lo, wgk_ref[...]) + bgk_ref[...]
    ld_ref[...] = (jnp.minimum(z, 0.0) - jnp.log(1.0 + jnp.exp(-jnp.abs(z)))) * (1.0 / GATE_NORM)


def _in_even(x, mods, g, w_in, w_gk, b_gk, cos, sin):
    tm = TM_IN
    row = lambda i: (i, 0)
    full = lambda i: (0, 0)
    widths = (QA_W, QA_W, VA_W, VA_W, 2 * QA_W, QB_W, KB_W, KB_W)
    dtypes = (F32, F32, BF16, F32, F32, BF16, F32, F32)
    return pl.pallas_call(
        _in_even_kernel,
        grid=(N_TOK // tm,),
        in_specs=[
            pl.BlockSpec((tm, D_MODEL), row),
            pl.BlockSpec((None, 6, D_MODEL), lambda i: (_mod_row(i, tm), 0, 0)),
            pl.BlockSpec((4, D_MODEL), full),
            pl.BlockSpec((D_MODEL, AB_IN), full),
            pl.BlockSpec((2 * GK_RANK, 2 * QA_W), full),
            pl.BlockSpec((1, 2 * QA_W), full),
            pl.BlockSpec((tm, LANES), row),
            pl.BlockSpec((tm, LANES), row),
        ],
        out_specs=[pl.BlockSpec((tm, w), row) for w in widths],
        out_shape=[jax.ShapeDtypeStruct((N_TOK, w), d) for w, d in zip(widths, dtypes)],
        compiler_params=_cparams(("parallel",)),
        name="in_even",
    )(x, mods, g, w_in, w_gk, b_gk, cos, sin)


def _gla_kernel(seq_len, has_s0, *refs):
    if has_s0:
        (qa_ref, ka_ref, va_ref, ga_ref, ld_ref, gg_ref, s0_ref, o_ref, st_ref,
         qi_ref, ki_ref, qcat_ref, ks_ref, dec_ref, stcat_ref, acc_ref) = refs
    else:
        (qa_ref, ka_ref, va_ref, ga_ref, ld_ref, gg_ref, o_ref, st_ref,
         qi_ref, ki_ref, qcat_ref, ks_ref, dec_ref, stcat_ref, acc_ref) = refs
        s0_ref = None
    n_groups = seq_len // GLA_GROUP
    cpg = GLA_GROUP // GLA_CHUNK
    n_chunks = seq_len // GLA_CHUNK
    n_pairs = H_A // 2
    pair_k = 2 * DK_A
    pair_v = 2 * DV_A

    def chunk_masks():
        r_i = lax.broadcasted_iota(jnp.int32, (GLA_GROUP, GLA_GROUP), 0)
        c_i = lax.broadcasted_iota(jnp.int32, (GLA_GROUP, GLA_GROUP), 1)
        same = (r_i // GLA_CHUNK) == (c_i // GLA_CHUNK)
        return same, same & (c_i <= r_i), same & (c_i >= r_i)

    def scale_group(j, carry):
        same, mask_f, mask_b = chunk_masks()
        t_cum = (_ones_where(mask_f), _ones_where(mask_b))
        t_all = _ones_where(same)
        s_r = lax.broadcasted_iota(jnp.int32, (8, GLA_GROUP), 0)
        s_c = lax.broadcasted_iota(jnp.int32, (8, GLA_GROUP), 1)
        t_sel = _ones_where(s_r == s_c // GLA_CHUNK)
        rows = pl.ds(pl.multiple_of(j * GLA_GROUP, GLA_GROUP), GLA_GROUP)
        q = qa_ref[rows, :]
        k = ka_ref[rows, :]
        for d in range(2):
            parts = _split3(ld_ref[rows, d * QA_W:(d + 1) * QA_W])
            b = _dot3(t_cum[d], parts)
            tot = _dot3(t_all, parts)
            dec8 = jnp.exp(_dot3(t_sel, parts))
            ref = 0.5 * tot
            qi_ref[d, rows, :] = (q * jnp.exp(b - ref)).astype(BF16)
            ki_ref[d, rows, :] = (k * jnp.exp(ref - b)).astype(BF16)
            ks_ref[d, rows, :] = (k * jnp.exp(tot - b)).astype(BF16)
            q_inter = (q * jnp.exp(b)).astype(BF16)
            for p in range(n_pairs):
                qcat_ref[rows, p * 2 * pair_k + d * pair_k:p * 2 * pair_k + (d + 1) * pair_k] = (
                    q_inter[:, p * pair_k:(p + 1) * pair_k])
            for cc in range(cpg):
                dec_ref[d, j * cpg + cc] = jnp.broadcast_to(dec8[cc:cc + 1, :], (8, QA_W))
        return carry

    lax.fori_loop(0, n_groups, scale_group, 0)

    if has_s0:
        st_ref[...] = s0_ref[...]
    else:
        st_ref[...] = jnp.zeros(st_ref.shape, F32)

    def scan_chunk(c, carry):
        bd_r = lax.broadcasted_iota(jnp.int32, (pair_v, pair_k), 0)
        bd_c = lax.broadcasted_iota(jnp.int32, (pair_v, pair_k), 1)
        bd_mask = (bd_r // DV_A) == (bd_c // DK_A)
        for d in range(2):
            cd = c if d == 0 else n_chunks - 1 - c
            rows = pl.ds(pl.multiple_of(cd * GLA_CHUNK, GLA_CHUNK), GLA_CHUNK)
            for p in range(n_pairs):
                st = st_ref[d, p]
                stcat_ref[cd, p, :, d * pair_k:(d + 1) * pair_k] = st.astype(BF16)
                u = _dot_tn(va_ref[rows, p * pair_v:(p + 1) * pair_v], ks_ref[d, rows, p * pair_k:(p + 1) * pair_k])
                dec = dec_ref[d, cd][0:1, p * pair_k:(p + 1) * pair_k]
                st_ref[d, p] = st * dec + jnp.where(bd_mask, u, 0.0)
        return carry

    lax.fori_loop(0, n_chunks, scan_chunk, 0)

    def out_group(j, carry):
        _, mask_f, mask_b = chunk_masks()
        lane = lax.broadcasted_iota(jnp.int32, (GLA_GROUP, pair_k), 1)
        rows = pl.ds(pl.multiple_of(j * GLA_GROUP, GLA_GROUP), GLA_GROUP)
        for p in range(n_pairs):
            for cc in range(cpg):
                c = j * cpg + cc
                crow = pl.ds(pl.multiple_of(c * GLA_CHUNK, GLA_CHUNK), GLA_CHUNK)
                acc_ref[crow, p * pair_v:(p + 1) * pair_v] = _dot_nt(
                    qcat_ref[crow, p * 2 * pair_k:(p + 1) * 2 * pair_k], stcat_ref[c, p])
        for h in range(H_A):
            p, hh = divmod(h, 2)
            head_lanes = _ones_where((lane // DK_A) == hh)
            att = None
            for d, mask in ((0, mask_f), (1, mask_b)):
                qm = qi_ref[d, rows, p * pair_k:(p + 1) * pair_k] * head_lanes
                a = jnp.where(mask, _dot_nt(qm, ki_ref[d, rows, p * pair_k:(p + 1) * pair_k]), 0.0)
                att = a if att is None else att + a
            vh = va_ref[rows, h * DV_A:(h + 1) * DV_A]
            o = acc_ref[rows, h * DV_A:(h + 1) * DV_A] + _dot(att.astype(BF16), vh)
            gate = _silu(ga_ref[rows, h * DV_A:(h + 1) * DV_A])
            o_ref[rows, h * DV_A:(h + 1) * DV_A] = (_rms(o, gg_ref[...]) * gate).astype(BF16)
        return carry

    lax.fori_loop(0, n_groups, out_group, 0)


def _gla(seq_len, n_seq, row_block0, qa, ka, va, ga, ld, g_gla, s0):
    has_s0 = s0 is not None
    n_chunks = seq_len // GLA_CHUNK
    n_pairs = H_A // 2
    rows = lambda b: (row_block0 + b, 0)
    st_spec = pl.BlockSpec((None, 2, n_pairs, 2 * DV_A, 2 * DK_A), lambda b: (b, 0, 0, 0, 0))
    in_specs = [
        pl.BlockSpec((seq_len, QA_W), rows),
        pl.BlockSpec((seq_len, QA_W), rows),
        pl.BlockSpec((seq_len, VA_W), rows),
        pl.BlockSpec((seq_len, VA_W), rows),
        pl.BlockSpec((seq_len, 2 * QA_W), rows),
        pl.BlockSpec((1, DV_A), lambda b: (0, 0)),
    ]
    args = [qa, ka, va, ga, ld, g_gla]
    if has_s0:
        in_specs.append(st_spec)
        args.append(s0)
    return pl.pallas_call(
        functools.partial(_gla_kernel, seq_len, has_s0),
        grid=(n_seq,),
        in_specs=in_specs,
        out_specs=[pl.BlockSpec((seq_len, VA_W), lambda b: (b, 0)), st_spec],
        out_shape=[jax.ShapeDtypeStruct((n_seq * seq_len, VA_W), BF16),
                   jax.ShapeDtypeStruct((n_seq, 2, n_pairs, 2 * DV_A, 2 * DK_A), F32)],
        scratch_shapes=[
            pltpu.VMEM((2, seq_len, QA_W), BF16),
            pltpu.VMEM((2, seq_len, QA_W), BF16),
            pltpu.VMEM((seq_len, 2 * QA_W), BF16),
            pltpu.VMEM((2, seq_len, QA_W), BF16),
            pltpu.VMEM((2, n_chunks, 8, QA_W), F32),
            pltpu.VMEM((n_chunks, n_pairs, 2 * DV_A, 4 * DK_A), BF16),
            pltpu.VMEM((seq_len, VA_W), F32),
        ],
        compiler_params=_cparams(("parallel",)),
        name="gla_s" if has_s0 else "gla_p",
    )(*args)


def _swa_head_softmax(pieces, sink):
    m = sink
    for s, _ in pieces:
        m = jnp.maximum(m, jnp.max(s, axis=-1, keepdims=True))
    den = jnp.exp(sink - m)
    acc = None
    for s, v in pieces:
        e = jnp.exp(s - m)
        den = den + jnp.sum(e, axis=-1, keepdims=True)
        pv = _dot(e.astype(BF16), v)
        acc = pv if acc is None else acc + pv
    return acc / den


def _swa_prompt_kernel(sink_ref, q_ref, k_ref, v_ref, o_ref):
    k = k_ref[...].astype(BF16)
    v = v_ref[...].astype(BF16)
    lane = lax.broadcasted_iota(jnp.int32, (SEQ, LANES), 1)
    for p in range(H_B // 2):
        qp = q_ref[:, p * LANES:(p + 1) * LANES]
        outs = []
        for g in range(KV_B):
            qm = qp * _ones_where((lane // D_B) == g)
            s = _dot_nt(qm, k)
            outs.append(_swa_head_softmax([(s, v)], sink_ref[2 * p + g]))
        o_ref[:, p * LANES:(p + 1) * LANES] = jnp.where(lane < D_B, outs[0], outs[1]).astype(BF16)


def _swa_prompt(sink, qb, kb, vb):
    seq = lambda b, s: (b, 0)
    return pl.pallas_call(
        _swa_prompt_kernel,
        grid_spec=pltpu.PrefetchScalarGridSpec(
            num_scalar_prefetch=1,
            grid=(BATCH,),
            in_specs=[pl.BlockSpec((SEQ, QB_W), seq), pl.BlockSpec((SEQ, KB_W), seq),
                      pl.BlockSpec((SEQ, KB_W), seq)],
            out_specs=pl.BlockSpec((SEQ, QB_W), seq),
        ),
        out_shape=jax.ShapeDtypeStruct((N_PROMPT, QB_W), BF16),
        compiler_params=_cparams(("parallel",)),
        name="swa_p",
    )(sink, qb, kb, vb)


def _swa_sample_kernel(sink_ref, q_ref, k_ref, v_ref, kc_ref, vc_ref, o_ref):
    n = pl.program_id(1)
    start = pl.multiple_of(jnp.clip((n - 1) * SWA_QB, 0, DEC_SEQ - SWA_WIN), SWA_QB)
    kl = k_ref[pl.ds(start, SWA_WIN), :].astype(BF16)
    vl = v_ref[pl.ds(start, SWA_WIN), :].astype(BF16)
    kc = kc_ref[...].astype(BF16)
    vc = vc_ref[...].astype(BF16)
    qi = n * SWA_QB + lax.broadcasted_iota(jnp.int32, (SWA_QB, SWA_WIN), 0)
    ki = start + lax.broadcasted_iota(jnp.int32, (SWA_QB, SWA_WIN), 1)
    band = jnp.abs(qi - ki) <= WINDOW
    lane = lax.broadcasted_iota(jnp.int32, (SWA_QB, LANES), 1)
    for p in range(H_B // 2):
        qp = q_ref[:, p * LANES:(p + 1) * LANES]
        outs = []
        for g in range(KV_B):
            qm = qp * _ones_where((lane // D_B) == g)
            s_ctx = _dot_nt(qm, kc)
            s_loc = jnp.where(band, _dot_nt(qm, kl), NEG_BIG)
            outs.append(_swa_head_softmax([(s_ctx, vc), (s_loc, vl)], sink_ref[2 * p + g]))
        o_ref[:, p * LANES:(p + 1) * LANES] = jnp.where(lane < D_B, outs[0], outs[1]).astype(BF16)


def _swa_sample(sink, qb, kb, vb, kc, vc):
    nqb = DEC_SEQ // SWA_QB
    q0 = N_PROMPT // SWA_QB
    s0 = N_PROMPT // DEC_SEQ
    return pl.pallas_call(
        _swa_sample_kernel,
        grid_spec=pltpu.PrefetchScalarGridSpec(
            num_scalar_prefetch=1,
            grid=(DEC_BATCH, nqb),
            in_specs=[
                pl.BlockSpec((SWA_QB, QB_W), lambda b, n, s: (q0 + b * nqb + n, 0)),
                pl.BlockSpec((DEC_SEQ, KB_W), lambda b, n, s: (s0 + b, 0)),
                pl.BlockSpec((DEC_SEQ, KB_W), lambda b, n, s: (s0 + b, 0)),
                pl.BlockSpec((None, PAST_LEN, KB_W), lambda b, n, s: (b, 0, 0)),
                pl.BlockSpec((None, PAST_LEN, KB_W), lambda b, n, s: (b, 0, 0)),
            ],
            out_specs=pl.BlockSpec((SWA_QB, QB_W), lambda b, n, s: (b * nqb + n, 0)),
        ),
        out_shape=jax.ShapeDtypeStruct((N_SAMPLE, QB_W), BF16),
        compiler_params=_cparams(("parallel", "parallel")),
        name="swa_s",
    )(sink, qb, kb, vb, kc, vc)


def _tile_heads(x):
    return jnp.concatenate([x] * H_C, axis=1)


def _in_odd_kernel(x_ref, mod_ref, g_ref, wd_ref, gq_ref, gkv_ref, wuq_ref, wuk_ref, wuv_ref, cos_ref, sin_ref,
                   q_ref, k_ref, v_ref, ckv_ref, kr_ref):
    h = _rms(x_ref[...], g_ref[0:1, :]) * (1.0 + mod_ref[1:2, :]) + mod_ref[0:1, :]
    hb = h.astype(BF16)
    cos = cos_ref[...]
    sin = sin_ref[...]
    c_q = _dot(hb, wd_ref[:, 0:Q_LORA])
    c_kv = _rms(_dot(hb, wd_ref[:, Q_LORA:Q_LORA + KV_LORA]), gkv_ref[...])
    kr = _rope(_dot(hb, wd_ref[:, Q_LORA + KV_LORA:DOWN_W]), cos, sin, ROPE_C // 4)
    ckv_ref[...] = c_kv
    kr_ref[...] = kr
    cqb = _rms(c_q, gq_ref[...]).astype(BF16)
    ckvb = c_kv.astype(BF16)
    scale = (NOPE_C + ROPE_C) ** -0.5
    for hd in range(H_C):
        sl = slice(hd * HEAD_PAD_C, (hd + 1) * HEAD_PAD_C)
        q_ref[:, sl] = (_rope(_dot(cqb, wuq_ref[:, sl]), cos, sin, ROPE_C // 4) * scale).astype(BF16)
        k_ref[:, sl] = (_dot(ckvb, wuk_ref[:, sl]) + kr).astype(BF16)
    v_ref[...] = _dot(ckvb, wuv_ref[...]).astype(BF16)


def _in_odd(x, mods, g, w_down, g_q, g_kv, w_uq, w_uk, w_uv, cos, sin):
    tm = TM_IN
    row = lambda i: (i, 0)
    full = lambda i: (0, 0)
    widths = (QC_W, QC_W, VC_W, KV_LORA, LANES)
    dtypes = (BF16, BF16, BF16, F32, F32)
    return pl.pallas_call(
        _in_odd_kernel,
        grid=(N_TOK // tm,),
        in_specs=[
            pl.BlockSpec((tm, D_MODEL), row),
            pl.BlockSpec((None, 6, D_MODEL), lambda i: (_mod_row(i, tm), 0, 0)),
            pl.BlockSpec((4, D_MODEL), full),
            pl.BlockSpec((D_MODEL, DOWN_W), full),
            pl.BlockSpec((1, Q_LORA), full),
            pl.BlockSpec((1, KV_LORA), full),
            pl.BlockSpec((Q_LORA, QC_W), full),
            pl.BlockSpec((KV_LORA, QC_W), full),
            pl.BlockSpec((KV_LORA, VC_W), full),
            pl.BlockSpec((tm, LANES), row),
            pl.BlockSpec((tm, LANES), row),
        ],
        out_specs=[pl.BlockSpec((tm, w), row) for w in widths],
        out_shape=[jax.ShapeDtypeStruct((N_TOK, w), d) for w, d in zip(widths, dtypes)],
        compiler_params=_cparams(("parallel",)),
        name="in_odd",
    )(x, mods, g, w_down, g_q, g_kv, w_uq, w_uk, w_uv, cos, sin)


def _mla_heads(q_ref, kv_pieces, o_ref, n_rows):
    lane = lax.broadcasted_iota(jnp.int32, (n_rows, LANES), 1)
    for pair in range(H_C // 2):
        outs = []
        for hh in range(2):
            hd = 2 * pair + hh
            sl = slice(hd * HEAD_PAD_C, (hd + 1) * HEAD_PAD_C)
            q = q_ref[:, sl]
            scores = [_dot_nt(q, k_ref[:, sl]) for k_ref, _ in kv_pieces]
            m = None
            for s in scores:
                sm = jnp.max(s, axis=-1, keepdims=True)
                m = sm if m is None else jnp.maximum(m, sm)
            den = None
            acc = None
            for s, (_, v_ref) in zip(scores, kv_pieces):
                e = jnp.exp(s - m)
                es = jnp.sum(e, axis=-1, keepdims=True)
                den = es if den is None else den + es
                pv = _dot(e.astype(BF16), v_ref[:, pair * LANES:(pair + 1) * LANES])
                acc = pv if acc is None else acc + pv
            outs.append(acc / den)
        o_ref[:, pair * LANES:(pair + 1) * LANES] = jnp.where(lane < V_C, outs[0], outs[1]).astype(BF16)


def _mla_prompt_kernel(q_ref, k_ref, v_ref, o_ref):
    _mla_heads(q_ref, [(k_ref, v_ref)], o_ref, SEQ)


def _mla_prompt(q, k, v):
    seq = lambda b: (b, 0)
    return pl.pallas_call(
        _mla_prompt_kernel,
        grid=(BATCH,),
        in_specs=[pl.BlockSpec((SEQ, QC_W), seq), pl.BlockSpec((SEQ, QC_W), seq), pl.BlockSpec((SEQ, VC_W), seq)],
        out_specs=pl.BlockSpec((SEQ, VC_W), seq),
        out_shape=jax.ShapeDtypeStruct((N_PROMPT, VC_W), BF16),
        compiler_params=_cparams(("parallel",)),
        name="mla_p",
    )(q, k, v)


def _mla_sample_kernel(q_ref, k_ref, v_ref, ckv_ref, kr_ref, wuk_ref, wuv_ref, o_ref, kc_ref, vc_ref):
    @pl.when(pl.program_id(1) == 0)
    def _():
        cb = ckv_ref[...].astype(BF16)
        kr = kr_ref[...]
        for hd in range(H_C):
            sl = slice(hd * HEAD_PAD_C, (hd + 1) * HEAD_PAD_C)
            kc_ref[:, sl] = (_dot(cb, wuk_ref[:, sl]) + kr).astype(BF16)
        vc_ref[...] = _dot(cb, wuv_ref[...]).astype(BF16)

    _mla_heads(q_ref, [(kc_ref, vc_ref), (k_ref, v_ref)], o_ref, MLA_QB)


def _mla_sample(q, k, v, ckv_ctx, kr_ctx, w_uk, w_uv):
    nqb = DEC_SEQ // MLA_QB
    q0 = N_PROMPT // MLA_QB
    s0 = N_PROMPT // DEC_SEQ
    full = lambda b, n: (0, 0)
    return pl.pallas_call(
        _mla_sample_kernel,
        grid=(DEC_BATCH, nqb),
        in_specs=[
            pl.BlockSpec((MLA_QB, QC_W), lambda b, n: (q0 + b * nqb + n, 0)),
            pl.BlockSpec((DEC_SEQ, QC_W), lambda b, n: (s0 + b, 0)),
            pl.BlockSpec((DEC_SEQ, VC_W), lambda b, n: (s0 + b, 0)),
            pl.BlockSpec((None, PAST_LEN, KV_LORA), lambda b, n: (b, 0, 0)),
            pl.BlockSpec((None, PAST_LEN, LANES), lambda b, n: (b, 0, 0)),
            pl.BlockSpec((KV_LORA, QC_W), full),
            pl.BlockSpec((KV_LORA, VC_W), full),
        ],
        out_specs=pl.BlockSpec((MLA_QB, VC_W), lambda b, n: (b * nqb + n, 0)),
        out_shape=jax.ShapeDtypeStruct((N_SAMPLE, VC_W), BF16),
        scratch_shapes=[pltpu.VMEM((PAST_LEN, QC_W), BF16), pltpu.VMEM((PAST_LEN, VC_W), BF16)],
        compiler_params=_cparams(("parallel", "arbitrary")),
        name="mla_s",
    )(q, k, v, ckv_ctx, kr_ctx, w_uk, w_uv)


def _out_kernel(x_ref, mod_ref, g_ref, a_ref, b_ref, wo_ref, w1_ref, w2_ref, o_ref, x1_ref, h2_ref, acc_ref):
    kk = pl.program_id(1)
    half = wo_ref.shape[0] // 2

    @pl.when(kk == 0)
    def _():
        mix = _dot(a_ref[...], wo_ref[0:half, :]) + _dot(b_ref[...], wo_ref[half:, :])
        x1 = x_ref[...] + mod_ref[2:3, :] * _rms(mix, g_ref[1:2, :])
        x1_ref[...] = x1
        h2_ref[...] = (_rms(x1, g_ref[2:3, :]) * (1.0 + mod_ref[4:5, :]) + mod_ref[3:4, :]).astype(BF16)
        acc_ref[...] = jnp.zeros(acc_ref.shape, F32)

    hid = jnp.maximum(_dot(h2_ref[...], w1_ref[...]), 0.0)
    acc_ref[...] += _dot((hid * hid).astype(BF16), w2_ref[...])

    @pl.when(kk == pl.num_programs(1) - 1)
    def _():
        o_ref[...] = x1_ref[...] + mod_ref[5:6, :] * _rms(acc_ref[...], g_ref[3:4, :])


def _out_layer(x, mods, g, mix_a, mix_b, b_block, w_o, w_ff1, w_ff2):
    tm = TM_OUT
    half = D_MODEL // 2
    return pl.pallas_call(
        _out_kernel,
        grid=(N_TOK // tm, D_FF // TK_FF),
        in_specs=[
            pl.BlockSpec((tm, D_MODEL), lambda i, k: (i, 0)),
            pl.BlockSpec((None, 6, D_MODEL), lambda i, k: (_mod_row(i, tm), 0, 0)),
            pl.BlockSpec((4, D_MODEL), lambda i, k: (0, 0)),
            pl.BlockSpec((tm, half), lambda i, k: (i, 0)),
            pl.BlockSpec((tm, half), lambda i, k: (i, b_block)),
            pl.BlockSpec((D_MODEL, D_MODEL), lambda i, k: (0, 0)),
            pl.BlockSpec((D_MODEL, TK_FF), lambda i, k: (0, k)),
            pl.BlockSpec((TK_FF, D_MODEL), lambda i, k: (k, 0)),
        ],
        out_specs=pl.BlockSpec((tm, D_MODEL), lambda i, k: (i, 0)),
        out_shape=jax.ShapeDtypeStruct((N_TOK, D_MODEL), F32),
        scratch_shapes=[pltpu.VMEM((tm, D_MODEL), F32), pltpu.VMEM((tm, D_MODEL), BF16),
                        pltpu.VMEM((tm, D_MODEL), F32)],
        compiler_params=_cparams(("parallel", "arbitrary")),
        name="out_mlp",
    )(x, mods, g, mix_a, mix_b, w_o, w_ff1, w_ff2)


def _rope_tables(head_dim, lane_off, group):
    nf = head_dim // 4
    n_rows = DEC_SEQ // GRID_W
    rows = jnp.repeat(jnp.arange(n_rows, dtype=F32), GRID_W)
    cols = jnp.tile(jnp.arange(GRID_W, dtype=F32), n_rows)
    inv = ROPE_BASE ** (-jnp.arange(nf, dtype=F32) / nf)
    ang = jnp.stack([rows[:, None] * inv, cols[:, None] * inv], axis=1)
    cos = jnp.broadcast_to(jnp.cos(ang)[:, :, None, :], (DEC_SEQ, 2, 2, nf)).reshape(DEC_SEQ, head_dim)
    sin = jnp.sin(ang)
    sin = jnp.stack([-sin, sin], axis=2).reshape(DEC_SEQ, head_dim)
    cos_g = jnp.ones((DEC_SEQ, group), F32).at[:, lane_off:lane_off + head_dim].set(cos)
    sin_g = jnp.zeros((DEC_SEQ, group), F32).at[:, lane_off:lane_off + head_dim].set(sin)
    reps = LANES // group
    cos_s = jnp.tile(jnp.tile(cos_g, (1, reps)), (DEC_BATCH, 1))
    sin_s = jnp.tile(jnp.tile(sin_g, (1, reps)), (DEC_BATCH, 1))
    cos_t = jnp.concatenate([jnp.ones((N_PROMPT, LANES), F32), cos_s], axis=0)
    sin_t = jnp.concatenate([jnp.zeros((N_PROMPT, LANES), F32), sin_s], axis=0)
    return cos_t, sin_t


_QB_HEAD_ORDER = np.array([g * G_B + p for p in range(G_B) for g in range(KV_B)])


def _state_to_pairs(s):
    st = jnp.swapaxes(s, -1, -2).reshape(s.shape[0], H_A // 2, 2, DV_A, DK_A)
    z = jnp.zeros_like(st[:, :, 0])
    top = jnp.concatenate([st[:, :, 0], z], axis=-1)
    bot = jnp.concatenate([z, st[:, :, 1]], axis=-1)
    return jnp.concatenate([top, bot], axis=-2)


def _pairs_to_state(sp):
    h0 = sp[:, :, :DV_A, :DK_A]
    h1 = sp[:, :, DV_A:, DK_A:]
    st = jnp.stack([h0, h1], axis=2).reshape(sp.shape[0], H_A, DV_A, DK_A)
    return jnp.swapaxes(st, -1, -2)


def kernel(x_prompt, x_sample, state_gla_fwd, state_gla_bwd, cache_swa_k, cache_swa_v, cache_mla_ckv, cache_mla_kr, c, c_ctx, w_mod, b_mod, g_norm, w_ff1, w_ff2, w_in_ab, w_gk_f, b_gk_f, w_gk_b, b_gk_b, g_gla, swa_sink, w_out_ab, w_mla_down, g_mla_q, g_mla_kv, w_mla_uq, w_mla_ukv, w_mla_o):
    x = jnp.concatenate([x_prompt.reshape(N_PROMPT, D_MODEL), x_sample.reshape(N_SAMPLE, D_MODEL)], axis=0)
    cvecs = jnp.concatenate([c_ctx[None, :], c, jnp.zeros((N_MOD_ROWS - 1 - DEC_BATCH, D_MODEL), F32)], axis=0)
    mods = _modulation(cvecs, w_mod, b_mod).reshape(DEPTH, N_MOD_ROWS, 6, D_MODEL)

    cos_b, sin_b = _rope_tables(D_B, 0, D_B)
    cos_c, sin_c = _rope_tables(ROPE_C, KR_OFF, HEAD_PAD_C)
    p_blk = N_PROMPT // DEC_SEQ

    st_f, st_b, sk, sv, ckv_out, ckr_out = [], [], [], [], [], []
    for l in range(DEPTH):
        i = l // 2
        g = g_norm[l]
        w1 = w_ff1[l].astype(BF16)
        w2 = w_ff2[l].astype(BF16)
        if l % 2 == 0:
            w = w_in_ab[i]
            o_lo = 2 * QA_W + 2 * VA_W
            o_qb = o_lo + 2 * GK_RANK
            wq_b = w[:, o_qb:o_qb + QB_W].reshape(D_MODEL, H_B, D_B)[:, _QB_HEAD_ORDER].reshape(D_MODEL, QB_W)
            w_in = jnp.concatenate([w[:, :o_lo], wq_b, w[:, o_qb + QB_W:], w[:, o_lo:o_qb]], axis=1).astype(BF16)
            zgk = jnp.zeros((GK_RANK, QA_W), F32)
            w_gk = jnp.concatenate([jnp.concatenate([w_gk_f[i], zgk], axis=1),
                                    jnp.concatenate([zgk, w_gk_b[i]], axis=1)], axis=0).astype(BF16)
            b_gk = jnp.concatenate([b_gk_f[i], b_gk_b[i]])[None, :]
            sink = swa_sink[i][_QB_HEAD_ORDER]
            wo = w_out_ab[i]
            wo_swa = wo[VA_W:].reshape(H_B, D_B, D_MODEL)[_QB_HEAD_ORDER].reshape(QB_W, D_MODEL)
            w_o = jnp.concatenate([wo[:VA_W], wo_swa], axis=0).astype(BF16)

            qa, ka, va, ga, ld, qb, kb, vb = _in_even(x, mods[l], g, w_in, w_gk, b_gk, cos_b, sin_b)
            gg = g_gla[i][None, :]
            o_gla_p, st_p = _gla(SEQ, BATCH, 0, qa, ka, va, ga, ld, gg, None)
            s0 = jnp.stack([_state_to_pairs(state_gla_fwd[:, i]), _state_to_pairs(state_gla_bwd[:, i])], axis=1)
            o_gla_s, _ = _gla(DEC_SEQ, DEC_BATCH, p_blk, qa, ka, va, ga, ld, gg, s0)
            o_swa_p = _swa_prompt(sink, qb, kb, vb)
            kc = cache_swa_k[:, i].reshape(DEC_BATCH, PAST_LEN, KB_W)
            vc = cache_swa_v[:, i].reshape(DEC_BATCH, PAST_LEN, KB_W)
            o_swa_s = _swa_sample(sink, qb, kb, vb, kc, vc)
            mix_a = jnp.concatenate([o_gla_p, o_gla_s], axis=0)
            mix_b = jnp.concatenate([o_swa_p, o_swa_s], axis=0)
            x = _out_layer(x, mods[l], g, mix_a, mix_b, 0, w_o, w1, w2)

            st_f.append(_pairs_to_state(st_p[:, 0]))
            st_b.append(_pairs_to_state(st_p[:, 1]))
            sk.append(kb[:N_PROMPT].reshape(BATCH, SEQ, KV_B, D_B))
            sv.append(vb[:N_PROMPT].reshape(BATCH, SEQ, KV_B, D_B))
        else:
            wd = w_mla_down[i]
            zc = jnp.zeros((D_MODEL, KR_OFF), F32)
            zt = jnp.zeros((D_MODEL, LANES - KR_OFF - ROPE_C), F32)
            w_down = jnp.concatenate([wd[:, :Q_LORA + KV_LORA], zc, wd[:, Q_LORA + KV_LORA:], zt], axis=1).astype(BF16)
            wuq = w_mla_uq[i].reshape(Q_LORA, H_C, NOPE_C + ROPE_C)
            w_uq = jnp.pad(wuq, ((0, 0), (0, 0), (0, HEAD_PAD_C - NOPE_C - ROPE_C))).reshape(Q_LORA, QC_W).astype(BF16)
            wukv = w_mla_ukv[i].reshape(KV_LORA, H_C, NOPE_C + V_C)
            w_uk = jnp.pad(wukv[:, :, :NOPE_C], ((0, 0), (0, 0), (0, HEAD_PAD_C - NOPE_C))).reshape(KV_LORA, QC_W).astype(BF16)
            w_uv = wukv[:, :, NOPE_C:].reshape(KV_LORA, VC_W).astype(BF16)
            w_o = w_mla_o[i].astype(BF16)

            q, k, v, ckv, kr = _in_odd(x, mods[l], g, w_down, g_mla_q[i][None, :], g_mla_kv[i][None, :],
                                       w_uq, w_uk, w_uv, cos_c, sin_c)
            o_p = _mla_prompt(q, k, v)
            kr_ctx = jnp.pad(cache_mla_kr[:, i], ((0, 0), (0, 0), (KR_OFF, LANES - KR_OFF - ROPE_C)))
            o_s = _mla_sample(q, k, v, cache_mla_ckv[:, i], kr_ctx, w_uk, w_uv)
            mix = jnp.concatenate([o_p, o_s], axis=0)
            x = _out_layer(x, mods[l], g, mix, mix, 1, w_o, w1, w2)

            ckv_out.append(ckv[:N_PROMPT].reshape(BATCH, SEQ, KV_LORA))
            ckr_out.append(kr[:N_PROMPT, KR_OFF:KR_OFF + ROPE_C].reshape(BATCH, SEQ, ROPE_C))

    y_prompt = x[:N_PROMPT].reshape(BATCH, SEQ, D_MODEL)
    y_sample = x[N_PROMPT:].reshape(DEC_BATCH, DEC_SEQ, D_MODEL)
    return (y_prompt, y_sample, jnp.stack(st_f, axis=1), jnp.stack(st_b, axis=1), jnp.stack(sk, axis=1),
            jnp.stack(sv, axis=1), jnp.stack(ckv_out, axis=1), jnp.stack(ckr_out, axis=1))
```

```python
import functools

import numpy as np
import jax
import jax.numpy as jnp
from jax import lax
from jax.experimental import pallas as pl
from jax.experimental.pallas import tpu as pltpu

F32 = jnp.float32
BF16 = jnp.bfloat16

D_MODEL = 1024
BATCH = 16
SEQ = 256
DEPTH = 4
DEC_BATCH = 2
DEC_SEQ = 1024
PAST_LEN = 512
GRID_W = 64
D_FF = 4 * D_MODEL
EPS = 1e-6
ROPE_BASE = 10000.0
H_A = 4
DK_A = 64
DV_A = 128
GK_RANK = 16
GATE_NORM = 16.0
GLA_CHUNK = 64
H_B = 8
KV_B = 2
G_B = H_B // KV_B
D_B = 64
WINDOW = 128
H_C = 16
NOPE_C = 64
ROPE_C = 32
V_C = 64
Q_LORA = 384
KV_LORA = 256

N_PROMPT = BATCH * SEQ
N_SAMPLE = DEC_BATCH * DEC_SEQ
N_TOK = N_PROMPT + N_SAMPLE
N_MOD_ROWS = 8
QA_W = H_A * DK_A
VA_W = H_A * DV_A
QB_W = H_B * D_B
KB_W = KV_B * D_B
LANES = 128
HEAD_PAD_C = 128
QC_W = H_C * HEAD_PAD_C
VC_W = H_C * V_C
DOWN_W = Q_LORA + KV_LORA + LANES
KR_OFF = NOPE_C

TM_IN = 512
TM_OUT = 1024
TK_FF = 512
TN_MOD = 1536
GLA_GROUP = 256
SWA_QB = 128
SWA_WIN = 3 * SWA_QB
MLA_QB = 256
VMEM_LIMIT = 56 * 1024 * 1024
NEG_BIG = -1e30


def _cparams(sem):
    return pltpu.CompilerParams(dimension_semantics=sem, vmem_limit_bytes=VMEM_LIMIT)


def _dot(a, b):
    return jnp.dot(a, b, preferred_element_type=F32)


def _dot_nt(a, b):
    return lax.dot_general(a, b, (((1,), (1,)), ((), ())), preferred_element_type=F32)


def _dot_tn(a, b):
    return lax.dot_general(a, b, (((0,), (0,)), ((), ())), preferred_element_type=F32)


def _rms(x, g):
    return x * lax.rsqrt(jnp.mean(x * x, axis=-1, keepdims=True) + EPS) * g


def _silu(x):
    return x / (1.0 + jnp.exp(-x))


def _rope(x, cos, sin, half):
    lane = lax.broadcasted_iota(jnp.int32, x.shape, 1)
    first = (lane % (2 * half)) < half
    partner = jnp.where(first, pltpu.roll(x, LANES - half, 1), pltpu.roll(x, half, 1))
    return x * cos + partner * sin


def _split3(x):
    x1 = x.astype(BF16)
    r1 = x - x1.astype(F32)
    x2 = r1.astype(BF16)
    x3 = (r1 - x2.astype(F32)).astype(BF16)
    return x1, x2, x3


def _dot3(t, parts):
    return _dot(t, parts[2]) + _dot(t, parts[1]) + _dot(t, parts[0])


def _ones_where(cond):
    return jnp.where(cond, 1.0, 0.0).astype(BF16)


def _mod_row(tile, tm):
    n_prompt_tiles = N_PROMPT // tm
    tiles_per_seq = DEC_SEQ // tm
    return jnp.where(tile < n_prompt_tiles, 0, 1 + (tile - n_prompt_tiles) // tiles_per_seq)


def _mod_kernel(c_ref, w_ref, b_ref, o_ref):
    s = _silu(c_ref[...])
    o_ref[...] = _dot(s.astype(BF16), w_ref[...].astype(BF16)) + b_ref[...]


def _modulation(cvecs, w_mod, b_mod):
    return pl.pallas_call(
        _mod_kernel,
        grid=(DEPTH, 6 * D_MODEL // TN_MOD),
        in_specs=[
            pl.BlockSpec((N_MOD_ROWS, D_MODEL), lambda l, j: (0, 0)),
            pl.BlockSpec((None, D_MODEL, TN_MOD), lambda l, j: (l, 0, j)),
            pl.BlockSpec((None, 1, TN_MOD), lambda l, j: (l, 0, j)),
        ],
        out_specs=pl.BlockSpec((None, N_MOD_ROWS, TN_MOD), lambda l, j: (l, 0, j)),
        out_shape=jax.ShapeDtypeStruct((DEPTH, N_MOD_ROWS, 6 * D_MODEL), F32),
        compiler_params=_cparams(("parallel", "parallel")),
        name="adaln_mod",
    )(cvecs, w_mod, b_mod.reshape(DEPTH, 1, 6 * D_MODEL))


_C_QA, _C_KA, _C_VA, _C_GA = 0, QA_W, 2 * QA_W, 2 * QA_W + VA_W
_C_QB = _C_GA + VA_W
_C_KB = _C_QB + QB_W
_C_VB = _C_KB + KB_W
_C_LO = _C_VB + KB_W
AB_IN = _C_LO + 2 * GK_RANK


def _load_x(n_x, refs, tm):
    if n_x == 1:
        return refs[0][...]
    return jnp.where(pl.program_id(0) < N_PROMPT // tm, refs[0][...], refs[1][...])


def _x_specs(n_x, tm):
    n_prompt_tiles = N_PROMPT // tm
    if n_x == 1:
        return [pl.BlockSpec((tm, D_MODEL), lambda i, *_: (i, 0))]
    return [pl.BlockSpec((tm, D_MODEL), lambda i, *_: (jnp.minimum(i, n_prompt_tiles - 1), 0)),
            pl.BlockSpec((tm, D_MODEL), lambda i, *_: (jnp.maximum(i - n_prompt_tiles, 0), 0))]


def _in_even_kernel(n_x, *refs):
    (mod_ref, g_ref, w_ref, wgk_ref, bgk_ref, cos_ref, sin_ref,
     qa_ref, ka_ref, va_ref, ga_ref, ld_ref, qb_ref, kb_ref, vb_ref) = refs[n_x:]
    x = _load_x(n_x, refs, TM_IN)
    h = _rms(x, g_ref[0:1, :]) * (1.0 + mod_ref[1:2, :]) + mod_ref[0:1, :]
    hb = h.astype(BF16)
    qa_ref[...] = _dot(hb, w_ref[:, _C_QA:_C_KA]) * (DK_A ** -0.5)
    ka_ref[...] = _dot(hb, w_ref[:, _C_KA:_C_VA])
    va_ref[...] = _dot(hb, w_ref[:, _C_VA:_C_GA]).astype(BF16)
    ga_ref[...] = _dot(hb, w_ref[:, _C_GA:_C_QB])
    cos = cos_ref[...]
    sin = sin_ref[...]
    for j in range(QB_W // LANES):
        qj = _dot(hb, w_ref[:, _C_QB + j * LANES:_C_QB + (j + 1) * LANES])
        qb_ref[:, j * LANES:(j + 1) * LANES] = (_rope(qj, cos, sin, D_B // 4) * (D_B ** -0.5)).astype(BF16)
    kb_ref[...] = _rope(_dot(hb, w_ref[:, _C_KB:_C_VB]), cos, sin, D_B // 4)
    vb_ref[...] = _dot(hb, w_ref[:, _C_VB:_C_LO])
    lo = _dot(hb, w_ref[:, _C_LO:AB_IN]).astype(BF16)
    z = _dot(lo, wgk_ref[...]) + bgk_ref[...]
    ld_ref[...] = (jnp.minimum(z, 0.0) - jnp.log(1.0 + jnp.exp(-jnp.abs(z)))) * (1.0 / GATE_NORM)


def _in_even(xs, mods, g, w_in, w_gk, b_gk, cos, sin):
    tm = TM_IN
    row = lambda i: (i, 0)
    full = lambda i: (0, 0)
    widths = (QA_W, QA_W, VA_W, VA_W, 2 * QA_W, QB_W, KB_W, KB_W)
    dtypes = (F32, F32, BF16, F32, F32, BF16, F32, F32)
    return pl.pallas_call(
        functools.partial(_in_even_kernel, len(xs)),
        grid=(N_TOK // tm,),
        in_specs=_x_specs(len(xs), tm) + [
            pl.BlockSpec((None, 6, D_MODEL), lambda i: (_mod_row(i, tm), 0, 0)),
            pl.BlockSpec((4, D_MODEL), full),
            pl.BlockSpec((D_MODEL, AB_IN), full),
            pl.BlockSpec((2 * GK_RANK, 2 * QA_W), full),
            pl.BlockSpec((1, 2 * QA_W), full),
            pl.BlockSpec((tm, LANES), row),
            pl.BlockSpec((tm, LANES), row),
        ],
        out_specs=[pl.BlockSpec((tm, w), row) for w in widths],
        out_shape=[jax.ShapeDtypeStruct((N_TOK, w), d) for w, d in zip(widths, dtypes)],
        compiler_params=_cparams(("parallel",)),
        name="in_even",
    )(*xs, mods, g, w_in, w_gk, b_gk, cos, sin)


def _gla_kernel(seq_len, has_s0, *refs):
    if has_s0:
        (qa_ref, ka_ref, va_ref, ga_ref, ld_ref, gg_ref, s0_ref, _, o_ref, st_ref,
         qi_ref, ki_ref, qcat_ref, ks_ref, dec_ref, stcat_ref, acc_ref) = refs
    else:
        (qa_ref, ka_ref, va_ref, ga_ref, ld_ref, gg_ref, o_ref, st_ref,
         qi_ref, ki_ref, qcat_ref, ks_ref, dec_ref, stcat_ref, acc_ref) = refs
        s0_ref = None
    n_groups = seq_len // GLA_GROUP
    cpg = GLA_GROUP // GLA_CHUNK
    n_chunks = seq_len // GLA_CHUNK
    n_pairs = H_A // 2
    pair_k = 2 * DK_A
    pair_v = 2 * DV_A

    def chunk_masks():
        r_i = lax.broadcasted_iota(jnp.int32, (GLA_GROUP, GLA_GROUP), 0)
        c_i = lax.broadcasted_iota(jnp.int32, (GLA_GROUP, GLA_GROUP), 1)
        same = (r_i // GLA_CHUNK) == (c_i // GLA_CHUNK)
        return same, same & (c_i <= r_i), same & (c_i >= r_i)

    def scale_group(j, carry):
        same, mask_f, mask_b = chunk_masks()
        t_cum = (_ones_where(mask_f), _ones_where(mask_b))
        t_all = _ones_where(same)
        s_r = lax.broadcasted_iota(jnp.int32, (8, GLA_GROUP), 0)
        s_c = lax.broadcasted_iota(jnp.int32, (8, GLA_GROUP), 1)
        t_sel = _ones_where(s_r == s_c // GLA_CHUNK)
        rows = pl.ds(pl.multiple_of(j * GLA_GROUP, GLA_GROUP), GLA_GROUP)
        q = qa_ref[rows, :]
        k = ka_ref[rows, :]
        for d in range(2):
            parts = _split3(ld_ref[rows, d * QA_W:(d + 1) * QA_W])
            b = _dot3(t_cum[d], parts)
            tot = _dot3(t_all, parts)
            dec8 = jnp.exp(_dot3(t_sel, parts))
            ref = 0.5 * tot
            qi_ref[d, rows, :] = (q * jnp.exp(b - ref)).astype(BF16)
            ki_ref[d, rows, :] = (k * jnp.exp(ref - b)).astype(BF16)
            ks_ref[d, rows, :] = (k * jnp.exp(tot - b)).astype(BF16)
            q_inter = (q * jnp.exp(b)).astype(BF16)
            for p in range(n_pairs):
                qcat_ref[rows, p * 2 * pair_k + d * pair_k:p * 2 * pair_k + (d + 1) * pair_k] = (
                    q_inter[:, p * pair_k:(p + 1) * pair_k])
            for cc in range(cpg):
                dec_ref[d, j * cpg + cc] = jnp.broadcast_to(dec8[cc:cc + 1, :], (8, QA_W))
        return carry

    lax.fori_loop(0, n_groups, scale_group, 0)

    if has_s0:
        st_ref[...] = s0_ref[...]
    else:
        st_ref[...] = jnp.zeros(st_ref.shape, F32)

    def scan_chunk(c, carry):
        bd_r = lax.broadcasted_iota(jnp.int32, (pair_v, pair_k), 0)
        bd_c = lax.broadcasted_iota(jnp.int32, (pair_v, pair_k), 1)
        bd_mask = (bd_r // DV_A) == (bd_c // DK_A)
        for d in range(2):
            cd = c if d == 0 else n_chunks - 1 - c
            rows = pl.ds(pl.multiple_of(cd * GLA_CHUNK, GLA_CHUNK), GLA_CHUNK)
            for p in range(n_pairs):
                st = st_ref[d, p]
                stcat_ref[cd, p, :, d * pair_k:(d + 1) * pair_k] = st.astype(BF16)
                u = _dot_tn(va_ref[rows, p * pair_v:(p + 1) * pair_v], ks_ref[d, rows, p * pair_k:(p + 1) * pair_k])
                dec = dec_ref[d, cd][0:1, p * pair_k:(p + 1) * pair_k]
                st_ref[d, p] = st * dec + jnp.where(bd_mask, u, 0.0)
        return carry

    lax.fori_loop(0, n_chunks, scan_chunk, 0)

    def out_group(j, carry):
        _, mask_f, mask_b = chunk_masks()
        lane = lax.broadcasted_iota(jnp.int32, (GLA_GROUP, pair_k), 1)
        rows = pl.ds(pl.multiple_of(j * GLA_GROUP, GLA_GROUP), GLA_GROUP)
        for p in range(n_pairs):
            for cc in range(cpg):
                c = j * cpg + cc
                crow = pl.ds(pl.multiple_of(c * GLA_CHUNK, GLA_CHUNK), GLA_CHUNK)
                acc_ref[crow, p * pair_v:(p + 1) * pair_v] = _dot_nt(
                    qcat_ref[crow, p * 2 * pair_k:(p + 1) * 2 * pair_k], stcat_ref[c, p])
        for h in range(H_A):
            p, hh = divmod(h, 2)
            head_lanes = _ones_where((lane // DK_A) == hh)
            att = None
            for d, mask in ((0, mask_f), (1, mask_b)):
                qm = qi_ref[d, rows, p * pair_k:(p + 1) * pair_k] * head_lanes
                a = jnp.where(mask, _dot_nt(qm, ki_ref[d, rows, p * pair_k:(p + 1) * pair_k]), 0.0)
                att = a if att is None else att + a
            vh = va_ref[rows, h * DV_A:(h + 1) * DV_A]
            o = acc_ref[rows, h * DV_A:(h + 1) * DV_A] + _dot(att.astype(BF16), vh)
            gate = _silu(ga_ref[rows, h * DV_A:(h + 1) * DV_A])
            o_ref[rows, h * DV_A:(h + 1) * DV_A] = (_rms(o, gg_ref[...]) * gate).astype(BF16)
        return carry

    lax.fori_loop(0, n_groups, out_group, 0)


def _gla(seq_len, n_seq, row_block0, qa, ka, va, ga, ld, g_gla, s0=None, prev_out=None):
    has_s0 = s0 is not None
    n_chunks = seq_len // GLA_CHUNK
    n_pairs = H_A // 2
    rows = lambda b: (row_block0 + b, 0)
    st_spec = pl.BlockSpec((None, 2, n_pairs, 2 * DV_A, 2 * DK_A), lambda b: (b, 0, 0, 0, 0))
    in_specs = [
        pl.BlockSpec((seq_len, QA_W), rows),
        pl.BlockSpec((seq_len, QA_W), rows),
        pl.BlockSpec((seq_len, VA_W), rows),
        pl.BlockSpec((seq_len, VA_W), rows),
        pl.BlockSpec((seq_len, 2 * QA_W), rows),
        pl.BlockSpec((1, DV_A), lambda b: (0, 0)),
    ]
    args = [qa, ka, va, ga, ld, g_gla]
    aliases = {}
    if has_s0:
        in_specs += [st_spec, pl.BlockSpec(memory_space=pl.ANY)]
        args += [s0, prev_out]
        aliases = {len(args) - 1: 0}
    return pl.pallas_call(
        functools.partial(_gla_kernel, seq_len, has_s0),
        grid=(n_seq,),
        in_specs=in_specs,
        out_specs=[pl.BlockSpec((seq_len, VA_W), rows), st_spec],
        input_output_aliases=aliases,
        out_shape=[jax.ShapeDtypeStruct((N_TOK, VA_W), BF16),
                   jax.ShapeDtypeStruct((n_seq, 2, n_pairs, 2 * DV_A, 2 * DK_A), F32)],
        scratch_shapes=[
            pltpu.VMEM((2, seq_len, QA_W), BF16),
            pltpu.VMEM((2, seq_len, QA_W), BF16),
            pltpu.VMEM((seq_len, 2 * QA_W), BF16),
            pltpu.VMEM((2, seq_len, QA_W), BF16),
            pltpu.VMEM((2, n_chunks, 8, QA_W), F32),
            pltpu.VMEM((n_chunks, n_pairs, 2 * DV_A, 4 * DK_A), BF16),
            pltpu.VMEM((seq_len, VA_W), F32),
        ],
        compiler_params=_cparams(("parallel",)),
        name="gla_s" if has_s0 else "gla_p",
    )(*args)


def _swa_head_softmax(pieces, sink):
    m = sink
    for s, _ in pieces:
        m = jnp.maximum(m, jnp.max(s, axis=-1, keepdims=True))
    den = jnp.exp(sink - m)
    acc = None
    for s, v in pieces:
        e = jnp.exp(s - m)
        den = den + jnp.sum(e, axis=-1, keepdims=True)
        pv = _dot(e.astype(BF16), v)
        acc = pv if acc is None else acc + pv
    return acc / den


def _swa_prompt_kernel(sink_ref, q_ref, k_ref, v_ref, o_ref):
    k = k_ref[...].astype(BF16)
    v = v_ref[...].astype(BF16)
    lane = lax.broadcasted_iota(jnp.int32, (SEQ, LANES), 1)
    for p in range(H_B // 2):
        qp = q_ref[:, p * LANES:(p + 1) * LANES]
        outs = []
        for g in range(KV_B):
            qm = qp * _ones_where((lane // D_B) == g)
            s = _dot_nt(qm, k)
            outs.append(_swa_head_softmax([(s, v)], sink_ref[2 * p + g]))
        o_ref[:, p * LANES:(p + 1) * LANES] = jnp.where(lane < D_B, outs[0], outs[1]).astype(BF16)


def _swa_prompt(sink, qb, kb, vb):
    seq = lambda b: (b, 0)
    return pl.pallas_call(
        _swa_prompt_kernel,
        grid=(BATCH,),
        in_specs=[pl.BlockSpec(memory_space=pltpu.SMEM), pl.BlockSpec((SEQ, QB_W), seq),
                  pl.BlockSpec((SEQ, KB_W), seq), pl.BlockSpec((SEQ, KB_W), seq)],
        out_specs=pl.BlockSpec((SEQ, QB_W), seq),
        out_shape=jax.ShapeDtypeStruct((N_TOK, QB_W), BF16),
        compiler_params=_cparams(("parallel",)),
        name="swa_p",
    )(sink, qb, kb, vb)


def _swa_sample_kernel(sink_ref, q_ref, k_ref, v_ref, kc_ref, vc_ref, _, o_ref):
    n = pl.program_id(1)
    start = pl.multiple_of(jnp.clip((n - 1) * SWA_QB, 0, DEC_SEQ - SWA_WIN), SWA_QB)
    kl = k_ref[pl.ds(start, SWA_WIN), :].astype(BF16)
    vl = v_ref[pl.ds(start, SWA_WIN), :].astype(BF16)
    kc = kc_ref[...].astype(BF16)
    vc = vc_ref[...].astype(BF16)
    qi = n * SWA_QB + lax.broadcasted_iota(jnp.int32, (SWA_QB, SWA_WIN), 0)
    ki = start + lax.broadcasted_iota(jnp.int32, (SWA_QB, SWA_WIN), 1)
    band = jnp.abs(qi - ki) <= WINDOW
    lane = lax.broadcasted_iota(jnp.int32, (SWA_QB, LANES), 1)
    for p in range(H_B // 2):
        qp = q_ref[:, p * LANES:(p + 1) * LANES]
        outs = []
        for g in range(KV_B):
            qm = qp * _ones_where((lane // D_B) == g)
            s_ctx = _dot_nt(qm, kc)
            s_loc = jnp.where(band, _dot_nt(qm, kl), NEG_BIG)
            outs.append(_swa_head_softmax([(s_ctx, vc), (s_loc, vl)], sink_ref[2 * p + g]))
        o_ref[:, p * LANES:(p + 1) * LANES] = jnp.where(lane < D_B, outs[0], outs[1]).astype(BF16)


def _swa_sample(sink, qb, kb, vb, kc, vc, prev_out):
    nqb = DEC_SEQ // SWA_QB
    q0 = N_PROMPT // SWA_QB
    s0 = N_PROMPT // DEC_SEQ
    q_rows = lambda b, n: (q0 + b * nqb + n, 0)
    return pl.pallas_call(
        _swa_sample_kernel,
        grid=(DEC_BATCH, nqb),
        in_specs=[
            pl.BlockSpec(memory_space=pltpu.SMEM),
            pl.BlockSpec((SWA_QB, QB_W), q_rows),
            pl.BlockSpec((DEC_SEQ, KB_W), lambda b, n: (s0 + b, 0)),
            pl.BlockSpec((DEC_SEQ, KB_W), lambda b, n: (s0 + b, 0)),
            pl.BlockSpec((None, PAST_LEN, KB_W), lambda b, n: (b, 0, 0)),
            pl.BlockSpec((None, PAST_LEN, KB_W), lambda b, n: (b, 0, 0)),
            pl.BlockSpec(memory_space=pl.ANY),
        ],
        out_specs=pl.BlockSpec((SWA_QB, QB_W), q_rows),
        out_shape=jax.ShapeDtypeStruct((N_TOK, QB_W), BF16),
        input_output_aliases={6: 0},
        compiler_params=_cparams(("parallel", "parallel")),
        name="swa_s",
    )(sink, qb, kb, vb, kc, vc, prev_out)


def _tile_heads(x):
    return jnp.concatenate([x] * H_C, axis=1)


def _in_odd_kernel(x_ref, mod_ref, g_ref, wd_ref, gq_ref, gkv_ref, wuq_ref, wuk_ref, wuv_ref, cos_ref, sin_ref,
                   q_ref, k_ref, v_ref, ckv_ref, kr_ref):
    h = _rms(x_ref[...], g_ref[0:1, :]) * (1.0 + mod_ref[1:2, :]) + mod_ref[0:1, :]
    hb = h.astype(BF16)
    cos = cos_ref[...]
    sin = sin_ref[...]
    c_q = _dot(hb, wd_ref[:, 0:Q_LORA])
    c_kv = _rms(_dot(hb, wd_ref[:, Q_LORA:Q_LORA + KV_LORA]), gkv_ref[...])
    kr = _rope(_dot(hb, wd_ref[:, Q_LORA + KV_LORA:DOWN_W]), cos, sin, ROPE_C // 4)
    ckv_ref[...] = c_kv
    kr_ref[...] = kr
    cqb = _rms(c_q, gq_ref[...]).astype(BF16)
    ckvb = c_kv.astype(BF16)
    scale = (NOPE_C + ROPE_C) ** -0.5
    for hd in range(H_C):
        sl = slice(hd * HEAD_PAD_C, (hd + 1) * HEAD_PAD_C)
        q_ref[:, sl] = (_rope(_dot(cqb, wuq_ref[:, sl]), cos, sin, ROPE_C // 4) * scale).astype(BF16)
        k_ref[:, sl] = (_dot(ckvb, wuk_ref[:, sl]) + kr).astype(BF16)
    v_ref[...] = _dot(ckvb, wuv_ref[...]).astype(BF16)


def _in_odd(x, mods, g, w_down, g_q, g_kv, w_uq, w_uk, w_uv, cos, sin):
    tm = TM_IN
    row = lambda i: (i, 0)
    full = lambda i: (0, 0)
    widths = (QC_W, QC_W, VC_W, KV_LORA, LANES)
    dtypes = (BF16, BF16, BF16, F32, F32)
    return pl.pallas_call(
        _in_odd_kernel,
        grid=(N_TOK // tm,),
        in_specs=[
            pl.BlockSpec((tm, D_MODEL), row),
            pl.BlockSpec((None, 6, D_MODEL), lambda i: (_mod_row(i, tm), 0, 0)),
            pl.BlockSpec((4, D_MODEL), full),
            pl.BlockSpec((D_MODEL, DOWN_W), full),
            pl.BlockSpec((1, Q_LORA), full),
            pl.BlockSpec((1, KV_LORA), full),
            pl.BlockSpec((Q_LORA, QC_W), full),
            pl.BlockSpec((KV_LORA, QC_W), full),
            pl.BlockSpec((KV_LORA, VC_W), full),
            pl.BlockSpec((tm, LANES), row),
            pl.BlockSpec((tm, LANES), row),
        ],
        out_specs=[pl.BlockSpec((tm, w), row) for w in widths],
        out_shape=[jax.ShapeDtypeStruct((N_TOK, w), d) for w, d in zip(widths, dtypes)],
        compiler_params=_cparams(("parallel",)),
        name="in_odd",
    )(x, mods, g, w_down, g_q, g_kv, w_uq, w_uk, w_uv, cos, sin)


def _mla_heads(q_ref, kv_pieces, o_ref, n_rows):
    lane = lax.broadcasted_iota(jnp.int32, (n_rows, LANES), 1)
    for pair in range(H_C // 2):
        outs = []
        for hh in range(2):
            hd = 2 * pair + hh
            sl = slice(hd * HEAD_PAD_C, (hd + 1) * HEAD_PAD_C)
            q = q_ref[:, sl]
            scores = [_dot_nt(q, k_ref[:, sl]) for k_ref, _ in kv_pieces]
            m = None
            for s in scores:
                sm = jnp.max(s, axis=-1, keepdims=True)
                m = sm if m is None else jnp.maximum(m, sm)
            den = None
            acc = None
            for s, (_, v_ref) in zip(scores, kv_pieces):
                e = jnp.exp(s - m)
                es = jnp.sum(e, axis=-1, keepdims=True)
                den = es if den is None else den + es
                pv = _dot(e.astype(BF16), v_ref[:, pair * LANES:(pair + 1) * LANES])
                acc = pv if acc is None else acc + pv
            outs.append(acc / den)
        o_ref[:, pair * LANES:(pair + 1) * LANES] = jnp.where(lane < V_C, outs[0], outs[1]).astype(BF16)


def _mla_prompt_kernel(q_ref, k_ref, v_ref, o_ref):
    _mla_heads(q_ref, [(k_ref, v_ref)], o_ref, SEQ)


def _mla_prompt(q, k, v):
    seq = lambda b: (b, 0)
    return pl.pallas_call(
        _mla_prompt_kernel,
        grid=(BATCH,),
        in_specs=[pl.BlockSpec((SEQ, QC_W), seq), pl.BlockSpec((SEQ, QC_W), seq), pl.BlockSpec((SEQ, VC_W), seq)],
        out_specs=pl.BlockSpec((SEQ, VC_W), seq),
        out_shape=jax.ShapeDtypeStruct((N_TOK, VC_W), BF16),
        compiler_params=_cparams(("parallel",)),
        name="mla_p",
    )(q, k, v)


def _mla_sample_kernel(q_ref, k_ref, v_ref, ckv_ref, kr_ref, wuk_ref, wuv_ref, _, o_ref, kc_ref, vc_ref):
    @pl.when(pl.program_id(1) == 0)
    def _():
        cb = ckv_ref[...].astype(BF16)
        kr = kr_ref[...]
        for hd in range(H_C):
            sl = slice(hd * HEAD_PAD_C, (hd + 1) * HEAD_PAD_C)
            kc_ref[:, sl] = (_dot(cb, wuk_ref[:, sl]) + kr).astype(BF16)
        vc_ref[...] = _dot(cb, wuv_ref[...]).astype(BF16)

    _mla_heads(q_ref, [(kc_ref, vc_ref), (k_ref, v_ref)], o_ref, MLA_QB)


def _mla_sample(q, k, v, ckv_ctx, kr_ctx, w_uk, w_uv, prev_out):
    nqb = DEC_SEQ // MLA_QB
    q0 = N_PROMPT // MLA_QB
    s0 = N_PROMPT // DEC_SEQ
    full = lambda b, n: (0, 0)
    q_rows = lambda b, n: (q0 + b * nqb + n, 0)
    return pl.pallas_call(
        _mla_sample_kernel,
        grid=(DEC_BATCH, nqb),
        in_specs=[
            pl.BlockSpec((MLA_QB, QC_W), q_rows),
            pl.BlockSpec((DEC_SEQ, QC_W), lambda b, n: (s0 + b, 0)),
            pl.BlockSpec((DEC_SEQ, VC_W), lambda b, n: (s0 + b, 0)),
            pl.BlockSpec((None, PAST_LEN, KV_LORA), lambda b, n: (b, 0, 0)),
            pl.BlockSpec((None, PAST_LEN, LANES), lambda b, n: (b, 0, 0)),
            pl.BlockSpec((KV_LORA, QC_W), full),
            pl.BlockSpec((KV_LORA, VC_W), full),
            pl.BlockSpec(memory_space=pl.ANY),
        ],
        out_specs=pl.BlockSpec((MLA_QB, VC_W), q_rows),
        out_shape=jax.ShapeDtypeStruct((N_TOK, VC_W), BF16),
        scratch_shapes=[pltpu.VMEM((PAST_LEN, QC_W), BF16), pltpu.VMEM((PAST_LEN, VC_W), BF16)],
        input_output_aliases={7: 0},
        compiler_params=_cparams(("parallel", "arbitrary")),
        name="mla_s",
    )(q, k, v, ckv_ctx, kr_ctx, w_uk, w_uv, prev_out)


def _out_kernel(n_x, n_out, *refs):
    mod_ref, g_ref, a_ref, b_ref, wo_ref, w1_ref, w2_ref = refs[n_x:n_x + 7]
    out_refs = refs[n_x + 7:n_x + 7 + n_out]
    x1_ref, h2_ref, acc_ref = refs[n_x + 7 + n_out:]
    kk = pl.program_id(1)
    half = wo_ref.shape[0] // 2

    @pl.when(kk == 0)
    def _():
        mix = _dot(a_ref[...], wo_ref[0:half, :]) + _dot(b_ref[...], wo_ref[half:, :])
        x1 = _load_x(n_x, refs, TM_OUT) + mod_ref[2:3, :] * _rms(mix, g_ref[1:2, :])
        x1_ref[...] = x1
        h2_ref[...] = (_rms(x1, g_ref[2:3, :]) * (1.0 + mod_ref[4:5, :]) + mod_ref[3:4, :]).astype(BF16)
        acc_ref[...] = jnp.zeros(acc_ref.shape, F32)

    hid = jnp.maximum(_dot(h2_ref[...], w1_ref[...].astype(BF16)), 0.0)
    acc_ref[...] += _dot((hid * hid).astype(BF16), w2_ref[...].astype(BF16))

    @pl.when(kk == pl.num_programs(1) - 1)
    def _():
        y = x1_ref[...] + mod_ref[5:6, :] * _rms(acc_ref[...], g_ref[3:4, :])
        if n_out == 1:
            out_refs[0][...] = y
        else:
            is_prompt = pl.program_id(0) < N_PROMPT // TM_OUT

            @pl.when(is_prompt)
            def _():
                out_refs[0][...] = y

            @pl.when(jnp.logical_not(is_prompt))
            def _():
                out_refs[1][...] = y


def _out_layer(xs, mods, g, mix_a, mix_b, b_block, w_o, w_ff1, w_ff2, split_out):
    tm = TM_OUT
    half = D_MODEL // 2
    n_prompt_tiles = N_PROMPT // tm
    if split_out:
        out_specs = [pl.BlockSpec((tm, D_MODEL), lambda i, k: (jnp.minimum(i, n_prompt_tiles - 1), 0)),
                     pl.BlockSpec((tm, D_MODEL), lambda i, k: (jnp.maximum(i - n_prompt_tiles, 0), 0))]
        out_shape = [jax.ShapeDtypeStruct((N_PROMPT, D_MODEL), F32), jax.ShapeDtypeStruct((N_SAMPLE, D_MODEL), F32)]
    else:
        out_specs = [pl.BlockSpec((tm, D_MODEL), lambda i, k: (i, 0))]
        out_shape = [jax.ShapeDtypeStruct((N_TOK, D_MODEL), F32)]
    return pl.pallas_call(
        functools.partial(_out_kernel, len(xs), len(out_specs)),
        grid=(N_TOK // tm, D_FF // TK_FF),
        in_specs=_x_specs(len(xs), tm) + [
            pl.BlockSpec((None, 6, D_MODEL), lambda i, k: (_mod_row(i, tm), 0, 0)),
            pl.BlockSpec((4, D_MODEL), lambda i, k: (0, 0)),
            pl.BlockSpec((tm, half), lambda i, k: (i, 0)),
            pl.BlockSpec((tm, half), lambda i, k: (i, b_block)),
            pl.BlockSpec((D_MODEL, D_MODEL), lambda i, k: (0, 0)),
            pl.BlockSpec((D_MODEL, TK_FF), lambda i, k: (0, k)),
            pl.BlockSpec((TK_FF, D_MODEL), lambda i, k: (k, 0)),
        ],
        out_specs=out_specs,
        out_shape=out_shape,
        scratch_shapes=[pltpu.VMEM((tm, D_MODEL), F32), pltpu.VMEM((tm, D_MODEL), BF16),
                        pltpu.VMEM((tm, D_MODEL), F32)],
        compiler_params=_cparams(("arbitrary", "arbitrary")),
        name="out_mlp",
    )(*xs, mods, g, mix_a, mix_b, w_o, w_ff1, w_ff2)


def _rope_tables(head_dim, lane_off, group):
    nf = head_dim // 4
    n_rows = DEC_SEQ // GRID_W
    rows = jnp.repeat(jnp.arange(n_rows, dtype=F32), GRID_W)
    cols = jnp.tile(jnp.arange(GRID_W, dtype=F32), n_rows)
    inv = ROPE_BASE ** (-jnp.arange(nf, dtype=F32) / nf)
    ang = jnp.stack([rows[:, None] * inv, cols[:, None] * inv], axis=1)
    cos = jnp.broadcast_to(jnp.cos(ang)[:, :, None, :], (DEC_SEQ, 2, 2, nf)).reshape(DEC_SEQ, head_dim)
    sin = jnp.sin(ang)
    sin = jnp.stack([-sin, sin], axis=2).reshape(DEC_SEQ, head_dim)
    cos_g = jnp.ones((DEC_SEQ, group), F32).at[:, lane_off:lane_off + head_dim].set(cos)
    sin_g = jnp.zeros((DEC_SEQ, group), F32).at[:, lane_off:lane_off + head_dim].set(sin)
    reps = LANES // group
    cos_s = jnp.tile(jnp.tile(cos_g, (1, reps)), (DEC_BATCH, 1))
    sin_s = jnp.tile(jnp.tile(sin_g, (1, reps)), (DEC_BATCH, 1))
    cos_t = jnp.concatenate([jnp.ones((N_PROMPT, LANES), F32), cos_s], axis=0)
    sin_t = jnp.concatenate([jnp.zeros((N_PROMPT, LANES), F32), sin_s], axis=0)
    return cos_t, sin_t


def _pair_heads(a, axis):
    return jnp.swapaxes(a, axis, axis + 1)


def _state_to_pairs(s):
    st = jnp.swapaxes(s, -1, -2).reshape(s.shape[0], H_A // 2, 2, DV_A, DK_A)
    z = jnp.zeros_like(st[:, :, 0])
    top = jnp.concatenate([st[:, :, 0], z], axis=-1)
    bot = jnp.concatenate([z, st[:, :, 1]], axis=-1)
    return jnp.concatenate([top, bot], axis=-2)


def _pairs_to_state(sp):
    h0 = sp[:, :, :DV_A, :DK_A]
    h1 = sp[:, :, DV_A:, DK_A:]
    st = jnp.stack([h0, h1], axis=2).reshape(sp.shape[0], H_A, DV_A, DK_A)
    return jnp.swapaxes(st, -1, -2)


def kernel(x_prompt, x_sample, state_gla_fwd, state_gla_bwd, cache_swa_k, cache_swa_v, cache_mla_ckv, cache_mla_kr, c, c_ctx, w_mod, b_mod, g_norm, w_ff1, w_ff2, w_in_ab, w_gk_f, b_gk_f, w_gk_b, b_gk_b, g_gla, swa_sink, w_out_ab, w_mla_down, g_mla_q, g_mla_kv, w_mla_uq, w_mla_ukv, w_mla_o):
    xs = (x_prompt.reshape(N_PROMPT, D_MODEL), x_sample.reshape(N_SAMPLE, D_MODEL))
    cvecs =jnp.concatenate([c_ctx[None, :], c, jnp.zeros((N_MOD_ROWS - 1 - DEC_BATCH, D_MODEL), F32)], axis=0)
    mods = _modulation(cvecs, w_mod, b_mod).reshape(DEPTH, N_MOD_ROWS, 6, D_MODEL)

    cos_b, sin_b = _rope_tables(D_B, 0, D_B)
    cos_c, sin_c = _rope_tables(ROPE_C, KR_OFF, HEAD_PAD_C)
    p_blk = N_PROMPT // DEC_SEQ

    st_f, st_b, sk, sv, ckv_out, ckr_out = [], [], [], [], [], []
    for l in range(DEPTH):
        i = l // 2
        g = g_norm[l]
        w1 = w_ff1[l]
        w2 = w_ff2[l]
        last = l == DEPTH - 1
        if l % 2 == 0:
            w = w_in_ab[i]
            o_lo = 2 * QA_W + 2 * VA_W
            o_qb = o_lo + 2 * GK_RANK
            wq_b = _pair_heads(w[:, o_qb:o_qb + QB_W].reshape(D_MODEL, KV_B, G_B, D_B), 1).reshape(D_MODEL, QB_W)
            w_in =jnp.concatenate([w[:, :o_lo], wq_b, w[:, o_qb + QB_W:], w[:, o_lo:o_qb]], axis=1).astype(BF16)
            zgk = jnp.zeros((GK_RANK, QA_W), F32)
            w_gk = jnp.concatenate([jnp.concatenate([w_gk_f[i], zgk], axis=1),
                                    jnp.concatenate([zgk, w_gk_b[i]], axis=1)], axis=0).astype(BF16)
            b_gk = jnp.concatenate([b_gk_f[i], b_gk_b[i]])[None, :]
            sink = _pair_heads(swa_sink[i].reshape(KV_B, G_B), 0).reshape(H_B)
            wo = w_out_ab[i]
            wo_swa = _pair_heads(wo[VA_W:].reshape(KV_B, G_B, D_B, D_MODEL), 0).reshape(QB_W, D_MODEL)
            w_o = jnp.concatenate([wo[:VA_W], wo_swa], axis=0).astype(BF16)

            qa, ka, va, ga, ld, qb, kb, vb = _in_even(xs, mods[l], g, w_in, w_gk, b_gk, cos_b, sin_b)
            gg = g_gla[i][None, :]
            o_gla, st_p = _gla(SEQ, BATCH, 0, qa, ka, va, ga, ld, gg)
            s0 = jnp.stack([_state_to_pairs(state_gla_fwd[:, i]), _state_to_pairs(state_gla_bwd[:, i])], axis=1)
            o_gla, _ = _gla(DEC_SEQ, DEC_BATCH, p_blk, qa, ka, va, ga, ld, gg, s0, o_gla)
            o_swa = _swa_prompt(sink, qb, kb, vb)
            kc = cache_swa_k[:, i].reshape(DEC_BATCH, PAST_LEN, KB_W)
            vc = cache_swa_v[:, i].reshape(DEC_BATCH, PAST_LEN, KB_W)
            o_swa = _swa_sample(sink, qb, kb, vb, kc, vc, o_swa)
            xs = _out_layer(xs, mods[l], g, o_gla, o_swa, 0, w_o, w1, w2, last)

            st_f.append(_pairs_to_state(st_p[:, 0]))
            st_b.append(_pairs_to_state(st_p[:, 1]))
            sk.append(kb[:N_PROMPT].reshape(BATCH, SEQ, KV_B, D_B))
            sv.append(vb[:N_PROMPT].reshape(BATCH, SEQ, KV_B, D_B))
        else:
            wd = w_mla_down[i]
            zc = jnp.zeros((D_MODEL, KR_OFF), F32)
            zt = jnp.zeros((D_MODEL, LANES - KR_OFF - ROPE_C), F32)
            w_down = jnp.concatenate([wd[:, :Q_LORA + KV_LORA], zc, wd[:, Q_LORA + KV_LORA:], zt], axis=1).astype(BF16)
            wuq = w_mla_uq[i].reshape(Q_LORA, H_C, NOPE_C + ROPE_C)
            w_uq = jnp.pad(wuq, ((0, 0), (0, 0), (0, HEAD_PAD_C - NOPE_C - ROPE_C))).reshape(Q_LORA, QC_W).astype(BF16)
            wukv = w_mla_ukv[i].reshape(KV_LORA, H_C, NOPE_C + V_C)
            w_uk = jnp.pad(wukv[:, :, :NOPE_C], ((0, 0), (0, 0), (0, HEAD_PAD_C - NOPE_C))).reshape(KV_LORA, QC_W).astype(BF16)
            w_uv = wukv[:, :, NOPE_C:].reshape(KV_LORA, VC_W).astype(BF16)
            w_o = w_mla_o[i].astype(BF16)

            q, k, v, ckv, kr = _in_odd(xs[0], mods[l], g, w_down, g_mla_q[i][None, :], g_mla_kv[i][None, :],
                                       w_uq, w_uk, w_uv, cos_c, sin_c)
            o_mla = _mla_prompt(q, k, v)
            kr_ctx = jnp.pad(cache_mla_kr[:, i], ((0, 0), (0, 0), (KR_OFF, LANES - KR_OFF - ROPE_C)))
            o_mla = _mla_sample(q, k, v, cache_mla_ckv[:, i], kr_ctx, w_uk, w_uv, o_mla)
            xs = _out_layer(xs, mods[l], g, o_mla, o_mla, 1, w_o, w1, w2, last)

            ckv_out.append(ckv[:N_PROMPT].reshape(BATCH, SEQ, KV_LORA))
            ckr_out.append(kr[:N_PROMPT, KR_OFF:KR_OFF + ROPE_C].reshape(BATCH, SEQ, ROPE_C))

    y_prompt = xs[0].reshape(BATCH, SEQ, D_MODEL)
    y_sample = xs[1].reshape(DEC_BATCH, DEC_SEQ, D_MODEL)
    return (y_prompt, y_sample, jnp.stack(st_f, axis=1), jnp.stack(st_b, axis=1), jnp.stack(sk, axis=1),
            jnp.stack(sv, axis=1), jnp.stack(ckv_out, axis=1), jnp.stack(ckr_out, axis=1))
```

```python
import functools

import jax
import jax.numpy as jnp
from jax import lax
from jax.experimental import pallas as pl
from jax.experimental.pallas import tpu as pltpu

F32 = jnp.float32
BF16 = jnp.bfloat16

D_MODEL = 1024
BATCH = 16
SEQ = 256
DEPTH = 4
DEC_BATCH = 2
DEC_SEQ = 1024
PAST_LEN = 512
GRID_W = 64
D_FF = 4 * D_MODEL
EPS = 1e-6
ROPE_BASE = 10000.0
H_A = 4
DK_A = 64
DV_A = 128
GK_RANK = 16
GATE_NORM = 16.0
GLA_CHUNK = 64
H_B = 8
KV_B = 2
G_B = H_B // KV_B
D_B = 64
WINDOW = 128
H_C = 16
NOPE_C = 64
ROPE_C = 32
V_C = 64
Q_LORA = 384
KV_LORA = 256

N_PROMPT = BATCH * SEQ
N_SAMPLE = DEC_BATCH * DEC_SEQ
N_TOK = N_PROMPT + N_SAMPLE
N_MOD_ROWS = 8
QA_W = H_A * DK_A
VA_W = H_A * DV_A
QB_W = H_B * D_B
KB_W = KV_B * D_B
LANES = 128
HEAD_PAD_C = 128
QC_W = H_C * HEAD_PAD_C
VC_W = H_C * V_C
DOWN_W = Q_LORA + KV_LORA + LANES
KR_OFF = NOPE_C

TM_IN = 512
TM_OUT = 1024
TK_FF = 512
OUT_PROLOGUE_ROWS = 256
TN_MOD = 1536
GLA_GROUP = 256
GLA_SCAN_UNROLL = 4
SWA_QB = 128
SWA_WIN = 3 * SWA_QB
MLA_QB = 256
VMEM_LIMIT = 56 * 1024 * 1024
NEG_BIG = -1e30


def _cparams(sem):
    return pltpu.CompilerParams(dimension_semantics=sem, vmem_limit_bytes=VMEM_LIMIT)


def _dot(a, b):
    return jnp.dot(a, b, preferred_element_type=F32)


def _dot_nt(a, b):
    return lax.dot_general(a, b, (((1,), (1,)), ((), ())), preferred_element_type=F32)


def _dot_tn(a, b):
    return lax.dot_general(a, b, (((0,), (0,)), ((), ())), preferred_element_type=F32)


def _rms(x, g):
    return x * lax.rsqrt(jnp.mean(x * x, axis=-1, keepdims=True) + EPS) * g


def _silu(x):
    return x / (1.0 + jnp.exp(-x))


def _rope(x, cos, sin, half):
    lane = lax.broadcasted_iota(jnp.int32, x.shape, 1)
    first = (lane % (2 * half)) < half
    partner = jnp.where(first, pltpu.roll(x, LANES - half, 1), pltpu.roll(x, half, 1))
    return x * cos + partner * sin


def _split3(x):
    x1 = x.astype(BF16)
    r1 = x - x1.astype(F32)
    x2 = r1.astype(BF16)
    x3 = (r1 - x2.astype(F32)).astype(BF16)
    return x1, x2, x3


def _dot3(t, parts):
    return _dot(t, parts[2]) + _dot(t, parts[1]) + _dot(t, parts[0])


def _ones_where(cond):
    return jnp.where(cond, 1.0, 0.0).astype(BF16)


def _mod_row(tile, tm):
    n_prompt_tiles = N_PROMPT // tm
    tiles_per_seq = DEC_SEQ // tm
    return jnp.where(tile < n_prompt_tiles, 0, 1 + (tile - n_prompt_tiles) // tiles_per_seq)


def _mod_kernel(c_ref, w_ref, b_ref, o_ref):
    s = _silu(c_ref[...])
    o_ref[...] = _dot(s.astype(BF16), w_ref[...].astype(BF16)) + b_ref[...]


def _modulation(cvecs, w_mod, b_mod):
    return pl.pallas_call(
        _mod_kernel,
        grid=(DEPTH, 6 * D_MODEL // TN_MOD),
        in_specs=[
            pl.BlockSpec((N_MOD_ROWS, D_MODEL), lambda l, j: (0, 0)),
            pl.BlockSpec((None, D_MODEL, TN_MOD), lambda l, j: (l, 0, j)),
            pl.BlockSpec((None, 1, TN_MOD), lambda l, j: (l, 0, j)),
        ],
        out_specs=pl.BlockSpec((None, N_MOD_ROWS, TN_MOD), lambda l, j: (l, 0, j)),
        out_shape=jax.ShapeDtypeStruct((DEPTH, N_MOD_ROWS, 6 * D_MODEL), F32),
        compiler_params=_cparams(("parallel", "parallel")),
        name="adaln_mod",
    )(cvecs, w_mod, b_mod.reshape(DEPTH, 1, 6 * D_MODEL))


_C_QA, _C_KA, _C_VA, _C_GA = 0, QA_W, 2 * QA_W, 2 * QA_W + VA_W
_C_QB = _C_GA + VA_W
_C_KB = _C_QB + QB_W
_C_VB = _C_KB + KB_W
_C_LO = _C_VB + KB_W
AB_IN = _C_LO + 2 * GK_RANK


def _load_x(n_x, refs, tm, rows=slice(None)):
    if n_x == 1:
        return refs[0][rows, :]
    return jnp.where(pl.program_id(0) < N_PROMPT // tm, refs[0][rows, :], refs[1][rows, :])


def _x_specs(n_x, tm):
    n_prompt_tiles = N_PROMPT // tm
    if n_x == 1:
        return [pl.BlockSpec((tm, D_MODEL), lambda i, *_: (i, 0))]
    return [pl.BlockSpec((tm, D_MODEL), lambda i, *_: (jnp.minimum(i, n_prompt_tiles - 1), 0)),
            pl.BlockSpec((tm, D_MODEL), lambda i, *_: (jnp.maximum(i - n_prompt_tiles, 0), 0))]


def _in_even_kernel(n_x, *refs):
    (mod_ref, g_ref, w_ref, wgk_ref, bgk_ref, cos_ref, sin_ref,
     qa_ref, ka_ref, va_ref, ga_ref, ld_ref, qb_ref, kb_ref, vb_ref) = refs[n_x:]
    x = _load_x(n_x, refs, TM_IN)
    h = _rms(x, g_ref[0:1, :]) * (1.0 + mod_ref[1:2, :]) + mod_ref[0:1, :]
    hb = h.astype(BF16)
    qa_ref[...] = _dot(hb, w_ref[:, _C_QA:_C_KA]) * (DK_A ** -0.5)
    ka_ref[...] = _dot(hb, w_ref[:, _C_KA:_C_VA])
    va_ref[...] = _dot(hb, w_ref[:, _C_VA:_C_GA]).astype(BF16)
    ga_ref[...] = _dot(hb, w_ref[:, _C_GA:_C_QB])
    cos = cos_ref[...]
    sin = sin_ref[...]
    qb = _dot(hb, w_ref[:, _C_QB:_C_KB])
    for j in range(QB_W // LANES):
        qj = qb[:, j * LANES:(j + 1) * LANES]
        qb_ref[:, j * LANES:(j + 1) * LANES] = (_rope(qj, cos, sin, D_B // 4) * (D_B ** -0.5)).astype(BF16)
    kvb = _dot(hb, w_ref[:, _C_KB:_C_LO])
    kb_ref[...] = _rope(kvb[:, :KB_W], cos, sin, D_B // 4)
    vb_ref[...] = kvb[:, KB_W:]
    lo = _dot(hb, w_ref[:, _C_LO:AB_IN]).astype(BF16)
    z = _dot(lo, wgk_ref[...]) + bgk_ref[...]
    ld_ref[...] = (jnp.minimum(z, 0.0) - jnp.log(1.0 + jnp.exp(-jnp.abs(z)))) * (1.0 / GATE_NORM)


def _layer_spec(shape, idx):
    return pl.BlockSpec((None,) + shape, lambda *_: (idx,) + (0,) * len(shape))


def _mod_spec(layer, tm):
    return pl.BlockSpec((None, None, 6, D_MODEL), lambda i, *_: (layer, _mod_row(i, tm), 0, 0))


def _in_even(xs, layer, mods, g_norm, w_in, w_gk, b_gk, cos, sin):
    tm = TM_IN
    i_ab = layer // 2
    row = lambda i: (i, 0)
    widths = (QA_W, QA_W, VA_W, VA_W, 2 * QA_W, QB_W, KB_W, KB_W)
    dtypes = (F32, F32, BF16, F32, F32, BF16, F32, F32)
    return pl.pallas_call(
        functools.partial(_in_even_kernel, len(xs)),
        grid=(N_TOK // tm,),
        in_specs=_x_specs(len(xs), tm) + [
            _mod_spec(layer, tm),
            _layer_spec((4, D_MODEL), layer),
            _layer_spec((D_MODEL, AB_IN), i_ab),
            _layer_spec((2 * GK_RANK, 2 * QA_W), i_ab),
            _layer_spec((1, 2 * QA_W), i_ab),
            pl.BlockSpec((tm, LANES), row),
            pl.BlockSpec((tm, LANES), row),
        ],
        out_specs=[pl.BlockSpec((tm, w), row) for w in widths],
        out_shape=[jax.ShapeDtypeStruct((N_TOK, w), d) for w, d in zip(widths, dtypes)],
        compiler_params=_cparams(("parallel",)),
        name="in_even",
    )(*xs, mods, g_norm, w_in, w_gk, b_gk, cos, sin)


def _gla_kernel(seq_len, has_s0, *refs):
    if has_s0:
        (qa_ref, ka_ref, va_ref, ga_ref, ld_ref, gg_ref, s0_ref, _, o_ref, st_ref,
         qi_ref, ki_ref, qcat_ref, ks_ref, dec_ref, stcat_ref, acc_ref) = refs
    else:
        (qa_ref, ka_ref, va_ref, ga_ref, ld_ref, gg_ref, o_ref, st_ref,
         qi_ref, ki_ref, qcat_ref, ks_ref, dec_ref, stcat_ref, acc_ref) = refs
        s0_ref = None
    n_groups = seq_len // GLA_GROUP
    cpg = GLA_GROUP // GLA_CHUNK
    n_chunks = seq_len // GLA_CHUNK
    n_pairs = H_A // 2
    pair_k = 2 * DK_A
    pair_v = 2 * DV_A

    def chunk_masks():
        r_i = lax.broadcasted_iota(jnp.int32, (GLA_GROUP, GLA_GROUP), 0)
        c_i = lax.broadcasted_iota(jnp.int32, (GLA_GROUP, GLA_GROUP), 1)
        same = (r_i // GLA_CHUNK) == (c_i // GLA_CHUNK)
        return same, same & (c_i <= r_i), same & (c_i >= r_i)

    def scale_group(j, carry):
        same, mask_f, mask_b = chunk_masks()
        t_cum = (_ones_where(mask_f), _ones_where(mask_b))
        t_all = _ones_where(same)
        s_r = lax.broadcasted_iota(jnp.int32, (8, GLA_GROUP), 0)
        s_c = lax.broadcasted_iota(jnp.int32, (8, GLA_GROUP), 1)
        t_sel = _ones_where(s_r == s_c // GLA_CHUNK)
        rows = pl.ds(pl.multiple_of(j * GLA_GROUP, GLA_GROUP), GLA_GROUP)
        q = qa_ref[rows, :]
        k = ka_ref[rows, :]
        for d in range(2):
            parts = _split3(ld_ref[rows, d * QA_W:(d + 1) * QA_W])
            b = _dot3(t_cum[d], parts)
            tot = _dot3(t_all, parts)
            dec8 = jnp.exp(_dot3(t_sel, parts))
            ref = 0.5 * tot
            qi_ref[d, rows, :] = (q * jnp.exp(b - ref)).astype(BF16)
            ki_ref[d, rows, :] = (k * jnp.exp(ref - b)).astype(BF16)
            ks_ref[d, rows, :] = (k * jnp.exp(tot - b)).astype(BF16)
            q_inter = (q * jnp.exp(b)).astype(BF16)
            for p in range(n_pairs):
                qcat_ref[rows, p * 2 * pair_k + d * pair_k:p * 2 * pair_k + (d + 1) * pair_k] = (
                    q_inter[:, p * pair_k:(p + 1) * pair_k])
            for cc in range(cpg):
                dec_ref[d, j * cpg + cc] = jnp.broadcast_to(dec8[cc:cc + 1, :], (8, QA_W))
        return carry

    lax.fori_loop(0, n_groups, scale_group, 0)

    if has_s0:
        st_ref[...] = s0_ref[...]
    else:
        st_ref[...] = jnp.zeros(st_ref.shape, F32)

    def scan_chunk(c, carry):
        bd_r = lax.broadcasted_iota(jnp.int32, (pair_v, pair_k), 0)
        bd_c = lax.broadcasted_iota(jnp.int32, (pair_v, pair_k), 1)
        bd_mask = (bd_r // DV_A) == (bd_c // DK_A)
        for d in range(2):
            cd = c if d == 0 else n_chunks - 1 - c
            rows = pl.ds(pl.multiple_of(cd * GLA_CHUNK, GLA_CHUNK), GLA_CHUNK)
            for p in range(n_pairs):
                st = st_ref[d, p]
                stcat_ref[cd, p, :, d * pair_k:(d + 1) * pair_k] = st.astype(BF16)
                u = _dot_tn(va_ref[rows, p * pair_v:(p + 1) * pair_v], ks_ref[d, rows, p * pair_k:(p + 1) * pair_k])
                dec = dec_ref[d, cd][0:1, p * pair_k:(p + 1) * pair_k]
                st_ref[d, p] = st * dec + jnp.where(bd_mask, u, 0.0)
        return carry

    lax.fori_loop(0, n_chunks, scan_chunk, 0, unroll=GLA_SCAN_UNROLL)

    def out_group(j, carry):
        _, mask_f, mask_b = chunk_masks()
        lane = lax.broadcasted_iota(jnp.int32, (GLA_GROUP, pair_k), 1)
        rows = pl.ds(pl.multiple_of(j * GLA_GROUP, GLA_GROUP), GLA_GROUP)
        for p in range(n_pairs):
            for cc in range(cpg):
                c = j * cpg + cc
                crow = pl.ds(pl.multiple_of(c * GLA_CHUNK, GLA_CHUNK), GLA_CHUNK)
                acc_ref[crow, p * pair_v:(p + 1) * pair_v] = _dot_nt(
                    qcat_ref[crow, p * 2 * pair_k:(p + 1) * 2 * pair_k], stcat_ref[c, p])
        for h in range(H_A):
            p, hh = divmod(h, 2)
            head_lanes = _ones_where((lane // DK_A) == hh)
            att = None
            for d, mask in ((0, mask_f), (1, mask_b)):
                qm = qi_ref[d, rows, p * pair_k:(p + 1) * pair_k] * head_lanes
                a = jnp.where(mask, _dot_nt(qm, ki_ref[d, rows, p * pair_k:(p + 1) * pair_k]), 0.0)
                att = a if att is None else att + a
            vh = va_ref[rows, h * DV_A:(h + 1) * DV_A]
            o = acc_ref[rows, h * DV_A:(h + 1) * DV_A] + _dot(att.astype(BF16), vh)
            gate = _silu(ga_ref[rows, h * DV_A:(h + 1) * DV_A])
            o_ref[rows, h * DV_A:(h + 1) * DV_A] = (_rms(o, gg_ref[...]) * gate).astype(BF16)
        return carry

    lax.fori_loop(0, n_groups, out_group, 0)


def _gla(seq_len, n_seq, row_block0, qa, ka, va, ga, ld, g_gla, s0=None, prev_out=None):
    has_s0 = s0 is not None
    n_chunks = seq_len // GLA_CHUNK
    n_pairs = H_A // 2
    rows = lambda b: (row_block0 + b, 0)
    st_spec = pl.BlockSpec((None, 2, n_pairs, 2 * DV_A, 2 * DK_A), lambda b: (b, 0, 0, 0, 0))
    in_specs = [
        pl.BlockSpec((seq_len, QA_W), rows),
        pl.BlockSpec((seq_len, QA_W), rows),
        pl.BlockSpec((seq_len, VA_W), rows),
        pl.BlockSpec((seq_len, VA_W), rows),
        pl.BlockSpec((seq_len, 2 * QA_W), rows),
        pl.BlockSpec((1, DV_A), lambda b: (0, 0)),
    ]
    args = [qa, ka, va, ga, ld, g_gla]
    aliases = {}
    if has_s0:
        in_specs += [st_spec, pl.BlockSpec(memory_space=pl.ANY)]
        args += [s0, prev_out]
        aliases = {len(args) - 1: 0}
    return pl.pallas_call(
        functools.partial(_gla_kernel, seq_len, has_s0),
        grid=(n_seq,),
        in_specs=in_specs,
        out_specs=[pl.BlockSpec((seq_len, VA_W), rows), st_spec],
        input_output_aliases=aliases,
        out_shape=[jax.ShapeDtypeStruct((N_TOK, VA_W), BF16),
                   jax.ShapeDtypeStruct((n_seq, 2, n_pairs, 2 * DV_A, 2 * DK_A), F32)],
        scratch_shapes=[
            pltpu.VMEM((2, seq_len, QA_W), BF16),
            pltpu.VMEM((2, seq_len, QA_W), BF16),
            pltpu.VMEM((seq_len, 2 * QA_W), BF16),
            pltpu.VMEM((2, seq_len, QA_W), BF16),
            pltpu.VMEM((2, n_chunks, 8, QA_W), F32),
            pltpu.VMEM((n_chunks, n_pairs, 2 * DV_A, 4 * DK_A), BF16),
            pltpu.VMEM((seq_len, VA_W), F32),
        ],
        compiler_params=_cparams(("parallel",)),
        name="gla_s" if has_s0 else "gla_p",
    )(*args)


def _swa_head_softmax(pieces, sink):
    m = sink
    for s, _ in pieces:
        m = jnp.maximum(m, jnp.max(s, axis=-1, keepdims=True))
    den = jnp.exp(sink - m)
    acc = None
    for s, v in pieces:
        e = jnp.exp(s - m)
        den = den + jnp.sum(e, axis=-1, keepdims=True)
        pv = _dot(e.astype(BF16), v)
        acc = pv if acc is None else acc + pv
    return acc / den


def _swa_prompt_kernel(sink_ref, q_ref, k_ref, v_ref, o_ref):
    k = k_ref[...].astype(BF16)
    v = v_ref[...].astype(BF16)
    lane = lax.broadcasted_iota(jnp.int32, (SEQ, LANES), 1)
    for p in range(H_B // 2):
        qp = q_ref[:, p * LANES:(p + 1) * LANES]
        outs = []
        for g in range(KV_B):
            qm = qp * _ones_where((lane // D_B) == g)
            s = _dot_nt(qm, k)
            outs.append(_swa_head_softmax([(s, v)], sink_ref[2 * p + g]))
        o_ref[:, p * LANES:(p + 1) * LANES] = jnp.where(lane < D_B, outs[0], outs[1]).astype(BF16)


def _swa_prompt(sink, qb, kb, vb):
    seq = lambda b: (b, 0)
    return pl.pallas_call(
        _swa_prompt_kernel,
        grid=(BATCH,),
        in_specs=[pl.BlockSpec(memory_space=pltpu.SMEM), pl.BlockSpec((SEQ, QB_W), seq),
                  pl.BlockSpec((SEQ, KB_W), seq), pl.BlockSpec((SEQ, KB_W), seq)],
        out_specs=pl.BlockSpec((SEQ, QB_W), seq),
        out_shape=jax.ShapeDtypeStruct((N_TOK, QB_W), BF16),
        compiler_params=_cparams(("parallel",)),
        name="swa_p",
    )(sink, qb, kb, vb)


def _swa_sample_kernel(sink_ref, q_ref, k_ref, v_ref, kc_ref, vc_ref, _, o_ref):
    n = pl.program_id(1)
    start = pl.multiple_of(jnp.clip((n - 1) * SWA_QB, 0, DEC_SEQ - SWA_WIN), SWA_QB)
    kl = k_ref[pl.ds(start, SWA_WIN), :].astype(BF16)
    vl = v_ref[pl.ds(start, SWA_WIN), :].astype(BF16)
    kc = kc_ref[...].astype(BF16)
    vc = vc_ref[...].astype(BF16)
    qi = n * SWA_QB + lax.broadcasted_iota(jnp.int32, (SWA_QB, SWA_WIN), 0)
    ki = start + lax.broadcasted_iota(jnp.int32, (SWA_QB, SWA_WIN), 1)
    band = jnp.abs(qi - ki) <= WINDOW
    lane = lax.broadcasted_iota(jnp.int32, (SWA_QB, LANES), 1)
    for p in range(H_B // 2):
        qp = q_ref[:, p * LANES:(p + 1) * LANES]
        outs = []
        for g in range(KV_B):
            qm = qp * _ones_where((lane // D_B) == g)
            s_ctx = _dot_nt(qm, kc)
            s_loc = jnp.where(band, _dot_nt(qm, kl), NEG_BIG)
            outs.append(_swa_head_softmax([(s_ctx, vc), (s_loc, vl)], sink_ref[2 * p + g]))
        o_ref[:, p * LANES:(p + 1) * LANES] = jnp.where(lane < D_B, outs[0], outs[1]).astype(BF16)


def _swa_sample(sink, qb, kb, vb, kc, vc, i_ab, prev_out):
    nqb = DEC_SEQ // SWA_QB
    q0 = N_PROMPT // SWA_QB
    s0 = N_PROMPT // DEC_SEQ
    q_rows = lambda b, n: (q0 + b * nqb + n, 0)
    return pl.pallas_call(
        _swa_sample_kernel,
        grid=(DEC_BATCH, nqb),
        in_specs=[
            pl.BlockSpec(memory_space=pltpu.SMEM),
            pl.BlockSpec((SWA_QB, QB_W), q_rows),
            pl.BlockSpec((DEC_SEQ, KB_W), lambda b, n: (s0 + b, 0)),
            pl.BlockSpec((DEC_SEQ, KB_W), lambda b, n: (s0 + b, 0)),
            pl.BlockSpec((None, None, PAST_LEN, KB_W), lambda b, n: (b, i_ab, 0, 0)),
            pl.BlockSpec((None, None, PAST_LEN, KB_W), lambda b, n: (b, i_ab, 0, 0)),
            pl.BlockSpec(memory_space=pl.ANY),
        ],
        out_specs=pl.BlockSpec((SWA_QB, QB_W), q_rows),
        out_shape=jax.ShapeDtypeStruct((N_TOK, QB_W), BF16),
        input_output_aliases={6: 0},
        compiler_params=_cparams(("parallel", "parallel")),
        name="swa_s",
    )(sink, qb, kb, vb, kc, vc, prev_out)


HEADS_PER_DOT_C = 4


def _in_odd_kernel(x_ref, mod_ref, g_ref, wd_ref, gq_ref, gkv_ref, wuq_ref, wuk_ref, wuv_ref, cos_ref, sin_ref,
                   q_ref, k_ref, v_ref, ckv_ref, kr_ref):
    h = _rms(x_ref[...], g_ref[0:1, :]) * (1.0 + mod_ref[1:2, :]) + mod_ref[0:1, :]
    hb = h.astype(BF16)
    cos = cos_ref[...]
    sin = sin_ref[...]
    c_q = _dot(hb, wd_ref[:, 0:Q_LORA])
    c_kv = _rms(_dot(hb, wd_ref[:, Q_LORA:Q_LORA + KV_LORA]), gkv_ref[...])
    kr = _rope(_dot(hb, wd_ref[:, Q_LORA + KV_LORA:DOWN_W]), cos, sin, ROPE_C // 4)
    ckv_ref[...] = c_kv
    kr_ref[...] = kr
    cqb = _rms(c_q, gq_ref[...]).astype(BF16)
    ckvb = c_kv.astype(BF16)
    scale = (NOPE_C + ROPE_C) ** -0.5
    group_w = HEADS_PER_DOT_C * HEAD_PAD_C
    for grp in range(H_C // HEADS_PER_DOT_C):
        gsl = slice(grp * group_w, (grp + 1) * group_w)
        qg = _dot(cqb, wuq_ref[:, gsl])
        kg = _dot(ckvb, wuk_ref[:, gsl])
        for j in range(HEADS_PER_DOT_C):
            sl = slice(j * HEAD_PAD_C, (j + 1) * HEAD_PAD_C)
            osl = slice(grp * group_w + j * HEAD_PAD_C, grp * group_w + (j + 1) * HEAD_PAD_C)
            q_ref[:, osl] = (_rope(qg[:, sl], cos, sin, ROPE_C // 4) * scale).astype(BF16)
            k_ref[:, osl] = (kg[:, sl] + kr).astype(BF16)
    v_ref[...] = _dot(ckvb, wuv_ref[...]).astype(BF16)


def _in_odd(x, layer, mods, g_norm, w_down, g_q, g_kv, w_uq, w_uk, w_uv, cos, sin):
    tm = TM_IN
    i_c = layer // 2
    row = lambda i: (i, 0)
    widths = (QC_W, QC_W, VC_W, KV_LORA, LANES)
    dtypes = (BF16, BF16, BF16, F32, F32)
    return pl.pallas_call(
        _in_odd_kernel,
        grid=(N_TOK // tm,),
        in_specs=[
            pl.BlockSpec((tm, D_MODEL), row),
            _mod_spec(layer, tm),
            _layer_spec((4, D_MODEL), layer),
            _layer_spec((D_MODEL, DOWN_W), i_c),
            _layer_spec((1, Q_LORA), i_c),
            _layer_spec((1, KV_LORA), i_c),
            _layer_spec((Q_LORA, QC_W), i_c),
            _layer_spec((KV_LORA, QC_W), i_c),
            _layer_spec((KV_LORA, VC_W), i_c),
            pl.BlockSpec((tm, LANES), row),
            pl.BlockSpec((tm, LANES), row),
        ],
        out_specs=[pl.BlockSpec((tm, w), row) for w in widths],
        out_shape=[jax.ShapeDtypeStruct((N_TOK, w), d) for w, d in zip(widths, dtypes)],
        compiler_params=_cparams(("parallel",)),
        name="in_odd",
    )(x, mods, g_norm, w_down, g_q, g_kv, w_uq, w_uk, w_uv, cos, sin)


def _mla_heads(q_ref, kv_pieces, o_ref, n_rows):
    lane = lax.broadcasted_iota(jnp.int32, (n_rows, LANES), 1)
    for pair in range(H_C // 2):
        outs = []
        for hh in range(2):
            hd = 2 * pair + hh
            sl = slice(hd * HEAD_PAD_C, (hd + 1) * HEAD_PAD_C)
            q = q_ref[:, sl]
            scores = [_dot_nt(q, k_ref[:, sl]) for k_ref, _ in kv_pieces]
            m = None
            for s in scores:
                sm = jnp.max(s, axis=-1, keepdims=True)
                m = sm if m is None else jnp.maximum(m, sm)
            den = None
            acc = None
            for s, (_, v_ref) in zip(scores, kv_pieces):
                e = jnp.exp(s - m)
                es = jnp.sum(e, axis=-1, keepdims=True)
                den = es if den is None else den + es
                pv = _dot(e.astype(BF16), v_ref[:, pair * LANES:(pair + 1) * LANES])
                acc = pv if acc is None else acc + pv
            outs.append(acc / den)
        o_ref[:, pair * LANES:(pair + 1) * LANES] = jnp.where(lane < V_C, outs[0], outs[1]).astype(BF16)


def _mla_prompt_kernel(q_ref, k_ref, v_ref, o_ref):
    _mla_heads(q_ref, [(k_ref, v_ref)], o_ref, SEQ)


def _mla_prompt(q, k, v):
    seq = lambda b: (b, 0)
    return pl.pallas_call(
        _mla_prompt_kernel,
        grid=(BATCH,),
        in_specs=[pl.BlockSpec((SEQ, QC_W), seq), pl.BlockSpec((SEQ, QC_W), seq), pl.BlockSpec((SEQ, VC_W), seq)],
        out_specs=pl.BlockSpec((SEQ, VC_W), seq),
        out_shape=jax.ShapeDtypeStruct((N_TOK, VC_W), BF16),
        compiler_params=_cparams(("parallel",)),
        name="mla_p",
    )(q, k, v)


def _mla_sample_kernel(q_ref, k_ref, v_ref, ckv_ref, kr_ref, wuk_ref, wuv_ref, _, o_ref, kc_ref, vc_ref):
    @pl.when(pl.program_id(1) == 0)
    def _():
        cb = ckv_ref[...].astype(BF16)
        kr = kr_ref[...]
        group_w = HEADS_PER_DOT_C * HEAD_PAD_C
        for grp in range(H_C // HEADS_PER_DOT_C):
            kg = _dot(cb, wuk_ref[:, grp * group_w:(grp + 1) * group_w])
            for j in range(HEADS_PER_DOT_C):
                osl = slice(grp * group_w + j * HEAD_PAD_C, grp * group_w + (j + 1) * HEAD_PAD_C)
                kc_ref[:, osl] = (kg[:, j * HEAD_PAD_C:(j + 1) * HEAD_PAD_C] + kr).astype(BF16)
        vc_ref[...] = _dot(cb, wuv_ref[...]).astype(BF16)

    _mla_heads(q_ref, [(kc_ref, vc_ref), (k_ref, v_ref)], o_ref, MLA_QB)


def _mla_sample(q, k, v, ckv_ctx, kr_ctx, w_uk, w_uv, i_c, prev_out):
    nqb = DEC_SEQ // MLA_QB
    q0 = N_PROMPT // MLA_QB
    s0 = N_PROMPT // DEC_SEQ
    q_rows = lambda b, n: (q0 + b * nqb + n, 0)
    return pl.pallas_call(
        _mla_sample_kernel,
        grid=(DEC_BATCH, nqb),
        in_specs=[
            pl.BlockSpec((MLA_QB, QC_W), q_rows),
            pl.BlockSpec((DEC_SEQ, QC_W), lambda b, n: (s0 + b, 0)),
            pl.BlockSpec((DEC_SEQ, VC_W), lambda b, n: (s0 + b, 0)),
            pl.BlockSpec((None, None, PAST_LEN, KV_LORA), lambda b, n: (b, i_c, 0, 0)),
            pl.BlockSpec((None, None, PAST_LEN, LANES), lambda b, n: (b, i_c, 0, 0)),
            _layer_spec((KV_LORA, QC_W), i_c),
            _layer_spec((KV_LORA, VC_W), i_c),
            pl.BlockSpec(memory_space=pl.ANY),
        ],
        out_specs=pl.BlockSpec((MLA_QB, VC_W), q_rows),
        out_shape=jax.ShapeDtypeStruct((N_TOK, VC_W), BF16),
        scratch_shapes=[pltpu.VMEM((PAST_LEN, QC_W), BF16), pltpu.VMEM((PAST_LEN, VC_W), BF16)],
        input_output_aliases={7: 0},
        compiler_params=_cparams(("parallel", "arbitrary")),
        name="mla_s",
    )(q, k, v, ckv_ctx, kr_ctx, w_uk, w_uv, prev_out)


def _out_kernel(n_x, n_out, *refs):
    mod_ref, g_ref, a_ref, b_ref, wo_ref, w1_ref, w2_ref = refs[n_x:n_x + 7]
    out_refs = refs[n_x + 7:n_x + 7 + n_out]
    x1_ref, h2_ref, acc_ref = refs[n_x + 7 + n_out:]
    kk = pl.program_id(1)
    half = wo_ref.shape[0] // 2

    @pl.when(kk == 0)
    def _():
        for r in range(TM_OUT // OUT_PROLOGUE_ROWS):
            rows = slice(r * OUT_PROLOGUE_ROWS, (r + 1) * OUT_PROLOGUE_ROWS)
            mix = _dot(a_ref[rows, :], wo_ref[0:half, :]) + _dot(b_ref[rows, :], wo_ref[half:, :])
            x1 = _load_x(n_x, refs, TM_OUT, rows) + mod_ref[2:3, :] * _rms(mix, g_ref[1:2, :])
            x1_ref[rows, :] = x1
            h2_ref[rows, :] = (_rms(x1, g_ref[2:3, :]) * (1.0 + mod_ref[4:5, :]) + mod_ref[3:4, :]).astype(BF16)
        acc_ref[...] = jnp.zeros(acc_ref.shape, F32)

    hid = jnp.maximum(_dot(h2_ref[...], w1_ref[...].astype(BF16)), 0.0)
    acc_ref[...] += _dot((hid * hid).astype(BF16), w2_ref[...].astype(BF16))

    @pl.when(kk == pl.num_programs(1) - 1)
    def _():
        y = x1_ref[...] + mod_ref[5:6, :] * _rms(acc_ref[...], g_ref[3:4, :])
        if n_out == 1:
            out_refs[0][...] = y
        else:
            is_prompt = pl.program_id(0) < N_PROMPT // TM_OUT

            @pl.when(is_prompt)
            def _():
                out_refs[0][...] = y

            @pl.when(jnp.logical_not(is_prompt))
            def _():
                out_refs[1][...] = y


def _out_layer(xs, layer, mods, g_norm, mix_a, mix_b, b_block, w_o, i_o, w_ff1, w_ff2, split_out):
    tm = TM_OUT
    half = D_MODEL // 2
    n_prompt_tiles = N_PROMPT // tm
    if split_out:
        out_specs = [pl.BlockSpec((tm, D_MODEL), lambda i, k: (jnp.minimum(i, n_prompt_tiles - 1), 0)),
                     pl.BlockSpec((tm, D_MODEL), lambda i, k: (jnp.maximum(i - n_prompt_tiles, 0), 0))]
        out_shape = [jax.ShapeDtypeStruct((N_PROMPT, D_MODEL), F32), jax.ShapeDtypeStruct((N_SAMPLE, D_MODEL), F32)]
    else:
        out_specs = [pl.BlockSpec((tm, D_MODEL), lambda i, k: (i, 0))]
        out_shape = [jax.ShapeDtypeStruct((N_TOK, D_MODEL), F32)]
    return pl.pallas_call(
        functools.partial(_out_kernel, len(xs), len(out_specs)),
        grid=(N_TOK // tm, D_FF // TK_FF),
        in_specs=_x_specs(len(xs), tm) + [
            _mod_spec(layer, tm),
            _layer_spec((4, D_MODEL), layer),
            pl.BlockSpec((tm, half), lambda i, k: (i, 0)),
            pl.BlockSpec((tm, half), lambda i, k: (i, b_block)),
            _layer_spec((D_MODEL, D_MODEL), i_o),
            pl.BlockSpec((None, D_MODEL, TK_FF), lambda i, k: (layer, 0, k)),
            pl.BlockSpec((None, TK_FF, D_MODEL), lambda i, k: (layer, k, 0)),
        ],
        out_specs=out_specs,
        out_shape=out_shape,
        scratch_shapes=[pltpu.VMEM((tm, D_MODEL), F32), pltpu.VMEM((tm, D_MODEL), BF16),
                        pltpu.VMEM((tm, D_MODEL), F32)],
        compiler_params=_cparams(("arbitrary", "arbitrary")),
        name="out_mlp",
    )(*xs, mods, g_norm, mix_a, mix_b, w_o, w_ff1, w_ff2)


def _rope_tables(head_dim, lane_off, group):
    nf = head_dim // 4
    n_rows = DEC_SEQ // GRID_W
    rows = jnp.repeat(jnp.arange(n_rows, dtype=F32), GRID_W)
    cols = jnp.tile(jnp.arange(GRID_W, dtype=F32), n_rows)
    inv = ROPE_BASE ** (-jnp.arange(nf, dtype=F32) / nf)
    ang = jnp.stack([rows[:, None] * inv, cols[:, None] * inv], axis=1)
    cos = jnp.broadcast_to(jnp.cos(ang)[:, :, None, :], (DEC_SEQ, 2, 2, nf)).reshape(DEC_SEQ, head_dim)
    sin = jnp.sin(ang)
    sin = jnp.stack([-sin, sin], axis=2).reshape(DEC_SEQ, head_dim)
    cos_g = jnp.ones((DEC_SEQ, group), F32).at[:, lane_off:lane_off + head_dim].set(cos)
    sin_g = jnp.zeros((DEC_SEQ, group), F32).at[:, lane_off:lane_off + head_dim].set(sin)
    reps = LANES // group
    cos_s = jnp.tile(jnp.tile(cos_g, (1, reps)), (DEC_BATCH, 1))
    sin_s = jnp.tile(jnp.tile(sin_g, (1, reps)), (DEC_BATCH, 1))
    cos_t = jnp.concatenate([jnp.ones((N_PROMPT, LANES), F32), cos_s], axis=0)
    sin_t = jnp.concatenate([jnp.zeros((N_PROMPT, LANES), F32), sin_s], axis=0)
    return cos_t, sin_t


def _pair_heads(a, axis):
    return jnp.swapaxes(a, axis, axis + 1)


def _state_to_pairs(s):
    st = jnp.swapaxes(s, -1, -2).reshape(s.shape[0], H_A // 2, 2, DV_A, DK_A)
    z = jnp.zeros_like(st[:, :, 0])
    top = jnp.concatenate([st[:, :, 0], z], axis=-1)
    bot = jnp.concatenate([z, st[:, :, 1]], axis=-1)
    return jnp.concatenate([top, bot], axis=-2)


def _pairs_to_state(sp):
    h0 = sp[:, :, :DV_A, :DK_A]
    h1 = sp[:, :, DV_A:, DK_A:]
    st = jnp.stack([h0, h1], axis=2).reshape(sp.shape[0], H_A, DV_A, DK_A)
    return jnp.swapaxes(st, -1, -2)


def kernel(x_prompt, x_sample, state_gla_fwd, state_gla_bwd, cache_swa_k, cache_swa_v, cache_mla_ckv, cache_mla_kr, c, c_ctx, w_mod, b_mod, g_norm, w_ff1, w_ff2, w_in_ab, w_gk_f, b_gk_f, w_gk_b, b_gk_b, g_gla, swa_sink, w_out_ab, w_mla_down, g_mla_q, g_mla_kv, w_mla_uq, w_mla_ukv, w_mla_o):
    xs = (x_prompt.reshape(N_PROMPT, D_MODEL), x_sample.reshape(N_SAMPLE, D_MODEL))
    cvecs =jnp.concatenate([c_ctx[None, :], c, jnp.zeros((N_MOD_ROWS - 1 - DEC_BATCH, D_MODEL), F32)], axis=0)
    mods = _modulation(cvecs, w_mod, b_mod).reshape(DEPTH, N_MOD_ROWS, 6, D_MODEL)

    cos_b, sin_b = _rope_tables(D_B, 0, D_B)
    cos_c, sin_c = _rope_tables(ROPE_C, KR_OFF, HEAD_PAD_C)
    p_blk = N_PROMPT // DEC_SEQ

    n_ab = w_in_ab.shape[0]
    o_lo = 2 * QA_W + 2 * VA_W
    o_qb = o_lo + 2 * GK_RANK
    wq_b = _pair_heads(w_in_ab[:, :, o_qb:o_qb + QB_W].reshape(n_ab, D_MODEL, KV_B, G_B, D_B), 2)
    w_in = jnp.concatenate([w_in_ab[:, :, :o_lo], wq_b.reshape(n_ab, D_MODEL, QB_W), w_in_ab[:, :, o_qb + QB_W:],
                            w_in_ab[:, :, o_lo:o_qb]], axis=2).astype(BF16)
    zgk = jnp.zeros((n_ab, GK_RANK, QA_W), F32)
    w_gk = jnp.concatenate([jnp.concatenate([w_gk_f, zgk], axis=2),
                            jnp.concatenate([zgk, w_gk_b], axis=2)], axis=1).astype(BF16)
    b_gk = jnp.concatenate([b_gk_f, b_gk_b], axis=1)[:, None, :]
    sinks = _pair_heads(swa_sink.reshape(n_ab, KV_B, G_B), 1).reshape(n_ab, H_B)
    wo_swa = _pair_heads(w_out_ab[:, VA_W:].reshape(n_ab, KV_B, G_B, D_B, D_MODEL), 1).reshape(n_ab, QB_W, D_MODEL)
    w_o_ab = jnp.concatenate([w_out_ab[:, :VA_W], wo_swa], axis=1).astype(BF16)
    kc = cache_swa_k.reshape(DEC_BATCH, n_ab, PAST_LEN, KB_W)
    vc = cache_swa_v.reshape(DEC_BATCH, n_ab, PAST_LEN, KB_W)

    n_c = w_mla_down.shape[0]
    zc = jnp.zeros((n_c, D_MODEL, KR_OFF), F32)
    zt = jnp.zeros((n_c, D_MODEL, LANES - KR_OFF - ROPE_C), F32)
    w_down = jnp.concatenate([w_mla_down[:, :, :Q_LORA + KV_LORA], zc, w_mla_down[:, :, Q_LORA + KV_LORA:], zt],
                             axis=2).astype(BF16)
    wuq = w_mla_uq.reshape(n_c, Q_LORA, H_C, NOPE_C + ROPE_C)
    w_uq = jnp.pad(wuq, ((0, 0), (0, 0), (0, 0), (0, HEAD_PAD_C - NOPE_C - ROPE_C))).reshape(n_c, Q_LORA, QC_W).astype(BF16)
    wukv = w_mla_ukv.reshape(n_c, KV_LORA, H_C, NOPE_C + V_C)
    w_uk = jnp.pad(wukv[..., :NOPE_C], ((0, 0), (0, 0), (0, 0), (0, HEAD_PAD_C - NOPE_C))).reshape(n_c, KV_LORA, QC_W).astype(BF16)
    w_uv = wukv[..., NOPE_C:].reshape(n_c, KV_LORA, VC_W).astype(BF16)
    w_o_c = w_mla_o.astype(BF16)
    kr_ctx = jnp.pad(cache_mla_kr, ((0, 0), (0, 0), (0, 0), (KR_OFF, LANES - KR_OFF - ROPE_C)))
    g_q = g_mla_q[:, None, :]
    g_kv = g_mla_kv[:, None, :]

    st_f, st_b, sk, sv, ckv_out, ckr_out = [], [], [], [], [], []
    for l in range(DEPTH):
        i = l // 2
        last = l == DEPTH - 1
        if l % 2 == 0:
            qa, ka, va, ga, ld, qb, kb, vb = _in_even(xs, l, mods, g_norm, w_in, w_gk, b_gk, cos_b, sin_b)
            gg = g_gla[i][None, :]
            o_gla, st_p = _gla(SEQ, BATCH, 0, qa, ka, va, ga, ld, gg)
            s0 = jnp.stack([_state_to_pairs(state_gla_fwd[:, i]), _state_to_pairs(state_gla_bwd[:, i])], axis=1)
            o_gla, _ = _gla(DEC_SEQ, DEC_BATCH, p_blk, qa, ka, va, ga, ld, gg, s0, o_gla)
            o_swa = _swa_prompt(sinks[i], qb, kb, vb)
            o_swa = _swa_sample(sinks[i], qb, kb, vb, kc, vc, i, o_swa)
            xs = _out_layer(xs, l, mods, g_norm, o_gla, o_swa, 0, w_o_ab, i, w_ff1, w_ff2, last)

            st_f.append(_pairs_to_state(st_p[:, 0]))
            st_b.append(_pairs_to_state(st_p[:, 1]))
            sk.append(kb[:N_PROMPT].reshape(BATCH, SEQ, KV_B, D_B))
            sv.append(vb[:N_PROMPT].reshape(BATCH, SEQ, KV_B, D_B))
        else:
            q, k, v, ckv, kr = _in_odd(xs[0], l, mods, g_norm, w_down, g_q, g_kv, w_uq, w_uk, w_uv, cos_c, sin_c)
            o_mla = _mla_prompt(q, k, v)
            o_mla = _mla_sample(q, k, v, cache_mla_ckv, kr_ctx, w_uk, w_uv, i, o_mla)
            xs = _out_layer(xs, l, mods, g_norm, o_mla, o_mla, 1, w_o_c, i, w_ff1, w_ff2, last)

            ckv_out.append(ckv[:N_PROMPT].reshape(BATCH, SEQ, KV_LORA))
            ckr_out.append(kr[:N_PROMPT, KR_OFF:KR_OFF + ROPE_C].reshape(BATCH, SEQ, ROPE_C))

    y_prompt = xs[0].reshape(BATCH, SEQ, D_MODEL)
    y_sample = xs[1].reshape(DEC_BATCH, DEC_SEQ, D_MODEL)
    return (y_prompt, y_sample, jnp.stack(st_f, axis=1), jnp.stack(st_b, axis=1), jnp.stack(sk, axis=1),
            jnp.stack(sv, axis=1), jnp.stack(ckv_out, axis=1), jnp.stack(ckr_out, axis=1))
```

```python
import functools

import jax
import jax.numpy as jnp
from jax import lax
from jax.experimental import pallas as pl
from jax.experimental.pallas import tpu as pltpu

F32 = jnp.float32
BF16 = jnp.bfloat16

D_MODEL = 1024
BATCH = 16
SEQ = 256
DEPTH = 4
DEC_BATCH = 2
DEC_SEQ = 1024
PAST_LEN = 512
GRID_W = 64
D_FF = 4 * D_MODEL
EPS = 1e-6
ROPE_BASE = 10000.0
H_A = 4
DK_A = 64
DV_A = 128
GK_RANK = 16
GATE_NORM = 16.0
GLA_CHUNK = 64
H_B = 8
KV_B = 2
G_B = H_B // KV_B
D_B = 64
WINDOW = 128
H_C = 16
NOPE_C = 64
ROPE_C = 32
V_C = 64
Q_LORA = 384
KV_LORA = 256

N_PROMPT = BATCH * SEQ
N_SAMPLE = DEC_BATCH * DEC_SEQ
N_TOK = N_PROMPT + N_SAMPLE
N_MOD_ROWS = 8
QA_W = H_A * DK_A
VA_W = H_A * DV_A
QB_W = H_B * D_B
KB_W = KV_B * D_B
LANES = 128
HEAD_PAD_C = 128
QC_W = H_C * HEAD_PAD_C
VC_W = H_C * V_C
DOWN_RAW_W = Q_LORA + KV_LORA + ROPE_C
DOWN_W = Q_LORA + KV_LORA + LANES
KR_OFF = NOPE_C

TM_IN = 512
TM_OUT = 1024
TK_FF = 512
OUT_PROLOGUE_ROWS = 256
TN_MOD = 1536
PREP_ROWS = 256
GLA_GROUP = 256
GLA_SCAN_UNROLL = 4
GLA_SAFE_TOTAL = 160.0
SWA_QB = 128
SWA_WIN = 3 * SWA_QB
MLA_QB = 256
HEADS_PER_DOT_C = 4
VMEM_LIMIT = 56 * 1024 * 1024
NEG_BIG = -1e30


def _cparams(sem):
    return pltpu.CompilerParams(dimension_semantics=sem, vmem_limit_bytes=VMEM_LIMIT)


def _dot(a, b):
    return jnp.dot(a, b, preferred_element_type=F32)


def _dot_nt(a, b):
    return lax.dot_general(a, b, (((1,), (1,)), ((), ())), preferred_element_type=F32)


def _dot_tn(a, b):
    return lax.dot_general(a, b, (((0,), (0,)), ((), ())), preferred_element_type=F32)


def _rms(x, g):
    return x * lax.rsqrt(jnp.mean(x * x, axis=-1, keepdims=True) + EPS) * g


def _silu(x):
    return x / (1.0 + jnp.exp(-x))


def _rope(x, cos, sin, half):
    lane = lax.broadcasted_iota(jnp.int32, x.shape, 1)
    first = (lane % (2 * half)) < half
    partner = jnp.where(first, pltpu.roll(x, LANES - half, 1), pltpu.roll(x, half, 1))
    return x * cos + partner * sin


def _split3(x):
    x1 = x.astype(BF16)
    r1 = x - x1.astype(F32)
    x2 = r1.astype(BF16)
    x3 = (r1 - x2.astype(F32)).astype(BF16)
    return x1, x2, x3


def _dot3(t, parts):
    return _dot(t, parts[2]) + _dot(t, parts[1]) + _dot(t, parts[0])


def _ones_where(cond):
    return jnp.where(cond, 1.0, 0.0).astype(BF16)


def _mod_row(tile, tm):
    n_prompt_tiles = N_PROMPT // tm
    tiles_per_seq = DEC_SEQ // tm
    return jnp.where(tile < n_prompt_tiles, 0, 1 + (tile - n_prompt_tiles) // tiles_per_seq)


def _layer_spec(shape, idx):
    return pl.BlockSpec((None,) + shape, lambda *_: (idx,) + (0,) * len(shape))


def _mod_spec(layer, tm):
    return pl.BlockSpec((None, None, 6, D_MODEL), lambda i, *_: (layer, _mod_row(i, tm), 0, 0))


def _split_specs(tm, width, col=0):
    n_prompt_tiles = N_PROMPT // tm
    return [pl.BlockSpec((tm, width), lambda i, *_: (jnp.minimum(i, n_prompt_tiles - 1), col)),
            pl.BlockSpec((tm, width), lambda i, *_: (jnp.maximum(i - n_prompt_tiles, 0), col))]


def _split_shapes(width, dtype):
    return [jax.ShapeDtypeStruct((N_PROMPT, width), dtype), jax.ShapeDtypeStruct((N_SAMPLE, width), dtype)]


def _is_prompt_tile(tm):
    return pl.program_id(0) < N_PROMPT // tm


def _load_split(pair, tm, rows=slice(None)):
    return jnp.where(_is_prompt_tile(tm), pair[0][rows, :], pair[1][rows, :])


def _store_split(pair, tm, value):
    is_prompt = _is_prompt_tile(tm)

    @pl.when(is_prompt)
    def _():
        pair[0][...] = value

    @pl.when(jnp.logical_not(is_prompt))
    def _():
        pair[1][...] = value


_C_QA, _C_KA, _C_VA, _C_GA = 0, QA_W, 2 * QA_W, 2 * QA_W + VA_W
_C_QB = _C_GA + VA_W
_C_KB = _C_QB + QB_W
_C_VB = _C_KB + KB_W
_C_LO = _C_VB + KB_W
AB_IN = _C_LO + 2 * GK_RANK


def _prep_even_kernel(wt_ref, o_ref):
    raw_lo = _C_QB
    raw_qb = raw_lo + 2 * GK_RANK
    o_ref[:, 0:_C_QB] = wt_ref[0:raw_lo, :].T.astype(BF16)
    o_ref[:, _C_QB:_C_LO] = wt_ref[raw_qb:AB_IN, :].T.astype(BF16)
    o_ref[:, _C_LO:AB_IN] = wt_ref[raw_lo:raw_lo + LANES, :].T[:, 0:2 * GK_RANK].astype(BF16)


def _prep_even(w_in_t):
    n = w_in_t.shape[0]
    return pl.pallas_call(
        _prep_even_kernel,
        grid=(n, D_MODEL // PREP_ROWS),
        in_specs=[pl.BlockSpec((None, AB_IN, PREP_ROWS), lambda l, r: (l, 0, r))],
        out_specs=pl.BlockSpec((None, PREP_ROWS, AB_IN), lambda l, r: (l, r, 0)),
        out_shape=jax.ShapeDtypeStruct((n, D_MODEL, AB_IN), BF16),
        compiler_params=_cparams(("parallel", "parallel")),
        name="prep_even",
    )(w_in_t)


def _prep_odd_kernel(wdt_ref, wuq_ref, wukv_ref, od_ref, ouq_ref, ouk_ref, ouv_ref):
    n_ckv = Q_LORA + KV_LORA
    od_ref[:, 0:n_ckv] = wdt_ref[0:n_ckv, :].T.astype(BF16)
    tail = wdt_ref[DOWN_RAW_W - LANES:DOWN_RAW_W, :].T
    od_ref[:, n_ckv:DOWN_W] = jnp.concatenate(
        [jnp.zeros((D_MODEL, KR_OFF), F32), tail[:, LANES - ROPE_C:],
         jnp.zeros((D_MODEL, LANES - KR_OFF - ROPE_C), F32)], axis=1).astype(BF16)
    hd_q = NOPE_C + ROPE_C
    for h in range(H_C):
        ouq_ref[:, h * HEAD_PAD_C:(h + 1) * HEAD_PAD_C] = jnp.concatenate(
            [wuq_ref[:, h * hd_q:(h + 1) * hd_q], jnp.zeros((Q_LORA, HEAD_PAD_C - hd_q), F32)], axis=1).astype(BF16)
    wukv = wukv_ref[...]
    lane = lax.broadcasted_iota(jnp.int32, wukv.shape, 1)
    ouk_ref[...] = jnp.where(lane % HEAD_PAD_C < NOPE_C, wukv, 0.0).astype(BF16)
    for p in range(H_C // 2):
        ouv_ref[:, p * LANES:(p + 1) * LANES] = jnp.concatenate(
            [wukv[:, (2 * p) * HEAD_PAD_C + NOPE_C:(2 * p + 1) * HEAD_PAD_C],
             wukv[:, (2 * p + 1) * HEAD_PAD_C + NOPE_C:(2 * p + 2) * HEAD_PAD_C]], axis=1).astype(BF16)


def _prep_odd(w_down_t, w_uq, w_ukv):
    n = w_down_t.shape[0]
    spec = lambda r, c: pl.BlockSpec((None, r, c), lambda l: (l, 0, 0))
    return pl.pallas_call(
        _prep_odd_kernel,
        grid=(n,),
        in_specs=[spec(DOWN_RAW_W, D_MODEL), spec(Q_LORA, H_C * (NOPE_C + ROPE_C)), spec(KV_LORA, QC_W)],
        out_specs=[spec(D_MODEL, DOWN_W), spec(Q_LORA, QC_W), spec(KV_LORA, QC_W), spec(KV_LORA, VC_W)],
        out_shape=[jax.ShapeDtypeStruct((n, D_MODEL, DOWN_W), BF16), jax.ShapeDtypeStruct((n, Q_LORA, QC_W), BF16),
                   jax.ShapeDtypeStruct((n, KV_LORA, QC_W), BF16), jax.ShapeDtypeStruct((n, KV_LORA, VC_W), BF16)],
        compiler_params=_cparams(("parallel",)),
        name="prep_odd",
    )(w_down_t, w_uq, w_ukv)


def _mod_kernel(c_ref, w_ref, b_ref, o_ref):
    s = _silu(c_ref[...])
    o_ref[...] = _dot(s.astype(BF16), w_ref[...].astype(BF16)) + b_ref[...]


def _modulation(cvecs, w_mod, b_mod):
    return pl.pallas_call(
        _mod_kernel,
        grid=(DEPTH, 6 * D_MODEL // TN_MOD),
        in_specs=[
            pl.BlockSpec((N_MOD_ROWS, D_MODEL), lambda l, j: (0, 0)),
            pl.BlockSpec((None, D_MODEL, TN_MOD), lambda l, j: (l, 0, j)),
            pl.BlockSpec((None, 1, TN_MOD), lambda l, j: (l, 0, j)),
        ],
        out_specs=pl.BlockSpec((None, N_MOD_ROWS, TN_MOD), lambda l, j: (l, 0, j)),
        out_shape=jax.ShapeDtypeStruct((DEPTH, N_MOD_ROWS, 6 * D_MODEL), F32),
        compiler_params=_cparams(("parallel", "parallel")),
        name="adaln_mod",
    )(cvecs, w_mod, b_mod.reshape(DEPTH, 1, 6 * D_MODEL))


def _x_specs(n_x, tm):
    if n_x == 1:
        return [pl.BlockSpec((tm, D_MODEL), lambda i, *_: (i, 0))]
    return _split_specs(tm, D_MODEL)


def _load_x(n_x, refs, tm, rows=slice(None)):
    if n_x == 1:
        return refs[0][rows, :]
    return _load_split(refs[:2], tm, rows)


def _in_even_kernel(n_x, *refs):
    (mod_ref, g_ref, w_ref, wgk_ref, bgk_ref, cos_ref, sin_ref,
     qa_ref, ka_ref, va_ref, ga_ref, ld_ref, qb_ref, ldmin_ref, kbt_ref, vbt_ref, kbs_ref, vbs_ref) = refs[n_x:]
    x = _load_x(n_x, refs, TM_IN)
    h = _rms(x, g_ref[0:1, :]) * (1.0 + mod_ref[1:2, :]) + mod_ref[0:1, :]
    hb = h.astype(BF16)
    qa_ref[...] = _dot(hb, w_ref[:, _C_QA:_C_KA]) * (DK_A ** -0.5)
    ka_ref[...] = _dot(hb, w_ref[:, _C_KA:_C_VA])
    va_ref[...] = _dot(hb, w_ref[:, _C_VA:_C_GA]).astype(BF16)
    ga_ref[...] = _dot(hb, w_ref[:, _C_GA:_C_QB])
    cos = cos_ref[...]
    sin = sin_ref[...]
    qb = _dot(hb, w_ref[:, _C_QB:_C_KB])
    for j in range(QB_W // LANES):
        qj = qb[:, j * LANES:(j + 1) * LANES]
        qb_ref[:, j * LANES:(j + 1) * LANES] = (_rope(qj, cos, sin, D_B // 4) * (D_B ** -0.5)).astype(BF16)
    kvb = _dot(hb, w_ref[:, _C_KB:_C_LO])
    kb = _rope(kvb[:, :KB_W], cos, sin, D_B // 4)
    vb = kvb[:, KB_W:]
    lo = _dot(hb, w_ref[:, _C_LO:AB_IN]).astype(BF16)
    z = _dot(lo, wgk_ref[...]) + bgk_ref[...]
    ld = (jnp.minimum(z, 0.0) - jnp.log(1.0 + jnp.exp(-jnp.abs(z)))) * (1.0 / GATE_NORM)
    ld_ref[...] = ld
    ld_min = jnp.min(jnp.min(ld, axis=0, keepdims=True), axis=1, keepdims=True)
    ldmin_ref[...] = jnp.broadcast_to(ld_min, ldmin_ref.shape)

    is_prompt = _is_prompt_tile(TM_IN)

    @pl.when(is_prompt)
    def _():
        for s in range(TM_IN // SEQ):
            kbt_ref[s] = kb[s * SEQ:(s + 1) * SEQ, :].T
            vbt_ref[s] = vb[s * SEQ:(s + 1) * SEQ, :].T

    @pl.when(jnp.logical_not(is_prompt))
    def _():
        kbs_ref[...] = kb
        vbs_ref[...] = vb


def _in_even(xs, layer, mods, g_norm, w_in, w_gk, b_gk, cos, sin):
    tm = TM_IN
    i_ab = layer // 2
    n_prompt_tiles = N_PROMPT // tm
    seq_per_tile = tm // SEQ
    row = lambda i: (i, 0)
    widths = (QA_W, QA_W, VA_W, VA_W, 2 * QA_W, QB_W)
    dtypes = (F32, F32, BF16, F32, F32, BF16)
    kv_specs = 2 * [pl.BlockSpec((seq_per_tile, KB_W, SEQ), lambda i: (jnp.minimum(i, n_prompt_tiles - 1), 0, 0))] + \
        2 * [pl.BlockSpec((tm, KB_W), lambda i: (jnp.maximum(i - n_prompt_tiles, 0), 0))]
    kv_shapes = 2 * [jax.ShapeDtypeStruct((BATCH, KB_W, SEQ), F32)] + 2 * [jax.ShapeDtypeStruct((N_SAMPLE, KB_W), F32)]
    return pl.pallas_call(
        functools.partial(_in_even_kernel, len(xs)),
        grid=(N_TOK // tm,),
        in_specs=_x_specs(len(xs), tm) + [
            _mod_spec(layer, tm),
            _layer_spec((4, D_MODEL), layer),
            _layer_spec((D_MODEL, AB_IN), i_ab),
            _layer_spec((2 * GK_RANK, 2 * QA_W), i_ab),
            _layer_spec((1, 2 * QA_W), i_ab),
            pl.BlockSpec((tm, LANES), row),
            pl.BlockSpec((tm, LANES), row),
        ],
        out_specs=[pl.BlockSpec((tm, w), row) for w in widths] + [
            pl.BlockSpec((None, 8, LANES), lambda i: (i, 0, 0))] + kv_specs,
        out_shape=[jax.ShapeDtypeStruct((N_TOK, w), d) for w, d in zip(widths, dtypes)] + [
            jax.ShapeDtypeStruct((N_TOK // tm, 8, LANES), F32)] + kv_shapes,
        compiler_params=_cparams(("arbitrary",)),
        name="in_even",
    )(*xs, mods, g_norm, w_in, w_gk, b_gk, cos, sin)


def _gla_kernel(seq_len, has_s0, exact, *refs):
    qa_ref, ka_ref, va_ref, ga_ref, ld_ref, gg_ref = refs[:6]
    s0_refs = refs[6:8] if has_s0 else None
    o_ref, stf_ref, stb_ref = refs[8:11] if has_s0 else refs[6:9]
    st_ref, qi_ref, ki_ref, qcat_ref, ks_ref, dec_ref, stcat_ref, acc_ref = refs[-8:]
    b_ref = qi_ref if exact else None
    n_groups = seq_len // GLA_GROUP
    cpg = GLA_GROUP // GLA_CHUNK
    n_chunks = seq_len // GLA_CHUNK
    n_pairs = H_A // 2
    pair_k = 2 * DK_A
    pair_v = 2 * DV_A

    def chunk_masks():
        r_i = lax.broadcasted_iota(jnp.int32, (GLA_GROUP, GLA_GROUP), 0)
        c_i = lax.broadcasted_iota(jnp.int32, (GLA_GROUP, GLA_GROUP), 1)
        same = (r_i // GLA_CHUNK) == (c_i // GLA_CHUNK)
        return same, same & (c_i <= r_i), same & (c_i >= r_i)

    def scale_group(j, carry):
        same, mask_f, mask_b = chunk_masks()
        t_cum = (_ones_where(mask_f), _ones_where(mask_b))
        t_all = _ones_where(same)
        s_r = lax.broadcasted_iota(jnp.int32, (8, GLA_GROUP), 0)
        s_c = lax.broadcasted_iota(jnp.int32, (8, GLA_GROUP), 1)
        t_sel = _ones_where(s_r == s_c // GLA_CHUNK)
        rows = pl.ds(pl.multiple_of(j * GLA_GROUP, GLA_GROUP), GLA_GROUP)
        q = qa_ref[rows, :]
        k = ka_ref[rows, :]
        for d in range(2):
            parts = _split3(ld_ref[rows, d * QA_W:(d + 1) * QA_W])
            b = _dot3(t_cum[d], parts)
            tot = _dot3(t_all, parts)
            dec8 = jnp.exp(_dot3(t_sel, parts))
            ref = 0.5 * tot
            if exact:
                b_ref[d, rows, :] = b
            else:
                qi_ref[d, rows, :] = (q * jnp.exp(b - ref)).astype(BF16)
                ki_ref[d, rows, :] = (k * jnp.exp(ref - b)).astype(BF16)
            ks_ref[d, rows, :] = (k * jnp.exp(tot - b)).astype(BF16)
            q_inter = (q * jnp.exp(b)).astype(BF16)
            for p in range(n_pairs):
                qcat_ref[rows, p * 2 * pair_k + d * pair_k:p * 2 * pair_k + (d + 1) * pair_k] = (
                    q_inter[:, p * pair_k:(p + 1) * pair_k])
            for cc in range(cpg):
                dec_ref[d, j * cpg + cc] = jnp.broadcast_to(dec8[cc:cc + 1, :], (8, QA_W))
        return carry

    lax.fori_loop(0, n_groups, scale_group, 0)

    zpad = jnp.zeros((DK_A, DV_A), F32)
    for d in range(2):
        for p in range(n_pairs):
            if has_s0:
                top = jnp.concatenate([s0_refs[d][2 * p], zpad], axis=0).T
                bot = jnp.concatenate([zpad, s0_refs[d][2 * p + 1]], axis=0).T
                st_ref[d, p] = jnp.concatenate([top, bot], axis=0)
            else:
                st_ref[d, p] = jnp.zeros((pair_v, pair_k), F32)

    def scan_chunk(c, carry):
        bd_r = lax.broadcasted_iota(jnp.int32, (pair_v, pair_k), 0)
        bd_c = lax.broadcasted_iota(jnp.int32, (pair_v, pair_k), 1)
        bd_mask = (bd_r // DV_A) == (bd_c // DK_A)
        for d in range(2):
            cd = c if d == 0 else n_chunks - 1 - c
            rows = pl.ds(pl.multiple_of(cd * GLA_CHUNK, GLA_CHUNK), GLA_CHUNK)
            for p in range(n_pairs):
                st = st_ref[d, p]
                stcat_ref[cd, p, :, d * pair_k:(d + 1) * pair_k] = st.astype(BF16)
                u = _dot_tn(va_ref[rows, p * pair_v:(p + 1) * pair_v], ks_ref[d, rows, p * pair_k:(p + 1) * pair_k])
                dec = dec_ref[d, cd][0:1, p * pair_k:(p + 1) * pair_k]
                st_ref[d, p] = st * dec + jnp.where(bd_mask, u, 0.0)
        return carry

    lax.fori_loop(0, n_chunks, scan_chunk, 0, unroll=GLA_SCAN_UNROLL)

    for d, out_ref in ((0, stf_ref), (1, stb_ref)):
        for p in range(n_pairs):
            st = st_ref[d, p]
            out_ref[2 * p] = st[0:DV_A, :].T[0:DK_A, :]
            out_ref[2 * p + 1] = st[DV_A:pair_v, :].T[DK_A:pair_k, :]

    def exact_intra(row0, h):
        p, hh = divmod(h, 2)
        lanes = slice(p * pair_k, (p + 1) * pair_k)
        head_lanes = (lax.broadcasted_iota(jnp.int32, (GLA_CHUNK, pair_k), 1) // DK_A) == hh
        s_idx = lax.broadcasted_iota(jnp.int32, (GLA_CHUNK, GLA_CHUNK), 0)
        t_idx = lax.broadcasted_iota(jnp.int32, (GLA_CHUNK, GLA_CHUNK), 1)
        outs = []
        for cc in range(cpg):
            r0 = row0 + cc * GLA_CHUNK
            crow = pl.ds(pl.multiple_of(r0, GLA_CHUNK), GLA_CHUNK)
            kc = ka_ref[crow, lanes]
            att_t = jnp.zeros((GLA_CHUNK, GLA_CHUNK), F32)
            for d in range(2):
                bc = b_ref[d, crow, lanes]

                def row_step(t, att_t, d=d, bc=bc, kc=kc, r0=r0):
                    grp = pl.ds(pl.multiple_of(r0 + (t // 8) * 8, 8), 8)
                    pick = lax.broadcasted_iota(jnp.int32, (8, pair_k), 0) == t % 8
                    bt = jnp.sum(jnp.where(pick, b_ref[d, grp, lanes], 0.0), axis=0, keepdims=True)
                    qt = jnp.sum(jnp.where(pick, qa_ref[grp, lanes], 0.0), axis=0, keepdims=True)
                    w = jnp.where(head_lanes, qt * kc * jnp.exp(jnp.minimum(bt - bc, 0.0)), 0.0)
                    col = jnp.sum(w, axis=1, keepdims=True)
                    allowed = (s_idx <= t) if d == 0 else (s_idx >= t)
                    return jnp.where((t_idx == t) & allowed, att_t + col, att_t)

                att_t = lax.fori_loop(0, GLA_CHUNK, row_step, att_t)
            outs.append(_dot_tn(att_t.astype(BF16), va_ref[crow, h * DV_A:(h + 1) * DV_A]))
        return jnp.concatenate(outs, axis=0)

    def out_group(j, carry):
        _, mask_f, mask_b = chunk_masks()
        lane = lax.broadcasted_iota(jnp.int32, (GLA_GROUP, pair_k), 1)
        rows = pl.ds(pl.multiple_of(j * GLA_GROUP, GLA_GROUP), GLA_GROUP)
        for p in range(n_pairs):
            for cc in range(cpg):
                c = j * cpg + cc
                crow = pl.ds(pl.multiple_of(c * GLA_CHUNK, GLA_CHUNK), GLA_CHUNK)
                acc_ref[crow, p * pair_v:(p + 1) * pair_v] = _dot_nt(
                    qcat_ref[crow, p * 2 * pair_k:(p + 1) * 2 * pair_k], stcat_ref[c, p])
        for h in range(H_A):
            p, hh = divmod(h, 2)
            if exact:
                intra = exact_intra(j * GLA_GROUP, h)
            else:
                head_lanes = _ones_where((lane // DK_A) == hh)
                att = None
                for d, mask in ((0, mask_f), (1, mask_b)):
                    qm = qi_ref[d, rows, p * pair_k:(p + 1) * pair_k] * head_lanes
                    a = jnp.where(mask, _dot_nt(qm, ki_ref[d, rows, p * pair_k:(p + 1) * pair_k]), 0.0)
                    att = a if att is None else att + a
                intra = _dot(att.astype(BF16), va_ref[rows, h * DV_A:(h + 1) * DV_A])
            o = acc_ref[rows, h * DV_A:(h + 1) * DV_A] + intra
            gate = _silu(ga_ref[rows, h * DV_A:(h + 1) * DV_A])
            o_ref[rows, h * DV_A:(h + 1) * DV_A] = (_rms(o, gg_ref[...]) * gate).astype(BF16)
        return carry

    lax.fori_loop(0, n_groups, out_group, 0)


def _gla(seq_len, n_seq, row_block0, qa, ka, va, ga, ld, g_gla, i_ab, s0=None, exact=False):
    has_s0 = s0 is not None
    n_chunks = seq_len // GLA_CHUNK
    n_pairs = H_A // 2
    rows = lambda b: (row_block0 + b, 0)
    st_spec = pl.BlockSpec((None, H_A, DK_A, DV_A), lambda b: (b, 0, 0, 0))
    in_specs = [
        pl.BlockSpec((seq_len, QA_W), rows),
        pl.BlockSpec((seq_len, QA_W), rows),
        pl.BlockSpec((seq_len, VA_W), rows),
        pl.BlockSpec((seq_len, VA_W), rows),
        pl.BlockSpec((seq_len, 2 * QA_W), rows),
        _layer_spec((1, DV_A), i_ab),
    ]
    args = [qa, ka, va, ga, ld, g_gla]
    if has_s0:
        s0_spec = pl.BlockSpec((None, None, H_A, DK_A, DV_A), lambda b: (b, i_ab, 0, 0, 0))
        in_specs += [s0_spec, s0_spec]
        args += list(s0)
    st_shape = jax.ShapeDtypeStruct((n_seq, H_A, DK_A, DV_A), F32)
    return pl.pallas_call(
        functools.partial(_gla_kernel, seq_len, has_s0, exact),
        grid=(n_seq,),
        in_specs=in_specs,
        out_specs=[pl.BlockSpec((seq_len, VA_W), lambda b: (b, 0)), st_spec, st_spec],
        out_shape=[jax.ShapeDtypeStruct((n_seq * seq_len, VA_W), BF16), st_shape, st_shape],
        scratch_shapes=[
            pltpu.VMEM((2, n_pairs, 2 * DV_A, 2 * DK_A), F32),
            pltpu.VMEM((2, seq_len, QA_W), F32 if exact else BF16),
            pltpu.VMEM((2, seq_len, QA_W), BF16),
            pltpu.VMEM((seq_len, 2 * QA_W), BF16),
            pltpu.VMEM((2, seq_len, QA_W), BF16),
            pltpu.VMEM((2, n_chunks, 8, QA_W), F32),
            pltpu.VMEM((n_chunks, n_pairs, 2 * DV_A, 4 * DK_A), BF16),
            pltpu.VMEM((seq_len, VA_W), F32),
        ],
        compiler_params=_cparams(("parallel",)),
        name=("gla_s" if has_s0 else "gla_p") + ("_exact" if exact else ""),
    )(*args)


def _swa_head_softmax(pieces, sink):
    m = sink
    for s, _, _ in pieces:
        m = jnp.maximum(m, jnp.max(s, axis=-1, keepdims=True))
    den = jnp.exp(sink - m)
    acc = None
    for s, v, transposed in pieces:
        e = jnp.exp(s - m)
        den = den + jnp.sum(e, axis=-1, keepdims=True)
        pv = _dot_nt(e.astype(BF16), v) if transposed else _dot(e.astype(BF16), v)
        acc = pv if acc is None else acc + pv
    return acc / den


def _dup_groups(x):
    lane = lax.broadcasted_iota(jnp.int32, x.shape, 1)
    swapped = pltpu.roll(x, D_B, 1)
    low = lane < D_B
    return jnp.where(low, x, swapped).astype(BF16), jnp.where(low, swapped, x).astype(BF16)


def _dup_groups_t(xt):
    g0, g1 = xt[0:D_B, :], xt[D_B:, :]
    return jnp.concatenate([g0, g0], axis=0).astype(BF16), jnp.concatenate([g1, g1], axis=0).astype(BF16)


def _swa_heads(sink_ref, i_ab, q_ref, kv_pieces, o_ref, n_rows):
    lane = lax.broadcasted_iota(jnp.int32, (n_rows, LANES), 1)
    for j in range(H_B // 2):
        g = (2 * j) // G_B
        qp = q_ref[:, j * LANES:(j + 1) * LANES]
        outs = []
        for hh in range(2):
            qm = qp * _ones_where((lane // D_B) == hh)
            pieces = []
            for keys, vals, transposed, mask in kv_pieces:
                s = _dot(qm, keys[g]) if transposed else _dot_nt(qm, keys[g])
                if mask is not None:
                    s = jnp.where(mask, s, NEG_BIG)
                pieces.append((s, vals[g], transposed))
            outs.append(_swa_head_softmax(pieces, sink_ref[i_ab, 2 * j + hh]))
        o_ref[:, j * LANES:(j + 1) * LANES] = jnp.where(lane < D_B, outs[0], outs[1]).astype(BF16)


def _swa_prompt_kernel(i_ab, sink_ref, q_ref, kt_ref, vt_ref, o_ref):
    _swa_heads(sink_ref, i_ab, q_ref, [(_dup_groups_t(kt_ref[...]), _dup_groups_t(vt_ref[...]), True, None)],
               o_ref, SEQ)


def _swa_prompt(sink, i_ab, qb, kbt, vbt):
    seq = lambda b: (b, 0)
    seq_t = pl.BlockSpec((None, KB_W, SEQ), lambda b: (b, 0, 0))
    return pl.pallas_call(
        functools.partial(_swa_prompt_kernel, i_ab),
        grid=(BATCH,),
        in_specs=[pl.BlockSpec(memory_space=pltpu.SMEM), pl.BlockSpec((SEQ, QB_W), seq), seq_t, seq_t],
        out_specs=pl.BlockSpec((SEQ, QB_W), seq),
        out_shape=jax.ShapeDtypeStruct((N_PROMPT, QB_W), BF16),
        compiler_params=_cparams(("parallel",)),
        name="swa_p",
    )(sink, qb, kbt, vbt)


def _swa_sample_kernel(i_ab, sink_ref, q_ref, k_ref, v_ref, kct_ref, vct_ref, o_ref):
    n = pl.program_id(1)
    start = pl.multiple_of(jnp.clip((n - 1) * SWA_QB, 0, DEC_SEQ - SWA_WIN), SWA_QB)
    local = (_dup_groups(k_ref[pl.ds(start, SWA_WIN), :]), _dup_groups(v_ref[pl.ds(start, SWA_WIN), :]))
    ctx = (_dup_groups_t(kct_ref[...]), _dup_groups_t(vct_ref[...]))
    qi = n * SWA_QB + lax.broadcasted_iota(jnp.int32, (SWA_QB, SWA_WIN), 0)
    ki = start + lax.broadcasted_iota(jnp.int32, (SWA_QB, SWA_WIN), 1)
    band = jnp.abs(qi - ki) <= WINDOW
    _swa_heads(sink_ref, i_ab, q_ref, [ctx + (True, None), local + (False, band)], o_ref, SWA_QB)


def _swa_sample(sink, i_ab, qb, kb, vb, kc, vc):
    nqb = DEC_SEQ // SWA_QB
    q0 = N_PROMPT // SWA_QB
    return pl.pallas_call(
        functools.partial(_swa_sample_kernel, i_ab),
        grid=(DEC_BATCH, nqb),
        in_specs=[
            pl.BlockSpec(memory_space=pltpu.SMEM),
            pl.BlockSpec((SWA_QB, QB_W), lambda b, n: (q0 + b * nqb + n, 0)),
            pl.BlockSpec((DEC_SEQ, KB_W), lambda b, n: (b, 0)),
            pl.BlockSpec((DEC_SEQ, KB_W), lambda b, n: (b, 0)),
            pl.BlockSpec((None, None, KB_W, PAST_LEN), lambda b, n: (b, i_ab, 0, 0)),
            pl.BlockSpec((None, None, KB_W, PAST_LEN), lambda b, n: (b, i_ab, 0, 0)),
        ],
        out_specs=pl.BlockSpec((SWA_QB, QB_W), lambda b, n: (b * nqb + n, 0)),
        out_shape=jax.ShapeDtypeStruct((N_SAMPLE, QB_W), BF16),
        compiler_params=_cparams(("parallel", "parallel")),
        name="swa_s",
    )(sink, qb, kb, vb, kc, vc)


def _in_odd_kernel(x_ref, mod_ref, g_ref, wd_ref, gq_ref, gkv_ref, wuq_ref, wuk_ref, wuv_ref, cos_ref, sin_ref,
                   q_ref, k_ref, v_ref, ckv_ref, krt_ref):
    h = _rms(x_ref[...], g_ref[0:1, :]) * (1.0 + mod_ref[1:2, :]) + mod_ref[0:1, :]
    hb = h.astype(BF16)
    cos = cos_ref[...]
    sin = sin_ref[...]
    c_q = _dot(hb, wd_ref[:, 0:Q_LORA])
    c_kv = _rms(_dot(hb, wd_ref[:, Q_LORA:Q_LORA + KV_LORA]), gkv_ref[...])
    kr = _rope(_dot(hb, wd_ref[:, Q_LORA + KV_LORA:DOWN_W]), cos, sin, ROPE_C // 4)
    cqb = _rms(c_q, gq_ref[...]).astype(BF16)
    ckvb = c_kv.astype(BF16)
    scale = (NOPE_C + ROPE_C) ** -0.5
    group_w = HEADS_PER_DOT_C * HEAD_PAD_C
    for grp in range(H_C // HEADS_PER_DOT_C):
        gsl = slice(grp * group_w, (grp + 1) * group_w)
        qg = _dot(cqb, wuq_ref[:, gsl])
        kg = _dot(ckvb, wuk_ref[:, gsl])
        for j in range(HEADS_PER_DOT_C):
            sl = slice(j * HEAD_PAD_C, (j + 1) * HEAD_PAD_C)
            osl = slice(grp * group_w + j * HEAD_PAD_C, grp * group_w + (j + 1) * HEAD_PAD_C)
            q_ref[:, osl] = (_rope(qg[:, sl], cos, sin, ROPE_C // 4) * scale).astype(BF16)
            k_ref[:, osl] = (kg[:, sl] + kr).astype(BF16)
    v_ref[...] = _dot(ckvb, wuv_ref[...]).astype(BF16)

    @pl.when(_is_prompt_tile(TM_IN))
    def _():
        ckv_ref[...] = c_kv
        for s in range(TM_IN // SEQ):
            krt_ref[s] = kr[s * SEQ:(s + 1) * SEQ, :].T[KR_OFF:KR_OFF + ROPE_C, :]


def _in_odd(x, layer, mods, g_norm, w_down, g_q, g_kv, w_uq, w_uk, w_uv, cos, sin):
    tm = TM_IN
    i_c = layer // 2
    n_prompt_tiles = N_PROMPT // tm
    row = lambda i: (i, 0)
    prompt_row = lambda i: (jnp.minimum(i, n_prompt_tiles - 1), 0)
    prompt_seq = lambda i: (jnp.minimum(i, n_prompt_tiles - 1), 0, 0)
    widths = (QC_W, QC_W, VC_W)
    return pl.pallas_call(
        _in_odd_kernel,
        grid=(N_TOK // tm,),
        in_specs=[
            pl.BlockSpec((tm, D_MODEL), row),
            _mod_spec(layer, tm),
            _layer_spec((4, D_MODEL), layer),
            _layer_spec((D_MODEL, DOWN_W), i_c),
            _layer_spec((1, Q_LORA), i_c),
            _layer_spec((1, KV_LORA), i_c),
            _layer_spec((Q_LORA, QC_W), i_c),
            _layer_spec((KV_LORA, QC_W), i_c),
            _layer_spec((KV_LORA, VC_W), i_c),
            pl.BlockSpec((tm, LANES), row),
            pl.BlockSpec((tm, LANES), row),
        ],
        out_specs=[pl.BlockSpec((tm, w), row) for w in widths] + [
            pl.BlockSpec((tm, KV_LORA), prompt_row), pl.BlockSpec((tm // SEQ, ROPE_C, SEQ), prompt_seq)],
        out_shape=[jax.ShapeDtypeStruct((N_TOK, w), BF16) for w in widths] + [
            jax.ShapeDtypeStruct((N_PROMPT, KV_LORA), F32), jax.ShapeDtypeStruct((BATCH, ROPE_C, SEQ), F32)],
        compiler_params=_cparams(("arbitrary",)),
        name="in_odd",
    )(x, mods, g_norm, w_down, g_q, g_kv, w_uq, w_uk, w_uv, cos, sin)


def _mla_heads(q_ref, kv_pieces, o_ref, n_rows):
    lane = lax.broadcasted_iota(jnp.int32, (n_rows, LANES), 1)
    for pair in range(H_C // 2):
        outs = []
        for hh in range(2):
            hd = 2 * pair + hh
            sl = slice(hd * HEAD_PAD_C, (hd + 1) * HEAD_PAD_C)
            q = q_ref[:, sl]
            scores = [_dot_nt(q, k_ref[:, sl]) for k_ref, _ in kv_pieces]
            m = None
            for s in scores:
                sm = jnp.max(s, axis=-1, keepdims=True)
                m = sm if m is None else jnp.maximum(m, sm)
            den = None
            acc = None
            for s, (_, v_ref) in zip(scores, kv_pieces):
                e = jnp.exp(s - m)
                es = jnp.sum(e, axis=-1, keepdims=True)
                den = es if den is None else den + es
                pv = _dot(e.astype(BF16), v_ref[:, pair * LANES:(pair + 1) * LANES])
                acc = pv if acc is None else acc + pv
            outs.append(acc / den)
        o_ref[:, pair * LANES:(pair + 1) * LANES] = jnp.where(lane < V_C, outs[0], outs[1]).astype(BF16)


def _mla_prompt_kernel(q_ref, k_ref, v_ref, o_ref):
    _mla_heads(q_ref, [(k_ref, v_ref)], o_ref, SEQ)


def _mla_prompt(q, k, v):
    seq = lambda b: (b, 0)
    return pl.pallas_call(
        _mla_prompt_kernel,
        grid=(BATCH,),
        in_specs=[pl.BlockSpec((SEQ, QC_W), seq), pl.BlockSpec((SEQ, QC_W), seq), pl.BlockSpec((SEQ, VC_W), seq)],
        out_specs=pl.BlockSpec((SEQ, VC_W), seq),
        out_shape=jax.ShapeDtypeStruct((N_PROMPT, VC_W), BF16),
        compiler_params=_cparams(("parallel",)),
        name="mla_p",
    )(q, k, v)


def _mla_sample_kernel(q_ref, k_ref, v_ref, ckv_ref, krt_ref, wuk_ref, wuv_ref, o_ref, kc_ref, vc_ref):
    @pl.when(pl.program_id(1) == 0)
    def _():
        cb = ckv_ref[...].astype(BF16)
        kr = jnp.concatenate([jnp.zeros((KR_OFF, PAST_LEN), F32), krt_ref[...],
                              jnp.zeros((LANES - KR_OFF - ROPE_C, PAST_LEN), F32)], axis=0).T
        group_w = HEADS_PER_DOT_C * HEAD_PAD_C
        for grp in range(H_C // HEADS_PER_DOT_C):
            kg = _dot(cb, wuk_ref[:, grp * group_w:(grp + 1) * group_w])
            for j in range(HEADS_PER_DOT_C):
                osl = slice(grp * group_w + j * HEAD_PAD_C, grp * group_w + (j + 1) * HEAD_PAD_C)
                kc_ref[:, osl] = (kg[:, j * HEAD_PAD_C:(j + 1) * HEAD_PAD_C] + kr).astype(BF16)
        vc_ref[...] = _dot(cb, wuv_ref[...]).astype(BF16)

    _mla_heads(q_ref, [(kc_ref, vc_ref), (k_ref, v_ref)], o_ref, MLA_QB)


def _mla_sample(q, k, v, ckv_ctx, kr_ctx, w_uk, w_uv, i_c):
    nqb = DEC_SEQ // MLA_QB
    q0 = N_PROMPT // MLA_QB
    s0 = N_PROMPT // DEC_SEQ
    return pl.pallas_call(
        _mla_sample_kernel,
        grid=(DEC_BATCH, nqb),
        in_specs=[
            pl.BlockSpec((MLA_QB, QC_W), lambda b, n: (q0 + b * nqb + n, 0)),
            pl.BlockSpec((DEC_SEQ, QC_W), lambda b, n: (s0 + b, 0)),
            pl.BlockSpec((DEC_SEQ, VC_W), lambda b, n: (s0 + b, 0)),
            pl.BlockSpec((None, None, PAST_LEN, KV_LORA), lambda b, n: (b, i_c, 0, 0)),
            pl.BlockSpec((None, None, ROPE_C, PAST_LEN), lambda b, n: (b, i_c, 0, 0)),
            _layer_spec((KV_LORA, QC_W), i_c),
            _layer_spec((KV_LORA, VC_W), i_c),
        ],
        out_specs=pl.BlockSpec((MLA_QB, VC_W), lambda b, n: (b * nqb + n, 0)),
        out_shape=jax.ShapeDtypeStruct((N_SAMPLE, VC_W), BF16),
        scratch_shapes=[pltpu.VMEM((PAST_LEN, QC_W), BF16), pltpu.VMEM((PAST_LEN, VC_W), BF16)],
        compiler_params=_cparams(("parallel", "arbitrary")),
        name="mla_s",
    )(q, k, v, ckv_ctx, kr_ctx, w_uk, w_uv)


def _out_kernel(n_x, *refs):
    a_refs = refs[n_x + 2:n_x + 4]
    b_refs = refs[n_x + 4:n_x + 6]
    mod_ref, g_ref = refs[n_x:n_x + 2]
    wo_ref, w1_ref, w2_ref = refs[n_x + 6:n_x + 9]
    o_ref, h2_ref, acc_ref = refs[n_x + 9:]
    x1_ref = o_ref
    kk = pl.program_id(1)
    half = wo_ref.shape[0] // 2

    @pl.when(kk == 0)
    def _():
        wo_a = wo_ref[0:half, :].astype(BF16)
        wo_b = wo_ref[half:, :].astype(BF16)
        for r in range(TM_OUT // OUT_PROLOGUE_ROWS):
            rows = slice(r * OUT_PROLOGUE_ROWS, (r + 1) * OUT_PROLOGUE_ROWS)
            a = _load_split(a_refs, TM_OUT, rows)
            b = _load_split(b_refs, TM_OUT, rows)
            mix = _dot(a, wo_a) + _dot(b, wo_b)
            x1 = _load_x(n_x, refs, TM_OUT, rows) + mod_ref[2:3, :] * _rms(mix, g_ref[1:2, :])
            x1_ref[rows, :] = x1
            h2_ref[rows, :] = (_rms(x1, g_ref[2:3, :]) * (1.0 + mod_ref[4:5, :]) + mod_ref[3:4, :]).astype(BF16)
        acc_ref[...] = jnp.zeros(acc_ref.shape, F32)

    hid = jnp.maximum(_dot(h2_ref[...], w1_ref[...].astype(BF16)), 0.0)
    acc_ref[...] += _dot((hid * hid).astype(BF16), w2_ref[...].astype(BF16))

    @pl.when(kk == pl.num_programs(1) - 1)
    def _():
        o_ref[...] = x1_ref[...] + mod_ref[5:6, :] * _rms(acc_ref[...], g_ref[3:4, :])


def _out_layer(xs, layer, mods, g_norm, mix_a, mix_b, b_col, w_o, i_o, w_ff1, w_ff2):
    tm = TM_OUT
    half = D_MODEL // 2
    return pl.pallas_call(
        functools.partial(_out_kernel, len(xs)),
        grid=(N_TOK // tm, D_FF // TK_FF),
        in_specs=_x_specs(len(xs), tm) + [
            _mod_spec(layer, tm),
            _layer_spec((4, D_MODEL), layer),
        ] + _split_specs(tm, half, 0) + _split_specs(tm, half, b_col) + [
            _layer_spec((D_MODEL, D_MODEL), i_o),
            pl.BlockSpec((None, D_MODEL, TK_FF), lambda i, k: (layer, 0, k)),
            pl.BlockSpec((None, TK_FF, D_MODEL), lambda i, k: (layer, k, 0)),
        ],
        out_specs=pl.BlockSpec((tm, D_MODEL), lambda i, k: (i, 0)),
        out_shape=jax.ShapeDtypeStruct((N_TOK, D_MODEL), F32),
        scratch_shapes=[pltpu.VMEM((tm, D_MODEL), BF16), pltpu.VMEM((tm, D_MODEL), F32)],
        compiler_params=_cparams(("parallel", "arbitrary")),
        name="out_mlp",
    )(*xs, mods, g_norm, *mix_a, *mix_b, w_o, w_ff1, w_ff2)


def _rope_tables(head_dim, lane_off, group):
    nf = head_dim // 4
    n_rows = DEC_SEQ // GRID_W
    rows = jnp.repeat(jnp.arange(n_rows, dtype=F32), GRID_W)
    cols = jnp.tile(jnp.arange(GRID_W, dtype=F32), n_rows)
    inv = ROPE_BASE ** (-jnp.arange(nf, dtype=F32) / nf)
    ang = jnp.stack([rows[:, None] * inv, cols[:, None] * inv], axis=1)
    cos = jnp.broadcast_to(jnp.cos(ang)[:, :, None, :], (DEC_SEQ, 2, 2, nf)).reshape(DEC_SEQ, head_dim)
    sin = jnp.sin(ang)
    sin = jnp.stack([-sin, sin], axis=2).reshape(DEC_SEQ, head_dim)
    cos_g = jnp.ones((DEC_SEQ, group), F32).at[:, lane_off:lane_off + head_dim].set(cos)
    sin_g = jnp.zeros((DEC_SEQ, group), F32).at[:, lane_off:lane_off + head_dim].set(sin)
    reps = LANES // group
    cos_s = jnp.tile(jnp.tile(cos_g, (1, reps)), (DEC_BATCH, 1))
    sin_s = jnp.tile(jnp.tile(sin_g, (1, reps)), (DEC_BATCH, 1))
    cos_t = jnp.concatenate([jnp.ones((N_PROMPT, LANES), F32), cos_s], axis=0)
    sin_t = jnp.concatenate([jnp.zeros((N_PROMPT, LANES), F32), sin_s], axis=0)
    return cos_t, sin_t


def kernel(x_prompt, x_sample, state_gla_fwd, state_gla_bwd, cache_swa_k, cache_swa_v, cache_mla_ckv, cache_mla_kr, c, c_ctx, w_mod, b_mod, g_norm, w_ff1, w_ff2, w_in_ab, w_gk_f, b_gk_f, w_gk_b, b_gk_b, g_gla, swa_sink, w_out_ab, w_mla_down, g_mla_q, g_mla_kv, w_mla_uq, w_mla_ukv, w_mla_o):
    xs = (x_prompt.reshape(N_PROMPT, D_MODEL), x_sample.reshape(N_SAMPLE, D_MODEL))
    cvecs = jnp.concatenate([c_ctx[None, :], c, jnp.zeros((N_MOD_ROWS - 1 - DEC_BATCH, D_MODEL), F32)], axis=0)
    mods = _modulation(cvecs, w_mod, b_mod).reshape(DEPTH, N_MOD_ROWS, 6, D_MODEL)

    cos_b, sin_b = _rope_tables(D_B, 0, D_B)
    cos_c, sin_c = _rope_tables(ROPE_C, KR_OFF, HEAD_PAD_C)
    p_blk = N_PROMPT // DEC_SEQ

    n_ab = w_in_ab.shape[0]
    w_in = _prep_even(jnp.swapaxes(w_in_ab, 1, 2))
    zgk = jnp.zeros((n_ab, GK_RANK, QA_W), F32)
    w_gk = jnp.concatenate([jnp.concatenate([w_gk_f, zgk], axis=2),
                            jnp.concatenate([zgk, w_gk_b], axis=2)], axis=1).astype(BF16)
    b_gk = jnp.concatenate([b_gk_f, b_gk_b], axis=1)[:, None, :]
    gg = g_gla[:, None, :]
    kc = jnp.transpose(cache_swa_k, (0, 1, 3, 4, 2)).reshape(DEC_BATCH, n_ab, KB_W, PAST_LEN)
    vc = jnp.transpose(cache_swa_v, (0, 1, 3, 4, 2)).reshape(DEC_BATCH, n_ab, KB_W, PAST_LEN)
    w_down, w_uq, w_uk, w_uv = _prep_odd(jnp.swapaxes(w_mla_down, 1, 2), w_mla_uq, w_mla_ukv)
    g_q = g_mla_q[:, None, :]
    g_kv = g_mla_kv[:, None, :]
    kr_ctx = jnp.swapaxes(cache_mla_kr, 2, 3)

    def gla_layer(i_ab, exact):
        def run(qa, ka, va, ga, ld):
            o_p, stf, stb = _gla(SEQ, BATCH, 0, qa, ka, va, ga, ld, gg, i_ab, exact=exact)
            o_s, _, _ = _gla(DEC_SEQ, DEC_BATCH, p_blk, qa, ka, va, ga, ld, gg, i_ab,
                             (state_gla_fwd, state_gla_bwd), exact=exact)
            return o_p, o_s, stf, stb
        return run

    st_f, st_b, sk, sv, ckv_out, ckr_out = [], [], [], [], [], []
    for l in range(DEPTH):
        i = l // 2
        if l % 2 == 0:
            qa, ka, va, ga, ld, qb, ld_min, kbt, vbt, kb_s, vb_s = _in_even(
                xs, l, mods, g_norm, w_in, w_gk, b_gk, cos_b, sin_b)
            factorisable = jnp.min(ld_min) * GLA_CHUNK >= -GLA_SAFE_TOTAL
            o_gla_p, o_gla_s, stf, stb = lax.cond(factorisable, gla_layer(i, False), gla_layer(i, True),
                                                  qa, ka, va, ga, ld)
            o_swa_p = _swa_prompt(swa_sink, i, qb, kbt, vbt)
            o_swa_s = _swa_sample(swa_sink, i, qb, kb_s, vb_s, kc, vc)
            xs = (_out_layer(xs, l, mods, g_norm, (o_gla_p, o_gla_s), (o_swa_p, o_swa_s), 0, w_out_ab, i,
                             w_ff1, w_ff2),)
            st_f.append(stf)
            st_b.append(stb)
            sk.append(kbt)
            sv.append(vbt)
        else:
            q, k, v, ckv, krt = _in_odd(xs[0], l, mods, g_norm, w_down, g_q, g_kv, w_uq, w_uk, w_uv, cos_c, sin_c)
            o_mla = (_mla_prompt(q, k, v), _mla_sample(q, k, v, cache_mla_ckv, kr_ctx, w_uk, w_uv, i))
            xs = (_out_layer(xs, l, mods, g_norm, o_mla, o_mla, 1, w_mla_o, i, w_ff1, w_ff2),)
            ckv_out.append(ckv.reshape(BATCH, SEQ, KV_LORA))
            ckr_out.append(krt)

    y_prompt = xs[0][:N_PROMPT].reshape(BATCH, SEQ, D_MODEL)
    y_sample = xs[0][N_PROMPT:].reshape(DEC_BATCH, DEC_SEQ, D_MODEL)
    swa_cache = lambda parts: jnp.transpose(
        jnp.stack(parts, axis=1).reshape(BATCH, len(parts), KV_B, D_B, SEQ), (0, 1, 4, 2, 3))
    return (y_prompt, y_sample, jnp.stack(st_f, axis=1), jnp.stack(st_b, axis=1), swa_cache(sk), swa_cache(sv),
            jnp.stack(ckv_out, axis=1), jnp.swapaxes(jnp.stack(ckr_out, axis=1), 2, 3))
```

```python
import functools

import jax
import jax.numpy as jnp
from jax import lax
from jax.experimental import pallas as pl
from jax.experimental.pallas import tpu as pltpu

F32 = jnp.float32
BF16 = jnp.bfloat16

D_MODEL = 1024
BATCH = 16
SEQ = 256
DEPTH = 4
DEC_BATCH = 2
DEC_SEQ = 1024
PAST_LEN = 512
GRID_W = 64
D_FF = 4 * D_MODEL
EPS = 1e-6
ROPE_BASE = 10000.0
H_A = 4
DK_A = 64
DV_A = 128
GK_RANK = 16
GATE_NORM = 16.0
GLA_CHUNK = 64
H_B = 8
KV_B = 2
G_B = H_B // KV_B
D_B = 64
WINDOW = 128
H_C = 16
NOPE_C = 64
ROPE_C = 32
V_C = 64
Q_LORA = 384
KV_LORA = 256

N_PROMPT = BATCH * SEQ
N_SAMPLE = DEC_BATCH * DEC_SEQ
N_TOK = N_PROMPT + N_SAMPLE
N_MOD_ROWS = 8
QA_W = H_A * DK_A
VA_W = H_A * DV_A
QB_W = H_B * D_B
KB_W = KV_B * D_B
LANES = 128
HEAD_PAD_C = 128
QC_W = H_C * HEAD_PAD_C
VC_W = H_C * V_C
DOWN_RAW_W = Q_LORA + KV_LORA + ROPE_C
DOWN_W = Q_LORA + KV_LORA + LANES
KR_OFF = NOPE_C

TM_IN = 512
TM_OUT = 1024
TK_FF = 512
OUT_PROLOGUE_ROWS = 256
TN_MOD = 1536
PREP_ROWS = 256
GLA_GROUP = 256
GLA_SCAN_UNROLL = 4
GLA_SAFE_TOTAL = 160.0
SWA_QB = 128
SWA_WIN = 3 * SWA_QB
MLA_QB = 256
HEADS_PER_DOT_C = 4
MLA_HEAD_GROUP_P = 16
MLA_HEAD_GROUP_S = 4
VMEM_LIMIT = 56 * 1024 * 1024
NEG_BIG = -1e30


def _cparams(sem):
    return pltpu.CompilerParams(dimension_semantics=sem, vmem_limit_bytes=VMEM_LIMIT)


def _dot(a, b):
    return jnp.dot(a, b, preferred_element_type=F32)


def _dot_nt(a, b):
    return lax.dot_general(a, b, (((1,), (1,)), ((), ())), preferred_element_type=F32)


def _dot_tn(a, b):
    return lax.dot_general(a, b, (((0,), (0,)), ((), ())), preferred_element_type=F32)


def _rms(x, g):
    return x * lax.rsqrt(jnp.mean(x * x, axis=-1, keepdims=True) + EPS) * g


def _silu(x):
    return x / (1.0 + jnp.exp(-x))


def _rope(x, cos, sin, half):
    lane = lax.broadcasted_iota(jnp.int32, x.shape, 1)
    first = (lane % (2 * half)) < half
    partner = jnp.where(first, pltpu.roll(x, LANES - half, 1), pltpu.roll(x, half, 1))
    return x * cos + partner * sin


def _split3(x):
    x1 = x.astype(BF16)
    r1 = x - x1.astype(F32)
    x2 = r1.astype(BF16)
    x3 = (r1 - x2.astype(F32)).astype(BF16)
    return x1, x2, x3


def _dot3(t, parts):
    return _dot(t, parts[2]) + _dot(t, parts[1]) + _dot(t, parts[0])


def _ones_where(cond):
    return jnp.where(cond, 1.0, 0.0).astype(BF16)


def _mod_row(tile, tm):
    n_prompt_tiles = N_PROMPT // tm
    tiles_per_seq = DEC_SEQ // tm
    return jnp.where(tile < n_prompt_tiles, 0, 1 + (tile - n_prompt_tiles) // tiles_per_seq)


def _layer_spec(shape, idx):
    return pl.BlockSpec((None,) + shape, lambda *_: (idx,) + (0,) * len(shape))


def _mod_spec(layer, tm):
    return pl.BlockSpec((None, None, 6, D_MODEL), lambda i, *_: (layer, _mod_row(i, tm), 0, 0))


def _split_specs(tm, width, col=0):
    n_prompt_tiles = N_PROMPT // tm
    return [pl.BlockSpec((tm, width), lambda i, *_: (jnp.minimum(i, n_prompt_tiles - 1), col)),
            pl.BlockSpec((tm, width), lambda i, *_: (jnp.maximum(i - n_prompt_tiles, 0), col))]


def _split_shapes(width, dtype):
    return [jax.ShapeDtypeStruct((N_PROMPT, width), dtype), jax.ShapeDtypeStruct((N_SAMPLE, width), dtype)]


def _is_prompt_tile(tm):
    return pl.program_id(0) < N_PROMPT // tm


def _load_split(pair, tm, rows=slice(None)):
    return jnp.where(_is_prompt_tile(tm), pair[0][rows, :], pair[1][rows, :])


def _store_split(pair, tm, value):
    is_prompt = _is_prompt_tile(tm)

    @pl.when(is_prompt)
    def _():
        pair[0][...] = value

    @pl.when(jnp.logical_not(is_prompt))
    def _():
        pair[1][...] = value


_C_QA, _C_KA, _C_VA, _C_GA = 0, QA_W, 2 * QA_W, 2 * QA_W + VA_W
_C_QB = _C_GA + VA_W
_C_KB = _C_QB + QB_W
_C_VB = _C_KB + KB_W
_C_LO = _C_VB + KB_W
AB_IN = _C_LO + 2 * GK_RANK


def _prep_even_kernel(wt_ref, o_ref):
    raw_lo = _C_QB
    raw_qb = raw_lo + 2 * GK_RANK
    o_ref[:, 0:_C_QB] = wt_ref[0:raw_lo, :].T.astype(BF16)
    o_ref[:, _C_QB:_C_LO] = wt_ref[raw_qb:AB_IN, :].T.astype(BF16)
    o_ref[:, _C_LO:AB_IN] = wt_ref[raw_lo:raw_lo + LANES, :].T[:, 0:2 * GK_RANK].astype(BF16)


def _prep_even(w_in_t):
    n = w_in_t.shape[0]
    return pl.pallas_call(
        _prep_even_kernel,
        grid=(n, D_MODEL // PREP_ROWS),
        in_specs=[pl.BlockSpec((None, AB_IN, PREP_ROWS), lambda l, r: (l, 0, r))],
        out_specs=pl.BlockSpec((None, PREP_ROWS, AB_IN), lambda l, r: (l, r, 0)),
        out_shape=jax.ShapeDtypeStruct((n, D_MODEL, AB_IN), BF16),
        compiler_params=_cparams(("parallel", "parallel")),
        name="prep_even",
    )(w_in_t)


def _prep_odd_kernel(wdt_ref, wuq_ref, wukv_ref, od_ref, ouq_ref, ouk_ref, ouvt_ref):
    n_ckv = Q_LORA + KV_LORA
    od_ref[:, 0:n_ckv] = wdt_ref[0:n_ckv, :].T.astype(BF16)
    tail = wdt_ref[DOWN_RAW_W - LANES:DOWN_RAW_W, :].T
    od_ref[:, n_ckv:DOWN_W] = jnp.concatenate(
        [jnp.zeros((D_MODEL, KR_OFF), F32), tail[:, LANES - ROPE_C:],
         jnp.zeros((D_MODEL, LANES - KR_OFF - ROPE_C), F32)], axis=1).astype(BF16)
    hd_q = NOPE_C + ROPE_C
    for h in range(H_C):
        ouq_ref[:, h * HEAD_PAD_C:(h + 1) * HEAD_PAD_C] = jnp.concatenate(
            [wuq_ref[:, h * hd_q:(h + 1) * hd_q], jnp.zeros((Q_LORA, HEAD_PAD_C - hd_q), F32)], axis=1).astype(BF16)
    wukv = wukv_ref[...]
    lane = lax.broadcasted_iota(jnp.int32, wukv.shape, 1)
    ouk_ref[...] = jnp.where(lane % HEAD_PAD_C < NOPE_C, wukv, 0.0).astype(BF16)
    wukv_t = wukv.T
    for h in range(H_C):
        ouvt_ref[h * V_C:(h + 1) * V_C, :] = wukv_t[h * HEAD_PAD_C + NOPE_C:(h + 1) * HEAD_PAD_C, :].astype(BF16)


def _prep_odd(w_down_t, w_uq, w_ukv):
    n = w_down_t.shape[0]
    spec = lambda r, c: pl.BlockSpec((None, r, c), lambda l: (l, 0, 0))
    return pl.pallas_call(
        _prep_odd_kernel,
        grid=(n,),
        in_specs=[spec(DOWN_RAW_W, D_MODEL), spec(Q_LORA, H_C * (NOPE_C + ROPE_C)), spec(KV_LORA, QC_W)],
        out_specs=[spec(D_MODEL, DOWN_W), spec(Q_LORA, QC_W), spec(KV_LORA, QC_W), spec(VC_W, KV_LORA)],
        out_shape=[jax.ShapeDtypeStruct((n, D_MODEL, DOWN_W), BF16), jax.ShapeDtypeStruct((n, Q_LORA, QC_W), BF16),
                   jax.ShapeDtypeStruct((n, KV_LORA, QC_W), BF16), jax.ShapeDtypeStruct((n, VC_W, KV_LORA), BF16)],
        compiler_params=_cparams(("parallel",)),
        name="prep_odd",
    )(w_down_t, w_uq, w_ukv)


def _mod_kernel(c_ref, w_ref, b_ref, o_ref):
    s = _silu(c_ref[...])
    o_ref[...] = _dot(s.astype(BF16), w_ref[...].astype(BF16)) + b_ref[...]


def _modulation(cvecs, w_mod, b_mod):
    return pl.pallas_call(
        _mod_kernel,
        grid=(DEPTH, 6 * D_MODEL // TN_MOD),
        in_specs=[
            pl.BlockSpec((N_MOD_ROWS, D_MODEL), lambda l, j: (0, 0)),
            pl.BlockSpec((None, D_MODEL, TN_MOD), lambda l, j: (l, 0, j)),
            pl.BlockSpec((None, 1, TN_MOD), lambda l, j: (l, 0, j)),
        ],
        out_specs=pl.BlockSpec((None, N_MOD_ROWS, TN_MOD), lambda l, j: (l, 0, j)),
        out_shape=jax.ShapeDtypeStruct((DEPTH, N_MOD_ROWS, 6 * D_MODEL), F32),
        compiler_params=_cparams(("parallel", "parallel")),
        name="adaln_mod",
    )(cvecs, w_mod, b_mod.reshape(DEPTH, 1, 6 * D_MODEL))


def _x_specs(n_x, tm):
    if n_x == 1:
        return [pl.BlockSpec((tm, D_MODEL), lambda i, *_: (i, 0))]
    return _split_specs(tm, D_MODEL)


def _load_x(n_x, refs, tm, rows=slice(None)):
    if n_x == 1:
        return refs[0][rows, :]
    return _load_split(refs[:2], tm, rows)


def _in_even_kernel(n_x, *refs):
    (mod_ref, g_ref, w_ref, wgk_ref, bgk_ref, cos_ref, sin_ref,
     qa_ref, ka_ref, va_ref, ga_ref, ld_ref, qb_ref, ldmin_ref, kbt_ref, vbt_ref, kbs_ref, vbs_ref) = refs[n_x:]
    x = _load_x(n_x, refs, TM_IN)
    h = _rms(x, g_ref[0:1, :]) * (1.0 + mod_ref[1:2, :]) + mod_ref[0:1, :]
    hb = h.astype(BF16)
    qa_ref[...] = _dot(hb, w_ref[:, _C_QA:_C_KA]) * (DK_A ** -0.5)
    ka_ref[...] = _dot(hb, w_ref[:, _C_KA:_C_VA])
    va_ref[...] = _dot(hb, w_ref[:, _C_VA:_C_GA]).astype(BF16)
    ga_ref[...] = _dot(hb, w_ref[:, _C_GA:_C_QB])
    cos = cos_ref[...]
    sin = sin_ref[...]
    qb = _dot(hb, w_ref[:, _C_QB:_C_KB])
    for j in range(QB_W // LANES):
        qj = qb[:, j * LANES:(j + 1) * LANES]
        qb_ref[:, j * LANES:(j + 1) * LANES] = (_rope(qj, cos, sin, D_B // 4) * (D_B ** -0.5)).astype(BF16)
    kvb = _dot(hb, w_ref[:, _C_KB:_C_LO])
    kb = _rope(kvb[:, :KB_W], cos, sin, D_B // 4)
    vb = kvb[:, KB_W:]
    lo = _dot(hb, w_ref[:, _C_LO:AB_IN]).astype(BF16)
    z = _dot(lo, wgk_ref[...]) + bgk_ref[...]
    ld = (jnp.minimum(z, 0.0) - jnp.log(1.0 + jnp.exp(-jnp.abs(z)))) * (1.0 / GATE_NORM)
    ld_ref[...] = ld
    ld_min = jnp.min(jnp.min(ld, axis=0, keepdims=True), axis=1, keepdims=True)
    ldmin_ref[...] = jnp.broadcast_to(ld_min, ldmin_ref.shape)

    is_prompt = _is_prompt_tile(TM_IN)

    @pl.when(is_prompt)
    def _():
        for s in range(TM_IN // SEQ):
            kbt_ref[s] = kb[s * SEQ:(s + 1) * SEQ, :].T
            vbt_ref[s] = vb[s * SEQ:(s + 1) * SEQ, :].T

    @pl.when(jnp.logical_not(is_prompt))
    def _():
        kbs_ref[...] = kb
        vbs_ref[...] = vb


def _in_even(xs, layer, mods, g_norm, w_in, w_gk, b_gk, cos, sin):
    tm = TM_IN
    i_ab = layer // 2
    n_prompt_tiles = N_PROMPT // tm
    seq_per_tile = tm // SEQ
    row = lambda i: (i, 0)
    widths = (QA_W, QA_W, VA_W, VA_W, 2 * QA_W, QB_W)
    dtypes = (F32, F32, BF16, F32, F32, BF16)
    kv_specs = 2 * [pl.BlockSpec((seq_per_tile, KB_W, SEQ), lambda i: (jnp.minimum(i, n_prompt_tiles - 1), 0, 0))] + \
        2 * [pl.BlockSpec((tm, KB_W), lambda i: (jnp.maximum(i - n_prompt_tiles, 0), 0))]
    kv_shapes = 2 * [jax.ShapeDtypeStruct((BATCH, KB_W, SEQ), F32)] + 2 * [jax.ShapeDtypeStruct((N_SAMPLE, KB_W), F32)]
    return pl.pallas_call(
        functools.partial(_in_even_kernel, len(xs)),
        grid=(N_TOK // tm,),
        in_specs=_x_specs(len(xs), tm) + [
            _mod_spec(layer, tm),
            _layer_spec((4, D_MODEL), layer),
            _layer_spec((D_MODEL, AB_IN), i_ab),
            _layer_spec((2 * GK_RANK, 2 * QA_W), i_ab),
            _layer_spec((1, 2 * QA_W), i_ab),
            pl.BlockSpec((tm, LANES), row),
            pl.BlockSpec((tm, LANES), row),
        ],
        out_specs=[pl.BlockSpec((tm, w), row) for w in widths] + [
            pl.BlockSpec((None, 8, LANES), lambda i: (i, 0, 0))] + kv_specs,
        out_shape=[jax.ShapeDtypeStruct((N_TOK, w), d) for w, d in zip(widths, dtypes)] + [
            jax.ShapeDtypeStruct((N_TOK // tm, 8, LANES), F32)] + kv_shapes,
        compiler_params=_cparams(("arbitrary",)),
        name="in_even",
    )(*xs, mods, g_norm, w_in, w_gk, b_gk, cos, sin)


def _gla_kernel(seq_len, has_s0, exact, *refs):
    qa_ref, ka_ref, va_ref, ga_ref, ld_ref, gg_ref = refs[:6]
    s0_refs = refs[6:8] if has_s0 else None
    o_ref, stf_ref, stb_ref = refs[8:11] if has_s0 else refs[6:9]
    st_ref, qi_ref, ki_ref, qcat_ref, ks_ref, dec_ref, stcat_ref, acc_ref = refs[-8:]
    b_ref = qi_ref if exact else None
    n_groups = seq_len // GLA_GROUP
    cpg = GLA_GROUP // GLA_CHUNK
    n_chunks = seq_len // GLA_CHUNK
    n_pairs = H_A // 2
    pair_k = 2 * DK_A
    pair_v = 2 * DV_A

    def chunk_masks():
        r_i = lax.broadcasted_iota(jnp.int32, (GLA_GROUP, GLA_GROUP), 0)
        c_i = lax.broadcasted_iota(jnp.int32, (GLA_GROUP, GLA_GROUP), 1)
        same = (r_i // GLA_CHUNK) == (c_i // GLA_CHUNK)
        return same, same & (c_i <= r_i), same & (c_i >= r_i)

    def scale_group(j, carry):
        same, mask_f, mask_b = chunk_masks()
        t_cum = (_ones_where(mask_f), _ones_where(mask_b))
        t_all = _ones_where(same)
        s_r = lax.broadcasted_iota(jnp.int32, (8, GLA_GROUP), 0)
        s_c = lax.broadcasted_iota(jnp.int32, (8, GLA_GROUP), 1)
        t_sel = _ones_where(s_r == s_c // GLA_CHUNK)
        rows = pl.ds(pl.multiple_of(j * GLA_GROUP, GLA_GROUP), GLA_GROUP)
        q = qa_ref[rows, :]
        k = ka_ref[rows, :]
        for d in range(2):
            parts = _split3(ld_ref[rows, d * QA_W:(d + 1) * QA_W])
            b = _dot3(t_cum[d], parts)
            tot = _dot3(t_all, parts)
            dec8 = jnp.exp(_dot3(t_sel, parts))
            ref = 0.5 * tot
            if exact:
                b_ref[d, rows, :] = b
            else:
                qi_ref[d, rows, :] = (q * jnp.exp(b - ref)).astype(BF16)
                ki_ref[d, rows, :] = (k * jnp.exp(ref - b)).astype(BF16)
            ks_ref[d, rows, :] = (k * jnp.exp(tot - b)).astype(BF16)
            q_inter = (q * jnp.exp(b)).astype(BF16)
            for p in range(n_pairs):
                qcat_ref[rows, p * 2 * pair_k + d * pair_k:p * 2 * pair_k + (d + 1) * pair_k] = (
                    q_inter[:, p * pair_k:(p + 1) * pair_k])
            for cc in range(cpg):
                dec_ref[d, j * cpg + cc] = jnp.broadcast_to(dec8[cc:cc + 1, :], (8, QA_W))
        return carry

    lax.fori_loop(0, n_groups, scale_group, 0)

    zpad = jnp.zeros((DK_A, DV_A), F32)
    for d in range(2):
        for p in range(n_pairs):
            if has_s0:
                top = jnp.concatenate([s0_refs[d][2 * p], zpad], axis=0).T
                bot = jnp.concatenate([zpad, s0_refs[d][2 * p + 1]], axis=0).T
                st_ref[d, p] = jnp.concatenate([top, bot], axis=0)
            else:
                st_ref[d, p] = jnp.zeros((pair_v, pair_k), F32)

    def scan_chunk(c, carry):
        bd_r = lax.broadcasted_iota(jnp.int32, (pair_v, pair_k), 0)
        bd_c = lax.broadcasted_iota(jnp.int32, (pair_v, pair_k), 1)
        bd_mask = (bd_r // DV_A) == (bd_c // DK_A)
        for d in range(2):
            cd = c if d == 0 else n_chunks - 1 - c
            rows = pl.ds(pl.multiple_of(cd * GLA_CHUNK, GLA_CHUNK), GLA_CHUNK)
            for p in range(n_pairs):
                st = st_ref[d, p]
                stcat_ref[cd, p, :, d * pair_k:(d + 1) * pair_k] = st.astype(BF16)
                u = _dot_tn(va_ref[rows, p * pair_v:(p + 1) * pair_v], ks_ref[d, rows, p * pair_k:(p + 1) * pair_k])
                dec = dec_ref[d, cd][0:1, p * pair_k:(p + 1) * pair_k]
                st_ref[d, p] = st * dec + jnp.where(bd_mask, u, 0.0)
        return carry

    lax.fori_loop(0, n_chunks, scan_chunk, 0, unroll=GLA_SCAN_UNROLL)

    for d, out_ref in ((0, stf_ref), (1, stb_ref)):
        for p in range(n_pairs):
            st = st_ref[d, p]
            out_ref[2 * p] = st[0:DV_A, :].T[0:DK_A, :]
            out_ref[2 * p + 1] = st[DV_A:pair_v, :].T[DK_A:pair_k, :]

    def exact_intra(row0, h):
        p, hh = divmod(h, 2)
        lanes = slice(p * pair_k, (p + 1) * pair_k)
        head_lanes = (lax.broadcasted_iota(jnp.int32, (GLA_CHUNK, pair_k), 1) // DK_A) == hh
        s_idx = lax.broadcasted_iota(jnp.int32, (GLA_CHUNK, GLA_CHUNK), 0)
        t_idx = lax.broadcasted_iota(jnp.int32, (GLA_CHUNK, GLA_CHUNK), 1)
        outs = []
        for cc in range(cpg):
            r0 = row0 + cc * GLA_CHUNK
            crow = pl.ds(pl.multiple_of(r0, GLA_CHUNK), GLA_CHUNK)
            kc = ka_ref[crow, lanes]
            att_t = jnp.zeros((GLA_CHUNK, GLA_CHUNK), F32)
            for d in range(2):
                bc = b_ref[d, crow, lanes]

                def row_step(t, att_t, d=d, bc=bc, kc=kc, r0=r0):
                    grp = pl.ds(pl.multiple_of(r0 + (t // 8) * 8, 8), 8)
                    pick = lax.broadcasted_iota(jnp.int32, (8, pair_k), 0) == t % 8
                    bt = jnp.sum(jnp.where(pick, b_ref[d, grp, lanes], 0.0), axis=0, keepdims=True)
                    qt = jnp.sum(jnp.where(pick, qa_ref[grp, lanes], 0.0), axis=0, keepdims=True)
                    w = jnp.where(head_lanes, qt * kc * jnp.exp(jnp.minimum(bt - bc, 0.0)), 0.0)
                    col = jnp.sum(w, axis=1, keepdims=True)
                    allowed = (s_idx <= t) if d == 0 else (s_idx >= t)
                    return jnp.where((t_idx == t) & allowed, att_t + col, att_t)

                att_t = lax.fori_loop(0, GLA_CHUNK, row_step, att_t)
            outs.append(_dot_tn(att_t.astype(BF16), va_ref[crow, h * DV_A:(h + 1) * DV_A]))
        return jnp.concatenate(outs, axis=0)

    def out_group(j, carry):
        _, mask_f, mask_b = chunk_masks()
        lane = lax.broadcasted_iota(jnp.int32, (GLA_GROUP, pair_k), 1)
        rows = pl.ds(pl.multiple_of(j * GLA_GROUP, GLA_GROUP), GLA_GROUP)
        for p in range(n_pairs):
            for cc in range(cpg):
                c = j * cpg + cc
                crow = pl.ds(pl.multiple_of(c * GLA_CHUNK, GLA_CHUNK), GLA_CHUNK)
                acc_ref[crow, p * pair_v:(p + 1) * pair_v] = _dot_nt(
                    qcat_ref[crow, p * 2 * pair_k:(p + 1) * 2 * pair_k], stcat_ref[c, p])
        if exact:
            intras = [exact_intra(j * GLA_GROUP, h) for h in range(H_A)]
        else:
            atts = []
            for h in range(H_A):
                p, hh = divmod(h, 2)
                head_lanes = _ones_where((lane // DK_A) == hh)
                att = None
                for d, mask in ((0, mask_f), (1, mask_b)):
                    qm = qi_ref[d, rows, p * pair_k:(p + 1) * pair_k] * head_lanes
                    a = jnp.where(mask, _dot_nt(qm, ki_ref[d, rows, p * pair_k:(p + 1) * pair_k]), 0.0)
                    att = a if att is None else att + a
                atts.append(att.astype(BF16))
            intras = [_dot(atts[h], va_ref[rows, h * DV_A:(h + 1) * DV_A]) for h in range(H_A)]
        for h in range(H_A):
            o = acc_ref[rows, h * DV_A:(h + 1) * DV_A] + intras[h]
            gate = _silu(ga_ref[rows, h * DV_A:(h + 1) * DV_A])
            o_ref[rows, h * DV_A:(h + 1) * DV_A] = (_rms(o, gg_ref[...]) * gate).astype(BF16)
        return carry

    lax.fori_loop(0, n_groups, out_group, 0)


def _gla(seq_len, n_seq, row_block0, qa, ka, va, ga, ld, g_gla, i_ab, s0=None, exact=False):
    has_s0 = s0 is not None
    n_chunks = seq_len // GLA_CHUNK
    n_pairs = H_A // 2
    rows = lambda b: (row_block0 + b, 0)
    st_spec = pl.BlockSpec((None, H_A, DK_A, DV_A), lambda b: (b, 0, 0, 0))
    in_specs = [
        pl.BlockSpec((seq_len, QA_W), rows),
        pl.BlockSpec((seq_len, QA_W), rows),
        pl.BlockSpec((seq_len, VA_W), rows),
        pl.BlockSpec((seq_len, VA_W), rows),
        pl.BlockSpec((seq_len, 2 * QA_W), rows),
        _layer_spec((1, DV_A), i_ab),
    ]
    args = [qa, ka, va, ga, ld, g_gla]
    if has_s0:
        s0_spec = pl.BlockSpec((None, None, H_A, DK_A, DV_A), lambda b: (b, i_ab, 0, 0, 0))
        in_specs += [s0_spec, s0_spec]
        args += list(s0)
    st_shape = jax.ShapeDtypeStruct((n_seq, H_A, DK_A, DV_A), F32)
    return pl.pallas_call(
        functools.partial(_gla_kernel, seq_len, has_s0, exact),
        grid=(n_seq,),
        in_specs=in_specs,
        out_specs=[pl.BlockSpec((seq_len, VA_W), lambda b: (b, 0)), st_spec, st_spec],
        out_shape=[jax.ShapeDtypeStruct((n_seq * seq_len, VA_W), BF16), st_shape, st_shape],
        scratch_shapes=[
            pltpu.VMEM((2, n_pairs, 2 * DV_A, 2 * DK_A), F32),
            pltpu.VMEM((2, seq_len, QA_W), F32 if exact else BF16),
            pltpu.VMEM((2, seq_len, QA_W), BF16),
            pltpu.VMEM((seq_len, 2 * QA_W), BF16),
            pltpu.VMEM((2, seq_len, QA_W), BF16),
            pltpu.VMEM((2, n_chunks, 8, QA_W), F32),
            pltpu.VMEM((n_chunks, n_pairs, 2 * DV_A, 4 * DK_A), BF16),
            pltpu.VMEM((seq_len, VA_W), F32),
        ],
        compiler_params=_cparams(("parallel",)),
        name=("gla_s" if has_s0 else "gla_p") + ("_exact" if exact else ""),
    )(*args)


def _swa_head_softmax(pieces, sink):
    m = sink
    for s, _, _ in pieces:
        m = jnp.maximum(m, jnp.max(s, axis=-1, keepdims=True))
    den = jnp.exp(sink - m)
    acc = None
    for s, v, transposed in pieces:
        e = jnp.exp(s - m)
        den = den + jnp.sum(e, axis=-1, keepdims=True)
        pv = _dot_nt(e.astype(BF16), v) if transposed else _dot(e.astype(BF16), v)
        acc = pv if acc is None else acc + pv
    return acc / den


def _dup_groups(x):
    lane = lax.broadcasted_iota(jnp.int32, x.shape, 1)
    swapped = pltpu.roll(x, D_B, 1)
    low = lane < D_B
    return jnp.where(low, x, swapped).astype(BF16), jnp.where(low, swapped, x).astype(BF16)


def _dup_groups_t(xt):
    g0, g1 = xt[0:D_B, :], xt[D_B:, :]
    return jnp.concatenate([g0, g0], axis=0).astype(BF16), jnp.concatenate([g1, g1], axis=0).astype(BF16)


def _swa_heads(sink_ref, i_ab, q_ref, kv_pieces, o_ref, n_rows):
    lane = lax.broadcasted_iota(jnp.int32, (n_rows, LANES), 1)
    head_pieces = []
    for h in range(H_B):
        j, hh = divmod(h, 2)
        g = h // G_B
        qm = q_ref[:, j * LANES:(j + 1) * LANES] * _ones_where((lane // D_B) == hh)
        pieces = []
        for keys, vals, transposed, mask in kv_pieces:
            s = _dot(qm, keys[g]) if transposed else _dot_nt(qm, keys[g])
            if mask is not None:
                s = jnp.where(mask, s, NEG_BIG)
            pieces.append((s, vals[g], transposed))
        head_pieces.append(pieces)
    outs = [_swa_head_softmax(pieces, sink_ref[i_ab, h]) for h, pieces in enumerate(head_pieces)]
    for j in range(H_B // 2):
        o_ref[:, j * LANES:(j + 1) * LANES] = jnp.where(lane < D_B, outs[2 * j], outs[2 * j + 1]).astype(BF16)


def _swa_prompt_kernel(i_ab, sink_ref, q_ref, kt_ref, vt_ref, o_ref):
    _swa_heads(sink_ref, i_ab, q_ref, [(_dup_groups_t(kt_ref[...]), _dup_groups_t(vt_ref[...]), True, None)],
               o_ref, SEQ)


def _swa_prompt(sink, i_ab, qb, kbt, vbt):
    seq = lambda b: (b, 0)
    seq_t = pl.BlockSpec((None, KB_W, SEQ), lambda b: (b, 0, 0))
    return pl.pallas_call(
        functools.partial(_swa_prompt_kernel, i_ab),
        grid=(BATCH,),
        in_specs=[pl.BlockSpec(memory_space=pltpu.SMEM), pl.BlockSpec((SEQ, QB_W), seq), seq_t, seq_t],
        out_specs=pl.BlockSpec((SEQ, QB_W), seq),
        out_shape=jax.ShapeDtypeStruct((N_PROMPT, QB_W), BF16),
        compiler_params=_cparams(("parallel",)),
        name="swa_p",
    )(sink, qb, kbt, vbt)


def _swa_sample_kernel(i_ab, sink_ref, q_ref, k_ref, v_ref, kct_ref, vct_ref, o_ref):
    n = pl.program_id(1)
    start = pl.multiple_of(jnp.clip((n - 1) * SWA_QB, 0, DEC_SEQ - SWA_WIN), SWA_QB)
    local = (_dup_groups(k_ref[pl.ds(start, SWA_WIN), :]), _dup_groups(v_ref[pl.ds(start, SWA_WIN), :]))
    ctx = (_dup_groups_t(kct_ref[...]), _dup_groups_t(vct_ref[...]))
    qi = n * SWA_QB + lax.broadcasted_iota(jnp.int32, (SWA_QB, SWA_WIN), 0)
    ki = start + lax.broadcasted_iota(jnp.int32, (SWA_QB, SWA_WIN), 1)
    band = jnp.abs(qi - ki) <= WINDOW
    _swa_heads(sink_ref, i_ab, q_ref, [ctx + (True, None), local + (False, band)], o_ref, SWA_QB)


def _swa_sample(sink, i_ab, qb, kb, vb, kc, vc):
    nqb = DEC_SEQ // SWA_QB
    q0 = N_PROMPT // SWA_QB
    return pl.pallas_call(
        functools.partial(_swa_sample_kernel, i_ab),
        grid=(DEC_BATCH, nqb),
        in_specs=[
            pl.BlockSpec(memory_space=pltpu.SMEM),
            pl.BlockSpec((SWA_QB, QB_W), lambda b, n: (q0 + b * nqb + n, 0)),
            pl.BlockSpec((DEC_SEQ, KB_W), lambda b, n: (b, 0)),
            pl.BlockSpec((DEC_SEQ, KB_W), lambda b, n: (b, 0)),
            pl.BlockSpec((None, None, KB_W, PAST_LEN), lambda b, n: (b, i_ab, 0, 0)),
            pl.BlockSpec((None, None, KB_W, PAST_LEN), lambda b, n: (b, i_ab, 0, 0)),
        ],
        out_specs=pl.BlockSpec((SWA_QB, QB_W), lambda b, n: (b * nqb + n, 0)),
        out_shape=jax.ShapeDtypeStruct((N_SAMPLE, QB_W), BF16),
        compiler_params=_cparams(("parallel", "parallel")),
        name="swa_s",
    )(sink, qb, kb, vb, kc, vc)


def _in_odd_kernel(x_ref, mod_ref, g_ref, wd_ref, gq_ref, gkv_ref, wuq_ref, wuk_ref, wuvt_ref, cos_ref, sin_ref,
                   q_ref, k_ref, vt_ref, ckv_ref, krt_ref):
    h = _rms(x_ref[...], g_ref[0:1, :]) * (1.0 + mod_ref[1:2, :]) + mod_ref[0:1, :]
    hb = h.astype(BF16)
    cos = cos_ref[...]
    sin = sin_ref[...]
    c_q = _dot(hb, wd_ref[:, 0:Q_LORA])
    c_kv = _rms(_dot(hb, wd_ref[:, Q_LORA:Q_LORA + KV_LORA]), gkv_ref[...])
    kr = _rope(_dot(hb, wd_ref[:, Q_LORA + KV_LORA:DOWN_W]), cos, sin, ROPE_C // 4)
    cqb = _rms(c_q, gq_ref[...]).astype(BF16)
    ckvb = c_kv.astype(BF16)
    scale = (NOPE_C + ROPE_C) ** -0.5
    group_w = HEADS_PER_DOT_C * HEAD_PAD_C
    for grp in range(H_C // HEADS_PER_DOT_C):
        gsl = slice(grp * group_w, (grp + 1) * group_w)
        qg = _dot(cqb, wuq_ref[:, gsl])
        kg = _dot(ckvb, wuk_ref[:, gsl])
        for j in range(HEADS_PER_DOT_C):
            sl = slice(j * HEAD_PAD_C, (j + 1) * HEAD_PAD_C)
            osl = slice(grp * group_w + j * HEAD_PAD_C, grp * group_w + (j + 1) * HEAD_PAD_C)
            q_ref[:, osl] = (_rope(qg[:, sl], cos, sin, ROPE_C // 4) * scale).astype(BF16)
            k_ref[:, osl] = (kg[:, sl] + kr).astype(BF16)
    vt_ref[...] = _dot_nt(wuvt_ref[...], ckvb).astype(BF16)

    @pl.when(_is_prompt_tile(TM_IN))
    def _():
        ckv_ref[...] = c_kv
        for s in range(TM_IN // SEQ):
            krt_ref[s] = kr[s * SEQ:(s + 1) * SEQ, :].T[KR_OFF:KR_OFF + ROPE_C, :]


def _in_odd(x, layer, mods, g_norm, w_down, g_q, g_kv, w_uq, w_uk, w_uvt, cos, sin):
    tm = TM_IN
    i_c = layer // 2
    n_prompt_tiles = N_PROMPT // tm
    row = lambda i: (i, 0)
    prompt_row = lambda i: (jnp.minimum(i, n_prompt_tiles - 1), 0)
    prompt_seq = lambda i: (jnp.minimum(i, n_prompt_tiles - 1), 0, 0)
    return pl.pallas_call(
        _in_odd_kernel,
        grid=(N_TOK // tm,),
        in_specs=[
            pl.BlockSpec((tm, D_MODEL), row),
            _mod_spec(layer, tm),
            _layer_spec((4, D_MODEL), layer),
            _layer_spec((D_MODEL, DOWN_W), i_c),
            _layer_spec((1, Q_LORA), i_c),
            _layer_spec((1, KV_LORA), i_c),
            _layer_spec((Q_LORA, QC_W), i_c),
            _layer_spec((KV_LORA, QC_W), i_c),
            _layer_spec((VC_W, KV_LORA), i_c),
            pl.BlockSpec((tm, LANES), row),
            pl.BlockSpec((tm, LANES), row),
        ],
        out_specs=[pl.BlockSpec((tm, QC_W), row), pl.BlockSpec((tm, QC_W), row),
                   pl.BlockSpec((VC_W, tm), lambda i: (0, i)),
                   pl.BlockSpec((tm, KV_LORA), prompt_row), pl.BlockSpec((tm // SEQ, ROPE_C, SEQ), prompt_seq)],
        out_shape=[jax.ShapeDtypeStruct((N_TOK, QC_W), BF16), jax.ShapeDtypeStruct((N_TOK, QC_W), BF16),
                   jax.ShapeDtypeStruct((VC_W, N_TOK), BF16),
                   jax.ShapeDtypeStruct((N_PROMPT, KV_LORA), F32), jax.ShapeDtypeStruct((BATCH, ROPE_C, SEQ), F32)],
        compiler_params=_cparams(("arbitrary",)),
        name="in_odd",
    )(x, mods, g_norm, w_down, g_q, g_kv, w_uq, w_uk, w_uvt, cos, sin)


def _reduce_rows(x, op, reduce_fn):
    while x.shape[0] % 16 == 0:
        half = x.shape[0] // 2
        x = op(x[:half], x[half:])
    return reduce_fn(x, axis=0, keepdims=True)


def _mla_heads(q_ref, kv_pieces, o_ref, ot_ref, group):
    for h0 in range(0, H_C, group):
        heads = range(h0, h0 + group)
        scores = []
        for hd in heads:
            sl = slice(hd * HEAD_PAD_C, (hd + 1) * HEAD_PAD_C)
            scores.append([_dot_nt(k_ref[:, sl], q_ref[:, sl]) for k_ref, _ in kv_pieces])
        exps, dens = [], []
        for per_piece in scores:
            m = None
            for s in per_piece:
                sm = _reduce_rows(s, jnp.maximum, jnp.max)
                m = sm if m is None else jnp.maximum(m, sm)
            es = [jnp.exp(s - m) for s in per_piece]
            den = None
            for e in es:
                part = _reduce_rows(e, jnp.add, jnp.sum)
                den = part if den is None else den + part
            exps.append([e.astype(BF16) for e in es])
            dens.append(den)
        for hd, es, den in zip(heads, exps, dens):
            acc = None
            for e, (_, vt_ref) in zip(es, kv_pieces):
                pv = _dot(vt_ref[hd * V_C:(hd + 1) * V_C, :], e)
                acc = pv if acc is None else acc + pv
            ot_ref[hd * V_C:(hd + 1) * V_C, :] = acc / den
    o_ref[...] = ot_ref[...].T.astype(BF16)


def _mla_prompt_kernel(q_ref, k_ref, vt_ref, o_ref, ot_ref):
    _mla_heads(q_ref, [(k_ref, vt_ref)], o_ref, ot_ref, MLA_HEAD_GROUP_P)


def _mla_prompt(q, k, vt):
    seq = lambda b: (b, 0)
    return pl.pallas_call(
        _mla_prompt_kernel,
        grid=(BATCH,),
        in_specs=[pl.BlockSpec((SEQ, QC_W), seq), pl.BlockSpec((SEQ, QC_W), seq),
                  pl.BlockSpec((VC_W, SEQ), lambda b: (0, b))],
        out_specs=pl.BlockSpec((SEQ, VC_W), seq),
        out_shape=jax.ShapeDtypeStruct((N_PROMPT, VC_W), BF16),
        scratch_shapes=[pltpu.VMEM((VC_W, SEQ), F32)],
        compiler_params=_cparams(("parallel",)),
        name="mla_p",
    )(q, k, vt)


def _mla_sample_kernel(q_ref, k_ref, vt_ref, ckv_ref, krt_ref, wuk_ref, wuvt_ref, o_ref, kc_ref, vct_ref, ot_ref):
    @pl.when(pl.program_id(1) == 0)
    def _():
        cb = ckv_ref[...].astype(BF16)
        kr = jnp.concatenate([jnp.zeros((KR_OFF, PAST_LEN), F32), krt_ref[...],
                              jnp.zeros((LANES - KR_OFF - ROPE_C, PAST_LEN), F32)], axis=0).T
        group_w = HEADS_PER_DOT_C * HEAD_PAD_C
        for grp in range(H_C // HEADS_PER_DOT_C):
            kg = _dot(cb, wuk_ref[:, grp * group_w:(grp + 1) * group_w])
            for j in range(HEADS_PER_DOT_C):
                osl = slice(grp * group_w + j * HEAD_PAD_C, grp * group_w + (j + 1) * HEAD_PAD_C)
                kc_ref[:, osl] = (kg[:, j * HEAD_PAD_C:(j + 1) * HEAD_PAD_C] + kr).astype(BF16)
        vct_ref[...] = _dot_nt(wuvt_ref[...], cb).astype(BF16)

    _mla_heads(q_ref, [(kc_ref, vct_ref), (k_ref, vt_ref)], o_ref, ot_ref, MLA_HEAD_GROUP_S)


def _mla_sample(q, k, vt, ckv_ctx, kr_ctx, w_uk, w_uvt, i_c):
    nqb = DEC_SEQ // MLA_QB
    q0 = N_PROMPT // MLA_QB
    s0 = N_PROMPT // DEC_SEQ
    return pl.pallas_call(
        _mla_sample_kernel,
        grid=(DEC_BATCH, nqb),
        in_specs=[
            pl.BlockSpec((MLA_QB, QC_W), lambda b, n: (q0 + b * nqb + n, 0)),
            pl.BlockSpec((DEC_SEQ, QC_W), lambda b, n: (s0 + b, 0)),
            pl.BlockSpec((VC_W, DEC_SEQ), lambda b, n: (0, s0 + b)),
            pl.BlockSpec((None, None, PAST_LEN, KV_LORA), lambda b, n: (b, i_c, 0, 0)),
            pl.BlockSpec((None, None, ROPE_C, PAST_LEN), lambda b, n: (b, i_c, 0, 0)),
            _layer_spec((KV_LORA, QC_W), i_c),
            _layer_spec((VC_W, KV_LORA), i_c),
        ],
        out_specs=pl.BlockSpec((MLA_QB, VC_W), lambda b, n: (b * nqb + n, 0)),
        out_shape=jax.ShapeDtypeStruct((N_SAMPLE, VC_W), BF16),
        scratch_shapes=[pltpu.VMEM((PAST_LEN, QC_W), BF16), pltpu.VMEM((VC_W, PAST_LEN), BF16),
                        pltpu.VMEM((VC_W, MLA_QB), F32)],
        compiler_params=_cparams(("parallel", "arbitrary")),
        name="mla_s",
    )(q, k, vt, ckv_ctx, kr_ctx, w_uk, w_uvt)


def _out_kernel(n_x, *refs):
    a_refs = refs[n_x + 2:n_x + 4]
    b_refs = refs[n_x + 4:n_x + 6]
    mod_ref, g_ref = refs[n_x:n_x + 2]
    wo_ref, w1_ref, w2_ref = refs[n_x + 6:n_x + 9]
    o_ref, h2_ref, acc_ref = refs[n_x + 9:]
    x1_ref = o_ref
    kk = pl.program_id(1)
    half = wo_ref.shape[0] // 2

    row_chunks = [slice(r * OUT_PROLOGUE_ROWS, (r + 1) * OUT_PROLOGUE_ROWS) for r in range(TM_OUT // OUT_PROLOGUE_ROWS)]

    @pl.when(kk == 0)
    def _():
        wo_a = wo_ref[0:half, :].astype(BF16)
        wo_b = wo_ref[half:, :].astype(BF16)
        gate_g1 = mod_ref[2:3, :] * g_ref[1:2, :]
        scale_g2 = g_ref[2:3, :] * (1.0 + mod_ref[4:5, :])
        for rows in row_chunks:
            a = _load_split(a_refs, TM_OUT, rows)
            b = _load_split(b_refs, TM_OUT, rows)
            mix = _dot(a, wo_a) + _dot(b, wo_b)
            x1 = _load_x(n_x, refs, TM_OUT, rows) + _rms(mix, gate_g1)
            x1_ref[rows, :] = x1
            h2_ref[rows, :] = (_rms(x1, scale_g2) + mod_ref[3:4, :]).astype(BF16)
        acc_ref[...] = jnp.zeros(acc_ref.shape, F32)

    def ffn(rows, w1, w2):
        hid = jnp.maximum(_dot(h2_ref[rows, :], w1), 0.0)
        return _dot((hid * hid).astype(BF16), w2)

    is_last = kk == pl.num_programs(1) - 1

    @pl.when(jnp.logical_not(is_last))
    def _():
        acc_ref[...] += ffn(slice(None), w1_ref[...].astype(BF16), w2_ref[...].astype(BF16))

    @pl.when(is_last)
    def _():
        w1 = w1_ref[...].astype(BF16)
        w2 = w2_ref[...].astype(BF16)
        gate_g3 = mod_ref[5:6, :] * g_ref[3:4, :]
        for rows in row_chunks:
            o_ref[rows, :] = x1_ref[rows, :] + _rms(acc_ref[rows, :] + ffn(rows, w1, w2), gate_g3)


def _out_layer(xs, layer, mods, g_norm, mix_a, mix_b, b_col, w_o, i_o, w_ff1, w_ff2):
    tm = TM_OUT
    half = D_MODEL // 2
    return pl.pallas_call(
        functools.partial(_out_kernel, len(xs)),
        grid=(N_TOK // tm, D_FF // TK_FF),
        in_specs=_x_specs(len(xs), tm) + [
            _mod_spec(layer, tm),
            _layer_spec((4, D_MODEL), layer),
        ] + _split_specs(tm, half, 0) + _split_specs(tm, half, b_col) + [
            _layer_spec((D_MODEL, D_MODEL), i_o),
            pl.BlockSpec((None, D_MODEL, TK_FF), lambda i, k: (layer, 0, k)),
            pl.BlockSpec((None, TK_FF, D_MODEL), lambda i, k: (layer, k, 0)),
        ],
        out_specs=pl.BlockSpec((tm, D_MODEL), lambda i, k: (i, 0)),
        out_shape=jax.ShapeDtypeStruct((N_TOK, D_MODEL), F32),
        scratch_shapes=[pltpu.VMEM((tm, D_MODEL), BF16), pltpu.VMEM((tm, D_MODEL), F32)],
        compiler_params=_cparams(("parallel", "arbitrary")),
        name="out_mlp",
    )(*xs, mods, g_norm, *mix_a, *mix_b, w_o, w_ff1, w_ff2)


def _rope_tables(head_dim, lane_off, group):
    nf = head_dim // 4
    n_rows = DEC_SEQ // GRID_W
    rows = jnp.repeat(jnp.arange(n_rows, dtype=F32), GRID_W)
    cols = jnp.tile(jnp.arange(GRID_W, dtype=F32), n_rows)
    inv = ROPE_BASE ** (-jnp.arange(nf, dtype=F32) / nf)
    ang = jnp.stack([rows[:, None] * inv, cols[:, None] * inv], axis=1)
    cos = jnp.broadcast_to(jnp.cos(ang)[:, :, None, :], (DEC_SEQ, 2, 2, nf)).reshape(DEC_SEQ, head_dim)
    sin = jnp.sin(ang)
    sin = jnp.stack([-sin, sin], axis=2).reshape(DEC_SEQ, head_dim)
    cos_g = jnp.ones((DEC_SEQ, group), F32).at[:, lane_off:lane_off + head_dim].set(cos)
    sin_g = jnp.zeros((DEC_SEQ, group), F32).at[:, lane_off:lane_off + head_dim].set(sin)
    reps = LANES // group
    cos_s = jnp.tile(jnp.tile(cos_g, (1, reps)), (DEC_BATCH, 1))
    sin_s = jnp.tile(jnp.tile(sin_g, (1, reps)), (DEC_BATCH, 1))
    cos_t = jnp.concatenate([jnp.ones((N_PROMPT, LANES), F32), cos_s], axis=0)
    sin_t = jnp.concatenate([jnp.zeros((N_PROMPT, LANES), F32), sin_s], axis=0)
    return cos_t, sin_t


def kernel(x_prompt, x_sample, state_gla_fwd, state_gla_bwd, cache_swa_k, cache_swa_v, cache_mla_ckv, cache_mla_kr, c, c_ctx, w_mod, b_mod, g_norm, w_ff1, w_ff2, w_in_ab, w_gk_f, b_gk_f, w_gk_b, b_gk_b, g_gla, swa_sink, w_out_ab, w_mla_down, g_mla_q, g_mla_kv, w_mla_uq, w_mla_ukv, w_mla_o):
    xs = (x_prompt.reshape(N_PROMPT, D_MODEL), x_sample.reshape(N_SAMPLE, D_MODEL))
    cvecs = jnp.concatenate([c_ctx[None, :], c, jnp.zeros((N_MOD_ROWS - 1 - DEC_BATCH, D_MODEL), F32)], axis=0)
    mods = _modulation(cvecs, w_mod, b_mod).reshape(DEPTH, N_MOD_ROWS, 6, D_MODEL)

    cos_b, sin_b = _rope_tables(D_B, 0, D_B)
    cos_c, sin_c = _rope_tables(ROPE_C, KR_OFF, HEAD_PAD_C)
    p_blk = N_PROMPT // DEC_SEQ

    n_ab = w_in_ab.shape[0]
    w_in = _prep_even(jnp.swapaxes(w_in_ab, 1, 2))
    zgk = jnp.zeros((n_ab, GK_RANK, QA_W), F32)
    w_gk = jnp.concatenate([jnp.concatenate([w_gk_f, zgk], axis=2),
                            jnp.concatenate([zgk, w_gk_b], axis=2)], axis=1).astype(BF16)
    b_gk = jnp.concatenate([b_gk_f, b_gk_b], axis=1)[:, None, :]
    gg = g_gla[:, None, :]
    kc = jnp.transpose(cache_swa_k, (0, 1, 3, 4, 2)).reshape(DEC_BATCH, n_ab, KB_W, PAST_LEN)
    vc = jnp.transpose(cache_swa_v, (0, 1, 3, 4, 2)).reshape(DEC_BATCH, n_ab, KB_W, PAST_LEN)
    w_down, w_uq, w_uk, w_uvt = _prep_odd(jnp.swapaxes(w_mla_down, 1, 2), w_mla_uq, w_mla_ukv)
    g_q = g_mla_q[:, None, :]
    g_kv = g_mla_kv[:, None, :]
    kr_ctx = jnp.swapaxes(cache_mla_kr, 2, 3)

    def gla_layer(i_ab, exact):
        def run(qa, ka, va, ga, ld):
            o_p, stf, stb = _gla(SEQ, BATCH, 0, qa, ka, va, ga, ld, gg, i_ab, exact=exact)
            o_s, _, _ = _gla(DEC_SEQ, DEC_BATCH, p_blk, qa, ka, va, ga, ld, gg, i_ab,
                             (state_gla_fwd, state_gla_bwd), exact=exact)
            return o_p, o_s, stf, stb
        return run

    st_f, st_b, sk, sv, ckv_out, ckr_out = [], [], [], [], [], []
    for l in range(DEPTH):
        i = l // 2
        if l % 2 == 0:
            qa, ka, va, ga, ld, qb, ld_min, kbt, vbt, kb_s, vb_s = _in_even(
                xs, l, mods, g_norm, w_in, w_gk, b_gk, cos_b, sin_b)
            factorisable = jnp.min(ld_min) * GLA_CHUNK >= -GLA_SAFE_TOTAL
            o_gla_p, o_gla_s, stf, stb = lax.cond(factorisable, gla_layer(i, False), gla_layer(i, True),
                                                  qa, ka, va, ga, ld)
            o_swa_p = _swa_prompt(swa_sink, i, qb, kbt, vbt)
            o_swa_s = _swa_sample(swa_sink, i, qb, kb_s, vb_s, kc, vc)
            xs = (_out_layer(xs, l, mods, g_norm, (o_gla_p, o_gla_s), (o_swa_p, o_swa_s), 0, w_out_ab, i,
                             w_ff1, w_ff2),)
            st_f.append(stf)
            st_b.append(stb)
            sk.append(kbt)
            sv.append(vbt)
        else:
            q, k, vt, ckv, krt = _in_odd(xs[0], l, mods, g_norm, w_down, g_q, g_kv, w_uq, w_uk, w_uvt, cos_c, sin_c)
            o_mla = (_mla_prompt(q, k, vt), _mla_sample(q, k, vt, cache_mla_ckv, kr_ctx, w_uk, w_uvt, i))
            xs = (_out_layer(xs, l, mods, g_norm, o_mla, o_mla, 1, w_mla_o, i, w_ff1, w_ff2),)
            ckv_out.append(ckv.reshape(BATCH, SEQ, KV_LORA))
            ckr_out.append(krt)

    y_prompt = xs[0][:N_PROMPT].reshape(BATCH, SEQ, D_MODEL)
    y_sample = xs[0][N_PROMPT:].reshape(DEC_BATCH, DEC_SEQ, D_MODEL)
    swa_cache = lambda parts: jnp.transpose(
        jnp.stack(parts, axis=1).reshape(BATCH, len(parts), KV_B, D_B, SEQ), (0, 1, 4, 2, 3))
    return (y_prompt, y_sample, jnp.stack(st_f, axis=1), jnp.stack(st_b, axis=1), swa_cache(sk), swa_cache(sv),
            jnp.stack(ckv_out, axis=1), jnp.swapaxes(jnp.stack(ckr_out, axis=1), 2, 3))
```

```python
import functools

import jax
import jax.numpy as jnp
from jax import lax
from jax.experimental import pallas as pl
from jax.experimental.pallas import tpu as pltpu

F32 = jnp.float32
BF16 = jnp.bfloat16

D_MODEL = 1024
BATCH = 16
SEQ = 256
DEPTH = 4
DEC_BATCH = 2
DEC_SEQ = 1024
PAST_LEN = 512
GRID_W = 64
D_FF = 4 * D_MODEL
EPS = 1e-6
ROPE_BASE = 10000.0
H_A = 4
DK_A = 64
DV_A = 128
GK_RANK = 16
GATE_NORM = 16.0
GLA_CHUNK = 64
H_B = 8
KV_B = 2
G_B = H_B // KV_B
D_B = 64
WINDOW = 128
H_C = 16
NOPE_C = 64
ROPE_C = 32
V_C = 64
Q_LORA = 384
KV_LORA = 256

N_PROMPT = BATCH * SEQ
N_SAMPLE = DEC_BATCH * DEC_SEQ
N_TOK = N_PROMPT + N_SAMPLE
N_MOD_ROWS = 8
QA_W = H_A * DK_A
VA_W = H_A * DV_A
QB_W = H_B * D_B
KB_W = KV_B * D_B
LANES = 128
HEAD_PAD_C = 128
QC_W = H_C * HEAD_PAD_C
VC_W = H_C * V_C
DOWN_RAW_W = Q_LORA + KV_LORA + ROPE_C
DOWN_W = Q_LORA + KV_LORA + LANES
ROPE_HALF_C = ROPE_C // 2
NOPE_LO_C = LANES // 2 - ROPE_HALF_C

TM_IN = 512
TM_OUT = 1024
TK_FF = 512
OUT_PROLOGUE_ROWS = 256
TN_MOD = 3072
PREP_ROWS = 256
GLA_GROUP = 256
GLA_SCAN_UNROLL = 4
GLA_SAFE_TOTAL = 160.0
SWA_QB = 128
SWA_WIN = 3 * SWA_QB
MLA_QB = 256
HEADS_PER_DOT_C = 4
MLA_HEAD_GROUP_P = 16
MLA_HEAD_GROUP_S = 4
VMEM_LIMIT = 60 * 1024 * 1024
NEG_BIG = -1e30


def _cparams(sem):
    return pltpu.CompilerParams(dimension_semantics=sem, vmem_limit_bytes=VMEM_LIMIT)


def _dot(a, b):
    return jnp.dot(a, b, preferred_element_type=F32)


def _dot_nt(a, b):
    return lax.dot_general(a, b, (((1,), (1,)), ((), ())), preferred_element_type=F32)


def _dot_tn(a, b):
    return lax.dot_general(a, b, (((0,), (0,)), ((), ())), preferred_element_type=F32)


def _rms(x, g):
    return x * lax.rsqrt(jnp.mean(x * x, axis=-1, keepdims=True) + EPS) * g


def _silu(x):
    return x / (1.0 + jnp.exp(-x))


def _rope(x, cos, sin, half):
    lane = lax.broadcasted_iota(jnp.int32, x.shape, 1)
    first = (lane % (2 * half)) < half
    partner = jnp.where(first, pltpu.roll(x, LANES - half, 1), pltpu.roll(x, half, 1))
    return x * cos + partner * sin


def _rope_c(x, cos, sin):
    return x * cos + pltpu.roll(x, LANES // 2, 1) * sin


def _rope_split_c(r):
    q = ROPE_C // 4
    first = jnp.concatenate([r[..., 0:q], r[..., 2 * q:3 * q]], axis=-1)
    second = jnp.concatenate([r[..., q:2 * q], r[..., 3 * q:4 * q]], axis=-1)
    return first, second


def _split3(x):
    x1 = x.astype(BF16)
    r1 = x - x1.astype(F32)
    x2 = r1.astype(BF16)
    x3 = (r1 - x2.astype(F32)).astype(BF16)
    return x1, x2, x3


def _dot3(t, parts):
    return _dot(t, parts[2]) + _dot(t, parts[1]) + _dot(t, parts[0])


def _ones_where(cond):
    return jnp.where(cond, 1.0, 0.0).astype(BF16)


def _mod_row(tile, tm):
    n_prompt_tiles = N_PROMPT // tm
    tiles_per_seq = DEC_SEQ // tm
    return jnp.where(tile < n_prompt_tiles, 0, 1 + (tile - n_prompt_tiles) // tiles_per_seq)


def _layer_spec(shape, idx):
    return pl.BlockSpec((None,) + shape, lambda *_: (idx,) + (0,) * len(shape))


def _mod_spec(layer, tm):
    return pl.BlockSpec((None, None, 6, D_MODEL), lambda i, *_: (layer, _mod_row(i, tm), 0, 0))


def _split_specs(tm, width, col=0):
    n_prompt_tiles = N_PROMPT // tm
    return [pl.BlockSpec((tm, width), lambda i, *_: (jnp.minimum(i, n_prompt_tiles - 1), col)),
            pl.BlockSpec((tm, width), lambda i, *_: (jnp.maximum(i - n_prompt_tiles, 0), col))]


def _split_shapes(width, dtype):
    return [jax.ShapeDtypeStruct((N_PROMPT, width), dtype), jax.ShapeDtypeStruct((N_SAMPLE, width), dtype)]


def _is_prompt_tile(tm):
    return pl.program_id(0) < N_PROMPT // tm


def _load_split(pair, tm, rows=slice(None)):
    return jnp.where(_is_prompt_tile(tm), pair[0][rows, :], pair[1][rows, :])


def _store_split(pair, tm, value, rows=slice(None)):
    is_prompt = _is_prompt_tile(tm)

    @pl.when(is_prompt)
    def _():
        pair[0][rows, :] = value

    @pl.when(jnp.logical_not(is_prompt))
    def _():
        pair[1][rows, :] = value


_C_QA, _C_KA, _C_VA, _C_GA = 0, QA_W, 2 * QA_W, 2 * QA_W + VA_W
_C_QB = _C_GA + VA_W
_C_KB = _C_QB + QB_W
_C_VB = _C_KB + KB_W
_C_LO = _C_VB + KB_W
AB_IN = _C_LO + 2 * GK_RANK


def _prep_even_kernel(wt_ref, o_ref):
    raw_lo = _C_QB
    raw_qb = raw_lo + 2 * GK_RANK
    o_ref[:, 0:_C_QB] = wt_ref[0:raw_lo, :].T.astype(BF16)
    o_ref[:, _C_QB:_C_LO] = wt_ref[raw_qb:AB_IN, :].T.astype(BF16)
    o_ref[:, _C_LO:AB_IN] = wt_ref[raw_lo:raw_lo + LANES, :].T[:, 0:2 * GK_RANK].astype(BF16)


def _prep_even(w_in_t):
    n = w_in_t.shape[0]
    return pl.pallas_call(
        _prep_even_kernel,
        grid=(n, D_MODEL // PREP_ROWS),
        in_specs=[pl.BlockSpec((None, AB_IN, PREP_ROWS), lambda l, r: (l, 0, r))],
        out_specs=pl.BlockSpec((None, PREP_ROWS, AB_IN), lambda l, r: (l, r, 0)),
        out_shape=jax.ShapeDtypeStruct((n, D_MODEL, AB_IN), BF16),
        compiler_params=_cparams(("parallel", "parallel")),
        name="prep_even",
    )(w_in_t)


def _prep_odd_kernel(wdt_ref, wuq_ref, wukv_ref, od_ref, ouq_ref, ouk_ref, ouvt_ref):
    def head_group(nope, rope_first, rope_second):
        rows = nope.shape[0]
        zero_half = jnp.zeros((rows, ROPE_HALF_C), F32)
        return jnp.concatenate(
            [zero_half if rope_first is None else rope_first, nope[:, 0:NOPE_LO_C],
             zero_half if rope_second is None else rope_second, nope[:, NOPE_LO_C:NOPE_C],
             jnp.zeros((rows, HEAD_PAD_C - NOPE_C - ROPE_C), F32)], axis=1).astype(BF16)

    n_ckv = Q_LORA + KV_LORA
    od_ref[:, 0:n_ckv] = wdt_ref[0:n_ckv, :].T.astype(BF16)
    tail = wdt_ref[DOWN_RAW_W - LANES:DOWN_RAW_W, :].T
    od_ref[:, n_ckv:DOWN_W] = head_group(jnp.zeros((D_MODEL, NOPE_C), F32), *_rope_split_c(tail[:, LANES - ROPE_C:]))
    hd_q = NOPE_C + ROPE_C
    for h in range(H_C):
        wq = wuq_ref[:, h * hd_q:(h + 1) * hd_q]
        ouq_ref[:, h * HEAD_PAD_C:(h + 1) * HEAD_PAD_C] = head_group(wq[:, 0:NOPE_C], *_rope_split_c(wq[:, NOPE_C:]))
    wukv = wukv_ref[...]
    for h in range(H_C):
        ouk_ref[:, h * HEAD_PAD_C:(h + 1) * HEAD_PAD_C] = head_group(
            wukv[:, h * HEAD_PAD_C:h * HEAD_PAD_C + NOPE_C], None, None)
    wukv_t = wukv.T
    for h in range(H_C):
        ouvt_ref[h * V_C:(h + 1) * V_C, :] = wukv_t[h * HEAD_PAD_C + NOPE_C:(h + 1) * HEAD_PAD_C, :].astype(BF16)


def _prep_odd(w_down_t, w_uq, w_ukv):
    n = w_down_t.shape[0]
    spec = lambda r, c: pl.BlockSpec((None, r, c), lambda l: (l, 0, 0))
    return pl.pallas_call(
        _prep_odd_kernel,
        grid=(n,),
        in_specs=[spec(DOWN_RAW_W, D_MODEL), spec(Q_LORA, H_C * (NOPE_C + ROPE_C)), spec(KV_LORA, QC_W)],
        out_specs=[spec(D_MODEL, DOWN_W), spec(Q_LORA, QC_W), spec(KV_LORA, QC_W), spec(VC_W, KV_LORA)],
        out_shape=[jax.ShapeDtypeStruct((n, D_MODEL, DOWN_W), BF16), jax.ShapeDtypeStruct((n, Q_LORA, QC_W), BF16),
                   jax.ShapeDtypeStruct((n, KV_LORA, QC_W), BF16), jax.ShapeDtypeStruct((n, VC_W, KV_LORA), BF16)],
        compiler_params=_cparams(("parallel",)),
        name="prep_odd",
    )(w_down_t, w_uq, w_ukv)


def _mod_kernel(c_ref, w_ref, b_ref, o_ref):
    s = _silu(c_ref[...])
    o_ref[...] = _dot(s.astype(BF16), w_ref[...].astype(BF16)) + b_ref[...]


def _modulation(cvecs, w_mod, b_mod):
    return pl.pallas_call(
        _mod_kernel,
        grid=(DEPTH, 6 * D_MODEL // TN_MOD),
        in_specs=[
            pl.BlockSpec((N_MOD_ROWS, D_MODEL), lambda l, j: (0, 0)),
            pl.BlockSpec((None, D_MODEL, TN_MOD), lambda l, j: (l, 0, j)),
            pl.BlockSpec((None, 1, TN_MOD), lambda l, j: (l, 0, j)),
        ],
        out_specs=pl.BlockSpec((None, N_MOD_ROWS, TN_MOD), lambda l, j: (l, 0, j)),
        out_shape=jax.ShapeDtypeStruct((DEPTH, N_MOD_ROWS, 6 * D_MODEL), F32),
        compiler_params=_cparams(("parallel", "parallel")),
        name="adaln_mod",
    )(cvecs, w_mod, b_mod.reshape(DEPTH, 1, 6 * D_MODEL))


def _x_specs(n_x, tm):
    if n_x == 1:
        return [pl.BlockSpec((tm, D_MODEL), lambda i, *_: (i, 0))]
    return _split_specs(tm, D_MODEL)


def _load_x(n_x, refs, tm, rows=slice(None)):
    if n_x == 1:
        return refs[0][rows, :]
    return _load_split(refs[:2], tm, rows)


def _in_even_kernel(n_x, *refs):
    (mod_ref, g_ref, w_ref, wgk_ref, bgk_ref, cos_ref, sin_ref,
     qa_ref, ka_ref, va_ref, ga_ref, ld_ref, qb_ref, ldmin_ref, kbt_ref, vbt_ref, kbs_ref, vbs_ref) = refs[n_x:]
    x = _load_x(n_x, refs, TM_IN)
    h = _rms(x, g_ref[0:1, :]) * (1.0 + mod_ref[1:2, :]) + mod_ref[0:1, :]
    hb = h.astype(BF16)
    qa_ref[...] = _dot(hb, w_ref[:, _C_QA:_C_KA]) * (DK_A ** -0.5)
    ka_ref[...] = _dot(hb, w_ref[:, _C_KA:_C_VA])
    va_ref[...] = _dot(hb, w_ref[:, _C_VA:_C_GA]).astype(BF16)
    ga_ref[...] = _dot(hb, w_ref[:, _C_GA:_C_QB])
    cos = cos_ref[...]
    sin = sin_ref[...]
    qb = _dot(hb, w_ref[:, _C_QB:_C_KB])
    for j in range(QB_W // LANES):
        qj = qb[:, j * LANES:(j + 1) * LANES]
        qb_ref[:, j * LANES:(j + 1) * LANES] = (_rope(qj, cos, sin, D_B // 4) * (D_B ** -0.5)).astype(BF16)
    kvb = _dot(hb, w_ref[:, _C_KB:_C_LO])
    kb = _rope(kvb[:, :KB_W], cos, sin, D_B // 4)
    vb = kvb[:, KB_W:]
    lo = _dot(hb, w_ref[:, _C_LO:AB_IN]).astype(BF16)
    z = _dot(lo, wgk_ref[...]) + bgk_ref[...]
    ld = (jnp.minimum(z, 0.0) - jnp.log(1.0 + jnp.exp(-jnp.abs(z)))) * (1.0 / GATE_NORM)
    ld_ref[...] = ld
    ld_min = jnp.min(jnp.min(ld, axis=0, keepdims=True), axis=1, keepdims=True)
    ldmin_ref[...] = jnp.broadcast_to(ld_min, ldmin_ref.shape)

    is_prompt = _is_prompt_tile(TM_IN)

    @pl.when(is_prompt)
    def _():
        for s in range(TM_IN // SEQ):
            kbt_ref[s] = kb[s * SEQ:(s + 1) * SEQ, :].T
            vbt_ref[s] = vb[s * SEQ:(s + 1) * SEQ, :].T

    @pl.when(jnp.logical_not(is_prompt))
    def _():
        kbs_ref[...] = kb
        vbs_ref[...] = vb


def _in_even(xs, layer, mods, g_norm, w_in, w_gk, b_gk, cos, sin):
    tm = TM_IN
    i_ab = layer // 2
    n_prompt_tiles = N_PROMPT // tm
    seq_per_tile = tm // SEQ
    row = lambda i: (i, 0)
    widths = (QA_W, QA_W, VA_W, VA_W, 2 * QA_W, QB_W)
    dtypes = (F32, F32, BF16, F32, F32, BF16)
    kv_specs = 2 * [pl.BlockSpec((seq_per_tile, KB_W, SEQ), lambda i: (jnp.minimum(i, n_prompt_tiles - 1), 0, 0))] + \
        2 * [pl.BlockSpec((tm, KB_W), lambda i: (jnp.maximum(i - n_prompt_tiles, 0), 0))]
    kv_shapes = 2 * [jax.ShapeDtypeStruct((BATCH, KB_W, SEQ), F32)] + 2 * [jax.ShapeDtypeStruct((N_SAMPLE, KB_W), F32)]
    return pl.pallas_call(
        functools.partial(_in_even_kernel, len(xs)),
        grid=(N_TOK // tm,),
        in_specs=_x_specs(len(xs), tm) + [
            _mod_spec(layer, tm),
            _layer_spec((4, D_MODEL), layer),
            _layer_spec((D_MODEL, AB_IN), i_ab),
            _layer_spec((2 * GK_RANK, 2 * QA_W), i_ab),
            _layer_spec((1, 2 * QA_W), i_ab),
            pl.BlockSpec((tm, LANES), row),
            pl.BlockSpec((tm, LANES), row),
        ],
        out_specs=[pl.BlockSpec((tm, w), row) for w in widths] + [
            pl.BlockSpec((None, 8, LANES), lambda i: (i, 0, 0))] + kv_specs,
        out_shape=[jax.ShapeDtypeStruct((N_TOK, w), d) for w, d in zip(widths, dtypes)] + [
            jax.ShapeDtypeStruct((N_TOK // tm, 8, LANES), F32)] + kv_shapes,
        compiler_params=_cparams(("arbitrary",)),
        name="in_even",
    )(*xs, mods, g_norm, w_in, w_gk, b_gk, cos, sin)


def _gla_kernel(seq_len, has_s0, exact, *refs):
    qa_ref, ka_ref, va_ref, ga_ref, ld_ref, gg_ref = refs[:6]
    s0_refs = refs[6:8] if has_s0 else None
    o_ref, stf_ref, stb_ref = refs[8:11] if has_s0 else refs[6:9]
    st_ref, qi_ref, ki_ref, qcat_ref, ks_ref, dec_ref, stcat_ref, acc_ref = refs[-8:]
    b_ref = qi_ref if exact else None
    n_groups = seq_len // GLA_GROUP
    cpg = GLA_GROUP // GLA_CHUNK
    n_chunks = seq_len // GLA_CHUNK
    n_pairs = H_A // 2
    pair_k = 2 * DK_A
    pair_v = 2 * DV_A

    def chunk_masks():
        r_i = lax.broadcasted_iota(jnp.int32, (GLA_GROUP, GLA_GROUP), 0)
        c_i = lax.broadcasted_iota(jnp.int32, (GLA_GROUP, GLA_GROUP), 1)
        same = (r_i // GLA_CHUNK) == (c_i // GLA_CHUNK)
        return same, same & (c_i <= r_i), same & (c_i >= r_i)

    def scale_group(j, carry):
        same, mask_f, mask_b = chunk_masks()
        t_cum = (_ones_where(mask_f), _ones_where(mask_b))
        t_all = _ones_where(same)
        s_r = lax.broadcasted_iota(jnp.int32, (8, GLA_GROUP), 0)
        s_c = lax.broadcasted_iota(jnp.int32, (8, GLA_GROUP), 1)
        t_sel = _ones_where(s_r == s_c // GLA_CHUNK)
        rows = pl.ds(pl.multiple_of(j * GLA_GROUP, GLA_GROUP), GLA_GROUP)
        q = qa_ref[rows, :]
        k = ka_ref[rows, :]
        for d in range(2):
            parts = _split3(ld_ref[rows, d * QA_W:(d + 1) * QA_W])
            b = _dot3(t_cum[d], parts)
            tot = _dot3(t_all, parts)
            dec8 = jnp.exp(_dot3(t_sel, parts))
            ref = 0.5 * tot
            if exact:
                b_ref[d, rows, :] = b
            else:
                qi_ref[d, rows, :] = (q * jnp.exp(b - ref)).astype(BF16)
                ki_ref[d, rows, :] = (k * jnp.exp(ref - b)).astype(BF16)
            ks_ref[d, rows, :] = (k * jnp.exp(tot - b)).astype(BF16)
            q_inter = (q * jnp.exp(b)).astype(BF16)
            for p in range(n_pairs):
                qcat_ref[rows, p * 2 * pair_k + d * pair_k:p * 2 * pair_k + (d + 1) * pair_k] = (
                    q_inter[:, p * pair_k:(p + 1) * pair_k])
            for cc in range(cpg):
                dec_ref[d, j * cpg + cc] = jnp.broadcast_to(dec8[cc:cc + 1, :], (8, QA_W))
        return carry

    lax.fori_loop(0, n_groups, scale_group, 0)

    zpad = jnp.zeros((DK_A, DV_A), F32)
    for d in range(2):
        for p in range(n_pairs):
            if has_s0:
                top = jnp.concatenate([s0_refs[d][2 * p], zpad], axis=0).T
                bot = jnp.concatenate([zpad, s0_refs[d][2 * p + 1]], axis=0).T
                st_ref[d, p] = jnp.concatenate([top, bot], axis=0)
            else:
                st_ref[d, p] = jnp.zeros((pair_v, pair_k), F32)

    def scan_chunk(c, carry):
        bd_r = lax.broadcasted_iota(jnp.int32, (pair_v, pair_k), 0)
        bd_c = lax.broadcasted_iota(jnp.int32, (pair_v, pair_k), 1)
        bd_mask = (bd_r // DV_A) == (bd_c // DK_A)
        for d in range(2):
            cd = c if d == 0 else n_chunks - 1 - c
            rows = pl.ds(pl.multiple_of(cd * GLA_CHUNK, GLA_CHUNK), GLA_CHUNK)
            for p in range(n_pairs):
                st = st_ref[d, p]
                stcat_ref[cd, p, :, d * pair_k:(d + 1) * pair_k] = st.astype(BF16)
                u = _dot_tn(va_ref[rows, p * pair_v:(p + 1) * pair_v], ks_ref[d, rows, p * pair_k:(p + 1) * pair_k])
                dec = dec_ref[d, cd][0:1, p * pair_k:(p + 1) * pair_k]
                st_ref[d, p] = st * dec + jnp.where(bd_mask, u, 0.0)
        return carry

    lax.fori_loop(0, n_chunks, scan_chunk, 0, unroll=GLA_SCAN_UNROLL)

    for d, out_ref in ((0, stf_ref), (1, stb_ref)):
        for p in range(n_pairs):
            st = st_ref[d, p]
            out_ref[2 * p] = st[0:DV_A, :].T[0:DK_A, :]
            out_ref[2 * p + 1] = st[DV_A:pair_v, :].T[DK_A:pair_k, :]

    def exact_intra(row0, h):
        p, hh = divmod(h, 2)
        lanes = slice(p * pair_k, (p + 1) * pair_k)
        head_lanes = (lax.broadcasted_iota(jnp.int32, (GLA_CHUNK, pair_k), 1) // DK_A) == hh
        s_idx = lax.broadcasted_iota(jnp.int32, (GLA_CHUNK, GLA_CHUNK), 0)
        t_idx = lax.broadcasted_iota(jnp.int32, (GLA_CHUNK, GLA_CHUNK), 1)
        outs = []
        for cc in range(cpg):
            r0 = row0 + cc * GLA_CHUNK
            crow = pl.ds(pl.multiple_of(r0, GLA_CHUNK), GLA_CHUNK)
            kc = ka_ref[crow, lanes]
            att_t = jnp.zeros((GLA_CHUNK, GLA_CHUNK), F32)
            for d in range(2):
                bc = b_ref[d, crow, lanes]

                def row_step(t, att_t, d=d, bc=bc, kc=kc, r0=r0):
                    grp = pl.ds(pl.multiple_of(r0 + (t // 8) * 8, 8), 8)
                    pick = lax.broadcasted_iota(jnp.int32, (8, pair_k), 0) == t % 8
                    bt = jnp.sum(jnp.where(pick, b_ref[d, grp, lanes], 0.0), axis=0, keepdims=True)
                    qt = jnp.sum(jnp.where(pick, qa_ref[grp, lanes], 0.0), axis=0, keepdims=True)
                    w = jnp.where(head_lanes, qt * kc * jnp.exp(jnp.minimum(bt - bc, 0.0)), 0.0)
                    col = jnp.sum(w, axis=1, keepdims=True)
                    allowed = (s_idx <= t) if d == 0 else (s_idx >= t)
                    return jnp.where((t_idx == t) & allowed, att_t + col, att_t)

                att_t = lax.fori_loop(0, GLA_CHUNK, row_step, att_t)
            outs.append(_dot_tn(att_t.astype(BF16), va_ref[crow, h * DV_A:(h + 1) * DV_A]))
        return jnp.concatenate(outs, axis=0)

    def out_group(j, carry):
        _, mask_f, mask_b = chunk_masks()
        lane = lax.broadcasted_iota(jnp.int32, (GLA_GROUP, pair_k), 1)
        rows = pl.ds(pl.multiple_of(j * GLA_GROUP, GLA_GROUP), GLA_GROUP)
        for p in range(n_pairs):
            for cc in range(cpg):
                c = j * cpg + cc
                crow = pl.ds(pl.multiple_of(c * GLA_CHUNK, GLA_CHUNK), GLA_CHUNK)
                acc_ref[crow, p * pair_v:(p + 1) * pair_v] = _dot_nt(
                    qcat_ref[crow, p * 2 * pair_k:(p + 1) * 2 * pair_k], stcat_ref[c, p])
        if exact:
            intras = [exact_intra(j * GLA_GROUP, h) for h in range(H_A)]
        else:
            atts = []
            for h in range(H_A):
                p, hh = divmod(h, 2)
                head_lanes = _ones_where((lane // DK_A) == hh)
                att = None
                for d, mask in ((0, mask_f), (1, mask_b)):
                    qm = qi_ref[d, rows, p * pair_k:(p + 1) * pair_k] * head_lanes
                    a = jnp.where(mask, _dot_nt(qm, ki_ref[d, rows, p * pair_k:(p + 1) * pair_k]), 0.0)
                    att = a if att is None else att + a
                atts.append(att.astype(BF16))
            intras = [_dot(atts[h], va_ref[rows, h * DV_A:(h + 1) * DV_A]) for h in range(H_A)]
        for h in range(H_A):
            o = acc_ref[rows, h * DV_A:(h + 1) * DV_A] + intras[h]
            gate = _silu(ga_ref[rows, h * DV_A:(h + 1) * DV_A])
            o_ref[rows, h * DV_A:(h + 1) * DV_A] = (_rms(o, gg_ref[...]) * gate).astype(BF16)
        return carry

    lax.fori_loop(0, n_groups, out_group, 0)


def _gla(seq_len, n_seq, row_block0, qa, ka, va, ga, ld, g_gla, i_ab, s0=None, exact=False):
    has_s0 = s0 is not None
    n_chunks = seq_len // GLA_CHUNK
    n_pairs = H_A // 2
    rows = lambda b: (row_block0 + b, 0)
    st_spec = pl.BlockSpec((None, H_A, DK_A, DV_A), lambda b: (b, 0, 0, 0))
    in_specs = [
        pl.BlockSpec((seq_len, QA_W), rows),
        pl.BlockSpec((seq_len, QA_W), rows),
        pl.BlockSpec((seq_len, VA_W), rows),
        pl.BlockSpec((seq_len, VA_W), rows),
        pl.BlockSpec((seq_len, 2 * QA_W), rows),
        _layer_spec((1, DV_A), i_ab),
    ]
    args = [qa, ka, va, ga, ld, g_gla]
    if has_s0:
        s0_spec = pl.BlockSpec((None, None, H_A, DK_A, DV_A), lambda b: (b, i_ab, 0, 0, 0))
        in_specs += [s0_spec, s0_spec]
        args += list(s0)
    st_shape = jax.ShapeDtypeStruct((n_seq, H_A, DK_A, DV_A), F32)
    return pl.pallas_call(
        functools.partial(_gla_kernel, seq_len, has_s0, exact),
        grid=(n_seq,),
        in_specs=in_specs,
        out_specs=[pl.BlockSpec((seq_len, VA_W), lambda b: (b, 0)), st_spec, st_spec],
        out_shape=[jax.ShapeDtypeStruct((n_seq * seq_len, VA_W), BF16), st_shape, st_shape],
        scratch_shapes=[
            pltpu.VMEM((2, n_pairs, 2 * DV_A, 2 * DK_A), F32),
            pltpu.VMEM((2, seq_len, QA_W), F32 if exact else BF16),
            pltpu.VMEM((2, seq_len, QA_W), BF16),
            pltpu.VMEM((seq_len, 2 * QA_W), BF16),
            pltpu.VMEM((2, seq_len, QA_W), BF16),
            pltpu.VMEM((2, n_chunks, 8, QA_W), F32),
            pltpu.VMEM((n_chunks, n_pairs, 2 * DV_A, 4 * DK_A), BF16),
            pltpu.VMEM((seq_len, VA_W), F32),
        ],
        compiler_params=_cparams(("parallel",)),
        name=("gla_s" if has_s0 else "gla_p") + ("_exact" if exact else ""),
    )(*args)


def _swa_head_softmax(pieces, sink):
    m = sink
    for s, _, _ in pieces:
        m = jnp.maximum(m, jnp.max(s, axis=-1, keepdims=True))
    den = jnp.exp(sink - m)
    acc = None
    for s, v, transposed in pieces:
        e = jnp.exp(s - m)
        den = den + jnp.sum(e, axis=-1, keepdims=True)
        pv = _dot_nt(e.astype(BF16), v) if transposed else _dot(e.astype(BF16), v)
        acc = pv if acc is None else acc + pv
    return acc / den


def _dup_groups(x):
    lane = lax.broadcasted_iota(jnp.int32, x.shape, 1)
    swapped = pltpu.roll(x, D_B, 1)
    low = lane < D_B
    return jnp.where(low, x, swapped).astype(BF16), jnp.where(low, swapped, x).astype(BF16)


def _dup_groups_t(xt):
    g0, g1 = xt[0:D_B, :], xt[D_B:, :]
    return jnp.concatenate([g0, g0], axis=0).astype(BF16), jnp.concatenate([g1, g1], axis=0).astype(BF16)


def _swa_heads(sink_ref, i_ab, q_ref, kv_pieces, o_ref, n_rows):
    lane = lax.broadcasted_iota(jnp.int32, (n_rows, LANES), 1)
    head_pieces = []
    for h in range(H_B):
        j, hh = divmod(h, 2)
        g = h // G_B
        qm = q_ref[:, j * LANES:(j + 1) * LANES] * _ones_where((lane // D_B) == hh)
        pieces = []
        for keys, vals, transposed, mask in kv_pieces:
            s = _dot(qm, keys[g]) if transposed else _dot_nt(qm, keys[g])
            if mask is not None:
                s = jnp.where(mask, s, NEG_BIG)
            pieces.append((s, vals[g], transposed))
        head_pieces.append(pieces)
    outs = [_swa_head_softmax(pieces, sink_ref[i_ab, h]) for h, pieces in enumerate(head_pieces)]
    for j in range(H_B // 2):
        o_ref[:, j * LANES:(j + 1) * LANES] = jnp.where(lane < D_B, outs[2 * j], outs[2 * j + 1]).astype(BF16)


def _swa_prompt_kernel(i_ab, sink_ref, q_ref, kt_ref, vt_ref, o_ref):
    _swa_heads(sink_ref, i_ab, q_ref, [(_dup_groups_t(kt_ref[...]), _dup_groups_t(vt_ref[...]), True, None)],
               o_ref, SEQ)


def _swa_prompt(sink, i_ab, qb, kbt, vbt):
    seq = lambda b: (b, 0)
    seq_t = pl.BlockSpec((None, KB_W, SEQ), lambda b: (b, 0, 0))
    return pl.pallas_call(
        functools.partial(_swa_prompt_kernel, i_ab),
        grid=(BATCH,),
        in_specs=[pl.BlockSpec(memory_space=pltpu.SMEM), pl.BlockSpec((SEQ, QB_W), seq), seq_t, seq_t],
        out_specs=pl.BlockSpec((SEQ, QB_W), seq),
        out_shape=jax.ShapeDtypeStruct((N_PROMPT, QB_W), BF16),
        compiler_params=_cparams(("parallel",)),
        name="swa_p",
    )(sink, qb, kbt, vbt)


def _swa_sample_kernel(i_ab, sink_ref, q_ref, k_ref, v_ref, kct_ref, vct_ref, o_ref):
    n = pl.program_id(1)
    start = pl.multiple_of(jnp.clip((n - 1) * SWA_QB, 0, DEC_SEQ - SWA_WIN), SWA_QB)
    local = (_dup_groups(k_ref[pl.ds(start, SWA_WIN), :]), _dup_groups(v_ref[pl.ds(start, SWA_WIN), :]))
    ctx = (_dup_groups_t(kct_ref[...]), _dup_groups_t(vct_ref[...]))
    qi = n * SWA_QB + lax.broadcasted_iota(jnp.int32, (SWA_QB, SWA_WIN), 0)
    ki = start + lax.broadcasted_iota(jnp.int32, (SWA_QB, SWA_WIN), 1)
    band = jnp.abs(qi - ki) <= WINDOW
    _swa_heads(sink_ref, i_ab, q_ref, [ctx + (True, None), local + (False, band)], o_ref, SWA_QB)


def _swa_sample(sink, i_ab, qb, kb, vb, kc, vc):
    nqb = DEC_SEQ // SWA_QB
    q0 = N_PROMPT // SWA_QB
    return pl.pallas_call(
        functools.partial(_swa_sample_kernel, i_ab),
        grid=(DEC_BATCH, nqb),
        in_specs=[
            pl.BlockSpec(memory_space=pltpu.SMEM),
            pl.BlockSpec((SWA_QB, QB_W), lambda b, n: (q0 + b * nqb + n, 0)),
            pl.BlockSpec((DEC_SEQ, KB_W), lambda b, n: (b, 0)),
            pl.BlockSpec((DEC_SEQ, KB_W), lambda b, n: (b, 0)),
            pl.BlockSpec((None, None, KB_W, PAST_LEN), lambda b, n: (b, i_ab, 0, 0)),
            pl.BlockSpec((None, None, KB_W, PAST_LEN), lambda b, n: (b, i_ab, 0, 0)),
        ],
        out_specs=pl.BlockSpec((SWA_QB, QB_W), lambda b, n: (b * nqb + n, 0)),
        out_shape=jax.ShapeDtypeStruct((N_SAMPLE, QB_W), BF16),
        compiler_params=_cparams(("parallel", "parallel")),
        name="swa_s",
    )(sink, qb, kb, vb, kc, vc)


def _in_odd_kernel(x_ref, mod_ref, g_ref, wd_ref, gq_ref, gkv_ref, wuq_ref, wuk_ref, wuvt_ref, cos_ref, sin_ref,
                   q_ref, k_ref, vt_ref, ckv_ref, krt_ref):
    h = _rms(x_ref[...], g_ref[0:1, :]) * (1.0 + mod_ref[1:2, :]) + mod_ref[0:1, :]
    hb = h.astype(BF16)
    cos = cos_ref[...]
    sin = sin_ref[...]
    c_q = _dot(hb, wd_ref[:, 0:Q_LORA])
    c_kv = _rms(_dot(hb, wd_ref[:, Q_LORA:Q_LORA + KV_LORA]), gkv_ref[...])
    kr = _rope_c(_dot(hb, wd_ref[:, Q_LORA + KV_LORA:DOWN_W]), cos, sin)
    cqb = _rms(c_q, gq_ref[...]).astype(BF16)
    ckvb = c_kv.astype(BF16)
    scale = (NOPE_C + ROPE_C) ** -0.5
    group_w = HEADS_PER_DOT_C * HEAD_PAD_C
    for grp in range(H_C // HEADS_PER_DOT_C):
        gsl = slice(grp * group_w, (grp + 1) * group_w)
        qg = _dot(cqb, wuq_ref[:, gsl])
        kg = _dot(ckvb, wuk_ref[:, gsl])
        for j in range(HEADS_PER_DOT_C):
            sl = slice(j * HEAD_PAD_C, (j + 1) * HEAD_PAD_C)
            osl = slice(grp * group_w + j * HEAD_PAD_C, grp * group_w + (j + 1) * HEAD_PAD_C)
            q_ref[:, osl] = (_rope_c(qg[:, sl], cos, sin) * scale).astype(BF16)
            k_ref[:, osl] = (kg[:, sl] + kr).astype(BF16)
    vt_ref[...] = _dot_nt(wuvt_ref[...], ckvb).astype(BF16)

    @pl.when(_is_prompt_tile(TM_IN))
    def _():
        ckv_ref[...] = c_kv
        q4 = ROPE_C // 4
        half = LANES // 2
        for s in range(TM_IN // SEQ):
            t = kr[s * SEQ:(s + 1) * SEQ, :].T
            krt_ref[s] = jnp.concatenate([t[0:q4], t[half:half + q4], t[q4:2 * q4], t[half + q4:half + 2 * q4]], axis=0)


def _in_odd(x, layer, mods, g_norm, w_down, g_q, g_kv, w_uq, w_uk, w_uvt, cos, sin):
    tm = TM_IN
    i_c = layer // 2
    n_prompt_tiles = N_PROMPT // tm
    row = lambda i: (i, 0)
    prompt_row = lambda i: (jnp.minimum(i, n_prompt_tiles - 1), 0)
    prompt_seq = lambda i: (jnp.minimum(i, n_prompt_tiles - 1), 0, 0)
    return pl.pallas_call(
        _in_odd_kernel,
        grid=(N_TOK // tm,),
        in_specs=[
            pl.BlockSpec((tm, D_MODEL), row),
            _mod_spec(layer, tm),
            _layer_spec((4, D_MODEL), layer),
            _layer_spec((D_MODEL, DOWN_W), i_c),
            _layer_spec((1, Q_LORA), i_c),
            _layer_spec((1, KV_LORA), i_c),
            _layer_spec((Q_LORA, QC_W), i_c),
            _layer_spec((KV_LORA, QC_W), i_c),
            _layer_spec((VC_W, KV_LORA), i_c),
            pl.BlockSpec((tm, LANES), row),
            pl.BlockSpec((tm, LANES), row),
        ],
        out_specs=[pl.BlockSpec((tm, QC_W), row), pl.BlockSpec((tm, QC_W), row),
                   pl.BlockSpec((VC_W, tm), lambda i: (0, i)),
                   pl.BlockSpec((tm, KV_LORA), prompt_row), pl.BlockSpec((tm // SEQ, ROPE_C, SEQ), prompt_seq)],
        out_shape=[jax.ShapeDtypeStruct((N_TOK, QC_W), BF16), jax.ShapeDtypeStruct((N_TOK, QC_W), BF16),
                   jax.ShapeDtypeStruct((VC_W, N_TOK), BF16),
                   jax.ShapeDtypeStruct((N_PROMPT, KV_LORA), F32), jax.ShapeDtypeStruct((BATCH, ROPE_C, SEQ), F32)],
        compiler_params=_cparams(("arbitrary",)),
        name="in_odd",
    )(x, mods, g_norm, w_down, g_q, g_kv, w_uq, w_uk, w_uvt, cos, sin)


def _reduce_rows(x, op, reduce_fn):
    while x.shape[0] % 16 == 0:
        half = x.shape[0] // 2
        x = op(x[:half], x[half:])
    return reduce_fn(x, axis=0, keepdims=True)


def _mla_heads(q_ref, kv_pieces, o_ref, ot_ref, group):
    for h0 in range(0, H_C, group):
        heads = range(h0, h0 + group)
        scores = []
        for hd in heads:
            sl = slice(hd * HEAD_PAD_C, (hd + 1) * HEAD_PAD_C)
            scores.append([_dot_nt(k_ref[:, sl], q_ref[:, sl]) for k_ref, _ in kv_pieces])
        exps, dens = [], []
        for per_piece in scores:
            m = None
            for s in per_piece:
                sm = _reduce_rows(s, jnp.maximum, jnp.max)
                m = sm if m is None else jnp.maximum(m, sm)
            es = [jnp.exp(s - m) for s in per_piece]
            den = None
            for e in es:
                part = _reduce_rows(e, jnp.add, jnp.sum)
                den = part if den is None else den + part
            exps.append([e.astype(BF16) for e in es])
            dens.append(den)
        for hd, es, den in zip(heads, exps, dens):
            acc = None
            for e, (_, vt_ref) in zip(es, kv_pieces):
                pv = _dot(vt_ref[hd * V_C:(hd + 1) * V_C, :], e)
                acc = pv if acc is None else acc + pv
            ot_ref[hd * V_C:(hd + 1) * V_C, :] = acc / den
    o_ref[...] = ot_ref[...].T.astype(BF16)


def _mla_prompt_kernel(q_ref, k_ref, vt_ref, o_ref, ot_ref):
    _mla_heads(q_ref, [(k_ref, vt_ref)], o_ref, ot_ref, MLA_HEAD_GROUP_P)


def _mla_prompt(q, k, vt):
    seq = lambda b: (b, 0)
    return pl.pallas_call(
        _mla_prompt_kernel,
        grid=(BATCH,),
        in_specs=[pl.BlockSpec((SEQ, QC_W), seq), pl.BlockSpec((SEQ, QC_W), seq),
                  pl.BlockSpec((VC_W, SEQ), lambda b: (0, b))],
        out_specs=pl.BlockSpec((SEQ, VC_W), seq),
        out_shape=jax.ShapeDtypeStruct((N_PROMPT, VC_W), BF16),
        scratch_shapes=[pltpu.VMEM((VC_W, SEQ), F32)],
        compiler_params=_cparams(("parallel",)),
        name="mla_p",
    )(q, k, vt)


def _mla_sample_kernel(q_ref, k_ref, vt_ref, ckv_ref, krt_ref, wuk_ref, wuvt_ref, o_ref, kc_ref, vct_ref, ot_ref):
    @pl.when(pl.program_id(1) == 0)
    def _():
        cb = ckv_ref[...].astype(BF16)
        krt = krt_ref[...]
        q4 = ROPE_C // 4
        kr = jnp.concatenate(
            [krt[0:q4], krt[2 * q4:3 * q4], jnp.zeros((NOPE_LO_C, PAST_LEN), F32),
             krt[q4:2 * q4], krt[3 * q4:4 * q4], jnp.zeros((LANES // 2 - ROPE_HALF_C, PAST_LEN), F32)], axis=0).T
        group_w = HEADS_PER_DOT_C * HEAD_PAD_C
        for grp in range(H_C // HEADS_PER_DOT_C):
            kg = _dot(cb, wuk_ref[:, grp * group_w:(grp + 1) * group_w])
            for j in range(HEADS_PER_DOT_C):
                osl = slice(grp * group_w + j * HEAD_PAD_C, grp * group_w + (j + 1) * HEAD_PAD_C)
                kc_ref[:, osl] = (kg[:, j * HEAD_PAD_C:(j + 1) * HEAD_PAD_C] + kr).astype(BF16)
        vct_ref[...] = _dot_nt(wuvt_ref[...], cb).astype(BF16)

    _mla_heads(q_ref, [(kc_ref, vct_ref), (k_ref, vt_ref)], o_ref, ot_ref, MLA_HEAD_GROUP_S)


def _mla_sample(q, k, vt, ckv_ctx, kr_ctx, w_uk, w_uvt, i_c):
    nqb = DEC_SEQ // MLA_QB
    q0 = N_PROMPT // MLA_QB
    s0 = N_PROMPT // DEC_SEQ
    return pl.pallas_call(
        _mla_sample_kernel,
        grid=(DEC_BATCH, nqb),
        in_specs=[
            pl.BlockSpec((MLA_QB, QC_W), lambda b, n: (q0 + b * nqb + n, 0)),
            pl.BlockSpec((DEC_SEQ, QC_W), lambda b, n: (s0 + b, 0)),
            pl.BlockSpec((VC_W, DEC_SEQ), lambda b, n: (0, s0 + b)),
            pl.BlockSpec((None, None, PAST_LEN, KV_LORA), lambda b, n: (b, i_c, 0, 0)),
            pl.BlockSpec((None, None, ROPE_C, PAST_LEN), lambda b, n: (b, i_c, 0, 0)),
            _layer_spec((KV_LORA, QC_W), i_c),
            _layer_spec((VC_W, KV_LORA), i_c),
        ],
        out_specs=pl.BlockSpec((MLA_QB, VC_W), lambda b, n: (b * nqb + n, 0)),
        out_shape=jax.ShapeDtypeStruct((N_SAMPLE, VC_W), BF16),
        scratch_shapes=[pltpu.VMEM((PAST_LEN, QC_W), BF16), pltpu.VMEM((VC_W, PAST_LEN), BF16),
                        pltpu.VMEM((VC_W, MLA_QB), F32)],
        compiler_params=_cparams(("parallel", "arbitrary")),
        name="mla_s",
    )(q, k, vt, ckv_ctx, kr_ctx, w_uk, w_uvt)


def _out_kernel(n_x, n_out, *refs):
    a_refs = refs[n_x + 2:n_x + 4]
    b_refs = refs[n_x + 4:n_x + 6]
    mod_ref, g_ref = refs[n_x:n_x + 2]
    wo_ref, w1_ref, w2_ref = refs[n_x + 6:n_x + 9]
    out_refs = refs[n_x + 9:n_x + 9 + n_out]
    if n_out == 1:
        h2_ref, acc_ref = refs[n_x + 9 + n_out:]
        x1_ref = y_ref = out_refs[0]
    else:
        h2_ref, acc_ref, x1_ref = refs[n_x + 9 + n_out:]
        y_ref = acc_ref
    kk = pl.program_id(1)
    half = wo_ref.shape[0] // 2

    row_chunks = [slice(r * OUT_PROLOGUE_ROWS, (r + 1) * OUT_PROLOGUE_ROWS) for r in range(TM_OUT // OUT_PROLOGUE_ROWS)]

    @pl.when(kk == 0)
    def _():
        wo_a = wo_ref[0:half, :].astype(BF16)
        wo_b = wo_ref[half:, :].astype(BF16)
        gate_g1 = mod_ref[2:3, :] * g_ref[1:2, :]
        scale_g2 = g_ref[2:3, :] * (1.0 + mod_ref[4:5, :])
        for rows in row_chunks:
            a = _load_split(a_refs, TM_OUT, rows)
            b = _load_split(b_refs, TM_OUT, rows)
            mix = _dot(a, wo_a) + _dot(b, wo_b)
            x1 = _load_x(n_x, refs, TM_OUT, rows) + _rms(mix, gate_g1)
            x1_ref[rows, :] = x1
            h2_ref[rows, :] = (_rms(x1, scale_g2) + mod_ref[3:4, :]).astype(BF16)
        acc_ref[...] = jnp.zeros(acc_ref.shape, F32)

    def ffn(rows, w1, w2):
        hid = jnp.maximum(_dot(h2_ref[rows, :], w1), 0.0)
        return _dot((hid * hid).astype(BF16), w2)

    is_last = kk == pl.num_programs(1) - 1

    @pl.when(jnp.logical_not(is_last))
    def _():
        acc_ref[...] += ffn(slice(None), w1_ref[...].astype(BF16), w2_ref[...].astype(BF16))

    @pl.when(is_last)
    def _():
        w1 = w1_ref[...].astype(BF16)
        w2 = w2_ref[...].astype(BF16)
        gate_g3 = mod_ref[5:6, :] * g_ref[3:4, :]
        for rows in row_chunks:
            y_ref[rows, :] = x1_ref[rows, :] + _rms(acc_ref[rows, :] + ffn(rows, w1, w2), gate_g3)

    if n_out == 2:
        @pl.when(is_last)
        def _():
            _store_split(out_refs, TM_OUT, acc_ref[...])


def _out_layer(xs, layer, mods, g_norm, mix_a, mix_b, b_col, w_o, i_o, w_ff1, w_ff2, split_out):
    tm = TM_OUT
    half = D_MODEL // 2
    if split_out:
        out_specs, out_shape = _split_specs(tm, D_MODEL), _split_shapes(D_MODEL, F32)
    else:
        out_specs = [pl.BlockSpec((tm, D_MODEL), lambda i, k: (i, 0))]
        out_shape = [jax.ShapeDtypeStruct((N_TOK, D_MODEL), F32)]
    return pl.pallas_call(
        functools.partial(_out_kernel, len(xs), len(out_specs)),
        grid=(N_TOK // tm, D_FF // TK_FF),
        in_specs=_x_specs(len(xs), tm) + [
            _mod_spec(layer, tm),
            _layer_spec((4, D_MODEL), layer),
        ] + _split_specs(tm, half, 0) + _split_specs(tm, half, b_col) + [
            _layer_spec((D_MODEL, D_MODEL), i_o),
            pl.BlockSpec((None, D_MODEL, TK_FF), lambda i, k: (layer, 0, k)),
            pl.BlockSpec((None, TK_FF, D_MODEL), lambda i, k: (layer, k, 0)),
        ],
        out_specs=out_specs,
        out_shape=out_shape,
        scratch_shapes=[pltpu.VMEM((tm, D_MODEL), BF16), pltpu.VMEM((tm, D_MODEL), F32)] + (
            [pltpu.VMEM((tm, D_MODEL), F32)] if split_out else []),
        compiler_params=_cparams(("arbitrary", "arbitrary")),
        name="out_mlp",
    )(*xs, mods, g_norm, *mix_a, *mix_b, w_o, w_ff1, w_ff2)


def _rope_angles(head_dim):
    nf = head_dim // 4
    n_rows = DEC_SEQ // GRID_W
    rows = jnp.repeat(jnp.arange(n_rows, dtype=F32), GRID_W)
    cols = jnp.tile(jnp.arange(GRID_W, dtype=F32), n_rows)
    inv = ROPE_BASE ** (-jnp.arange(nf, dtype=F32) / nf)
    return jnp.stack([rows[:, None] * inv, cols[:, None] * inv], axis=1)


def _token_tables(cos_g, sin_g):
    cos_t = jnp.concatenate([jnp.ones((N_PROMPT, LANES), F32), jnp.tile(cos_g, (DEC_BATCH, 1))], axis=0)
    sin_t = jnp.concatenate([jnp.zeros((N_PROMPT, LANES), F32), jnp.tile(sin_g, (DEC_BATCH, 1))], axis=0)
    return cos_t, sin_t


def _rope_tables_b():
    nf = D_B // 4
    ang = _rope_angles(D_B)
    cos = jnp.broadcast_to(jnp.cos(ang)[:, :, None, :], (DEC_SEQ, 2, 2, nf)).reshape(DEC_SEQ, D_B)
    sin = jnp.sin(ang)
    sin = jnp.stack([-sin, sin], axis=2).reshape(DEC_SEQ, D_B)
    reps = LANES // D_B
    return _token_tables(jnp.tile(cos, (1, reps)), jnp.tile(sin, (1, reps)))


def _rope_tables_c():
    ang = _rope_angles(ROPE_C).reshape(DEC_SEQ, ROPE_HALF_C)
    cos, sin = jnp.cos(ang), jnp.sin(ang)
    half = LANES // 2
    cos_g = jnp.ones((DEC_SEQ, LANES), F32).at[:, 0:ROPE_HALF_C].set(cos).at[:, half:half + ROPE_HALF_C].set(cos)
    sin_g = jnp.zeros((DEC_SEQ, LANES), F32).at[:, 0:ROPE_HALF_C].set(-sin).at[:, half:half + ROPE_HALF_C].set(sin)
    return _token_tables(cos_g, sin_g)


def kernel(x_prompt, x_sample, state_gla_fwd, state_gla_bwd, cache_swa_k, cache_swa_v, cache_mla_ckv, cache_mla_kr, c, c_ctx, w_mod, b_mod, g_norm, w_ff1, w_ff2, w_in_ab, w_gk_f, b_gk_f, w_gk_b, b_gk_b, g_gla, swa_sink, w_out_ab, w_mla_down, g_mla_q, g_mla_kv, w_mla_uq, w_mla_ukv, w_mla_o):
    xs = (x_prompt.reshape(N_PROMPT, D_MODEL), x_sample.reshape(N_SAMPLE, D_MODEL))
    cvecs = jnp.concatenate([c_ctx[None, :], c, jnp.zeros((N_MOD_ROWS - 1 - DEC_BATCH, D_MODEL), F32)], axis=0)
    mods = _modulation(cvecs, w_mod, b_mod).reshape(DEPTH, N_MOD_ROWS, 6, D_MODEL)

    cos_b, sin_b = _rope_tables_b()
    cos_c, sin_c = _rope_tables_c()
    p_blk = N_PROMPT // DEC_SEQ

    n_ab = w_in_ab.shape[0]
    w_in = _prep_even(jnp.swapaxes(w_in_ab, 1, 2))
    zgk = jnp.zeros((n_ab, GK_RANK, QA_W), F32)
    w_gk = jnp.concatenate([jnp.concatenate([w_gk_f, zgk], axis=2),
                            jnp.concatenate([zgk, w_gk_b], axis=2)], axis=1).astype(BF16)
    b_gk = jnp.concatenate([b_gk_f, b_gk_b], axis=1)[:, None, :]
    gg = g_gla[:, None, :]
    kc = jnp.transpose(cache_swa_k, (0, 1, 3, 4, 2)).reshape(DEC_BATCH, n_ab, KB_W, PAST_LEN)
    vc = jnp.transpose(cache_swa_v, (0, 1, 3, 4, 2)).reshape(DEC_BATCH, n_ab, KB_W, PAST_LEN)
    w_down, w_uq, w_uk, w_uvt = _prep_odd(jnp.swapaxes(w_mla_down, 1, 2), w_mla_uq, w_mla_ukv)
    g_q = g_mla_q[:, None, :]
    g_kv = g_mla_kv[:, None, :]
    kr_ctx = jnp.swapaxes(cache_mla_kr, 2, 3)

    def gla_layer(i_ab, exact):
        def run(qa, ka, va, ga, ld):
            o_p, stf, stb = _gla(SEQ, BATCH, 0, qa, ka, va, ga, ld, gg, i_ab, exact=exact)
            o_s, _, _ = _gla(DEC_SEQ, DEC_BATCH, p_blk, qa, ka, va, ga, ld, gg, i_ab,
                             (state_gla_fwd, state_gla_bwd), exact=exact)
            return o_p, o_s, stf, stb
        return run

    st_f, st_b, sk, sv, ckv_out, ckr_out = [], [], [], [], [], []
    for l in range(DEPTH):
        i = l // 2
        if l % 2 == 0:
            qa, ka, va, ga, ld, qb, ld_min, kbt, vbt, kb_s, vb_s = _in_even(
                xs, l, mods, g_norm, w_in, w_gk, b_gk, cos_b, sin_b)
            factorisable = jnp.min(ld_min) * GLA_CHUNK >= -GLA_SAFE_TOTAL
            o_gla_p, o_gla_s, stf, stb = lax.cond(factorisable, gla_layer(i, False), gla_layer(i, True),
                                                  qa, ka, va, ga, ld)
            o_swa_p = _swa_prompt(swa_sink, i, qb, kbt, vbt)
            o_swa_s = _swa_sample(swa_sink, i, qb, kb_s, vb_s, kc, vc)
            xs = _out_layer(xs, l, mods, g_norm, (o_gla_p, o_gla_s), (o_swa_p, o_swa_s), 0, w_out_ab, i,
                            w_ff1, w_ff2, l == DEPTH - 1)
            st_f.append(stf)
            st_b.append(stb)
            sk.append(kbt)
            sv.append(vbt)
        else:
            q, k, vt, ckv, krt = _in_odd(xs[0], l, mods, g_norm, w_down, g_q, g_kv, w_uq, w_uk, w_uvt, cos_c, sin_c)
            o_mla = (_mla_prompt(q, k, vt), _mla_sample(q, k, vt, cache_mla_ckv, kr_ctx, w_uk, w_uvt, i))
            xs = _out_layer(xs, l, mods, g_norm, o_mla, o_mla, 1, w_mla_o, i, w_ff1, w_ff2, l == DEPTH - 1)
            ckv_out.append(ckv.reshape(BATCH, SEQ, KV_LORA))
            ckr_out.append(krt)

    y_prompt = xs[0].reshape(BATCH, SEQ, D_MODEL)
    y_sample = xs[1].reshape(DEC_BATCH, DEC_SEQ, D_MODEL)
    swa_cache = lambda parts: jnp.transpose(
        jnp.stack(parts, axis=1).reshape(BATCH, len(parts), KV_B, D_B, SEQ), (0, 1, 4, 2, 3))
    return (y_prompt, y_sample, jnp.stack(st_f, axis=1), jnp.stack(st_b, axis=1), swa_cache(sk), swa_cache(sv),
            jnp.stack(ckv_out, axis=1), jnp.swapaxes(jnp.stack(ckr_out, axis=1), 2, 3))
```

```python
import functools

import jax
import jax.numpy as jnp
from jax import lax
from jax.experimental import pallas as pl
from jax.experimental.pallas import tpu as pltpu

F32 = jnp.float32
BF16 = jnp.bfloat16

D_MODEL = 1024
BATCH = 16
SEQ = 256
DEPTH = 4
DEC_BATCH = 2
DEC_SEQ = 1024
PAST_LEN = 512
GRID_W = 64
D_FF = 4 * D_MODEL
EPS = 1e-6
ROPE_BASE = 10000.0
H_A = 4
DK_A = 64
DV_A = 128
GK_RANK = 16
GATE_NORM = 16.0
GLA_CHUNK = 64
H_B = 8
KV_B = 2
G_B = H_B // KV_B
D_B = 64
WINDOW = 128
H_C = 16
NOPE_C = 64
ROPE_C = 32
V_C = 64
Q_LORA = 384
KV_LORA = 256

N_PROMPT = BATCH * SEQ
N_SAMPLE = DEC_BATCH * DEC_SEQ
N_TOK = N_PROMPT + N_SAMPLE
N_MOD_ROWS = 8
QA_W = H_A * DK_A
VA_W = H_A * DV_A
QB_W = H_B * D_B
KB_W = KV_B * D_B
LANES = 128
HEAD_PAD_C = 128
QC_W = H_C * HEAD_PAD_C
VC_W = H_C * V_C
DOWN_RAW_W = Q_LORA + KV_LORA + ROPE_C
DOWN_W = Q_LORA + KV_LORA + LANES
ROPE_HALF_C = ROPE_C // 2
NOPE_LO_C = LANES // 2 - ROPE_HALF_C

TM_IN = 512
TM_OUT = 1024
TK_FF = 512
OUT_PROLOGUE_ROWS = 256
TN_MOD = 3072
PREP_ROWS = 256
GLA_GROUP = 256
GLA_SCAN_UNROLL = 4
GLA_SAFE_TOTAL = 160.0
SWA_QB = 128
SWA_WIN = 3 * SWA_QB
MLA_QB = 256
HEADS_PER_DOT_C = 4
MLA_HEAD_GROUP_P = 16
MLA_HEAD_GROUP_S = 4
VMEM_LIMIT = 60 * 1024 * 1024
NEG_BIG = -1e30


def _cparams(sem):
    return pltpu.CompilerParams(dimension_semantics=sem, vmem_limit_bytes=VMEM_LIMIT)


def _dot(a, b):
    return jnp.dot(a, b, preferred_element_type=F32)


def _dot_nt(a, b):
    return lax.dot_general(a, b, (((1,), (1,)), ((), ())), preferred_element_type=F32)


def _dot_tn(a, b):
    return lax.dot_general(a, b, (((0,), (0,)), ((), ())), preferred_element_type=F32)


def _rms(x, g):
    return x * lax.rsqrt(jnp.mean(x * x, axis=-1, keepdims=True) + EPS) * g


def _silu(x):
    return x / (1.0 + jnp.exp(-x))


def _rope(x, cos, sin, half):
    lane = lax.broadcasted_iota(jnp.int32, x.shape, 1)
    first = (lane % (2 * half)) < half
    partner = jnp.where(first, pltpu.roll(x, LANES - half, 1), pltpu.roll(x, half, 1))
    return x * cos + partner * sin


def _rope_c(x, cos, sin):
    return x * cos + pltpu.roll(x, LANES // 2, 1) * sin


def _rope_split_c(r):
    q = ROPE_C // 4
    first = jnp.concatenate([r[..., 0:q], r[..., 2 * q:3 * q]], axis=-1)
    second = jnp.concatenate([r[..., q:2 * q], r[..., 3 * q:4 * q]], axis=-1)
    return first, second


def _split3(x):
    x1 = x.astype(BF16)
    r1 = x - x1.astype(F32)
    x2 = r1.astype(BF16)
    x3 = (r1 - x2.astype(F32)).astype(BF16)
    return x1, x2, x3


def _dot3(t, parts):
    return _dot(t, parts[2]) + _dot(t, parts[1]) + _dot(t, parts[0])


def _ones_where(cond):
    return jnp.where(cond, 1.0, 0.0).astype(BF16)


def _mod_row(tile, tm):
    n_prompt_tiles = N_PROMPT // tm
    tiles_per_seq = DEC_SEQ // tm
    return jnp.where(tile < n_prompt_tiles, 0, 1 + (tile - n_prompt_tiles) // tiles_per_seq)


def _layer_spec(shape, idx):
    return pl.BlockSpec((None,) + shape, lambda *_: (idx,) + (0,) * len(shape))


def _mod_spec(tm):
    return pl.BlockSpec((None, 6, D_MODEL), lambda i, *_: (_mod_row(i, tm), 0, 0))


def _split_specs(tm, width, col=0):
    n_prompt_tiles = N_PROMPT // tm
    return [pl.BlockSpec((tm, width), lambda i, *_: (jnp.minimum(i, n_prompt_tiles - 1), col)),
            pl.BlockSpec((tm, width), lambda i, *_: (jnp.maximum(i - n_prompt_tiles, 0), col))]


def _split_shapes(width, dtype):
    return [jax.ShapeDtypeStruct((N_PROMPT, width), dtype), jax.ShapeDtypeStruct((N_SAMPLE, width), dtype)]


def _is_prompt_tile(tm):
    return pl.program_id(0) < N_PROMPT // tm


def _load_split(pair, tm, rows=slice(None)):
    return jnp.where(_is_prompt_tile(tm), pair[0][rows, :], pair[1][rows, :])


def _store_split(pair, tm, value, rows=slice(None)):
    is_prompt = _is_prompt_tile(tm)

    @pl.when(is_prompt)
    def _():
        pair[0][rows, :] = value

    @pl.when(jnp.logical_not(is_prompt))
    def _():
        pair[1][rows, :] = value


_C_QA, _C_KA, _C_VA, _C_GA = 0, QA_W, 2 * QA_W, 2 * QA_W + VA_W
_C_QB = _C_GA + VA_W
_C_KB = _C_QB + QB_W
_C_VB = _C_KB + KB_W
_C_LO = _C_VB + KB_W
AB_IN = _C_LO + 2 * GK_RANK


def _prep_even_kernel(wt_ref, o_ref):
    raw_lo = _C_QB
    raw_qb = raw_lo + 2 * GK_RANK
    o_ref[:, 0:_C_QB] = wt_ref[0:raw_lo, :].T.astype(BF16)
    o_ref[:, _C_QB:_C_LO] = wt_ref[raw_qb:AB_IN, :].T.astype(BF16)
    o_ref[:, _C_LO:AB_IN] = wt_ref[raw_lo:raw_lo + LANES, :].T[:, 0:2 * GK_RANK].astype(BF16)


def _prep_even(w_in_t):
    n = w_in_t.shape[0]
    return pl.pallas_call(
        _prep_even_kernel,
        grid=(n, D_MODEL // PREP_ROWS),
        in_specs=[pl.BlockSpec((None, AB_IN, PREP_ROWS), lambda l, r: (l, 0, r))],
        out_specs=pl.BlockSpec((None, PREP_ROWS, AB_IN), lambda l, r: (l, r, 0)),
        out_shape=jax.ShapeDtypeStruct((n, D_MODEL, AB_IN), BF16),
        compiler_params=_cparams(("parallel", "parallel")),
        name="prep_even",
    )(w_in_t)


def _prep_odd_kernel(wdt_ref, wuq_ref, wukv_ref, od_ref, ouq_ref, ouk_ref, ouvt_ref):
    def head_group(nope, rope_first, rope_second):
        rows = nope.shape[0]
        zero_half = jnp.zeros((rows, ROPE_HALF_C), F32)
        return jnp.concatenate(
            [zero_half if rope_first is None else rope_first, nope[:, 0:NOPE_LO_C],
             zero_half if rope_second is None else rope_second, nope[:, NOPE_LO_C:NOPE_C],
             jnp.zeros((rows, HEAD_PAD_C - NOPE_C - ROPE_C), F32)], axis=1).astype(BF16)

    n_ckv = Q_LORA + KV_LORA
    od_ref[:, 0:n_ckv] = wdt_ref[0:n_ckv, :].T.astype(BF16)
    tail = wdt_ref[DOWN_RAW_W - LANES:DOWN_RAW_W, :].T
    od_ref[:, n_ckv:DOWN_W] = head_group(jnp.zeros((D_MODEL, NOPE_C), F32), *_rope_split_c(tail[:, LANES - ROPE_C:]))
    hd_q = NOPE_C + ROPE_C
    for h in range(H_C):
        wq = wuq_ref[:, h * hd_q:(h + 1) * hd_q]
        ouq_ref[:, h * HEAD_PAD_C:(h + 1) * HEAD_PAD_C] = head_group(wq[:, 0:NOPE_C], *_rope_split_c(wq[:, NOPE_C:]))
    wukv = wukv_ref[...]
    for h in range(H_C):
        ouk_ref[:, h * HEAD_PAD_C:(h + 1) * HEAD_PAD_C] = head_group(
            wukv[:, h * HEAD_PAD_C:h * HEAD_PAD_C + NOPE_C], None, None)
    wukv_t = wukv.T
    for h in range(H_C):
        ouvt_ref[h * V_C:(h + 1) * V_C, :] = wukv_t[h * HEAD_PAD_C + NOPE_C:(h + 1) * HEAD_PAD_C, :].astype(BF16)


def _prep_odd(w_down_t, w_uq, w_ukv):
    n = w_down_t.shape[0]
    spec = lambda r, c: pl.BlockSpec((None, r, c), lambda l: (l, 0, 0))
    return pl.pallas_call(
        _prep_odd_kernel,
        grid=(n,),
        in_specs=[spec(DOWN_RAW_W, D_MODEL), spec(Q_LORA, H_C * (NOPE_C + ROPE_C)), spec(KV_LORA, QC_W)],
        out_specs=[spec(D_MODEL, DOWN_W), spec(Q_LORA, QC_W), spec(KV_LORA, QC_W), spec(VC_W, KV_LORA)],
        out_shape=[jax.ShapeDtypeStruct((n, D_MODEL, DOWN_W), BF16), jax.ShapeDtypeStruct((n, Q_LORA, QC_W), BF16),
                   jax.ShapeDtypeStruct((n, KV_LORA, QC_W), BF16), jax.ShapeDtypeStruct((n, VC_W, KV_LORA), BF16)],
        compiler_params=_cparams(("parallel",)),
        name="prep_odd",
    )(w_down_t, w_uq, w_ukv)


def _mod_block(c_ref, w_ref, b_ref):
    s = _silu(c_ref[...])
    return _dot(s.astype(BF16), w_ref[...].astype(BF16)) + b_ref[...]


def _mod_kernel(c_ref, w_ref, b_ref, o_ref):
    o_ref[...] = _mod_block(c_ref, w_ref, b_ref)


def _modulation(cvecs, w_mod, b_mod, layer):
    return pl.pallas_call(
        _mod_kernel,
        grid=(6 * D_MODEL // TN_MOD,),
        in_specs=[
            pl.BlockSpec((N_MOD_ROWS, D_MODEL), lambda j: (0, 0)),
            pl.BlockSpec((None, D_MODEL, TN_MOD), lambda j: (layer, 0, j)),
            pl.BlockSpec((None, 1, TN_MOD), lambda j: (layer, 0, j)),
        ],
        out_specs=pl.BlockSpec((N_MOD_ROWS, TN_MOD), lambda j: (0, j)),
        out_shape=jax.ShapeDtypeStruct((N_MOD_ROWS, 6 * D_MODEL), F32),
        compiler_params=_cparams(("parallel",)),
        name="adaln_mod",
    )(cvecs, w_mod, b_mod)


def _x_specs(n_x, tm):
    if n_x == 1:
        return [pl.BlockSpec((tm, D_MODEL), lambda i, *_: (i, 0))]
    return _split_specs(tm, D_MODEL)


def _load_x(n_x, refs, tm, rows=slice(None)):
    if n_x == 1:
        return refs[0][rows, :]
    return _load_split(refs[:2], tm, rows)


def _in_even_kernel(n_x, *refs):
    (mod_ref, g_ref, w_ref, wgk_ref, bgk_ref, cos_ref, sin_ref,
     qa_ref, ka_ref, va_ref, ga_ref, ld_ref, qb_ref, ldmin_ref, kbt_ref, vbt_ref, kbs_ref, vbs_ref) = refs[n_x:]
    x = _load_x(n_x, refs, TM_IN)
    h = _rms(x, g_ref[0:1, :]) * (1.0 + mod_ref[1:2, :]) + mod_ref[0:1, :]
    hb = h.astype(BF16)
    qa_ref[...] = _dot(hb, w_ref[:, _C_QA:_C_KA]) * (DK_A ** -0.5)
    ka_ref[...] = _dot(hb, w_ref[:, _C_KA:_C_VA])
    va_ref[...] = _dot(hb, w_ref[:, _C_VA:_C_GA]).astype(BF16)
    ga_ref[...] = _dot(hb, w_ref[:, _C_GA:_C_QB])
    cos = cos_ref[...]
    sin = sin_ref[...]
    qb = _dot(hb, w_ref[:, _C_QB:_C_KB])
    for j in range(QB_W // LANES):
        qj = qb[:, j * LANES:(j + 1) * LANES]
        qb_ref[:, j * LANES:(j + 1) * LANES] = (_rope(qj, cos, sin, D_B // 4) * (D_B ** -0.5)).astype(BF16)
    kvb = _dot(hb, w_ref[:, _C_KB:_C_LO])
    kb = _rope(kvb[:, :KB_W], cos, sin, D_B // 4)
    vb = kvb[:, KB_W:]
    lo = _dot(hb, w_ref[:, _C_LO:AB_IN]).astype(BF16)
    z = _dot(lo, wgk_ref[...]) + bgk_ref[...]
    ld = (jnp.minimum(z, 0.0) - jnp.log(1.0 + jnp.exp(-jnp.abs(z)))) * (1.0 / GATE_NORM)
    ld_ref[...] = ld
    ld_min = jnp.min(jnp.min(ld, axis=0, keepdims=True), axis=1, keepdims=True)
    ldmin_ref[...] = jnp.broadcast_to(ld_min, ldmin_ref.shape)

    is_prompt = _is_prompt_tile(TM_IN)

    @pl.when(is_prompt)
    def _():
        for s in range(TM_IN // SEQ):
            kbt_ref[s] = kb[s * SEQ:(s + 1) * SEQ, :].T
            vbt_ref[s] = vb[s * SEQ:(s + 1) * SEQ, :].T

    @pl.when(jnp.logical_not(is_prompt))
    def _():
        kbs_ref[...] = kb
        vbs_ref[...] = vb


def _in_even(xs, layer, mods, g_norm, w_in, w_gk, b_gk, cos, sin):
    tm = TM_IN
    i_ab = layer // 2
    n_prompt_tiles = N_PROMPT // tm
    seq_per_tile = tm // SEQ
    row = lambda i: (i, 0)
    widths = (QA_W, QA_W, VA_W, VA_W, 2 * QA_W, QB_W)
    dtypes = (F32, F32, BF16, F32, F32, BF16)
    kv_specs = 2 * [pl.BlockSpec((seq_per_tile, KB_W, SEQ), lambda i: (jnp.minimum(i, n_prompt_tiles - 1), 0, 0))] + \
        2 * [pl.BlockSpec((tm, KB_W), lambda i: (jnp.maximum(i - n_prompt_tiles, 0), 0))]
    kv_shapes = 2 * [jax.ShapeDtypeStruct((BATCH, KB_W, SEQ), F32)] + 2 * [jax.ShapeDtypeStruct((N_SAMPLE, KB_W), F32)]
    return pl.pallas_call(
        functools.partial(_in_even_kernel, len(xs)),
        grid=(N_TOK // tm,),
        in_specs=_x_specs(len(xs), tm) + [
            _mod_spec(tm),
            _layer_spec((4, D_MODEL), layer),
            _layer_spec((D_MODEL, AB_IN), i_ab),
            _layer_spec((2 * GK_RANK, 2 * QA_W), i_ab),
            _layer_spec((1, 2 * QA_W), i_ab),
            pl.BlockSpec((tm, LANES), row),
            pl.BlockSpec((tm, LANES), row),
        ],
        out_specs=[pl.BlockSpec((tm, w), row) for w in widths] + [
            pl.BlockSpec((None, 8, LANES), lambda i: (i, 0, 0))] + kv_specs,
        out_shape=[jax.ShapeDtypeStruct((N_TOK, w), d) for w, d in zip(widths, dtypes)] + [
            jax.ShapeDtypeStruct((N_TOK // tm, 8, LANES), F32)] + kv_shapes,
        compiler_params=_cparams(("arbitrary",)),
        name="in_even",
    )(*xs, mods, g_norm, w_in, w_gk, b_gk, cos, sin)


def _gla_kernel(seq_len, has_s0, exact, *refs):
    qa_ref, ka_ref, va_ref, ga_ref, ld_ref, gg_ref = refs[:6]
    s0_refs = refs[6:8] if has_s0 else None
    o_ref, stf_ref, stb_ref = refs[8:11] if has_s0 else refs[6:9]
    st_ref, qi_ref, ki_ref, qcat_ref, ks_ref, dec_ref, stcat_ref, acc_ref = refs[-8:]
    b_ref = qi_ref if exact else None
    n_groups = seq_len // GLA_GROUP
    cpg = GLA_GROUP // GLA_CHUNK
    n_chunks = seq_len // GLA_CHUNK
    n_pairs = H_A // 2
    pair_k = 2 * DK_A
    pair_v = 2 * DV_A

    def chunk_masks():
        r_i = lax.broadcasted_iota(jnp.int32, (GLA_GROUP, GLA_GROUP), 0)
        c_i = lax.broadcasted_iota(jnp.int32, (GLA_GROUP, GLA_GROUP), 1)
        same = (r_i // GLA_CHUNK) == (c_i // GLA_CHUNK)
        return same, same & (c_i <= r_i), same & (c_i >= r_i)

    def scale_group(j, carry):
        same, mask_f, mask_b = chunk_masks()
        t_cum = (_ones_where(mask_f), _ones_where(mask_b))
        t_all = _ones_where(same)
        s_r = lax.broadcasted_iota(jnp.int32, (8, GLA_GROUP), 0)
        s_c = lax.broadcasted_iota(jnp.int32, (8, GLA_GROUP), 1)
        t_sel = _ones_where(s_r == s_c // GLA_CHUNK)
        rows = pl.ds(pl.multiple_of(j * GLA_GROUP, GLA_GROUP), GLA_GROUP)
        q = qa_ref[rows, :]
        k = ka_ref[rows, :]
        for d in range(2):
            parts = _split3(ld_ref[rows, d * QA_W:(d + 1) * QA_W])
            b = _dot3(t_cum[d], parts)
            tot = _dot3(t_all, parts)
            dec8 = jnp.exp(_dot3(t_sel, parts))
            ref = 0.5 * tot
            if exact:
                b_ref[d, rows, :] = b
            else:
                qi_ref[d, rows, :] = (q * jnp.exp(b - ref)).astype(BF16)
                ki_ref[d, rows, :] = (k * jnp.exp(ref - b)).astype(BF16)
            ks_ref[d, rows, :] = (k * jnp.exp(tot - b)).astype(BF16)
            q_inter = (q * jnp.exp(b)).astype(BF16)
            for p in range(n_pairs):
                qcat_ref[rows, p * 2 * pair_k + d * pair_k:p * 2 * pair_k + (d + 1) * pair_k] = (
                    q_inter[:, p * pair_k:(p + 1) * pair_k])
            for cc in range(cpg):
                dec_ref[d, j * cpg + cc] = jnp.broadcast_to(dec8[cc:cc + 1, :], (8, QA_W))
        return carry

    lax.fori_loop(0, n_groups, scale_group, 0)

    zpad = jnp.zeros((DK_A, DV_A), F32)
    for d in range(2):
        for p in range(n_pairs):
            if has_s0:
                top = jnp.concatenate([s0_refs[d][2 * p], zpad], axis=0).T
                bot = jnp.concatenate([zpad, s0_refs[d][2 * p + 1]], axis=0).T
                st_ref[d, p] = jnp.concatenate([top, bot], axis=0)
            else:
                st_ref[d, p] = jnp.zeros((pair_v, pair_k), F32)

    def scan_chunk(c, carry):
        bd_r = lax.broadcasted_iota(jnp.int32, (pair_v, pair_k), 0)
        bd_c = lax.broadcasted_iota(jnp.int32, (pair_v, pair_k), 1)
        bd_mask = (bd_r // DV_A) == (bd_c // DK_A)
        for d in range(2):
            cd = c if d == 0 else n_chunks - 1 - c
            rows = pl.ds(pl.multiple_of(cd * GLA_CHUNK, GLA_CHUNK), GLA_CHUNK)
            for p in range(n_pairs):
                st = st_ref[d, p]
                stcat_ref[cd, p, :, d * pair_k:(d + 1) * pair_k] = st.astype(BF16)
                u = _dot_tn(va_ref[rows, p * pair_v:(p + 1) * pair_v], ks_ref[d, rows, p * pair_k:(p + 1) * pair_k])
                dec = dec_ref[d, cd][0:1, p * pair_k:(p + 1) * pair_k]
                st_ref[d, p] = st * dec + jnp.where(bd_mask, u, 0.0)
        return carry

    lax.fori_loop(0, n_chunks, scan_chunk, 0, unroll=GLA_SCAN_UNROLL)

    for d, out_ref in ((0, stf_ref), (1, stb_ref)):
        for p in range(n_pairs):
            st = st_ref[d, p]
            out_ref[2 * p] = st[0:DV_A, :].T[0:DK_A, :]
            out_ref[2 * p + 1] = st[DV_A:pair_v, :].T[DK_A:pair_k, :]

    def exact_intra(row0, h):
        p, hh = divmod(h, 2)
        lanes = slice(p * pair_k, (p + 1) * pair_k)
        head_lanes = (lax.broadcasted_iota(jnp.int32, (GLA_CHUNK, pair_k), 1) // DK_A) == hh
        s_idx = lax.broadcasted_iota(jnp.int32, (GLA_CHUNK, GLA_CHUNK), 0)
        t_idx = lax.broadcasted_iota(jnp.int32, (GLA_CHUNK, GLA_CHUNK), 1)
        outs = []
        for cc in range(cpg):
            r0 = row0 + cc * GLA_CHUNK
            crow = pl.ds(pl.multiple_of(r0, GLA_CHUNK), GLA_CHUNK)
            kc = ka_ref[crow, lanes]
            att_t = jnp.zeros((GLA_CHUNK, GLA_CHUNK), F32)
            for d in range(2):
                bc = b_ref[d, crow, lanes]

                def row_step(t, att_t, d=d, bc=bc, kc=kc, r0=r0):
                    grp = pl.ds(pl.multiple_of(r0 + (t // 8) * 8, 8), 8)
                    pick = lax.broadcasted_iota(jnp.int32, (8, pair_k), 0) == t % 8
                    bt = jnp.sum(jnp.where(pick, b_ref[d, grp, lanes], 0.0), axis=0, keepdims=True)
                    qt = jnp.sum(jnp.where(pick, qa_ref[grp, lanes], 0.0), axis=0, keepdims=True)
                    w = jnp.where(head_lanes, qt * kc * jnp.exp(jnp.minimum(bt - bc, 0.0)), 0.0)
                    col = jnp.sum(w, axis=1, keepdims=True)
                    allowed = (s_idx <= t) if d == 0 else (s_idx >= t)
                    return jnp.where((t_idx == t) & allowed, att_t + col, att_t)

                att_t = lax.fori_loop(0, GLA_CHUNK, row_step, att_t)
            outs.append(_dot_tn(att_t.astype(BF16), va_ref[crow, h * DV_A:(h + 1) * DV_A]))
        return jnp.concatenate(outs, axis=0)

    def out_group(j, carry):
        _, mask_f, mask_b = chunk_masks()
        lane = lax.broadcasted_iota(jnp.int32, (GLA_GROUP, pair_k), 1)
        rows = pl.ds(pl.multiple_of(j * GLA_GROUP, GLA_GROUP), GLA_GROUP)
        for p in range(n_pairs):
            for cc in range(cpg):
                c = j * cpg + cc
                crow = pl.ds(pl.multiple_of(c * GLA_CHUNK, GLA_CHUNK), GLA_CHUNK)
                acc_ref[crow, p * pair_v:(p + 1) * pair_v] = _dot_nt(
                    qcat_ref[crow, p * 2 * pair_k:(p + 1) * 2 * pair_k], stcat_ref[c, p])
        if exact:
            intras = [exact_intra(j * GLA_GROUP, h) for h in range(H_A)]
        else:
            atts = []
            for h in range(H_A):
                p, hh = divmod(h, 2)
                head_lanes = _ones_where((lane // DK_A) == hh)
                att = None
                for d, mask in ((0, mask_f), (1, mask_b)):
                    qm = qi_ref[d, rows, p * pair_k:(p + 1) * pair_k] * head_lanes
                    a = jnp.where(mask, _dot_nt(qm, ki_ref[d, rows, p * pair_k:(p + 1) * pair_k]), 0.0)
                    att = a if att is None else att + a
                atts.append(att.astype(BF16))
            intras = [_dot(atts[h], va_ref[rows, h * DV_A:(h + 1) * DV_A]) for h in range(H_A)]
        for h in range(H_A):
            o = acc_ref[rows, h * DV_A:(h + 1) * DV_A] + intras[h]
            gate = _silu(ga_ref[rows, h * DV_A:(h + 1) * DV_A])
            o_ref[rows, h * DV_A:(h + 1) * DV_A] = (_rms(o, gg_ref[...]) * gate).astype(BF16)
        return carry

    lax.fori_loop(0, n_groups, out_group, 0)


def _gla(seq_len, n_seq, row_block0, qa, ka, va, ga, ld, g_gla, i_ab, s0=None, exact=False):
    has_s0 = s0 is not None
    n_chunks = seq_len // GLA_CHUNK
    n_pairs = H_A // 2
    rows = lambda b: (row_block0 + b, 0)
    st_spec = pl.BlockSpec((None, H_A, DK_A, DV_A), lambda b: (b, 0, 0, 0))
    in_specs = [
        pl.BlockSpec((seq_len, QA_W), rows),
        pl.BlockSpec((seq_len, QA_W), rows),
        pl.BlockSpec((seq_len, VA_W), rows),
        pl.BlockSpec((seq_len, VA_W), rows),
        pl.BlockSpec((seq_len, 2 * QA_W), rows),
        _layer_spec((1, DV_A), i_ab),
    ]
    args = [qa, ka, va, ga, ld, g_gla]
    if has_s0:
        s0_spec = pl.BlockSpec((None, None, H_A, DK_A, DV_A), lambda b: (b, i_ab, 0, 0, 0))
        in_specs += [s0_spec, s0_spec]
        args += list(s0)
    st_shape = jax.ShapeDtypeStruct((n_seq, H_A, DK_A, DV_A), F32)
    return pl.pallas_call(
        functools.partial(_gla_kernel, seq_len, has_s0, exact),
        grid=(n_seq,),
        in_specs=in_specs,
        out_specs=[pl.BlockSpec((seq_len, VA_W), lambda b: (b, 0)), st_spec, st_spec],
        out_shape=[jax.ShapeDtypeStruct((n_seq * seq_len, VA_W), BF16), st_shape, st_shape],
        scratch_shapes=[
            pltpu.VMEM((2, n_pairs, 2 * DV_A, 2 * DK_A), F32),
            pltpu.VMEM((2, seq_len, QA_W), F32 if exact else BF16),
            pltpu.VMEM((2, seq_len, QA_W), BF16),
            pltpu.VMEM((seq_len, 2 * QA_W), BF16),
            pltpu.VMEM((2, seq_len, QA_W), BF16),
            pltpu.VMEM((2, n_chunks, 8, QA_W), F32),
            pltpu.VMEM((n_chunks, n_pairs, 2 * DV_A, 4 * DK_A), BF16),
            pltpu.VMEM((seq_len, VA_W), F32),
        ],
        compiler_params=_cparams(("parallel",)),
        name=("gla_s" if has_s0 else "gla_p") + ("_exact" if exact else ""),
    )(*args)


def _swa_head_softmax(pieces, sink):
    m = sink
    for s, _, _ in pieces:
        m = jnp.maximum(m, jnp.max(s, axis=-1, keepdims=True))
    den = jnp.exp(sink - m)
    acc = None
    for s, v, transposed in pieces:
        e = jnp.exp(s - m)
        den = den + jnp.sum(e, axis=-1, keepdims=True)
        pv = _dot_nt(e.astype(BF16), v) if transposed else _dot(e.astype(BF16), v)
        acc = pv if acc is None else acc + pv
    return acc / den


def _dup_groups(x):
    lane = lax.broadcasted_iota(jnp.int32, x.shape, 1)
    swapped = pltpu.roll(x, D_B, 1)
    low = lane < D_B
    return jnp.where(low, x, swapped).astype(BF16), jnp.where(low, swapped, x).astype(BF16)


def _dup_groups_t(xt):
    g0, g1 = xt[0:D_B, :], xt[D_B:, :]
    return jnp.concatenate([g0, g0], axis=0).astype(BF16), jnp.concatenate([g1, g1], axis=0).astype(BF16)


def _swa_heads(sink_ref, i_ab, q_ref, kv_pieces, o_ref, n_rows):
    lane = lax.broadcasted_iota(jnp.int32, (n_rows, LANES), 1)
    head_pieces = []
    for h in range(H_B):
        j, hh = divmod(h, 2)
        g = h // G_B
        qm = q_ref[:, j * LANES:(j + 1) * LANES] * _ones_where((lane // D_B) == hh)
        pieces = []
        for keys, vals, transposed, mask in kv_pieces:
            s = _dot(qm, keys[g]) if transposed else _dot_nt(qm, keys[g])
            if mask is not None:
                s = jnp.where(mask, s, NEG_BIG)
            pieces.append((s, vals[g], transposed))
        head_pieces.append(pieces)
    outs = [_swa_head_softmax(pieces, sink_ref[i_ab, h]) for h, pieces in enumerate(head_pieces)]
    for j in range(H_B // 2):
        o_ref[:, j * LANES:(j + 1) * LANES] = jnp.where(lane < D_B, outs[2 * j], outs[2 * j + 1]).astype(BF16)


def _swa_prompt_kernel(i_ab, sink_ref, q_ref, kt_ref, vt_ref, o_ref):
    _swa_heads(sink_ref, i_ab, q_ref, [(_dup_groups_t(kt_ref[...]), _dup_groups_t(vt_ref[...]), True, None)],
               o_ref, SEQ)


def _swa_prompt(sink, i_ab, qb, kbt, vbt):
    seq = lambda b: (b, 0)
    seq_t = pl.BlockSpec((None, KB_W, SEQ), lambda b: (b, 0, 0))
    return pl.pallas_call(
        functools.partial(_swa_prompt_kernel, i_ab),
        grid=(BATCH,),
        in_specs=[pl.BlockSpec(memory_space=pltpu.SMEM), pl.BlockSpec((SEQ, QB_W), seq), seq_t, seq_t],
        out_specs=pl.BlockSpec((SEQ, QB_W), seq),
        out_shape=jax.ShapeDtypeStruct((N_PROMPT, QB_W), BF16),
        compiler_params=_cparams(("parallel",)),
        name="swa_p",
    )(sink, qb, kbt, vbt)


def _swa_sample_kernel(i_ab, sink_ref, q_ref, k_ref, v_ref, kct_ref, vct_ref, o_ref):
    n = pl.program_id(1)
    start = pl.multiple_of(jnp.clip((n - 1) * SWA_QB, 0, DEC_SEQ - SWA_WIN), SWA_QB)
    local = (_dup_groups(k_ref[pl.ds(start, SWA_WIN), :]), _dup_groups(v_ref[pl.ds(start, SWA_WIN), :]))
    ctx = (_dup_groups_t(kct_ref[...]), _dup_groups_t(vct_ref[...]))
    qi = n * SWA_QB + lax.broadcasted_iota(jnp.int32, (SWA_QB, SWA_WIN), 0)
    ki = start + lax.broadcasted_iota(jnp.int32, (SWA_QB, SWA_WIN), 1)
    band = jnp.abs(qi - ki) <= WINDOW
    _swa_heads(sink_ref, i_ab, q_ref, [ctx + (True, None), local + (False, band)], o_ref, SWA_QB)


def _swa_sample(sink, i_ab, qb, kb, vb, kc, vc):
    nqb = DEC_SEQ // SWA_QB
    q0 = N_PROMPT // SWA_QB
    return pl.pallas_call(
        functools.partial(_swa_sample_kernel, i_ab),
        grid=(DEC_BATCH, nqb),
        in_specs=[
            pl.BlockSpec(memory_space=pltpu.SMEM),
            pl.BlockSpec((SWA_QB, QB_W), lambda b, n: (q0 + b * nqb + n, 0)),
            pl.BlockSpec((DEC_SEQ, KB_W), lambda b, n: (b, 0)),
            pl.BlockSpec((DEC_SEQ, KB_W), lambda b, n: (b, 0)),
            pl.BlockSpec((None, None, KB_W, PAST_LEN), lambda b, n: (b, i_ab, 0, 0)),
            pl.BlockSpec((None, None, KB_W, PAST_LEN), lambda b, n: (b, i_ab, 0, 0)),
        ],
        out_specs=pl.BlockSpec((SWA_QB, QB_W), lambda b, n: (b * nqb + n, 0)),
        out_shape=jax.ShapeDtypeStruct((N_SAMPLE, QB_W), BF16),
        compiler_params=_cparams(("parallel", "parallel")),
        name="swa_s",
    )(sink, qb, kb, vb, kc, vc)


def _in_odd_kernel(x_ref, mod_ref, g_ref, wd_ref, gq_ref, gkv_ref, wuq_ref, wuk_ref, wuvt_ref, cos_ref, sin_ref,
                   q_ref, k_ref, vt_ref, ckv_ref, krt_ref):
    h = _rms(x_ref[...], g_ref[0:1, :]) * (1.0 + mod_ref[1:2, :]) + mod_ref[0:1, :]
    hb = h.astype(BF16)
    cos = cos_ref[...]
    sin = sin_ref[...]
    c_q = _dot(hb, wd_ref[:, 0:Q_LORA])
    c_kv = _rms(_dot(hb, wd_ref[:, Q_LORA:Q_LORA + KV_LORA]), gkv_ref[...])
    kr = _rope_c(_dot(hb, wd_ref[:, Q_LORA + KV_LORA:DOWN_W]), cos, sin)
    cqb = _rms(c_q, gq_ref[...]).astype(BF16)
    ckvb = c_kv.astype(BF16)
    scale = (NOPE_C + ROPE_C) ** -0.5
    group_w = HEADS_PER_DOT_C * HEAD_PAD_C
    for grp in range(H_C // HEADS_PER_DOT_C):
        gsl = slice(grp * group_w, (grp + 1) * group_w)
        qg = _dot(cqb, wuq_ref[:, gsl])
        kg = _dot(ckvb, wuk_ref[:, gsl])
        for j in range(HEADS_PER_DOT_C):
            sl = slice(j * HEAD_PAD_C, (j + 1) * HEAD_PAD_C)
            osl = slice(grp * group_w + j * HEAD_PAD_C, grp * group_w + (j + 1) * HEAD_PAD_C)
            q_ref[:, osl] = (_rope_c(qg[:, sl], cos, sin) * scale).astype(BF16)
            k_ref[:, osl] = (kg[:, sl] + kr).astype(BF16)
    vt_ref[...] = _dot_nt(wuvt_ref[...], ckvb).astype(BF16)

    @pl.when(_is_prompt_tile(TM_IN))
    def _():
        ckv_ref[...] = c_kv
        q4 = ROPE_C // 4
        half = LANES // 2
        for s in range(TM_IN // SEQ):
            t = kr[s * SEQ:(s + 1) * SEQ, :].T
            krt_ref[s] = jnp.concatenate([t[0:q4], t[half:half + q4], t[q4:2 * q4], t[half + q4:half + 2 * q4]], axis=0)


def _in_odd(x, layer, mods, g_norm, w_down, g_q, g_kv, w_uq, w_uk, w_uvt, cos, sin):
    tm = TM_IN
    i_c = layer // 2
    n_prompt_tiles = N_PROMPT // tm
    row = lambda i: (i, 0)
    prompt_row = lambda i: (jnp.minimum(i, n_prompt_tiles - 1), 0)
    prompt_seq = lambda i: (jnp.minimum(i, n_prompt_tiles - 1), 0, 0)
    return pl.pallas_call(
        _in_odd_kernel,
        grid=(N_TOK // tm,),
        in_specs=[
            pl.BlockSpec((tm, D_MODEL), row),
            _mod_spec(tm),
            _layer_spec((4, D_MODEL), layer),
            _layer_spec((D_MODEL, DOWN_W), i_c),
            _layer_spec((1, Q_LORA), i_c),
            _layer_spec((1, KV_LORA), i_c),
            _layer_spec((Q_LORA, QC_W), i_c),
            _layer_spec((KV_LORA, QC_W), i_c),
            _layer_spec((VC_W, KV_LORA), i_c),
            pl.BlockSpec((tm, LANES), row),
            pl.BlockSpec((tm, LANES), row),
        ],
        out_specs=[pl.BlockSpec((tm, QC_W), row), pl.BlockSpec((tm, QC_W), row),
                   pl.BlockSpec((VC_W, tm), lambda i: (0, i)),
                   pl.BlockSpec((tm, KV_LORA), prompt_row), pl.BlockSpec((tm // SEQ, ROPE_C, SEQ), prompt_seq)],
        out_shape=[jax.ShapeDtypeStruct((N_TOK, QC_W), BF16), jax.ShapeDtypeStruct((N_TOK, QC_W), BF16),
                   jax.ShapeDtypeStruct((VC_W, N_TOK), BF16),
                   jax.ShapeDtypeStruct((N_PROMPT, KV_LORA), F32), jax.ShapeDtypeStruct((BATCH, ROPE_C, SEQ), F32)],
        compiler_params=_cparams(("arbitrary",)),
        name="in_odd",
    )(x, mods, g_norm, w_down, g_q, g_kv, w_uq, w_uk, w_uvt, cos, sin)


def _reduce_rows(x, op, reduce_fn):
    while x.shape[0] % 16 == 0:
        half = x.shape[0] // 2
        x = op(x[:half], x[half:])
    return reduce_fn(x, axis=0, keepdims=True)


def _mla_heads(q_ref, kv_pieces, o_ref, ot_ref, group):
    for h0 in range(0, H_C, group):
        heads = range(h0, h0 + group)
        scores = []
        for hd in heads:
            sl = slice(hd * HEAD_PAD_C, (hd + 1) * HEAD_PAD_C)
            scores.append([_dot_nt(k_ref[:, sl], q_ref[:, sl]) for k_ref, _ in kv_pieces])
        exps, dens = [], []
        for per_piece in scores:
            m = None
            for s in per_piece:
                sm = _reduce_rows(s, jnp.maximum, jnp.max)
                m = sm if m is None else jnp.maximum(m, sm)
            es = [jnp.exp(s - m) for s in per_piece]
            den = None
            for e in es:
                part = _reduce_rows(e, jnp.add, jnp.sum)
                den = part if den is None else den + part
            exps.append([e.astype(BF16) for e in es])
            dens.append(den)
        for hd, es, den in zip(heads, exps, dens):
            acc = None
            for e, (_, vt_ref) in zip(es, kv_pieces):
                pv = _dot(vt_ref[hd * V_C:(hd + 1) * V_C, :], e)
                acc = pv if acc is None else acc + pv
            ot_ref[hd * V_C:(hd + 1) * V_C, :] = acc / den
    o_ref[...] = ot_ref[...].T.astype(BF16)


def _mla_prompt_kernel(q_ref, k_ref, vt_ref, o_ref, ot_ref):
    _mla_heads(q_ref, [(k_ref, vt_ref)], o_ref, ot_ref, MLA_HEAD_GROUP_P)


def _mla_prompt(q, k, vt):
    seq = lambda b: (b, 0)
    return pl.pallas_call(
        _mla_prompt_kernel,
        grid=(BATCH,),
        in_specs=[pl.BlockSpec((SEQ, QC_W), seq), pl.BlockSpec((SEQ, QC_W), seq),
                  pl.BlockSpec((VC_W, SEQ), lambda b: (0, b))],
        out_specs=pl.BlockSpec((SEQ, VC_W), seq),
        out_shape=jax.ShapeDtypeStruct((N_PROMPT, VC_W), BF16),
        scratch_shapes=[pltpu.VMEM((VC_W, SEQ), F32)],
        compiler_params=_cparams(("parallel",)),
        name="mla_p",
    )(q, k, vt)


def _mla_sample_kernel(q_ref, k_ref, vt_ref, ckv_ref, krt_ref, wuk_ref, wuvt_ref, o_ref, kc_ref, vct_ref, ot_ref):
    @pl.when(pl.program_id(1) == 0)
    def _():
        cb = ckv_ref[...].astype(BF16)
        krt = krt_ref[...]
        q4 = ROPE_C // 4
        kr = jnp.concatenate(
            [krt[0:q4], krt[2 * q4:3 * q4], jnp.zeros((NOPE_LO_C, PAST_LEN), F32),
             krt[q4:2 * q4], krt[3 * q4:4 * q4], jnp.zeros((LANES // 2 - ROPE_HALF_C, PAST_LEN), F32)], axis=0).T
        group_w = HEADS_PER_DOT_C * HEAD_PAD_C
        for grp in range(H_C // HEADS_PER_DOT_C):
            kg = _dot(cb, wuk_ref[:, grp * group_w:(grp + 1) * group_w])
            for j in range(HEADS_PER_DOT_C):
                osl = slice(grp * group_w + j * HEAD_PAD_C, grp * group_w + (j + 1) * HEAD_PAD_C)
                kc_ref[:, osl] = (kg[:, j * HEAD_PAD_C:(j + 1) * HEAD_PAD_C] + kr).astype(BF16)
        vct_ref[...] = _dot_nt(wuvt_ref[...], cb).astype(BF16)

    _mla_heads(q_ref, [(kc_ref, vct_ref), (k_ref, vt_ref)], o_ref, ot_ref, MLA_HEAD_GROUP_S)


def _mla_sample(q, k, vt, ckv_ctx, kr_ctx, w_uk, w_uvt, i_c):
    nqb = DEC_SEQ // MLA_QB
    q0 = N_PROMPT // MLA_QB
    s0 = N_PROMPT // DEC_SEQ
    return pl.pallas_call(
        _mla_sample_kernel,
        grid=(DEC_BATCH, nqb),
        in_specs=[
            pl.BlockSpec((MLA_QB, QC_W), lambda b, n: (q0 + b * nqb + n, 0)),
            pl.BlockSpec((DEC_SEQ, QC_W), lambda b, n: (s0 + b, 0)),
            pl.BlockSpec((VC_W, DEC_SEQ), lambda b, n: (0, s0 + b)),
            pl.BlockSpec((None, None, PAST_LEN, KV_LORA), lambda b, n: (b, i_c, 0, 0)),
            pl.BlockSpec((None, None, ROPE_C, PAST_LEN), lambda b, n: (b, i_c, 0, 0)),
            _layer_spec((KV_LORA, QC_W), i_c),
            _layer_spec((VC_W, KV_LORA), i_c),
        ],
        out_specs=pl.BlockSpec((MLA_QB, VC_W), lambda b, n: (b * nqb + n, 0)),
        out_shape=jax.ShapeDtypeStruct((N_SAMPLE, VC_W), BF16),
        scratch_shapes=[pltpu.VMEM((PAST_LEN, QC_W), BF16), pltpu.VMEM((VC_W, PAST_LEN), BF16),
                        pltpu.VMEM((VC_W, MLA_QB), F32)],
        compiler_params=_cparams(("parallel", "arbitrary")),
        name="mla_s",
    )(q, k, vt, ckv_ctx, kr_ctx, w_uk, w_uvt)


def _out_kernel(n_x, split_out, *refs):
    it = iter(refs[n_x:])
    mod_ref, g_ref = next(it), next(it)
    a_refs = (next(it), next(it))
    b_refs = (next(it), next(it))
    wo_ref, w1_ref, w2_ref = next(it), next(it), next(it)
    if split_out:
        next_mod = None
        out_refs = (next(it), next(it))
        h2_ref, acc_ref, x1_ref = next(it), next(it), next(it)
        y_ref = acc_ref
    else:
        next_mod = (next(it), next(it), next(it))
        x1_ref = y_ref = next(it)
        modn_ref = next(it)
        h2_ref, acc_ref = next(it), next(it)

    def emit_next_mod():
        if next_mod is not None:
            modn_ref[...] = _mod_block(*next_mod)

    kk = pl.program_id(1)
    half = wo_ref.shape[0] // 2

    row_chunks = [slice(r * OUT_PROLOGUE_ROWS, (r + 1) * OUT_PROLOGUE_ROWS) for r in range(TM_OUT // OUT_PROLOGUE_ROWS)]

    @pl.when(kk == 0)
    def _():
        wo_a = wo_ref[0:half, :].astype(BF16)
        wo_b = wo_ref[half:, :].astype(BF16)
        gate_g1 = mod_ref[2:3, :] * g_ref[1:2, :]
        scale_g2 = g_ref[2:3, :] * (1.0 + mod_ref[4:5, :])
        for rows in row_chunks:
            a = _load_split(a_refs, TM_OUT, rows)
            b = _load_split(b_refs, TM_OUT, rows)
            mix = _dot(a, wo_a) + _dot(b, wo_b)
            x1 = _load_x(n_x, refs, TM_OUT, rows) + _rms(mix, gate_g1)
            x1_ref[rows, :] = x1
            h2_ref[rows, :] = (_rms(x1, scale_g2) + mod_ref[3:4, :]).astype(BF16)
        acc_ref[...] = jnp.zeros(acc_ref.shape, F32)

    def ffn(rows, w1, w2):
        hid = jnp.maximum(_dot(h2_ref[rows, :], w1), 0.0)
        return _dot((hid * hid).astype(BF16), w2)

    is_last = kk == pl.num_programs(1) - 1

    @pl.when(jnp.logical_not(is_last))
    def _():
        acc_ref[...] += ffn(slice(None), w1_ref[...].astype(BF16), w2_ref[...].astype(BF16))
        emit_next_mod()

    @pl.when(is_last)
    def _():
        w1 = w1_ref[...].astype(BF16)
        w2 = w2_ref[...].astype(BF16)
        gate_g3 = mod_ref[5:6, :] * g_ref[3:4, :]
        for rows in row_chunks:
            y_ref[rows, :] = x1_ref[rows, :] + _rms(acc_ref[rows, :] + ffn(rows, w1, w2), gate_g3)
        emit_next_mod()

    if split_out:
        @pl.when(is_last)
        def _():
            _store_split(out_refs, TM_OUT, acc_ref[...])


def _out_layer(xs, layer, mods, g_norm, mix_a, mix_b, b_col, w_o, i_o, w_ff1, w_ff2, next_mod_args):
    tm = TM_OUT
    half = D_MODEL // 2
    n_k = D_FF // TK_FF
    split_out = next_mod_args is None
    if split_out:
        out_specs, out_shape = _split_specs(tm, D_MODEL), _split_shapes(D_MODEL, F32)
        next_specs, next_args = [], []
    else:
        mod_cols = 6 * D_MODEL // (N_TOK // tm * n_k)
        step = lambda i, k: i * n_k + k
        out_specs = [pl.BlockSpec((tm, D_MODEL), lambda i, k: (i, 0)),
                     pl.BlockSpec((N_MOD_ROWS, mod_cols), lambda i, k: (0, step(i, k)))]
        out_shape = [jax.ShapeDtypeStruct((N_TOK, D_MODEL), F32),
                     jax.ShapeDtypeStruct((N_MOD_ROWS, 6 * D_MODEL), F32)]
        next_specs = [pl.BlockSpec((N_MOD_ROWS, D_MODEL), lambda i, k: (0, 0)),
                      pl.BlockSpec((None, D_MODEL, mod_cols), lambda i, k: (layer + 1, 0, step(i, k))),
                      pl.BlockSpec((None, 1, mod_cols), lambda i, k: (layer + 1, 0, step(i, k)))]
        next_args = list(next_mod_args)
    return pl.pallas_call(
        functools.partial(_out_kernel, len(xs), split_out),
        grid=(N_TOK // tm, n_k),
        in_specs=_x_specs(len(xs), tm) + [
            _mod_spec(tm),
            _layer_spec((4, D_MODEL), layer),
        ] + _split_specs(tm, half, 0) + _split_specs(tm, half, b_col) + [
            _layer_spec((D_MODEL, D_MODEL), i_o),
            pl.BlockSpec((None, D_MODEL, TK_FF), lambda i, k: (layer, 0, k)),
            pl.BlockSpec((None, TK_FF, D_MODEL), lambda i, k: (layer, k, 0)),
        ] + next_specs,
        out_specs=out_specs,
        out_shape=out_shape,
        scratch_shapes=[pltpu.VMEM((tm, D_MODEL), BF16), pltpu.VMEM((tm, D_MODEL), F32)] + (
            [pltpu.VMEM((tm, D_MODEL), F32)] if split_out else []),
        compiler_params=_cparams(("arbitrary", "arbitrary")),
        name="out_mlp",
    )(*xs, mods, g_norm, *mix_a, *mix_b, w_o, w_ff1, w_ff2, *next_args)


def _rope_angles(head_dim):
    nf = head_dim // 4
    n_rows = DEC_SEQ // GRID_W
    rows = jnp.repeat(jnp.arange(n_rows, dtype=F32), GRID_W)
    cols = jnp.tile(jnp.arange(GRID_W, dtype=F32), n_rows)
    inv = ROPE_BASE ** (-jnp.arange(nf, dtype=F32) / nf)
    return jnp.stack([rows[:, None] * inv, cols[:, None] * inv], axis=1)


def _token_tables(cos_g, sin_g):
    cos_t = jnp.concatenate([jnp.ones((N_PROMPT, LANES), F32), jnp.tile(cos_g, (DEC_BATCH, 1))], axis=0)
    sin_t = jnp.concatenate([jnp.zeros((N_PROMPT, LANES), F32), jnp.tile(sin_g, (DEC_BATCH, 1))], axis=0)
    return cos_t, sin_t


def _rope_tables_b():
    nf = D_B // 4
    ang = _rope_angles(D_B)
    cos = jnp.broadcast_to(jnp.cos(ang)[:, :, None, :], (DEC_SEQ, 2, 2, nf)).reshape(DEC_SEQ, D_B)
    sin = jnp.sin(ang)
    sin = jnp.stack([-sin, sin], axis=2).reshape(DEC_SEQ, D_B)
    reps = LANES // D_B
    return _token_tables(jnp.tile(cos, (1, reps)), jnp.tile(sin, (1, reps)))


def _rope_tables_c():
    ang = _rope_angles(ROPE_C).reshape(DEC_SEQ, ROPE_HALF_C)
    cos, sin = jnp.cos(ang), jnp.sin(ang)
    half = LANES // 2
    cos_g = jnp.ones((DEC_SEQ, LANES), F32).at[:, 0:ROPE_HALF_C].set(cos).at[:, half:half + ROPE_HALF_C].set(cos)
    sin_g = jnp.zeros((DEC_SEQ, LANES), F32).at[:, 0:ROPE_HALF_C].set(-sin).at[:, half:half + ROPE_HALF_C].set(sin)
    return _token_tables(cos_g, sin_g)


def kernel(x_prompt, x_sample, state_gla_fwd, state_gla_bwd, cache_swa_k, cache_swa_v, cache_mla_ckv, cache_mla_kr, c, c_ctx, w_mod, b_mod, g_norm, w_ff1, w_ff2, w_in_ab, w_gk_f, b_gk_f, w_gk_b, b_gk_b, g_gla, swa_sink, w_out_ab, w_mla_down, g_mla_q, g_mla_kv, w_mla_uq, w_mla_ukv, w_mla_o):
    xs = (x_prompt.reshape(N_PROMPT, D_MODEL), x_sample.reshape(N_SAMPLE, D_MODEL))
    cvecs = jnp.concatenate([c_ctx[None, :], c, jnp.zeros((N_MOD_ROWS - 1 - DEC_BATCH, D_MODEL), F32)], axis=0)
    b_mod3 = b_mod.reshape(DEPTH, 1, 6 * D_MODEL)
    mods = _modulation(cvecs, w_mod, b_mod3, 0).reshape(N_MOD_ROWS, 6, D_MODEL)

    cos_b, sin_b = _rope_tables_b()
    cos_c, sin_c = _rope_tables_c()
    p_blk = N_PROMPT // DEC_SEQ

    n_ab = w_in_ab.shape[0]
    w_in = _prep_even(jnp.swapaxes(w_in_ab, 1, 2))
    zgk = jnp.zeros((n_ab, GK_RANK, QA_W), F32)
    w_gk = jnp.concatenate([jnp.concatenate([w_gk_f, zgk], axis=2),
                            jnp.concatenate([zgk, w_gk_b], axis=2)], axis=1).astype(BF16)
    b_gk = jnp.concatenate([b_gk_f, b_gk_b], axis=1)[:, None, :]
    gg = g_gla[:, None, :]
    kc = jnp.transpose(cache_swa_k, (0, 1, 3, 4, 2)).reshape(DEC_BATCH, n_ab, KB_W, PAST_LEN)
    vc = jnp.transpose(cache_swa_v, (0, 1, 3, 4, 2)).reshape(DEC_BATCH, n_ab, KB_W, PAST_LEN)
    w_down, w_uq, w_uk, w_uvt = _prep_odd(jnp.swapaxes(w_mla_down, 1, 2), w_mla_uq, w_mla_ukv)
    g_q = g_mla_q[:, None, :]
    g_kv = g_mla_kv[:, None, :]
    kr_ctx = jnp.swapaxes(cache_mla_kr, 2, 3)

    def gla_layer(i_ab, exact):
        def run(qa, ka, va, ga, ld):
            o_p, stf, stb = _gla(SEQ, BATCH, 0, qa, ka, va, ga, ld, gg, i_ab, exact=exact)
            o_s, _, _ = _gla(DEC_SEQ, DEC_BATCH, p_blk, qa, ka, va, ga, ld, gg, i_ab,
                             (state_gla_fwd, state_gla_bwd), exact=exact)
            return o_p, o_s, stf, stb
        return run

    st_f, st_b, sk, sv, ckv_out, ckr_out = [], [], [], [], [], []
    for l in range(DEPTH):
        i = l // 2
        next_mod_args = None if l == DEPTH - 1 else (cvecs, w_mod, b_mod3)
        if l % 2 == 0:
            qa, ka, va, ga, ld, qb, ld_min, kbt, vbt, kb_s, vb_s = _in_even(
                xs, l, mods, g_norm, w_in, w_gk, b_gk, cos_b, sin_b)
            factorisable = jnp.min(ld_min) * GLA_CHUNK >= -GLA_SAFE_TOTAL
            o_gla_p, o_gla_s, stf, stb = lax.cond(factorisable, gla_layer(i, False), gla_layer(i, True),
                                                  qa, ka, va, ga, ld)
            o_swa_p = _swa_prompt(swa_sink, i, qb, kbt, vbt)
            o_swa_s = _swa_sample(swa_sink, i, qb, kb_s, vb_s, kc, vc)
            outs = _out_layer(xs, l, mods, g_norm, (o_gla_p, o_gla_s), (o_swa_p, o_swa_s), 0, w_out_ab, i,
                              w_ff1, w_ff2, next_mod_args)
            st_f.append(stf)
            st_b.append(stb)
            sk.append(kbt)
            sv.append(vbt)
        else:
            q, k, vt, ckv, krt = _in_odd(xs[0], l, mods, g_norm, w_down, g_q, g_kv, w_uq, w_uk, w_uvt, cos_c, sin_c)
            o_mla = (_mla_prompt(q, k, vt), _mla_sample(q, k, vt, cache_mla_ckv, kr_ctx, w_uk, w_uvt, i))
            outs = _out_layer(xs, l, mods, g_norm, o_mla, o_mla, 1, w_mla_o, i, w_ff1, w_ff2, next_mod_args)
            ckv_out.append(ckv.reshape(BATCH, SEQ, KV_LORA))
            ckr_out.append(krt)
        if next_mod_args is None:
            xs = outs
        else:
            xs, mods = (outs[0],), outs[1].reshape(N_MOD_ROWS, 6, D_MODEL)

    y_prompt = xs[0].reshape(BATCH, SEQ, D_MODEL)
    y_sample = xs[1].reshape(DEC_BATCH, DEC_SEQ, D_MODEL)
    swa_cache = lambda parts: jnp.transpose(
        jnp.stack(parts, axis=1).reshape(BATCH, len(parts), KV_B, D_B, SEQ), (0, 1, 4, 2, 3))
    return (y_prompt, y_sample, jnp.stack(st_f, axis=1), jnp.stack(st_b, axis=1), swa_cache(sk), swa_cache(sv),
            jnp.stack(ckv_out, axis=1), jnp.swapaxes(jnp.stack(ckr_out, axis=1), 2, 3))
```

```python
import functools

import jax
import jax.numpy as jnp
from jax import lax
from jax.experimental import pallas as pl
from jax.experimental.pallas import tpu as pltpu

F32 = jnp.float32
BF16 = jnp.bfloat16

D_MODEL = 1024
BATCH = 16
SEQ = 256
DEPTH = 4
DEC_BATCH = 2
DEC_SEQ = 1024
PAST_LEN = 512
GRID_W = 64
D_FF = 4 * D_MODEL
EPS = 1e-6
ROPE_BASE = 10000.0
H_A = 4
DK_A = 64
DV_A = 128
GK_RANK = 16
GATE_NORM = 16.0
GLA_CHUNK = 64
H_B = 8
KV_B = 2
G_B = H_B // KV_B
D_B = 64
WINDOW = 128
H_C = 16
NOPE_C = 64
ROPE_C = 32
V_C = 64
Q_LORA = 384
KV_LORA = 256

N_PROMPT = BATCH * SEQ
N_SAMPLE = DEC_BATCH * DEC_SEQ
N_TOK = N_PROMPT + N_SAMPLE
SUBLANES = 8
N_MOD_ROWS = SUBLANES
QA_W = H_A * DK_A
VA_W = H_A * DV_A
QB_W = H_B * D_B
KB_W = KV_B * D_B
LANES = 128
HEAD_PAD_C = 128
QC_W = H_C * HEAD_PAD_C
VC_W = H_C * V_C
DOWN_RAW_W = Q_LORA + KV_LORA + ROPE_C
DOWN_W = Q_LORA + KV_LORA + LANES
ROPE_HALF_C = ROPE_C // 2
NOPE_LO_C = LANES // 2 - ROPE_HALF_C

TM_IN = 512
TM_OUT = 1024
TK_FF = 512
OUT_PROLOGUE_ROWS = 256
TN_MOD = 3072
PREP_ROWS = 256
GLA_GROUP = 256
GLA_SCAN_UNROLL = 4
GLA_SAFE_TOTAL = 160.0
SWA_QB = 128
SWA_WIN = 3 * SWA_QB
MLA_QB = 256
HEADS_PER_DOT_C = 4
MLA_HEAD_GROUP_P = 16
MLA_HEAD_GROUP_S = 4
VMEM_LIMIT = 60 * 1024 * 1024
NEG_BIG = -1e30


def _cparams(sem):
    return pltpu.CompilerParams(dimension_semantics=sem, vmem_limit_bytes=VMEM_LIMIT)


def _dot(a, b):
    return jnp.dot(a, b, preferred_element_type=F32)


def _dot_nt(a, b):
    return lax.dot_general(a, b, (((1,), (1,)), ((), ())), preferred_element_type=F32)


def _dot_tn(a, b):
    return lax.dot_general(a, b, (((0,), (0,)), ((), ())), preferred_element_type=F32)


def _rms(x, g):
    return x * lax.rsqrt(jnp.mean(x * x, axis=-1, keepdims=True) + EPS) * g


def _silu(x):
    return x / (1.0 + jnp.exp(-x))


def _rope(x, cos, sin, half):
    lane = lax.broadcasted_iota(jnp.int32, x.shape, 1)
    first = (lane % (2 * half)) < half
    partner = jnp.where(first, pltpu.roll(x, LANES - half, 1), pltpu.roll(x, half, 1))
    return x * cos + partner * sin


def _rope_c(x, cos, sin):
    return x * cos + pltpu.roll(x, LANES // 2, 1) * sin


def _rope_split_c(r):
    q = ROPE_C // 4
    first = jnp.concatenate([r[..., 0:q], r[..., 2 * q:3 * q]], axis=-1)
    second = jnp.concatenate([r[..., q:2 * q], r[..., 3 * q:4 * q]], axis=-1)
    return first, second


def _split3(x):
    x1 = x.astype(BF16)
    r1 = x - x1.astype(F32)
    x2 = r1.astype(BF16)
    x3 = (r1 - x2.astype(F32)).astype(BF16)
    return x1, x2, x3


def _dot3(t, parts):
    return _dot(t, parts[2]) + _dot(t, parts[1]) + _dot(t, parts[0])


def _ones_where(cond):
    return jnp.where(cond, 1.0, 0.0).astype(BF16)


def _mod_row(tile, tm):
    n_prompt_tiles = N_PROMPT // tm
    tiles_per_seq = DEC_SEQ // tm
    return jnp.where(tile < n_prompt_tiles, 0, 1 + (tile - n_prompt_tiles) // tiles_per_seq)


def _layer_spec(shape, idx):
    return pl.BlockSpec((None,) + shape, lambda *_: (idx,) + (0,) * len(shape))


def _mod_spec(tm):
    return pl.BlockSpec((None, 6, D_MODEL), lambda i, *_: (_mod_row(i, tm), 0, 0))


def _split_specs(tm, width, col=0):
    n_prompt_tiles = N_PROMPT // tm
    return [pl.BlockSpec((tm, width), lambda i, *_: (jnp.minimum(i, n_prompt_tiles - 1), col)),
            pl.BlockSpec((tm, width), lambda i, *_: (jnp.maximum(i - n_prompt_tiles, 0), col))]


def _split_shapes(width, dtype):
    return [jax.ShapeDtypeStruct((N_PROMPT, width), dtype), jax.ShapeDtypeStruct((N_SAMPLE, width), dtype)]


def _is_prompt_tile(tm):
    return pl.program_id(0) < N_PROMPT // tm


def _load_split(pair, tm, rows=slice(None)):
    return jnp.where(_is_prompt_tile(tm), pair[0][rows, :], pair[1][rows, :])


def _store_split(pair, tm, value, rows=slice(None)):
    is_prompt = _is_prompt_tile(tm)

    @pl.when(is_prompt)
    def _():
        pair[0][rows, :] = value

    @pl.when(jnp.logical_not(is_prompt))
    def _():
        pair[1][rows, :] = value


_C_QA, _C_KA, _C_VA, _C_GA = 0, QA_W, 2 * QA_W, 2 * QA_W + VA_W
_C_QB = _C_GA + VA_W
_C_KB = _C_QB + QB_W
_C_VB = _C_KB + KB_W
_C_LO = _C_VB + KB_W
AB_IN = _C_LO + 2 * GK_RANK


def _prep_even_kernel(wt_ref, o_ref):
    raw_lo = _C_QB
    raw_qb = raw_lo + 2 * GK_RANK
    o_ref[:, 0:_C_QB] = wt_ref[0:raw_lo, :].T.astype(BF16)
    o_ref[:, _C_QB:_C_LO] = wt_ref[raw_qb:AB_IN, :].T.astype(BF16)
    o_ref[:, _C_LO:AB_IN] = wt_ref[raw_lo:raw_lo + LANES, :].T[:, 0:2 * GK_RANK].astype(BF16)


def _prep_even(w_in_t):
    n = w_in_t.shape[0]
    return pl.pallas_call(
        _prep_even_kernel,
        grid=(n, D_MODEL // PREP_ROWS),
        in_specs=[pl.BlockSpec((None, AB_IN, PREP_ROWS), lambda l, r: (l, 0, r))],
        out_specs=pl.BlockSpec((None, PREP_ROWS, AB_IN), lambda l, r: (l, r, 0)),
        out_shape=jax.ShapeDtypeStruct((n, D_MODEL, AB_IN), BF16),
        compiler_params=_cparams(("parallel", "parallel")),
        name="prep_even",
    )(w_in_t)


def _prep_odd_kernel(wdt_ref, wuq_ref, wukv_ref, od_ref, ouq_ref, ouk_ref, ouvt_ref):
    def head_group(nope, rope_first, rope_second):
        rows = nope.shape[0]
        zero_half = jnp.zeros((rows, ROPE_HALF_C), F32)
        return jnp.concatenate(
            [zero_half if rope_first is None else rope_first, nope[:, 0:NOPE_LO_C],
             zero_half if rope_second is None else rope_second, nope[:, NOPE_LO_C:NOPE_C],
             jnp.zeros((rows, HEAD_PAD_C - NOPE_C - ROPE_C), F32)], axis=1).astype(BF16)

    n_ckv = Q_LORA + KV_LORA
    od_ref[:, 0:n_ckv] = wdt_ref[0:n_ckv, :].T.astype(BF16)
    tail = wdt_ref[DOWN_RAW_W - LANES:DOWN_RAW_W, :].T
    od_ref[:, n_ckv:DOWN_W] = head_group(jnp.zeros((D_MODEL, NOPE_C), F32), *_rope_split_c(tail[:, LANES - ROPE_C:]))
    hd_q = NOPE_C + ROPE_C
    for h in range(H_C):
        wq = wuq_ref[:, h * hd_q:(h + 1) * hd_q]
        ouq_ref[:, h * HEAD_PAD_C:(h + 1) * HEAD_PAD_C] = head_group(wq[:, 0:NOPE_C], *_rope_split_c(wq[:, NOPE_C:]))
    wukv = wukv_ref[...]
    for h in range(H_C):
        ouk_ref[:, h * HEAD_PAD_C:(h + 1) * HEAD_PAD_C] = head_group(
            wukv[:, h * HEAD_PAD_C:h * HEAD_PAD_C + NOPE_C], None, None)
    wukv_t = wukv.T
    for h in range(H_C):
        ouvt_ref[h * V_C:(h + 1) * V_C, :] = wukv_t[h * HEAD_PAD_C + NOPE_C:(h + 1) * HEAD_PAD_C, :].astype(BF16)


def _prep_odd(w_down_t, w_uq, w_ukv):
    n = w_down_t.shape[0]
    spec = lambda r, c: pl.BlockSpec((None, r, c), lambda l: (l, 0, 0))
    return pl.pallas_call(
        _prep_odd_kernel,
        grid=(n,),
        in_specs=[spec(DOWN_RAW_W, D_MODEL), spec(Q_LORA, H_C * (NOPE_C + ROPE_C)), spec(KV_LORA, QC_W)],
        out_specs=[spec(D_MODEL, DOWN_W), spec(Q_LORA, QC_W), spec(KV_LORA, QC_W), spec(VC_W, KV_LORA)],
        out_shape=[jax.ShapeDtypeStruct((n, D_MODEL, DOWN_W), BF16), jax.ShapeDtypeStruct((n, Q_LORA, QC_W), BF16),
                   jax.ShapeDtypeStruct((n, KV_LORA, QC_W), BF16), jax.ShapeDtypeStruct((n, VC_W, KV_LORA), BF16)],
        compiler_params=_cparams(("parallel",)),
        name="prep_odd",
    )(w_down_t, w_uq, w_ukv)


def _mod_block(c_ref, w_ref, b_ref):
    s = _silu(c_ref[...])
    return _dot(s.astype(BF16), w_ref[...].astype(BF16)) + b_ref[...]


def _mod_kernel(c_ref, w_ref, b_ref, o_ref):
    o_ref[...] = _mod_block(c_ref, w_ref, b_ref)


def _modulation(cvecs, w_mod, b_mod, layer):
    return pl.pallas_call(
        _mod_kernel,
        grid=(6 * D_MODEL // TN_MOD,),
        in_specs=[
            pl.BlockSpec((N_MOD_ROWS, D_MODEL), lambda j: (0, 0)),
            pl.BlockSpec((None, D_MODEL, TN_MOD), lambda j: (layer, 0, j)),
            pl.BlockSpec((None, 1, TN_MOD), lambda j: (layer, 0, j)),
        ],
        out_specs=pl.BlockSpec((N_MOD_ROWS, TN_MOD), lambda j: (0, j)),
        out_shape=jax.ShapeDtypeStruct((N_MOD_ROWS, 6 * D_MODEL), F32),
        compiler_params=_cparams(("parallel",)),
        name="adaln_mod",
    )(cvecs, w_mod, b_mod)


def _x_specs(n_x, tm):
    if n_x == 1:
        return [pl.BlockSpec((tm, D_MODEL), lambda i, *_: (i, 0))]
    return _split_specs(tm, D_MODEL)


def _load_x(n_x, refs, tm, rows=slice(None)):
    if n_x == 1:
        return refs[0][rows, :]
    return _load_split(refs[:2], tm, rows)


def _in_even_kernel(n_x, *refs):
    (mod_ref, g_ref, w_ref, wgk_ref, bgk_ref, cos_ref, sin_ref,
     qa_ref, ka_ref, va_ref, ga_ref, ld_ref, qb_ref, ldmin_ref, kbt_ref, vbt_ref, kbs_ref, vbs_ref) = refs[n_x:]
    x = _load_x(n_x, refs, TM_IN)
    h = _rms(x, g_ref[0:1, :]) * (1.0 + mod_ref[1:2, :]) + mod_ref[0:1, :]
    hb = h.astype(BF16)
    qa_ref[...] = _dot(hb, w_ref[:, _C_QA:_C_KA]) * (DK_A ** -0.5)
    ka_ref[...] = _dot(hb, w_ref[:, _C_KA:_C_VA])
    va_ref[...] = _dot(hb, w_ref[:, _C_VA:_C_GA]).astype(BF16)
    ga_ref[...] = _dot(hb, w_ref[:, _C_GA:_C_QB])
    cos = cos_ref[...]
    sin = sin_ref[...]
    qb = _dot(hb, w_ref[:, _C_QB:_C_KB])
    for j in range(QB_W // LANES):
        qj = qb[:, j * LANES:(j + 1) * LANES]
        qb_ref[:, j * LANES:(j + 1) * LANES] = (_rope(qj, cos, sin, D_B // 4) * (D_B ** -0.5)).astype(BF16)
    kvb = _dot(hb, w_ref[:, _C_KB:_C_LO])
    kb = _rope(kvb[:, :KB_W], cos, sin, D_B // 4)
    vb = kvb[:, KB_W:]
    lo = _dot(hb, w_ref[:, _C_LO:AB_IN]).astype(BF16)
    z = _dot(lo, wgk_ref[...]) + bgk_ref[...]
    ld = (jnp.minimum(z, 0.0) - jnp.log(1.0 + jnp.exp(-jnp.abs(z)))) * (1.0 / GATE_NORM)
    ld_ref[...] = ld
    ld_min = jnp.min(jnp.min(ld, axis=0, keepdims=True), axis=1, keepdims=True)
    ldmin_ref[...] = jnp.broadcast_to(ld_min, ldmin_ref.shape)

    is_prompt = _is_prompt_tile(TM_IN)

    @pl.when(is_prompt)
    def _():
        for s in range(TM_IN // SEQ):
            kbt_ref[s] = kb[s * SEQ:(s + 1) * SEQ, :].T
            vbt_ref[s] = vb[s * SEQ:(s + 1) * SEQ, :].T

    @pl.when(jnp.logical_not(is_prompt))
    def _():
        kbs_ref[...] = kb
        vbs_ref[...] = vb


def _in_even(xs, layer, mods, g_norm, w_in, w_gk, b_gk, cos, sin):
    tm = TM_IN
    i_ab = layer // 2
    n_prompt_tiles = N_PROMPT // tm
    seq_per_tile = tm // SEQ
    row = lambda i: (i, 0)
    widths = (QA_W, QA_W, VA_W, VA_W, 2 * QA_W, QB_W)
    dtypes = (F32, F32, BF16, F32, F32, BF16)
    kv_specs = 2 * [pl.BlockSpec((seq_per_tile, KB_W, SEQ), lambda i: (jnp.minimum(i, n_prompt_tiles - 1), 0, 0))] + \
        2 * [pl.BlockSpec((tm, KB_W), lambda i: (jnp.maximum(i - n_prompt_tiles, 0), 0))]
    kv_shapes = 2 * [jax.ShapeDtypeStruct((BATCH, KB_W, SEQ), F32)] + 2 * [jax.ShapeDtypeStruct((N_SAMPLE, KB_W), F32)]
    return pl.pallas_call(
        functools.partial(_in_even_kernel, len(xs)),
        grid=(N_TOK // tm,),
        in_specs=_x_specs(len(xs), tm) + [
            _mod_spec(tm),
            _layer_spec((4, D_MODEL), layer),
            _layer_spec((D_MODEL, AB_IN), i_ab),
            _layer_spec((2 * GK_RANK, 2 * QA_W), i_ab),
            _layer_spec((1, 2 * QA_W), i_ab),
            pl.BlockSpec((tm, LANES), row),
            pl.BlockSpec((tm, LANES), row),
        ],
        out_specs=[pl.BlockSpec((tm, w), row) for w in widths] + [
            pl.BlockSpec((None, SUBLANES, LANES), lambda i: (i, 0, 0))] + kv_specs,
        out_shape=[jax.ShapeDtypeStruct((N_TOK, w), d) for w, d in zip(widths, dtypes)] + [
            jax.ShapeDtypeStruct((N_TOK // tm, SUBLANES, LANES), F32)] + kv_shapes,
        compiler_params=_cparams(("arbitrary",)),
        name="in_even",
    )(*xs, mods, g_norm, w_in, w_gk, b_gk, cos, sin)


def _gla_kernel(seq_len, has_s0, exact, *refs):
    qa_ref, ka_ref, va_ref, ga_ref, ld_ref, gg_ref = refs[:6]
    s0_refs = refs[6:8] if has_s0 else None
    o_ref, stf_ref, stb_ref = refs[8:11] if has_s0 else refs[6:9]
    st_ref, qi_ref, ki_ref, qcat_ref, ks_ref, dec_ref, stcat_ref, acc_ref = refs[-8:]
    b_ref = qi_ref if exact else None
    n_groups = seq_len // GLA_GROUP
    cpg = GLA_GROUP // GLA_CHUNK
    n_chunks = seq_len // GLA_CHUNK
    n_pairs = H_A // 2
    pair_k = 2 * DK_A
    pair_v = 2 * DV_A

    def chunk_masks():
        r_i = lax.broadcasted_iota(jnp.int32, (GLA_GROUP, GLA_GROUP), 0)
        c_i = lax.broadcasted_iota(jnp.int32, (GLA_GROUP, GLA_GROUP), 1)
        same = (r_i // GLA_CHUNK) == (c_i // GLA_CHUNK)
        return same & (c_i <= r_i), same & (c_i >= r_i)

    def scale_group(j, carry):
        mask_f, mask_b = chunk_masks()
        t_cum = (_ones_where(mask_f), _ones_where(mask_b))
        s_r = lax.broadcasted_iota(jnp.int32, (SUBLANES, GLA_GROUP), 0)
        s_c = lax.broadcasted_iota(jnp.int32, (SUBLANES, GLA_GROUP), 1)
        t_sel = _ones_where(s_r == s_c // GLA_CHUNK)
        rows = pl.ds(pl.multiple_of(j * GLA_GROUP, GLA_GROUP), GLA_GROUP)
        q = qa_ref[rows, :]
        k = ka_ref[rows, :]
        for d in range(2):
            parts = _split3(ld_ref[rows, d * QA_W:(d + 1) * QA_W])
            b = _dot3(t_cum[d], parts)
            tot8 = _dot3(t_sel, parts)
            dec8 = jnp.exp(tot8)
            tot = jnp.concatenate(
                [jnp.broadcast_to(tot8[cc:cc + 1, :], (GLA_CHUNK, QA_W)) for cc in range(cpg)], axis=0)
            ref = 0.5 * tot
            if exact:
                b_ref[d, rows, :] = b
            else:
                qi_ref[d, rows, :] = (q * jnp.exp(b - ref)).astype(BF16)
                ki_ref[d, rows, :] = (k * jnp.exp(ref - b)).astype(BF16)
            ks_ref[d, rows, :] = (k * jnp.exp(tot - b)).astype(BF16)
            q_inter = (q * jnp.exp(b)).astype(BF16)
            for p in range(n_pairs):
                qcat_ref[rows, p * 2 * pair_k + d * pair_k:p * 2 * pair_k + (d + 1) * pair_k] = (
                    q_inter[:, p * pair_k:(p + 1) * pair_k])
            for cc in range(cpg):
                dec_ref[d, j * cpg + cc] = jnp.broadcast_to(dec8[cc:cc + 1, :], (SUBLANES, QA_W))
        return carry

    lax.fori_loop(0, n_groups, scale_group, 0)

    zpad = jnp.zeros((DK_A, DV_A), F32)
    for d in range(2):
        for p in range(n_pairs):
            if has_s0:
                top = jnp.concatenate([s0_refs[d][2 * p], zpad], axis=0).T
                bot = jnp.concatenate([zpad, s0_refs[d][2 * p + 1]], axis=0).T
                st_ref[d, p] = jnp.concatenate([top, bot], axis=0)
            else:
                st_ref[d, p] = jnp.zeros((pair_v, pair_k), F32)

    def scan_chunk(c, carry):
        bd_r = lax.broadcasted_iota(jnp.int32, (pair_v, pair_k), 0)
        bd_c = lax.broadcasted_iota(jnp.int32, (pair_v, pair_k), 1)
        bd_mask = (bd_r // DV_A) == (bd_c // DK_A)
        for d in range(2):
            cd = c if d == 0 else n_chunks - 1 - c
            rows = pl.ds(pl.multiple_of(cd * GLA_CHUNK, GLA_CHUNK), GLA_CHUNK)
            for p in range(n_pairs):
                st = st_ref[d, p]
                stcat_ref[cd, p, :, d * pair_k:(d + 1) * pair_k] = st.astype(BF16)
                u = _dot_tn(va_ref[rows, p * pair_v:(p + 1) * pair_v], ks_ref[d, rows, p * pair_k:(p + 1) * pair_k])
                dec = dec_ref[d, cd][0:1, p * pair_k:(p + 1) * pair_k]
                st_ref[d, p] = st * dec + jnp.where(bd_mask, u, 0.0)
        return carry

    lax.fori_loop(0, n_chunks, scan_chunk, 0, unroll=GLA_SCAN_UNROLL)

    for d, out_ref in ((0, stf_ref), (1, stb_ref)):
        for p in range(n_pairs):
            st = st_ref[d, p]
            out_ref[2 * p] = st[0:DV_A, :].T[0:DK_A, :]
            out_ref[2 * p + 1] = st[DV_A:pair_v, :].T[DK_A:pair_k, :]

    def exact_intra(row0, h):
        p, hh = divmod(h, 2)
        lanes = slice(p * pair_k, (p + 1) * pair_k)
        head_lanes = (lax.broadcasted_iota(jnp.int32, (GLA_CHUNK, pair_k), 1) // DK_A) == hh
        s_idx = lax.broadcasted_iota(jnp.int32, (GLA_CHUNK, GLA_CHUNK), 0)
        t_idx = lax.broadcasted_iota(jnp.int32, (GLA_CHUNK, GLA_CHUNK), 1)
        outs = []
        for cc in range(cpg):
            r0 = row0 + cc * GLA_CHUNK
            crow = pl.ds(pl.multiple_of(r0, GLA_CHUNK), GLA_CHUNK)
            kc = ka_ref[crow, lanes]
            att_t = jnp.zeros((GLA_CHUNK, GLA_CHUNK), F32)
            for d in range(2):
                bc = b_ref[d, crow, lanes]

                def row_step(t, att_t, d=d, bc=bc, kc=kc, r0=r0):
                    grp = pl.ds(pl.multiple_of(r0 + (t // SUBLANES) * SUBLANES, SUBLANES), SUBLANES)
                    pick = lax.broadcasted_iota(jnp.int32, (SUBLANES, pair_k), 0) == t % SUBLANES
                    bt = jnp.sum(jnp.where(pick, b_ref[d, grp, lanes], 0.0), axis=0, keepdims=True)
                    qt = jnp.sum(jnp.where(pick, qa_ref[grp, lanes], 0.0), axis=0, keepdims=True)
                    w = jnp.where(head_lanes, qt * kc * jnp.exp(jnp.minimum(bt - bc, 0.0)), 0.0)
                    col = jnp.sum(w, axis=1, keepdims=True)
                    allowed = (s_idx <= t) if d == 0 else (s_idx >= t)
                    return jnp.where((t_idx == t) & allowed, att_t + col, att_t)

                att_t = lax.fori_loop(0, GLA_CHUNK, row_step, att_t)
            outs.append(_dot_tn(att_t.astype(BF16), va_ref[crow, h * DV_A:(h + 1) * DV_A]))
        return jnp.concatenate(outs, axis=0)

    def out_group(j, carry):
        mask_f, mask_b = chunk_masks()
        lane = lax.broadcasted_iota(jnp.int32, (GLA_GROUP, pair_k), 1)
        rows = pl.ds(pl.multiple_of(j * GLA_GROUP, GLA_GROUP), GLA_GROUP)
        for p in range(n_pairs):
            for cc in range(cpg):
                c = j * cpg + cc
                crow = pl.ds(pl.multiple_of(c * GLA_CHUNK, GLA_CHUNK), GLA_CHUNK)
                acc_ref[crow, p * pair_v:(p + 1) * pair_v] = _dot_nt(
                    qcat_ref[crow, p * 2 * pair_k:(p + 1) * 2 * pair_k], stcat_ref[c, p])
        if exact:
            intras = [exact_intra(j * GLA_GROUP, h) for h in range(H_A)]
        else:
            atts = []
            for h in range(H_A):
                p, hh = divmod(h, 2)
                head_lanes = _ones_where((lane // DK_A) == hh)
                att = None
                for d, mask in ((0, mask_f), (1, mask_b)):
                    qm = qi_ref[d, rows, p * pair_k:(p + 1) * pair_k] * head_lanes
                    a = jnp.where(mask, _dot_nt(qm, ki_ref[d, rows, p * pair_k:(p + 1) * pair_k]), 0.0)
                    att = a if att is None else att + a
                atts.append(att.astype(BF16))
            intras = [_dot(atts[h], va_ref[rows, h * DV_A:(h + 1) * DV_A]) for h in range(H_A)]
        for h in range(H_A):
            o = acc_ref[rows, h * DV_A:(h + 1) * DV_A] + intras[h]
            gate = _silu(ga_ref[rows, h * DV_A:(h + 1) * DV_A])
            o_ref[rows, h * DV_A:(h + 1) * DV_A] = (_rms(o, gg_ref[...]) * gate).astype(BF16)
        return carry

    lax.fori_loop(0, n_groups, out_group, 0)


def _gla(seq_len, n_seq, row_block0, qa, ka, va, ga, ld, g_gla, i_ab, s0=None, exact=False):
    has_s0 = s0 is not None
    n_chunks = seq_len // GLA_CHUNK
    n_pairs = H_A // 2
    rows = lambda b: (row_block0 + b, 0)
    st_spec = pl.BlockSpec((None, H_A, DK_A, DV_A), lambda b: (b, 0, 0, 0))
    in_specs = [
        pl.BlockSpec((seq_len, QA_W), rows),
        pl.BlockSpec((seq_len, QA_W), rows),
        pl.BlockSpec((seq_len, VA_W), rows),
        pl.BlockSpec((seq_len, VA_W), rows),
        pl.BlockSpec((seq_len, 2 * QA_W), rows),
        _layer_spec((1, DV_A), i_ab),
    ]
    args = [qa, ka, va, ga, ld, g_gla]
    if has_s0:
        s0_spec = pl.BlockSpec((None, None, H_A, DK_A, DV_A), lambda b: (b, i_ab, 0, 0, 0))
        in_specs += [s0_spec, s0_spec]
        args += list(s0)
    st_shape = jax.ShapeDtypeStruct((n_seq, H_A, DK_A, DV_A), F32)
    return pl.pallas_call(
        functools.partial(_gla_kernel, seq_len, has_s0, exact),
        grid=(n_seq,),
        in_specs=in_specs,
        out_specs=[pl.BlockSpec((seq_len, VA_W), lambda b: (b, 0)), st_spec, st_spec],
        out_shape=[jax.ShapeDtypeStruct((n_seq * seq_len, VA_W), BF16), st_shape, st_shape],
        scratch_shapes=[
            pltpu.VMEM((2, n_pairs, 2 * DV_A, 2 * DK_A), F32),
            pltpu.VMEM((2, seq_len, QA_W), F32 if exact else BF16),
            pltpu.VMEM((2, seq_len, QA_W), BF16),
            pltpu.VMEM((seq_len, 2 * QA_W), BF16),
            pltpu.VMEM((2, seq_len, QA_W), BF16),
            pltpu.VMEM((2, n_chunks, SUBLANES, QA_W), F32),
            pltpu.VMEM((n_chunks, n_pairs, 2 * DV_A, 4 * DK_A), BF16),
            pltpu.VMEM((seq_len, VA_W), F32),
        ],
        compiler_params=_cparams(("parallel",)),
        name=("gla_s" if has_s0 else "gla_p") + ("_exact" if exact else ""),
    )(*args)


def _swa_head_softmax(pieces, sink):
    m = sink
    for s, _, _ in pieces:
        m = jnp.maximum(m, jnp.max(s, axis=-1, keepdims=True))
    den = jnp.exp(sink - m)
    acc = None
    for s, v, transposed in pieces:
        e = jnp.exp(s - m)
        den = den + jnp.sum(e, axis=-1, keepdims=True)
        pv = _dot_nt(e.astype(BF16), v) if transposed else _dot(e.astype(BF16), v)
        acc = pv if acc is None else acc + pv
    return acc / den


def _dup_groups(x):
    lane = lax.broadcasted_iota(jnp.int32, x.shape, 1)
    swapped = pltpu.roll(x, D_B, 1)
    low = lane < D_B
    return jnp.where(low, x, swapped).astype(BF16), jnp.where(low, swapped, x).astype(BF16)


def _dup_groups_t(xt):
    g0, g1 = xt[0:D_B, :], xt[D_B:, :]
    return jnp.concatenate([g0, g0], axis=0).astype(BF16), jnp.concatenate([g1, g1], axis=0).astype(BF16)


def _swa_heads(sink_ref, i_ab, q_ref, kv_pieces, o_ref, n_rows):
    lane = lax.broadcasted_iota(jnp.int32, (n_rows, LANES), 1)
    head_pieces = []
    for h in range(H_B):
        j, hh = divmod(h, 2)
        g = h // G_B
        qm = q_ref[:, j * LANES:(j + 1) * LANES] * _ones_where((lane // D_B) == hh)
        pieces = []
        for keys, vals, transposed, mask in kv_pieces:
            s = _dot(qm, keys[g]) if transposed else _dot_nt(qm, keys[g])
            if mask is not None:
                s = jnp.where(mask, s, NEG_BIG)
            pieces.append((s, vals[g], transposed))
        head_pieces.append(pieces)
    outs = [_swa_head_softmax(pieces, sink_ref[i_ab, h]) for h, pieces in enumerate(head_pieces)]
    for j in range(H_B // 2):
        o_ref[:, j * LANES:(j + 1) * LANES] = jnp.where(lane < D_B, outs[2 * j], outs[2 * j + 1]).astype(BF16)


def _swa_prompt_kernel(i_ab, sink_ref, q_ref, kt_ref, vt_ref, o_ref):
    _swa_heads(sink_ref, i_ab, q_ref, [(_dup_groups_t(kt_ref[...]), _dup_groups_t(vt_ref[...]), True, None)],
               o_ref, SEQ)


def _swa_prompt(sink, i_ab, qb, kbt, vbt):
    seq = lambda b: (b, 0)
    seq_t = pl.BlockSpec((None, KB_W, SEQ), lambda b: (b, 0, 0))
    return pl.pallas_call(
        functools.partial(_swa_prompt_kernel, i_ab),
        grid=(BATCH,),
        in_specs=[pl.BlockSpec(memory_space=pltpu.SMEM), pl.BlockSpec((SEQ, QB_W), seq), seq_t, seq_t],
        out_specs=pl.BlockSpec((SEQ, QB_W), seq),
        out_shape=jax.ShapeDtypeStruct((N_PROMPT, QB_W), BF16),
        compiler_params=_cparams(("parallel",)),
        name="swa_p",
    )(sink, qb, kbt, vbt)


def _swa_sample_kernel(i_ab, sink_ref, q_ref, k_ref, v_ref, kct_ref, vct_ref, o_ref):
    n = pl.program_id(1)
    start = pl.multiple_of(jnp.clip((n - 1) * SWA_QB, 0, DEC_SEQ - SWA_WIN), SWA_QB)
    local = (_dup_groups(k_ref[pl.ds(start, SWA_WIN), :]), _dup_groups(v_ref[pl.ds(start, SWA_WIN), :]))
    ctx = (_dup_groups_t(kct_ref[...]), _dup_groups_t(vct_ref[...]))
    qi = n * SWA_QB + lax.broadcasted_iota(jnp.int32, (SWA_QB, SWA_WIN), 0)
    ki = start + lax.broadcasted_iota(jnp.int32, (SWA_QB, SWA_WIN), 1)
    band = jnp.abs(qi - ki) <= WINDOW
    _swa_heads(sink_ref, i_ab, q_ref, [ctx + (True, None), local + (False, band)], o_ref, SWA_QB)


def _swa_sample(sink, i_ab, qb, kb, vb, kc, vc):
    nqb = DEC_SEQ // SWA_QB
    q0 = N_PROMPT // SWA_QB
    return pl.pallas_call(
        functools.partial(_swa_sample_kernel, i_ab),
        grid=(DEC_BATCH, nqb),
        in_specs=[
            pl.BlockSpec(memory_space=pltpu.SMEM),
            pl.BlockSpec((SWA_QB, QB_W), lambda b, n: (q0 + b * nqb + n, 0)),
            pl.BlockSpec((DEC_SEQ, KB_W), lambda b, n: (b, 0)),
            pl.BlockSpec((DEC_SEQ, KB_W), lambda b, n: (b, 0)),
            pl.BlockSpec((None, None, KB_W, PAST_LEN), lambda b, n: (b, i_ab, 0, 0)),
            pl.BlockSpec((None, None, KB_W, PAST_LEN), lambda b, n: (b, i_ab, 0, 0)),
        ],
        out_specs=pl.BlockSpec((SWA_QB, QB_W), lambda b, n: (b * nqb + n, 0)),
        out_shape=jax.ShapeDtypeStruct((N_SAMPLE, QB_W), BF16),
        compiler_params=_cparams(("parallel", "parallel")),
        name="swa_s",
    )(sink, qb, kb, vb, kc, vc)


def _in_odd_kernel(x_ref, mod_ref, g_ref, wd_ref, gq_ref, gkv_ref, wuq_ref, wuk_ref, wuvt_ref, cos_ref, sin_ref,
                   q_ref, k_ref, vt_ref, ckv_ref, krt_ref):
    h = _rms(x_ref[...], g_ref[0:1, :]) * (1.0 + mod_ref[1:2, :]) + mod_ref[0:1, :]
    hb = h.astype(BF16)
    cos = cos_ref[...]
    sin = sin_ref[...]
    c_q = _dot(hb, wd_ref[:, 0:Q_LORA])
    c_kv = _rms(_dot(hb, wd_ref[:, Q_LORA:Q_LORA + KV_LORA]), gkv_ref[...])
    kr = _rope_c(_dot(hb, wd_ref[:, Q_LORA + KV_LORA:DOWN_W]), cos, sin)
    cqb = _rms(c_q, gq_ref[...]).astype(BF16)
    ckvb = c_kv.astype(BF16)
    scale = (NOPE_C + ROPE_C) ** -0.5
    group_w = HEADS_PER_DOT_C * HEAD_PAD_C
    for grp in range(H_C // HEADS_PER_DOT_C):
        gsl = slice(grp * group_w, (grp + 1) * group_w)
        qg = _dot(cqb, wuq_ref[:, gsl])
        kg = _dot(ckvb, wuk_ref[:, gsl])
        for j in range(HEADS_PER_DOT_C):
            sl = slice(j * HEAD_PAD_C, (j + 1) * HEAD_PAD_C)
            osl = slice(grp * group_w + j * HEAD_PAD_C, grp * group_w + (j + 1) * HEAD_PAD_C)
            q_ref[:, osl] = (_rope_c(qg[:, sl], cos, sin) * scale).astype(BF16)
            k_ref[:, osl] = (kg[:, sl] + kr).astype(BF16)
    vt_ref[...] = _dot_nt(wuvt_ref[...], ckvb).astype(BF16)

    @pl.when(_is_prompt_tile(TM_IN))
    def _():
        ckv_ref[...] = c_kv
        q4 = ROPE_C // 4
        half = LANES // 2
        for s in range(TM_IN // SEQ):
            t = kr[s * SEQ:(s + 1) * SEQ, :].T
            krt_ref[s] = jnp.concatenate([t[0:q4], t[half:half + q4], t[q4:2 * q4], t[half + q4:half + 2 * q4]], axis=0)


def _in_odd(x, layer, mods, g_norm, w_down, g_q, g_kv, w_uq, w_uk, w_uvt, cos, sin):
    tm = TM_IN
    i_c = layer // 2
    n_prompt_tiles = N_PROMPT // tm
    row = lambda i: (i, 0)
    prompt_row = lambda i: (jnp.minimum(i, n_prompt_tiles - 1), 0)
    prompt_seq = lambda i: (jnp.minimum(i, n_prompt_tiles - 1), 0, 0)
    return pl.pallas_call(
        _in_odd_kernel,
        grid=(N_TOK // tm,),
        in_specs=[
            pl.BlockSpec((tm, D_MODEL), row),
            _mod_spec(tm),
            _layer_spec((4, D_MODEL), layer),
            _layer_spec((D_MODEL, DOWN_W), i_c),
            _layer_spec((1, Q_LORA), i_c),
            _layer_spec((1, KV_LORA), i_c),
            _layer_spec((Q_LORA, QC_W), i_c),
            _layer_spec((KV_LORA, QC_W), i_c),
            _layer_spec((VC_W, KV_LORA), i_c),
            pl.BlockSpec((tm, LANES), row),
            pl.BlockSpec((tm, LANES), row),
        ],
        out_specs=[pl.BlockSpec((tm, QC_W), row), pl.BlockSpec((tm, QC_W), row),
                   pl.BlockSpec((VC_W, tm), lambda i: (0, i)),
                   pl.BlockSpec((tm, KV_LORA), prompt_row), pl.BlockSpec((tm // SEQ, ROPE_C, SEQ), prompt_seq)],
        out_shape=[jax.ShapeDtypeStruct((N_TOK, QC_W), BF16), jax.ShapeDtypeStruct((N_TOK, QC_W), BF16),
                   jax.ShapeDtypeStruct((VC_W, N_TOK), BF16),
                   jax.ShapeDtypeStruct((N_PROMPT, KV_LORA), F32), jax.ShapeDtypeStruct((BATCH, ROPE_C, SEQ), F32)],
        compiler_params=_cparams(("arbitrary",)),
        name="in_odd",
    )(x, mods, g_norm, w_down, g_q, g_kv, w_uq, w_uk, w_uvt, cos, sin)


def _reduce_rows(x, op, reduce_fn):
    while x.shape[0] % (2 * SUBLANES) == 0:
        half = x.shape[0] // 2
        x = op(x[:half], x[half:])
    return reduce_fn(x, axis=0, keepdims=True)


def _mla_heads(q_ref, kv_pieces, o_ref, ot_ref, group):
    for h0 in range(0, H_C, group):
        heads = range(h0, h0 + group)
        scores = []
        for hd in heads:
            sl = slice(hd * HEAD_PAD_C, (hd + 1) * HEAD_PAD_C)
            scores.append([_dot_nt(k_ref[:, sl], q_ref[:, sl]) for k_ref, _ in kv_pieces])
        exps, dens = [], []
        for per_piece in scores:
            m = None
            for s in per_piece:
                sm = _reduce_rows(s, jnp.maximum, jnp.max)
                m = sm if m is None else jnp.maximum(m, sm)
            es = [jnp.exp(s - m) for s in per_piece]
            den = None
            for e in es:
                part = _reduce_rows(e, jnp.add, jnp.sum)
                den = part if den is None else den + part
            exps.append([e.astype(BF16) for e in es])
            dens.append(den)
        for hd, es, den in zip(heads, exps, dens):
            acc = None
            for e, (_, vt_ref) in zip(es, kv_pieces):
                pv = _dot(vt_ref[hd * V_C:(hd + 1) * V_C, :], e)
                acc = pv if acc is None else acc + pv
            ot_ref[hd * V_C:(hd + 1) * V_C, :] = acc / den
    o_ref[...] = ot_ref[...].T.astype(BF16)


def _mla_prompt_kernel(q_ref, k_ref, vt_ref, o_ref, ot_ref):
    _mla_heads(q_ref, [(k_ref, vt_ref)], o_ref, ot_ref, MLA_HEAD_GROUP_P)


def _mla_prompt(q, k, vt):
    seq = lambda b: (b, 0)
    return pl.pallas_call(
        _mla_prompt_kernel,
        grid=(BATCH,),
        in_specs=[pl.BlockSpec((SEQ, QC_W), seq), pl.BlockSpec((SEQ, QC_W), seq),
                  pl.BlockSpec((VC_W, SEQ), lambda b: (0, b))],
        out_specs=pl.BlockSpec((SEQ, VC_W), seq),
        out_shape=jax.ShapeDtypeStruct((N_PROMPT, VC_W), BF16),
        scratch_shapes=[pltpu.VMEM((VC_W, SEQ), F32)],
        compiler_params=_cparams(("parallel",)),
        name="mla_p",
    )(q, k, vt)


def _mla_sample_kernel(q_ref, k_ref, vt_ref, ckv_ref, krt_ref, wuk_ref, wuvt_ref, o_ref, kc_ref, vct_ref, ot_ref):
    @pl.when(pl.program_id(1) == 0)
    def _():
        cb = ckv_ref[...].astype(BF16)
        krt = krt_ref[...]
        q4 = ROPE_C // 4
        kr = jnp.concatenate(
            [krt[0:q4], krt[2 * q4:3 * q4], jnp.zeros((NOPE_LO_C, PAST_LEN), F32),
             krt[q4:2 * q4], krt[3 * q4:4 * q4], jnp.zeros((LANES // 2 - ROPE_HALF_C, PAST_LEN), F32)], axis=0).T
        group_w = HEADS_PER_DOT_C * HEAD_PAD_C
        for grp in range(H_C // HEADS_PER_DOT_C):
            kg = _dot(cb, wuk_ref[:, grp * group_w:(grp + 1) * group_w])
            for j in range(HEADS_PER_DOT_C):
                osl = slice(grp * group_w + j * HEAD_PAD_C, grp * group_w + (j + 1) * HEAD_PAD_C)
                kc_ref[:, osl] = (kg[:, j * HEAD_PAD_C:(j + 1) * HEAD_PAD_C] + kr).astype(BF16)
        vct_ref[...] = _dot_nt(wuvt_ref[...], cb).astype(BF16)

    _mla_heads(q_ref, [(kc_ref, vct_ref), (k_ref, vt_ref)], o_ref, ot_ref, MLA_HEAD_GROUP_S)


def _mla_sample(q, k, vt, ckv_ctx, kr_ctx, w_uk, w_uvt, i_c):
    nqb = DEC_SEQ // MLA_QB
    q0 = N_PROMPT // MLA_QB
    s0 = N_PROMPT // DEC_SEQ
    return pl.pallas_call(
        _mla_sample_kernel,
        grid=(DEC_BATCH, nqb),
        in_specs=[
            pl.BlockSpec((MLA_QB, QC_W), lambda b, n: (q0 + b * nqb + n, 0)),
            pl.BlockSpec((DEC_SEQ, QC_W), lambda b, n: (s0 + b, 0)),
            pl.BlockSpec((VC_W, DEC_SEQ), lambda b, n: (0, s0 + b)),
            pl.BlockSpec((None, None, PAST_LEN, KV_LORA), lambda b, n: (b, i_c, 0, 0)),
            pl.BlockSpec((None, None, ROPE_C, PAST_LEN), lambda b, n: (b, i_c, 0, 0)),
            _layer_spec((KV_LORA, QC_W), i_c),
            _layer_spec((VC_W, KV_LORA), i_c),
        ],
        out_specs=pl.BlockSpec((MLA_QB, VC_W), lambda b, n: (b * nqb + n, 0)),
        out_shape=jax.ShapeDtypeStruct((N_SAMPLE, VC_W), BF16),
        scratch_shapes=[pltpu.VMEM((PAST_LEN, QC_W), BF16), pltpu.VMEM((VC_W, PAST_LEN), BF16),
                        pltpu.VMEM((VC_W, MLA_QB), F32)],
        compiler_params=_cparams(("parallel", "arbitrary")),
        name="mla_s",
    )(q, k, vt, ckv_ctx, kr_ctx, w_uk, w_uvt)


def _out_kernel(n_x, split_out, *refs):
    it = iter(refs[n_x:])
    mod_ref, g_ref = next(it), next(it)
    a_refs = (next(it), next(it))
    b_refs = (next(it), next(it))
    wo_ref, w1_ref, w2_ref = next(it), next(it), next(it)
    if split_out:
        next_mod = None
        out_refs = (next(it), next(it))
        h2_ref, acc_ref, x1_ref = next(it), next(it), next(it)
        y_ref = acc_ref
    else:
        next_mod = (next(it), next(it), next(it))
        x1_ref = y_ref = next(it)
        modn_ref = next(it)
        h2_ref, acc_ref = next(it), next(it)

    def emit_next_mod():
        if next_mod is not None:
            modn_ref[...] = _mod_block(*next_mod)

    kk = pl.program_id(1)
    half = wo_ref.shape[0] // 2

    row_chunks = [slice(r * OUT_PROLOGUE_ROWS, (r + 1) * OUT_PROLOGUE_ROWS) for r in range(TM_OUT // OUT_PROLOGUE_ROWS)]

    @pl.when(kk == 0)
    def _():
        wo_a = wo_ref[0:half, :].astype(BF16)
        wo_b = wo_ref[half:, :].astype(BF16)
        gate_g1 = mod_ref[2:3, :] * g_ref[1:2, :]
        scale_g2 = g_ref[2:3, :] * (1.0 + mod_ref[4:5, :])
        for rows in row_chunks:
            a = _load_split(a_refs, TM_OUT, rows)
            b = _load_split(b_refs, TM_OUT, rows)
            mix = _dot(a, wo_a) + _dot(b, wo_b)
            x1 = _load_x(n_x, refs, TM_OUT, rows) + _rms(mix, gate_g1)
            x1_ref[rows, :] = x1
            h2_ref[rows, :] = (_rms(x1, scale_g2) + mod_ref[3:4, :]).astype(BF16)
        acc_ref[...] = jnp.zeros(acc_ref.shape, F32)

    def ffn(rows, w1, w2):
        hid = jnp.maximum(_dot(h2_ref[rows, :], w1), 0.0)
        return _dot((hid * hid).astype(BF16), w2)

    is_last = kk == pl.num_programs(1) - 1

    @pl.when(jnp.logical_not(is_last))
    def _():
        acc_ref[...] += ffn(slice(None), w1_ref[...].astype(BF16), w2_ref[...].astype(BF16))
        emit_next_mod()

    @pl.when(is_last)
    def _():
        w1 = w1_ref[...].astype(BF16)
        w2 = w2_ref[...].astype(BF16)
        gate_g3 = mod_ref[5:6, :] * g_ref[3:4, :]
        for rows in row_chunks:
            y_ref[rows, :] = x1_ref[rows, :] + _rms(acc_ref[rows, :] + ffn(rows, w1, w2), gate_g3)
        emit_next_mod()

    if split_out:
        @pl.when(is_last)
        def _():
            _store_split(out_refs, TM_OUT, acc_ref[...])


def _out_layer(xs, layer, mods, g_norm, mix_a, mix_b, b_col, w_o, i_o, w_ff1, w_ff2, next_mod_args):
    tm = TM_OUT
    half = D_MODEL // 2
    n_k = D_FF // TK_FF
    split_out = next_mod_args is None
    if split_out:
        out_specs, out_shape = _split_specs(tm, D_MODEL), _split_shapes(D_MODEL, F32)
        next_specs, next_args = [], []
    else:
        mod_cols = 6 * D_MODEL // (N_TOK // tm * n_k)
        step = lambda i, k: i * n_k + k
        out_specs = [pl.BlockSpec((tm, D_MODEL), lambda i, k: (i, 0)),
                     pl.BlockSpec((N_MOD_ROWS, mod_cols), lambda i, k: (0, step(i, k)))]
        out_shape = [jax.ShapeDtypeStruct((N_TOK, D_MODEL), F32),
                     jax.ShapeDtypeStruct((N_MOD_ROWS, 6 * D_MODEL), F32)]
        next_specs = [pl.BlockSpec((N_MOD_ROWS, D_MODEL), lambda i, k: (0, 0)),
                      pl.BlockSpec((None, D_MODEL, mod_cols), lambda i, k: (layer + 1, 0, step(i, k))),
                      pl.BlockSpec((None, 1, mod_cols), lambda i, k: (layer + 1, 0, step(i, k)))]
        next_args = list(next_mod_args)
    return pl.pallas_call(
        functools.partial(_out_kernel, len(xs), split_out),
        grid=(N_TOK // tm, n_k),
        in_specs=_x_specs(len(xs), tm) + [
            _mod_spec(tm),
            _layer_spec((4, D_MODEL), layer),
        ] + _split_specs(tm, half, 0) + _split_specs(tm, half, b_col) + [
            _layer_spec((D_MODEL, D_MODEL), i_o),
            pl.BlockSpec((None, D_MODEL, TK_FF), lambda i, k: (layer, 0, k)),
            pl.BlockSpec((None, TK_FF, D_MODEL), lambda i, k: (layer, k, 0)),
        ] + next_specs,
        out_specs=out_specs,
        out_shape=out_shape,
        scratch_shapes=[pltpu.VMEM((tm, D_MODEL), BF16), pltpu.VMEM((tm, D_MODEL), F32)] + (
            [pltpu.VMEM((tm, D_MODEL), F32)] if split_out else []),
        compiler_params=_cparams(("arbitrary", "arbitrary")),
        name="out_mlp",
    )(*xs, mods, g_norm, *mix_a, *mix_b, w_o, w_ff1, w_ff2, *next_args)


def _rope_angles(head_dim):
    nf = head_dim // 4
    n_rows = DEC_SEQ // GRID_W
    rows = jnp.repeat(jnp.arange(n_rows, dtype=F32), GRID_W)
    cols = jnp.tile(jnp.arange(GRID_W, dtype=F32), n_rows)
    inv = ROPE_BASE ** (-jnp.arange(nf, dtype=F32) / nf)
    return jnp.stack([rows[:, None] * inv, cols[:, None] * inv], axis=1)


def _token_tables(cos_g, sin_g):
    cos_t = jnp.concatenate([jnp.ones((N_PROMPT, LANES), F32), jnp.tile(cos_g, (DEC_BATCH, 1))], axis=0)
    sin_t = jnp.concatenate([jnp.zeros((N_PROMPT, LANES), F32), jnp.tile(sin_g, (DEC_BATCH, 1))], axis=0)
    return cos_t, sin_t


def _rope_tables_b():
    nf = D_B // 4
    ang = _rope_angles(D_B)
    cos = jnp.broadcast_to(jnp.cos(ang)[:, :, None, :], (DEC_SEQ, 2, 2, nf)).reshape(DEC_SEQ, D_B)
    sin = jnp.sin(ang)
    sin = jnp.stack([-sin, sin], axis=2).reshape(DEC_SEQ, D_B)
    reps = LANES // D_B
    return _token_tables(jnp.tile(cos, (1, reps)), jnp.tile(sin, (1, reps)))


def _rope_tables_c():
    ang = _rope_angles(ROPE_C).reshape(DEC_SEQ, ROPE_HALF_C)
    cos, sin = jnp.cos(ang), jnp.sin(ang)
    half = LANES // 2
    cos_g = jnp.ones((DEC_SEQ, LANES), F32).at[:, 0:ROPE_HALF_C].set(cos).at[:, half:half + ROPE_HALF_C].set(cos)
    sin_g = jnp.zeros((DEC_SEQ, LANES), F32).at[:, 0:ROPE_HALF_C].set(-sin).at[:, half:half + ROPE_HALF_C].set(sin)
    return _token_tables(cos_g, sin_g)


def kernel(x_prompt, x_sample, state_gla_fwd, state_gla_bwd, cache_swa_k, cache_swa_v, cache_mla_ckv, cache_mla_kr, c, c_ctx, w_mod, b_mod, g_norm, w_ff1, w_ff2, w_in_ab, w_gk_f, b_gk_f, w_gk_b, b_gk_b, g_gla, swa_sink, w_out_ab, w_mla_down, g_mla_q, g_mla_kv, w_mla_uq, w_mla_ukv, w_mla_o):
    xs = (x_prompt.reshape(N_PROMPT, D_MODEL), x_sample.reshape(N_SAMPLE, D_MODEL))
    cvecs = jnp.concatenate([c_ctx[None, :], c, jnp.zeros((N_MOD_ROWS - 1 - DEC_BATCH, D_MODEL), F32)], axis=0)
    b_mod3 = b_mod.reshape(DEPTH, 1, 6 * D_MODEL)
    mods = _modulation(cvecs, w_mod, b_mod3, 0).reshape(N_MOD_ROWS, 6, D_MODEL)

    cos_b, sin_b = _rope_tables_b()
    cos_c, sin_c = _rope_tables_c()
    p_blk = N_PROMPT // DEC_SEQ

    n_ab = w_in_ab.shape[0]
    w_in = _prep_even(jnp.swapaxes(w_in_ab, 1, 2))
    zgk = jnp.zeros((n_ab, GK_RANK, QA_W), F32)
    w_gk = jnp.concatenate([jnp.concatenate([w_gk_f, zgk], axis=2),
                            jnp.concatenate([zgk, w_gk_b], axis=2)], axis=1).astype(BF16)
    b_gk = jnp.concatenate([b_gk_f, b_gk_b], axis=1)[:, None, :]
    gg = g_gla[:, None, :]
    kc = jnp.transpose(cache_swa_k, (0, 1, 3, 4, 2)).reshape(DEC_BATCH, n_ab, KB_W, PAST_LEN)
    vc = jnp.transpose(cache_swa_v, (0, 1, 3, 4, 2)).reshape(DEC_BATCH, n_ab, KB_W, PAST_LEN)
    w_down, w_uq, w_uk, w_uvt = _prep_odd(jnp.swapaxes(w_mla_down, 1, 2), w_mla_uq, w_mla_ukv)
    g_q = g_mla_q[:, None, :]
    g_kv = g_mla_kv[:, None, :]
    kr_ctx = jnp.swapaxes(cache_mla_kr, 2, 3)

    def gla_layer(i_ab, exact):
        def run(qa, ka, va, ga, ld):
            o_p, stf, stb = _gla(SEQ, BATCH, 0, qa, ka, va, ga, ld, gg, i_ab, exact=exact)
            o_s, _, _ = _gla(DEC_SEQ, DEC_BATCH, p_blk, qa, ka, va, ga, ld, gg, i_ab,
                             (state_gla_fwd, state_gla_bwd), exact=exact)
            return o_p, o_s, stf, stb
        return run

    st_f, st_b, sk, sv, ckv_out, ckr_out = [], [], [], [], [], []
    for l in range(DEPTH):
        i = l // 2
        next_mod_args = None if l == DEPTH - 1 else (cvecs, w_mod, b_mod3)
        if l % 2 == 0:
            qa, ka, va, ga, ld, qb, ld_min, kbt, vbt, kb_s, vb_s = _in_even(
                xs, l, mods, g_norm, w_in, w_gk, b_gk, cos_b, sin_b)
            factorisable = jnp.min(ld_min) * GLA_CHUNK >= -GLA_SAFE_TOTAL
            o_gla_p, o_gla_s, stf, stb = lax.cond(factorisable, gla_layer(i, False), gla_layer(i, True),
                                                  qa, ka, va, ga, ld)
            o_swa_p = _swa_prompt(swa_sink, i, qb, kbt, vbt)
            o_swa_s = _swa_sample(swa_sink, i, qb, kb_s, vb_s, kc, vc)
            outs = _out_layer(xs, l, mods, g_norm, (o_gla_p, o_gla_s), (o_swa_p, o_swa_s), 0, w_out_ab, i,
                              w_ff1, w_ff2, next_mod_args)
            st_f.append(stf)
            st_b.append(stb)
            sk.append(kbt)
            sv.append(vbt)
        else:
            q, k, vt, ckv, krt = _in_odd(xs[0], l, mods, g_norm, w_down, g_q, g_kv, w_uq, w_uk, w_uvt, cos_c, sin_c)
            o_mla = (_mla_prompt(q, k, vt), _mla_sample(q, k, vt, cache_mla_ckv, kr_ctx, w_uk, w_uvt, i))
            outs = _out_layer(xs, l, mods, g_norm, o_mla, o_mla, 1, w_mla_o, i, w_ff1, w_ff2, next_mod_args)
            ckv_out.append(ckv.reshape(BATCH, SEQ, KV_LORA))
            ckr_out.append(krt)
        if next_mod_args is None:
            xs = outs
        else:
            xs, mods = (outs[0],), outs[1].reshape(N_MOD_ROWS, 6, D_MODEL)

    y_prompt = xs[0].reshape(BATCH, SEQ, D_MODEL)
    y_sample = xs[1].reshape(DEC_BATCH, DEC_SEQ, D_MODEL)
    swa_cache = lambda parts: jnp.transpose(
        jnp.stack(parts, axis=1).reshape(BATCH, len(parts), KV_B, D_B, SEQ), (0, 1, 4, 2, 3))
    return (y_prompt, y_sample, jnp.stack(st_f, axis=1), jnp.stack(st_b, axis=1), swa_cache(sk), swa_cache(sv),
            jnp.stack(ckv_out, axis=1), jnp.swapaxes(jnp.stack(ckr_out, axis=1), 2, 3))
```

```python
import functools

import jax
import jax.numpy as jnp
from jax import lax
from jax.experimental import pallas as pl
from jax.experimental.pallas import tpu as pltpu

F32 = jnp.float32
BF16 = jnp.bfloat16

D_MODEL = 1024
BATCH = 16
SEQ = 256
DEPTH = 4
DEC_BATCH = 2
DEC_SEQ = 1024
PAST_LEN = 512
GRID_W = 64
D_FF = 4 * D_MODEL
EPS = 1e-6
ROPE_BASE = 10000.0
H_A = 4
DK_A = 64
DV_A = 128
GK_RANK = 16
GATE_NORM = 16.0
GLA_CHUNK = 64
H_B = 8
KV_B = 2
G_B = H_B // KV_B
D_B = 64
WINDOW = 128
H_C = 16
NOPE_C = 64
ROPE_C = 32
V_C = 64
Q_LORA = 384
KV_LORA = 256

N_PROMPT = BATCH * SEQ
N_SAMPLE = DEC_BATCH * DEC_SEQ
N_TOK = N_PROMPT + N_SAMPLE
SUBLANES = 8
N_MOD_ROWS = SUBLANES
QA_W = H_A * DK_A
VA_W = H_A * DV_A
QB_W = H_B * D_B
KB_W = KV_B * D_B
LANES = 128
HEAD_PAD_C = 128
QC_W = H_C * HEAD_PAD_C
VC_W = H_C * V_C
DOWN_RAW_W = Q_LORA + KV_LORA + ROPE_C
DOWN_W = Q_LORA + KV_LORA + LANES
ROPE_HALF_C = ROPE_C // 2
NOPE_LO_C = LANES // 2 - ROPE_HALF_C

TM_IN = 1024
TM_OUT = 1024
TK_FF = 512
OUT_PROLOGUE_ROWS = 256
TN_MOD = 3072
PREP_ROWS = 256
GLA_GROUP = 256
GLA_SCAN_UNROLL = 8
GLA_GROUP_UNROLL = 4
GLA_SAFE_TOTAL = 160.0
SWA_QB = 128
SWA_WIN = 3 * SWA_QB
MLA_QB = 256
HEADS_PER_DOT_C = 4
MLA_HEAD_GROUP_P = 16
MLA_HEAD_GROUP_S = 4
VMEM_LIMIT = 60 * 1024 * 1024
NEG_BIG = -1e30


def _cparams(sem):
    return pltpu.CompilerParams(dimension_semantics=sem, vmem_limit_bytes=VMEM_LIMIT)


def _dot(a, b):
    return jnp.dot(a, b, preferred_element_type=F32)


def _dot_nt(a, b):
    return lax.dot_general(a, b, (((1,), (1,)), ((), ())), preferred_element_type=F32)


def _dot_tn(a, b):
    return lax.dot_general(a, b, (((0,), (0,)), ((), ())), preferred_element_type=F32)


def _rms(x, g):
    return x * lax.rsqrt(jnp.mean(x * x, axis=-1, keepdims=True) + EPS) * g


def _silu(x):
    return x / (1.0 + jnp.exp(-x))


def _rope(x, cos, sin, half):
    lane = lax.broadcasted_iota(jnp.int32, x.shape, 1)
    first = (lane % (2 * half)) < half
    partner = jnp.where(first, pltpu.roll(x, LANES - half, 1), pltpu.roll(x, half, 1))
    return x * cos + partner * sin


def _rope_c(x, cos, sin):
    return x * cos + pltpu.roll(x, LANES // 2, 1) * sin


def _rope_split_c(r):
    q = ROPE_C // 4
    first = jnp.concatenate([r[..., 0:q], r[..., 2 * q:3 * q]], axis=-1)
    second = jnp.concatenate([r[..., q:2 * q], r[..., 3 * q:4 * q]], axis=-1)
    return first, second


def _split3(x):
    x1 = x.astype(BF16)
    r1 = x - x1.astype(F32)
    x2 = r1.astype(BF16)
    x3 = (r1 - x2.astype(F32)).astype(BF16)
    return x1, x2, x3


def _dot3(t, parts):
    return _dot(t, parts[2]) + _dot(t, parts[1]) + _dot(t, parts[0])


def _ones_where(cond):
    return jnp.where(cond, 1.0, 0.0).astype(BF16)


def _mod_row(tile, tm):
    n_prompt_tiles = N_PROMPT // tm
    tiles_per_seq = DEC_SEQ // tm
    return jnp.where(tile < n_prompt_tiles, 0, 1 + (tile - n_prompt_tiles) // tiles_per_seq)


def _layer_spec(shape, idx):
    return pl.BlockSpec((None,) + shape, lambda *_: (idx,) + (0,) * len(shape))


def _mod_spec(tm):
    return pl.BlockSpec((None, 6, D_MODEL), lambda i, *_: (_mod_row(i, tm), 0, 0))


def _split_specs(tm, width, col=0):
    n_prompt_tiles = N_PROMPT // tm
    return [pl.BlockSpec((tm, width), lambda i, *_: (jnp.minimum(i, n_prompt_tiles - 1), col)),
            pl.BlockSpec((tm, width), lambda i, *_: (jnp.maximum(i - n_prompt_tiles, 0), col))]


def _split_shapes(width, dtype):
    return [jax.ShapeDtypeStruct((N_PROMPT, width), dtype), jax.ShapeDtypeStruct((N_SAMPLE, width), dtype)]


def _is_prompt_tile(tm):
    return pl.program_id(0) < N_PROMPT // tm


def _load_split(pair, tm, rows=slice(None)):
    return jnp.where(_is_prompt_tile(tm), pair[0][rows, :], pair[1][rows, :])


def _store_split(pair, tm, value, rows=slice(None)):
    is_prompt = _is_prompt_tile(tm)

    @pl.when(is_prompt)
    def _():
        pair[0][rows, :] = value

    @pl.when(jnp.logical_not(is_prompt))
    def _():
        pair[1][rows, :] = value


_C_QA, _C_KA, _C_VA, _C_GA = 0, QA_W, 2 * QA_W, 2 * QA_W + VA_W
_C_QB = _C_GA + VA_W
_C_KB = _C_QB + QB_W
_C_VB = _C_KB + KB_W
_C_LO = _C_VB + KB_W
AB_IN = _C_LO + 2 * GK_RANK


def _prep_even_kernel(wt_ref, o_ref):
    raw_lo = _C_QB
    raw_qb = raw_lo + 2 * GK_RANK
    o_ref[:, 0:_C_QB] = wt_ref[0:raw_lo, :].T.astype(BF16)
    o_ref[:, _C_QB:_C_LO] = wt_ref[raw_qb:AB_IN, :].T.astype(BF16)
    o_ref[:, _C_LO:AB_IN] = wt_ref[raw_lo:raw_lo + LANES, :].T[:, 0:2 * GK_RANK].astype(BF16)


def _prep_even(w_in_t):
    n = w_in_t.shape[0]
    return pl.pallas_call(
        _prep_even_kernel,
        grid=(n, D_MODEL // PREP_ROWS),
        in_specs=[pl.BlockSpec((None, AB_IN, PREP_ROWS), lambda l, r: (l, 0, r))],
        out_specs=pl.BlockSpec((None, PREP_ROWS, AB_IN), lambda l, r: (l, r, 0)),
        out_shape=jax.ShapeDtypeStruct((n, D_MODEL, AB_IN), BF16),
        compiler_params=_cparams(("parallel", "parallel")),
        name="prep_even",
    )(w_in_t)


def _prep_odd_kernel(wdt_ref, wuq_ref, wukv_ref, od_ref, ouq_ref, ouk_ref, ouvt_ref):
    def head_group(nope, rope_first, rope_second):
        rows = nope.shape[0]
        zero_half = jnp.zeros((rows, ROPE_HALF_C), F32)
        return jnp.concatenate(
            [zero_half if rope_first is None else rope_first, nope[:, 0:NOPE_LO_C],
             zero_half if rope_second is None else rope_second, nope[:, NOPE_LO_C:NOPE_C],
             jnp.zeros((rows, HEAD_PAD_C - NOPE_C - ROPE_C), F32)], axis=1).astype(BF16)

    n_ckv = Q_LORA + KV_LORA
    od_ref[:, 0:n_ckv] = wdt_ref[0:n_ckv, :].T.astype(BF16)
    tail = wdt_ref[DOWN_RAW_W - LANES:DOWN_RAW_W, :].T
    od_ref[:, n_ckv:DOWN_W] = head_group(jnp.zeros((D_MODEL, NOPE_C), F32), *_rope_split_c(tail[:, LANES - ROPE_C:]))
    hd_q = NOPE_C + ROPE_C
    for h in range(H_C):
        wq = wuq_ref[:, h * hd_q:(h + 1) * hd_q]
        ouq_ref[:, h * HEAD_PAD_C:(h + 1) * HEAD_PAD_C] = head_group(wq[:, 0:NOPE_C], *_rope_split_c(wq[:, NOPE_C:]))
    wukv = wukv_ref[...]
    for h in range(H_C):
        ouk_ref[:, h * HEAD_PAD_C:(h + 1) * HEAD_PAD_C] = head_group(
            wukv[:, h * HEAD_PAD_C:h * HEAD_PAD_C + NOPE_C], None, None)
    wukv_t = wukv.T
    for h in range(H_C):
        ouvt_ref[h * V_C:(h + 1) * V_C, :] = wukv_t[h * HEAD_PAD_C + NOPE_C:(h + 1) * HEAD_PAD_C, :].astype(BF16)


def _prep_odd(w_down_t, w_uq, w_ukv):
    n = w_down_t.shape[0]
    spec = lambda r, c: pl.BlockSpec((None, r, c), lambda l: (l, 0, 0))
    return pl.pallas_call(
        _prep_odd_kernel,
        grid=(n,),
        in_specs=[spec(DOWN_RAW_W, D_MODEL), spec(Q_LORA, H_C * (NOPE_C + ROPE_C)), spec(KV_LORA, QC_W)],
        out_specs=[spec(D_MODEL, DOWN_W), spec(Q_LORA, QC_W), spec(KV_LORA, QC_W), spec(VC_W, KV_LORA)],
        out_shape=[jax.ShapeDtypeStruct((n, D_MODEL, DOWN_W), BF16), jax.ShapeDtypeStruct((n, Q_LORA, QC_W), BF16),
                   jax.ShapeDtypeStruct((n, KV_LORA, QC_W), BF16), jax.ShapeDtypeStruct((n, VC_W, KV_LORA), BF16)],
        compiler_params=_cparams(("parallel",)),
        name="prep_odd",
    )(w_down_t, w_uq, w_ukv)


def _mod_block(c_ref, w_ref, b_ref):
    s = _silu(c_ref[...])
    return _dot(s.astype(BF16), w_ref[...].astype(BF16)) + b_ref[...]


def _mod_kernel(c_ref, w_ref, b_ref, o_ref):
    o_ref[...] = _mod_block(c_ref, w_ref, b_ref)


def _modulation(cvecs, w_mod, b_mod, layer):
    return pl.pallas_call(
        _mod_kernel,
        grid=(6 * D_MODEL // TN_MOD,),
        in_specs=[
            pl.BlockSpec((N_MOD_ROWS, D_MODEL), lambda j: (0, 0)),
            pl.BlockSpec((None, D_MODEL, TN_MOD), lambda j: (layer, 0, j)),
            pl.BlockSpec((None, 1, TN_MOD), lambda j: (layer, 0, j)),
        ],
        out_specs=pl.BlockSpec((N_MOD_ROWS, TN_MOD), lambda j: (0, j)),
        out_shape=jax.ShapeDtypeStruct((N_MOD_ROWS, 6 * D_MODEL), F32),
        compiler_params=_cparams(("parallel",)),
        name="adaln_mod",
    )(cvecs, w_mod, b_mod)


def _x_specs(n_x, tm):
    if n_x == 1:
        return [pl.BlockSpec((tm, D_MODEL), lambda i, *_: (i, 0))]
    return _split_specs(tm, D_MODEL)


def _load_x(n_x, refs, tm, rows=slice(None)):
    if n_x == 1:
        return refs[0][rows, :]
    return _load_split(refs[:2], tm, rows)


def _in_even_kernel(n_x, *refs):
    (mod_ref, g_ref, w_ref, wgk_ref, bgk_ref, cos_ref, sin_ref,
     qa_ref, ka_ref, va_ref, ga_ref, ld_ref, qb_ref, ldmin_ref, kbt_ref, vbt_ref, kbs_ref, vbs_ref) = refs[n_x:]
    x = _load_x(n_x, refs, TM_IN)
    h = _rms(x, g_ref[0:1, :]) * (1.0 + mod_ref[1:2, :]) + mod_ref[0:1, :]
    hb = h.astype(BF16)
    qa_ref[...] = _dot(hb, w_ref[:, _C_QA:_C_KA]) * (DK_A ** -0.5)
    ka_ref[...] = _dot(hb, w_ref[:, _C_KA:_C_VA])
    va_ref[...] = _dot(hb, w_ref[:, _C_VA:_C_GA]).astype(BF16)
    ga_ref[...] = _dot(hb, w_ref[:, _C_GA:_C_QB])
    cos = cos_ref[...]
    sin = sin_ref[...]
    qb = _dot(hb, w_ref[:, _C_QB:_C_KB])
    for j in range(QB_W // LANES):
        qj = qb[:, j * LANES:(j + 1) * LANES]
        qb_ref[:, j * LANES:(j + 1) * LANES] = (_rope(qj, cos, sin, D_B // 4) * (D_B ** -0.5)).astype(BF16)
    kvb = _dot(hb, w_ref[:, _C_KB:_C_LO])
    kb = _rope(kvb[:, :KB_W], cos, sin, D_B // 4)
    vb = kvb[:, KB_W:]
    lo = _dot(hb, w_ref[:, _C_LO:AB_IN]).astype(BF16)
    z = _dot(lo, wgk_ref[...]) + bgk_ref[...]
    ld = (jnp.minimum(z, 0.0) - jnp.log(1.0 + jnp.exp(-jnp.abs(z)))) * (1.0 / GATE_NORM)
    ld_ref[...] = ld
    ld_min = jnp.min(jnp.min(ld, axis=0, keepdims=True), axis=1, keepdims=True)
    ldmin_ref[...] = jnp.broadcast_to(ld_min, ldmin_ref.shape)

    is_prompt = _is_prompt_tile(TM_IN)

    @pl.when(is_prompt)
    def _():
        for s in range(TM_IN // SEQ):
            kbt_ref[s] = kb[s * SEQ:(s + 1) * SEQ, :].T
            vbt_ref[s] = vb[s * SEQ:(s + 1) * SEQ, :].T

    @pl.when(jnp.logical_not(is_prompt))
    def _():
        kbs_ref[...] = kb
        vbs_ref[...] = vb


def _in_even(xs, layer, mods, g_norm, w_in, w_gk, b_gk, cos, sin):
    tm = TM_IN
    i_ab = layer // 2
    n_prompt_tiles = N_PROMPT // tm
    seq_per_tile = tm // SEQ
    row = lambda i: (i, 0)
    widths = (QA_W, QA_W, VA_W, VA_W, 2 * QA_W, QB_W)
    dtypes = (F32, F32, BF16, F32, F32, BF16)
    kv_specs = 2 * [pl.BlockSpec((seq_per_tile, KB_W, SEQ), lambda i: (jnp.minimum(i, n_prompt_tiles - 1), 0, 0))] + \
        2 * [pl.BlockSpec((tm, KB_W), lambda i: (jnp.maximum(i - n_prompt_tiles, 0), 0))]
    kv_shapes = 2 * [jax.ShapeDtypeStruct((BATCH, KB_W, SEQ), F32)] + 2 * [jax.ShapeDtypeStruct((N_SAMPLE, KB_W), F32)]
    return pl.pallas_call(
        functools.partial(_in_even_kernel, len(xs)),
        grid=(N_TOK // tm,),
        in_specs=_x_specs(len(xs), tm) + [
            _mod_spec(tm),
            _layer_spec((4, D_MODEL), layer),
            _layer_spec((D_MODEL, AB_IN), i_ab),
            _layer_spec((2 * GK_RANK, 2 * QA_W), i_ab),
            _layer_spec((1, 2 * QA_W), i_ab),
            pl.BlockSpec((tm, LANES), row),
            pl.BlockSpec((tm, LANES), row),
        ],
        out_specs=[pl.BlockSpec((tm, w), row) for w in widths] + [
            pl.BlockSpec((None, SUBLANES, LANES), lambda i: (i, 0, 0))] + kv_specs,
        out_shape=[jax.ShapeDtypeStruct((N_TOK, w), d) for w, d in zip(widths, dtypes)] + [
            jax.ShapeDtypeStruct((N_TOK // tm, SUBLANES, LANES), F32)] + kv_shapes,
        compiler_params=_cparams(("arbitrary",)),
        name="in_even",
    )(*xs, mods, g_norm, w_in, w_gk, b_gk, cos, sin)


def _gla_kernel(seq_len, has_s0, exact, *refs):
    qa_ref, ka_ref, va_ref, ga_ref, ld_ref, gg_ref = refs[:6]
    s0_refs = refs[6:8] if has_s0 else None
    o_ref, stf_ref, stb_ref = refs[8:11] if has_s0 else refs[6:9]
    st_ref, qi_ref, ki_ref, qcat_ref, ks_ref, dec_ref, stcat_ref, acc_ref = refs[-8:]
    b_ref = qi_ref if exact else None
    n_groups = seq_len // GLA_GROUP
    cpg = GLA_GROUP // GLA_CHUNK
    n_chunks = seq_len // GLA_CHUNK
    n_pairs = H_A // 2
    pair_k = 2 * DK_A
    pair_v = 2 * DV_A

    def chunk_masks():
        r_i = lax.broadcasted_iota(jnp.int32, (GLA_GROUP, GLA_GROUP), 0)
        c_i = lax.broadcasted_iota(jnp.int32, (GLA_GROUP, GLA_GROUP), 1)
        same = (r_i // GLA_CHUNK) == (c_i // GLA_CHUNK)
        return same & (c_i <= r_i), same & (c_i >= r_i)

    def scale_group(j, carry):
        mask_f, mask_b = chunk_masks()
        t_cum = (_ones_where(mask_f), _ones_where(mask_b))
        s_r = lax.broadcasted_iota(jnp.int32, (SUBLANES, GLA_GROUP), 0)
        s_c = lax.broadcasted_iota(jnp.int32, (SUBLANES, GLA_GROUP), 1)
        t_sel = _ones_where(s_r == s_c // GLA_CHUNK)
        rows = pl.ds(pl.multiple_of(j * GLA_GROUP, GLA_GROUP), GLA_GROUP)
        q = qa_ref[rows, :]
        k = ka_ref[rows, :]
        for d in range(2):
            parts = _split3(ld_ref[rows, d * QA_W:(d + 1) * QA_W])
            b = _dot3(t_cum[d], parts)
            tot8 = _dot3(t_sel, parts)
            dec8 = jnp.exp(tot8)
            tot = jnp.concatenate(
                [jnp.broadcast_to(tot8[cc:cc + 1, :], (GLA_CHUNK, QA_W)) for cc in range(cpg)], axis=0)
            ref = 0.5 * tot
            if exact:
                b_ref[d, rows, :] = b
            else:
                qi_ref[d, rows, :] = (q * jnp.exp(b - ref)).astype(BF16)
                ki_ref[d, rows, :] = (k * jnp.exp(ref - b)).astype(BF16)
            ks_ref[d, rows, :] = (k * jnp.exp(tot - b)).astype(BF16)
            q_inter = (q * jnp.exp(b)).astype(BF16)
            for p in range(n_pairs):
                qcat_ref[rows, p * 2 * pair_k + d * pair_k:p * 2 * pair_k + (d + 1) * pair_k] = (
                    q_inter[:, p * pair_k:(p + 1) * pair_k])
            for cc in range(cpg):
                dec_ref[d, j * cpg + cc] = jnp.broadcast_to(dec8[cc:cc + 1, :], (SUBLANES, QA_W))
        return carry

    lax.fori_loop(0, n_groups, scale_group, 0, unroll=min(n_groups, GLA_GROUP_UNROLL))

    zpad = jnp.zeros((DK_A, DV_A), F32)
    for d in range(2):
        for p in range(n_pairs):
            if has_s0:
                top = jnp.concatenate([s0_refs[d][2 * p], zpad], axis=0).T
                bot = jnp.concatenate([zpad, s0_refs[d][2 * p + 1]], axis=0).T
                st_ref[d, p] = jnp.concatenate([top, bot], axis=0)
            else:
                st_ref[d, p] = jnp.zeros((pair_v, pair_k), F32)

    def scan_chunk(c, carry):
        bd_r = lax.broadcasted_iota(jnp.int32, (pair_v, pair_k), 0)
        bd_c = lax.broadcasted_iota(jnp.int32, (pair_v, pair_k), 1)
        bd_mask = (bd_r // DV_A) == (bd_c // DK_A)
        for d in range(2):
            cd = c if d == 0 else n_chunks - 1 - c
            rows = pl.ds(pl.multiple_of(cd * GLA_CHUNK, GLA_CHUNK), GLA_CHUNK)
            for p in range(n_pairs):
                st = st_ref[d, p]
                stcat_ref[cd, p, :, d * pair_k:(d + 1) * pair_k] = st.astype(BF16)
                u = _dot_tn(va_ref[rows, p * pair_v:(p + 1) * pair_v], ks_ref[d, rows, p * pair_k:(p + 1) * pair_k])
                dec = dec_ref[d, cd][0:1, p * pair_k:(p + 1) * pair_k]
                st_ref[d, p] = st * dec + jnp.where(bd_mask, u, 0.0)
        return carry

    lax.fori_loop(0, n_chunks, scan_chunk, 0, unroll=GLA_SCAN_UNROLL)

    for d, out_ref in ((0, stf_ref), (1, stb_ref)):
        for p in range(n_pairs):
            st = st_ref[d, p]
            out_ref[2 * p] = st[0:DV_A, :].T[0:DK_A, :]
            out_ref[2 * p + 1] = st[DV_A:pair_v, :].T[DK_A:pair_k, :]

    def exact_intra(row0, h):
        p, hh = divmod(h, 2)
        lanes = slice(p * pair_k, (p + 1) * pair_k)
        head_lanes = (lax.broadcasted_iota(jnp.int32, (GLA_CHUNK, pair_k), 1) // DK_A) == hh
        s_idx = lax.broadcasted_iota(jnp.int32, (GLA_CHUNK, GLA_CHUNK), 0)
        t_idx = lax.broadcasted_iota(jnp.int32, (GLA_CHUNK, GLA_CHUNK), 1)
        outs = []
        for cc in range(cpg):
            r0 = row0 + cc * GLA_CHUNK
            crow = pl.ds(pl.multiple_of(r0, GLA_CHUNK), GLA_CHUNK)
            kc = ka_ref[crow, lanes]
            att_t = jnp.zeros((GLA_CHUNK, GLA_CHUNK), F32)
            for d in range(2):
                bc = b_ref[d, crow, lanes]

                def row_step(t, att_t, d=d, bc=bc, kc=kc, r0=r0):
                    grp = pl.ds(pl.multiple_of(r0 + (t // SUBLANES) * SUBLANES, SUBLANES), SUBLANES)
                    pick = lax.broadcasted_iota(jnp.int32, (SUBLANES, pair_k), 0) == t % SUBLANES
                    bt = jnp.sum(jnp.where(pick, b_ref[d, grp, lanes], 0.0), axis=0, keepdims=True)
                    qt = jnp.sum(jnp.where(pick, qa_ref[grp, lanes], 0.0), axis=0, keepdims=True)
                    w = jnp.where(head_lanes, qt * kc * jnp.exp(jnp.minimum(bt - bc, 0.0)), 0.0)
                    col = jnp.sum(w, axis=1, keepdims=True)
                    allowed = (s_idx <= t) if d == 0 else (s_idx >= t)
                    return jnp.where((t_idx == t) & allowed, att_t + col, att_t)

                att_t = lax.fori_loop(0, GLA_CHUNK, row_step, att_t)
            outs.append(_dot_tn(att_t.astype(BF16), va_ref[crow, h * DV_A:(h + 1) * DV_A]))
        return jnp.concatenate(outs, axis=0)

    def out_group(j, carry):
        mask_f, mask_b = chunk_masks()
        lane = lax.broadcasted_iota(jnp.int32, (GLA_GROUP, pair_k), 1)
        rows = pl.ds(pl.multiple_of(j * GLA_GROUP, GLA_GROUP), GLA_GROUP)
        for p in range(n_pairs):
            for cc in range(cpg):
                c = j * cpg + cc
                crow = pl.ds(pl.multiple_of(c * GLA_CHUNK, GLA_CHUNK), GLA_CHUNK)
                acc_ref[crow, p * pair_v:(p + 1) * pair_v] = _dot_nt(
                    qcat_ref[crow, p * 2 * pair_k:(p + 1) * 2 * pair_k], stcat_ref[c, p])
        if exact:
            intras = [exact_intra(j * GLA_GROUP, h) for h in range(H_A)]
        else:
            atts = []
            for h in range(H_A):
                p, hh = divmod(h, 2)
                head_lanes = _ones_where((lane // DK_A) == hh)
                att = None
                for d, mask in ((0, mask_f), (1, mask_b)):
                    qm = qi_ref[d, rows, p * pair_k:(p + 1) * pair_k] * head_lanes
                    a = jnp.where(mask, _dot_nt(qm, ki_ref[d, rows, p * pair_k:(p + 1) * pair_k]), 0.0)
                    att = a if att is None else att + a
                atts.append(att.astype(BF16))
            intras = [_dot(atts[h], va_ref[rows, h * DV_A:(h + 1) * DV_A]) for h in range(H_A)]
        for h in range(H_A):
            o = acc_ref[rows, h * DV_A:(h + 1) * DV_A] + intras[h]
            gate = _silu(ga_ref[rows, h * DV_A:(h + 1) * DV_A])
            o_ref[rows, h * DV_A:(h + 1) * DV_A] = (_rms(o, gg_ref[...]) * gate).astype(BF16)
        return carry

    lax.fori_loop(0, n_groups, out_group, 0, unroll=min(n_groups, GLA_GROUP_UNROLL))


def _gla(seq_len, n_seq, row_block0, qa, ka, va, ga, ld, g_gla, i_ab, s0=None, exact=False):
    has_s0 = s0 is not None
    n_chunks = seq_len // GLA_CHUNK
    n_pairs = H_A // 2
    rows = lambda b: (row_block0 + b, 0)
    st_spec = pl.BlockSpec((None, H_A, DK_A, DV_A), lambda b: (b, 0, 0, 0))
    in_specs = [
        pl.BlockSpec((seq_len, QA_W), rows),
        pl.BlockSpec((seq_len, QA_W), rows),
        pl.BlockSpec((seq_len, VA_W), rows),
        pl.BlockSpec((seq_len, VA_W), rows),
        pl.BlockSpec((seq_len, 2 * QA_W), rows),
        _layer_spec((1, DV_A), i_ab),
    ]
    args = [qa, ka, va, ga, ld, g_gla]
    if has_s0:
        s0_spec = pl.BlockSpec((None, None, H_A, DK_A, DV_A), lambda b: (b, i_ab, 0, 0, 0))
        in_specs += [s0_spec, s0_spec]
        args += list(s0)
    st_shape = jax.ShapeDtypeStruct((n_seq, H_A, DK_A, DV_A), F32)
    return pl.pallas_call(
        functools.partial(_gla_kernel, seq_len, has_s0, exact),
        grid=(n_seq,),
        in_specs=in_specs,
        out_specs=[pl.BlockSpec((seq_len, VA_W), lambda b: (b, 0)), st_spec, st_spec],
        out_shape=[jax.ShapeDtypeStruct((n_seq * seq_len, VA_W), BF16), st_shape, st_shape],
        scratch_shapes=[
            pltpu.VMEM((2, n_pairs, 2 * DV_A, 2 * DK_A), F32),
            pltpu.VMEM((2, seq_len, QA_W), F32 if exact else BF16),
            pltpu.VMEM((2, seq_len, QA_W), BF16),
            pltpu.VMEM((seq_len, 2 * QA_W), BF16),
            pltpu.VMEM((2, seq_len, QA_W), BF16),
            pltpu.VMEM((2, n_chunks, SUBLANES, QA_W), F32),
            pltpu.VMEM((n_chunks, n_pairs, 2 * DV_A, 4 * DK_A), BF16),
            pltpu.VMEM((seq_len, VA_W), F32),
        ],
        compiler_params=_cparams(("parallel",)),
        name=("gla_s" if has_s0 else "gla_p") + ("_exact" if exact else ""),
    )(*args)


def _swa_head_softmax(pieces, sink):
    m = sink
    for s, _, _ in pieces:
        m = jnp.maximum(m, jnp.max(s, axis=-1, keepdims=True))
    den = jnp.exp(sink - m)
    acc = None
    for s, v, transposed in pieces:
        e = jnp.exp(s - m)
        den = den + jnp.sum(e, axis=-1, keepdims=True)
        pv = _dot_nt(e.astype(BF16), v) if transposed else _dot(e.astype(BF16), v)
        acc = pv if acc is None else acc + pv
    return acc / den


def _dup_groups(x):
    lane = lax.broadcasted_iota(jnp.int32, x.shape, 1)
    swapped = pltpu.roll(x, D_B, 1)
    low = lane < D_B
    return jnp.where(low, x, swapped).astype(BF16), jnp.where(low, swapped, x).astype(BF16)


def _dup_groups_t(xt):
    g0, g1 = xt[0:D_B, :], xt[D_B:, :]
    return jnp.concatenate([g0, g0], axis=0).astype(BF16), jnp.concatenate([g1, g1], axis=0).astype(BF16)


def _swa_heads(sink_ref, i_ab, q_ref, kv_pieces, o_ref, n_rows):
    lane = lax.broadcasted_iota(jnp.int32, (n_rows, LANES), 1)
    head_pieces = []
    for h in range(H_B):
        j, hh = divmod(h, 2)
        g = h // G_B
        qm = q_ref[:, j * LANES:(j + 1) * LANES] * _ones_where((lane // D_B) == hh)
        pieces = []
        for keys, vals, transposed, mask in kv_pieces:
            s = _dot(qm, keys[g]) if transposed else _dot_nt(qm, keys[g])
            if mask is not None:
                s = jnp.where(mask, s, NEG_BIG)
            pieces.append((s, vals[g], transposed))
        head_pieces.append(pieces)
    outs = [_swa_head_softmax(pieces, sink_ref[i_ab, h]) for h, pieces in enumerate(head_pieces)]
    for j in range(H_B // 2):
        o_ref[:, j * LANES:(j + 1) * LANES] = jnp.where(lane < D_B, outs[2 * j], outs[2 * j + 1]).astype(BF16)


def _swa_prompt_kernel(i_ab, sink_ref, q_ref, kt_ref, vt_ref, o_ref):
    _swa_heads(sink_ref, i_ab, q_ref, [(_dup_groups_t(kt_ref[...]), _dup_groups_t(vt_ref[...]), True, None)],
               o_ref, SEQ)


def _swa_prompt(sink, i_ab, qb, kbt, vbt):
    seq = lambda b: (b, 0)
    seq_t = pl.BlockSpec((None, KB_W, SEQ), lambda b: (b, 0, 0))
    return pl.pallas_call(
        functools.partial(_swa_prompt_kernel, i_ab),
        grid=(BATCH,),
        in_specs=[pl.BlockSpec(memory_space=pltpu.SMEM), pl.BlockSpec((SEQ, QB_W), seq), seq_t, seq_t],
        out_specs=pl.BlockSpec((SEQ, QB_W), seq),
        out_shape=jax.ShapeDtypeStruct((N_PROMPT, QB_W), BF16),
        compiler_params=_cparams(("parallel",)),
        name="swa_p",
    )(sink, qb, kbt, vbt)


def _swa_sample_kernel(i_ab, sink_ref, q_ref, k_ref, v_ref, kct_ref, vct_ref, o_ref):
    n = pl.program_id(1)
    start = pl.multiple_of(jnp.clip((n - 1) * SWA_QB, 0, DEC_SEQ - SWA_WIN), SWA_QB)
    local = (_dup_groups(k_ref[pl.ds(start, SWA_WIN), :]), _dup_groups(v_ref[pl.ds(start, SWA_WIN), :]))
    ctx = (_dup_groups_t(kct_ref[...]), _dup_groups_t(vct_ref[...]))
    qi = n * SWA_QB + lax.broadcasted_iota(jnp.int32, (SWA_QB, SWA_WIN), 0)
    ki = start + lax.broadcasted_iota(jnp.int32, (SWA_QB, SWA_WIN), 1)
    band = jnp.abs(qi - ki) <= WINDOW
    _swa_heads(sink_ref, i_ab, q_ref, [ctx + (True, None), local + (False, band)], o_ref, SWA_QB)


def _swa_sample(sink, i_ab, qb, kb, vb, kc, vc):
    nqb = DEC_SEQ // SWA_QB
    q0 = N_PROMPT // SWA_QB
    return pl.pallas_call(
        functools.partial(_swa_sample_kernel, i_ab),
        grid=(DEC_BATCH, nqb),
        in_specs=[
            pl.BlockSpec(memory_space=pltpu.SMEM),
            pl.BlockSpec((SWA_QB, QB_W), lambda b, n: (q0 + b * nqb + n, 0)),
            pl.BlockSpec((DEC_SEQ, KB_W), lambda b, n: (b, 0)),
            pl.BlockSpec((DEC_SEQ, KB_W), lambda b, n: (b, 0)),
            pl.BlockSpec((None, None, KB_W, PAST_LEN), lambda b, n: (b, i_ab, 0, 0)),
            pl.BlockSpec((None, None, KB_W, PAST_LEN), lambda b, n: (b, i_ab, 0, 0)),
        ],
        out_specs=pl.BlockSpec((SWA_QB, QB_W), lambda b, n: (b * nqb + n, 0)),
        out_shape=jax.ShapeDtypeStruct((N_SAMPLE, QB_W), BF16),
        compiler_params=_cparams(("parallel", "parallel")),
        name="swa_s",
    )(sink, qb, kb, vb, kc, vc)


def _in_odd_kernel(x_ref, mod_ref, g_ref, wd_ref, gq_ref, gkv_ref, wuq_ref, wuk_ref, wuvt_ref, cos_ref, sin_ref,
                   q_ref, k_ref, vt_ref, ckv_ref, krt_ref):
    h = _rms(x_ref[...], g_ref[0:1, :]) * (1.0 + mod_ref[1:2, :]) + mod_ref[0:1, :]
    hb = h.astype(BF16)
    cos = cos_ref[...]
    sin = sin_ref[...]
    c_q = _dot(hb, wd_ref[:, 0:Q_LORA])
    c_kv = _rms(_dot(hb, wd_ref[:, Q_LORA:Q_LORA + KV_LORA]), gkv_ref[...])
    kr = _rope_c(_dot(hb, wd_ref[:, Q_LORA + KV_LORA:DOWN_W]), cos, sin)
    cqb = _rms(c_q, gq_ref[...]).astype(BF16)
    ckvb = c_kv.astype(BF16)
    scale = (NOPE_C + ROPE_C) ** -0.5
    group_w = HEADS_PER_DOT_C * HEAD_PAD_C
    for grp in range(H_C // HEADS_PER_DOT_C):
        gsl = slice(grp * group_w, (grp + 1) * group_w)
        qg = _dot(cqb, wuq_ref[:, gsl])
        kg = _dot(ckvb, wuk_ref[:, gsl])
        for j in range(HEADS_PER_DOT_C):
            sl = slice(j * HEAD_PAD_C, (j + 1) * HEAD_PAD_C)
            osl = slice(grp * group_w + j * HEAD_PAD_C, grp * group_w + (j + 1) * HEAD_PAD_C)
            q_ref[:, osl] = (_rope_c(qg[:, sl], cos, sin) * scale).astype(BF16)
            k_ref[:, osl] = (kg[:, sl] + kr).astype(BF16)
    vt_ref[...] = _dot_nt(wuvt_ref[...], ckvb).astype(BF16)

    @pl.when(_is_prompt_tile(TM_IN))
    def _():
        ckv_ref[...] = c_kv
        q4 = ROPE_C // 4
        half = LANES // 2
        for s in range(TM_IN // SEQ):
            t = kr[s * SEQ:(s + 1) * SEQ, :].T
            krt_ref[s] = jnp.concatenate([t[0:q4], t[half:half + q4], t[q4:2 * q4], t[half + q4:half + 2 * q4]], axis=0)


def _in_odd(x, layer, mods, g_norm, w_down, g_q, g_kv, w_uq, w_uk, w_uvt, cos, sin):
    tm = TM_IN
    i_c = layer // 2
    n_prompt_tiles = N_PROMPT // tm
    row = lambda i: (i, 0)
    prompt_row = lambda i: (jnp.minimum(i, n_prompt_tiles - 1), 0)
    prompt_seq = lambda i: (jnp.minimum(i, n_prompt_tiles - 1), 0, 0)
    return pl.pallas_call(
        _in_odd_kernel,
        grid=(N_TOK // tm,),
        in_specs=[
            pl.BlockSpec((tm, D_MODEL), row),
            _mod_spec(tm),
            _layer_spec((4, D_MODEL), layer),
            _layer_spec((D_MODEL, DOWN_W), i_c),
            _layer_spec((1, Q_LORA), i_c),
            _layer_spec((1, KV_LORA), i_c),
            _layer_spec((Q_LORA, QC_W), i_c),
            _layer_spec((KV_LORA, QC_W), i_c),
            _layer_spec((VC_W, KV_LORA), i_c),
            pl.BlockSpec((tm, LANES), row),
            pl.BlockSpec((tm, LANES), row),
        ],
        out_specs=[pl.BlockSpec((tm, QC_W), row), pl.BlockSpec((tm, QC_W), row),
                   pl.BlockSpec((VC_W, tm), lambda i: (0, i)),
                   pl.BlockSpec((tm, KV_LORA), prompt_row), pl.BlockSpec((tm // SEQ, ROPE_C, SEQ), prompt_seq)],
        out_shape=[jax.ShapeDtypeStruct((N_TOK, QC_W), BF16), jax.ShapeDtypeStruct((N_TOK, QC_W), BF16),
                   jax.ShapeDtypeStruct((VC_W, N_TOK), BF16),
                   jax.ShapeDtypeStruct((N_PROMPT, KV_LORA), F32), jax.ShapeDtypeStruct((BATCH, ROPE_C, SEQ), F32)],
        compiler_params=_cparams(("arbitrary",)),
        name="in_odd",
    )(x, mods, g_norm, w_down, g_q, g_kv, w_uq, w_uk, w_uvt, cos, sin)


def _reduce_rows(x, op, reduce_fn):
    while x.shape[0] % (2 * SUBLANES) == 0:
        half = x.shape[0] // 2
        x = op(x[:half], x[half:])
    return reduce_fn(x, axis=0, keepdims=True)


def _mla_heads(q_ref, kv_pieces, o_ref, ot_ref, group):
    for h0 in range(0, H_C, group):
        heads = range(h0, h0 + group)
        scores = []
        for hd in heads:
            sl = slice(hd * HEAD_PAD_C, (hd + 1) * HEAD_PAD_C)
            scores.append([_dot_nt(k_ref[:, sl], q_ref[:, sl]) for k_ref, _ in kv_pieces])
        exps, dens = [], []
        for per_piece in scores:
            m = None
            for s in per_piece:
                sm = _reduce_rows(s, jnp.maximum, jnp.max)
                m = sm if m is None else jnp.maximum(m, sm)
            es = [jnp.exp(s - m) for s in per_piece]
            den = None
            for e in es:
                part = _reduce_rows(e, jnp.add, jnp.sum)
                den = part if den is None else den + part
            exps.append([e.astype(BF16) for e in es])
            dens.append(den)
        for hd, es, den in zip(heads, exps, dens):
            acc = None
            for e, (_, vt_ref) in zip(es, kv_pieces):
                pv = _dot(vt_ref[hd * V_C:(hd + 1) * V_C, :], e)
                acc = pv if acc is None else acc + pv
            ot_ref[hd * V_C:(hd + 1) * V_C, :] = acc / den
    o_ref[...] = ot_ref[...].T.astype(BF16)


def _mla_prompt_kernel(q_ref, k_ref, vt_ref, o_ref, ot_ref):
    _mla_heads(q_ref, [(k_ref, vt_ref)], o_ref, ot_ref, MLA_HEAD_GROUP_P)


def _mla_prompt(q, k, vt):
    seq = lambda b: (b, 0)
    return pl.pallas_call(
        _mla_prompt_kernel,
        grid=(BATCH,),
        in_specs=[pl.BlockSpec((SEQ, QC_W), seq), pl.BlockSpec((SEQ, QC_W), seq),
                  pl.BlockSpec((VC_W, SEQ), lambda b: (0, b))],
        out_specs=pl.BlockSpec((SEQ, VC_W), seq),
        out_shape=jax.ShapeDtypeStruct((N_PROMPT, VC_W), BF16),
        scratch_shapes=[pltpu.VMEM((VC_W, SEQ), F32)],
        compiler_params=_cparams(("parallel",)),
        name="mla_p",
    )(q, k, vt)


def _mla_sample_kernel(q_ref, k_ref, vt_ref, ckv_ref, krt_ref, wuk_ref, wuvt_ref, o_ref, kc_ref, vct_ref, ot_ref):
    @pl.when(pl.program_id(1) == 0)
    def _():
        cb = ckv_ref[...].astype(BF16)
        krt = krt_ref[...]
        q4 = ROPE_C // 4
        kr = jnp.concatenate(
            [krt[0:q4], krt[2 * q4:3 * q4], jnp.zeros((NOPE_LO_C, PAST_LEN), F32),
             krt[q4:2 * q4], krt[3 * q4:4 * q4], jnp.zeros((LANES // 2 - ROPE_HALF_C, PAST_LEN), F32)], axis=0).T
        group_w = HEADS_PER_DOT_C * HEAD_PAD_C
        for grp in range(H_C // HEADS_PER_DOT_C):
            kg = _dot(cb, wuk_ref[:, grp * group_w:(grp + 1) * group_w])
            for j in range(HEADS_PER_DOT_C):
                osl = slice(grp * group_w + j * HEAD_PAD_C, grp * group_w + (j + 1) * HEAD_PAD_C)
                kc_ref[:, osl] = (kg[:, j * HEAD_PAD_C:(j + 1) * HEAD_PAD_C] + kr).astype(BF16)
        vct_ref[...] = _dot_nt(wuvt_ref[...], cb).astype(BF16)

    _mla_heads(q_ref, [(kc_ref, vct_ref), (k_ref, vt_ref)], o_ref, ot_ref, MLA_HEAD_GROUP_S)


def _mla_sample(q, k, vt, ckv_ctx, kr_ctx, w_uk, w_uvt, i_c):
    nqb = DEC_SEQ // MLA_QB
    q0 = N_PROMPT // MLA_QB
    s0 = N_PROMPT // DEC_SEQ
    return pl.pallas_call(
        _mla_sample_kernel,
        grid=(DEC_BATCH, nqb),
        in_specs=[
            pl.BlockSpec((MLA_QB, QC_W), lambda b, n: (q0 + b * nqb + n, 0)),
            pl.BlockSpec((DEC_SEQ, QC_W), lambda b, n: (s0 + b, 0)),
            pl.BlockSpec((VC_W, DEC_SEQ), lambda b, n: (0, s0 + b)),
            pl.BlockSpec((None, None, PAST_LEN, KV_LORA), lambda b, n: (b, i_c, 0, 0)),
            pl.BlockSpec((None, None, ROPE_C, PAST_LEN), lambda b, n: (b, i_c, 0, 0)),
            _layer_spec((KV_LORA, QC_W), i_c),
            _layer_spec((VC_W, KV_LORA), i_c),
        ],
        out_specs=pl.BlockSpec((MLA_QB, VC_W), lambda b, n: (b * nqb + n, 0)),
        out_shape=jax.ShapeDtypeStruct((N_SAMPLE, VC_W), BF16),
        scratch_shapes=[pltpu.VMEM((PAST_LEN, QC_W), BF16), pltpu.VMEM((VC_W, PAST_LEN), BF16),
                        pltpu.VMEM((VC_W, MLA_QB), F32)],
        compiler_params=_cparams(("parallel", "arbitrary")),
        name="mla_s",
    )(q, k, vt, ckv_ctx, kr_ctx, w_uk, w_uvt)


def _out_kernel(n_x, split_out, *refs):
    it = iter(refs[n_x:])
    mod_ref, g_ref = next(it), next(it)
    a_refs = (next(it), next(it))
    b_refs = (next(it), next(it))
    wo_ref, w1_ref, w2_ref = next(it), next(it), next(it)
    if split_out:
        next_mod = None
        out_refs = (next(it), next(it))
        h2_ref, acc_ref, x1_ref = next(it), next(it), next(it)
        y_ref = acc_ref
    else:
        next_mod = (next(it), next(it), next(it))
        x1_ref = y_ref = next(it)
        modn_ref = next(it)
        h2_ref, acc_ref = next(it), next(it)

    def emit_next_mod():
        if next_mod is not None:
            modn_ref[...] = _mod_block(*next_mod)

    kk = pl.program_id(1)
    half = wo_ref.shape[0] // 2

    row_chunks = [slice(r * OUT_PROLOGUE_ROWS, (r + 1) * OUT_PROLOGUE_ROWS) for r in range(TM_OUT // OUT_PROLOGUE_ROWS)]

    @pl.when(kk == 0)
    def _():
        wo_a = wo_ref[0:half, :].astype(BF16)
        wo_b = wo_ref[half:, :].astype(BF16)
        gate_g1 = mod_ref[2:3, :] * g_ref[1:2, :]
        scale_g2 = g_ref[2:3, :] * (1.0 + mod_ref[4:5, :])
        for rows in row_chunks:
            a = _load_split(a_refs, TM_OUT, rows)
            b = _load_split(b_refs, TM_OUT, rows)
            mix = _dot(a, wo_a) + _dot(b, wo_b)
            x1 = _load_x(n_x, refs, TM_OUT, rows) + _rms(mix, gate_g1)
            x1_ref[rows, :] = x1
            h2_ref[rows, :] = (_rms(x1, scale_g2) + mod_ref[3:4, :]).astype(BF16)
        acc_ref[...] = jnp.zeros(acc_ref.shape, F32)

    def ffn(rows, w1, w2):
        hid = jnp.maximum(_dot(h2_ref[rows, :], w1), 0.0)
        return _dot((hid * hid).astype(BF16), w2)

    is_last = kk == pl.num_programs(1) - 1

    @pl.when(jnp.logical_not(is_last))
    def _():
        acc_ref[...] += ffn(slice(None), w1_ref[...].astype(BF16), w2_ref[...].astype(BF16))
        emit_next_mod()

    @pl.when(is_last)
    def _():
        w1 = w1_ref[...].astype(BF16)
        w2 = w2_ref[...].astype(BF16)
        gate_g3 = mod_ref[5:6, :] * g_ref[3:4, :]
        for rows in row_chunks:
            y_ref[rows, :] = x1_ref[rows, :] + _rms(acc_ref[rows, :] + ffn(rows, w1, w2), gate_g3)
        emit_next_mod()

    if split_out:
        @pl.when(is_last)
        def _():
            _store_split(out_refs, TM_OUT, acc_ref[...])


def _out_layer(xs, layer, mods, g_norm, mix_a, mix_b, b_col, w_o, i_o, w_ff1, w_ff2, next_mod_args):
    tm = TM_OUT
    half = D_MODEL // 2
    n_k = D_FF // TK_FF
    split_out = next_mod_args is None
    if split_out:
        out_specs, out_shape = _split_specs(tm, D_MODEL), _split_shapes(D_MODEL, F32)
        next_specs, next_args = [], []
    else:
        mod_cols = 6 * D_MODEL // (N_TOK // tm * n_k)
        step = lambda i, k: i * n_k + k
        out_specs = [pl.BlockSpec((tm, D_MODEL), lambda i, k: (i, 0)),
                     pl.BlockSpec((N_MOD_ROWS, mod_cols), lambda i, k: (0, step(i, k)))]
        out_shape = [jax.ShapeDtypeStruct((N_TOK, D_MODEL), F32),
                     jax.ShapeDtypeStruct((N_MOD_ROWS, 6 * D_MODEL), F32)]
        next_specs = [pl.BlockSpec((N_MOD_ROWS, D_MODEL), lambda i, k: (0, 0)),
                      pl.BlockSpec((None, D_MODEL, mod_cols), lambda i, k: (layer + 1, 0, step(i, k))),
                      pl.BlockSpec((None, 1, mod_cols), lambda i, k: (layer + 1, 0, step(i, k)))]
        next_args = list(next_mod_args)
    return pl.pallas_call(
        functools.partial(_out_kernel, len(xs), split_out),
        grid=(N_TOK // tm, n_k),
        in_specs=_x_specs(len(xs), tm) + [
            _mod_spec(tm),
            _layer_spec((4, D_MODEL), layer),
        ] + _split_specs(tm, half, 0) + _split_specs(tm, half, b_col) + [
            _layer_spec((D_MODEL, D_MODEL), i_o),
            pl.BlockSpec((None, D_MODEL, TK_FF), lambda i, k: (layer, 0, k)),
            pl.BlockSpec((None, TK_FF, D_MODEL), lambda i, k: (layer, k, 0)),
        ] + next_specs,
        out_specs=out_specs,
        out_shape=out_shape,
        scratch_shapes=[pltpu.VMEM((tm, D_MODEL), BF16), pltpu.VMEM((tm, D_MODEL), F32)] + (
            [pltpu.VMEM((tm, D_MODEL), F32)] if split_out else []),
        compiler_params=_cparams(("arbitrary", "arbitrary")),
        name="out_mlp",
    )(*xs, mods, g_norm, *mix_a, *mix_b, w_o, w_ff1, w_ff2, *next_args)


def _rope_angles(head_dim):
    nf = head_dim // 4
    n_rows = DEC_SEQ // GRID_W
    rows = jnp.repeat(jnp.arange(n_rows, dtype=F32), GRID_W)
    cols = jnp.tile(jnp.arange(GRID_W, dtype=F32), n_rows)
    inv = ROPE_BASE ** (-jnp.arange(nf, dtype=F32) / nf)
    return jnp.stack([rows[:, None] * inv, cols[:, None] * inv], axis=1)


def _token_tables(cos_g, sin_g):
    cos_t = jnp.concatenate([jnp.ones((N_PROMPT, LANES), F32), jnp.tile(cos_g, (DEC_BATCH, 1))], axis=0)
    sin_t = jnp.concatenate([jnp.zeros((N_PROMPT, LANES), F32), jnp.tile(sin_g, (DEC_BATCH, 1))], axis=0)
    return cos_t, sin_t


def _rope_tables_b():
    nf = D_B // 4
    ang = _rope_angles(D_B)
    cos = jnp.broadcast_to(jnp.cos(ang)[:, :, None, :], (DEC_SEQ, 2, 2, nf)).reshape(DEC_SEQ, D_B)
    sin = jnp.sin(ang)
    sin = jnp.stack([-sin, sin], axis=2).reshape(DEC_SEQ, D_B)
    reps = LANES // D_B
    return _token_tables(jnp.tile(cos, (1, reps)), jnp.tile(sin, (1, reps)))


def _rope_tables_c():
    ang = _rope_angles(ROPE_C).reshape(DEC_SEQ, ROPE_HALF_C)
    cos, sin = jnp.cos(ang), jnp.sin(ang)
    half = LANES // 2
    cos_g = jnp.ones((DEC_SEQ, LANES), F32).at[:, 0:ROPE_HALF_C].set(cos).at[:, half:half + ROPE_HALF_C].set(cos)
    sin_g = jnp.zeros((DEC_SEQ, LANES), F32).at[:, 0:ROPE_HALF_C].set(-sin).at[:, half:half + ROPE_HALF_C].set(sin)
    return _token_tables(cos_g, sin_g)


def kernel(x_prompt, x_sample, state_gla_fwd, state_gla_bwd, cache_swa_k, cache_swa_v, cache_mla_ckv, cache_mla_kr, c, c_ctx, w_mod, b_mod, g_norm, w_ff1, w_ff2, w_in_ab, w_gk_f, b_gk_f, w_gk_b, b_gk_b, g_gla, swa_sink, w_out_ab, w_mla_down, g_mla_q, g_mla_kv, w_mla_uq, w_mla_ukv, w_mla_o):
    xs = (x_prompt.reshape(N_PROMPT, D_MODEL), x_sample.reshape(N_SAMPLE, D_MODEL))
    cvecs = jnp.concatenate([c_ctx[None, :], c, jnp.zeros((N_MOD_ROWS - 1 - DEC_BATCH, D_MODEL), F32)], axis=0)
    b_mod3 = b_mod.reshape(DEPTH, 1, 6 * D_MODEL)
    mods = _modulation(cvecs, w_mod, b_mod3, 0).reshape(N_MOD_ROWS, 6, D_MODEL)

    cos_b, sin_b = _rope_tables_b()
    cos_c, sin_c = _rope_tables_c()
    p_blk = N_PROMPT // DEC_SEQ

    n_ab = w_in_ab.shape[0]
    w_in = _prep_even(jnp.swapaxes(w_in_ab, 1, 2))
    zgk = jnp.zeros((n_ab, GK_RANK, QA_W), F32)
    w_gk = jnp.concatenate([jnp.concatenate([w_gk_f, zgk], axis=2),
                            jnp.concatenate([zgk, w_gk_b], axis=2)], axis=1).astype(BF16)
    b_gk = jnp.concatenate([b_gk_f, b_gk_b], axis=1)[:, None, :]
    gg = g_gla[:, None, :]
    kc = jnp.transpose(cache_swa_k, (0, 1, 3, 4, 2)).reshape(DEC_BATCH, n_ab, KB_W, PAST_LEN)
    vc = jnp.transpose(cache_swa_v, (0, 1, 3, 4, 2)).reshape(DEC_BATCH, n_ab, KB_W, PAST_LEN)
    w_down, w_uq, w_uk, w_uvt = _prep_odd(jnp.swapaxes(w_mla_down, 1, 2), w_mla_uq, w_mla_ukv)
    g_q = g_mla_q[:, None, :]
    g_kv = g_mla_kv[:, None, :]
    kr_ctx = jnp.swapaxes(cache_mla_kr, 2, 3)

    def gla_layer(i_ab, exact):
        def run(qa, ka, va, ga, ld):
            o_p, stf, stb = _gla(SEQ, BATCH, 0, qa, ka, va, ga, ld, gg, i_ab, exact=exact)
            o_s, _, _ = _gla(DEC_SEQ, DEC_BATCH, p_blk, qa, ka, va, ga, ld, gg, i_ab,
                             (state_gla_fwd, state_gla_bwd), exact=exact)
            return o_p, o_s, stf, stb
        return run

    st_f, st_b, sk, sv, ckv_out, ckr_out = [], [], [], [], [], []
    for l in range(DEPTH):
        i = l // 2
        next_mod_args = None if l == DEPTH - 1 else (cvecs, w_mod, b_mod3)
        if l % 2 == 0:
            qa, ka, va, ga, ld, qb, ld_min, kbt, vbt, kb_s, vb_s = _in_even(
                xs, l, mods, g_norm, w_in, w_gk, b_gk, cos_b, sin_b)
            factorisable = jnp.min(ld_min) * GLA_CHUNK >= -GLA_SAFE_TOTAL
            o_gla_p, o_gla_s, stf, stb = lax.cond(factorisable, gla_layer(i, False), gla_layer(i, True),
                                                  qa, ka, va, ga, ld)
            o_swa_p = _swa_prompt(swa_sink, i, qb, kbt, vbt)
            o_swa_s = _swa_sample(swa_sink, i, qb, kb_s, vb_s, kc, vc)
            outs = _out_layer(xs, l, mods, g_norm, (o_gla_p, o_gla_s), (o_swa_p, o_swa_s), 0, w_out_ab, i,
                              w_ff1, w_ff2, next_mod_args)
            st_f.append(stf)
            st_b.append(stb)
            sk.append(kbt)
            sv.append(vbt)
        else:
            q, k, vt, ckv, krt = _in_odd(xs[0], l, mods, g_norm, w_down, g_q, g_kv, w_uq, w_uk, w_uvt, cos_c, sin_c)
            o_mla = (_mla_prompt(q, k, vt), _mla_sample(q, k, vt, cache_mla_ckv, kr_ctx, w_uk, w_uvt, i))
            outs = _out_layer(xs, l, mods, g_norm, o_mla, o_mla, 1, w_mla_o, i, w_ff1, w_ff2, next_mod_args)
            ckv_out.append(ckv.reshape(BATCH, SEQ, KV_LORA))
            ckr_out.append(krt)
        if next_mod_args is None:
            xs = outs
        else:
            xs, mods = (outs[0],), outs[1].reshape(N_MOD_ROWS, 6, D_MODEL)

    y_prompt = xs[0].reshape(BATCH, SEQ, D_MODEL)
    y_sample = xs[1].reshape(DEC_BATCH, DEC_SEQ, D_MODEL)
    swa_cache = lambda parts: jnp.transpose(
        jnp.stack(parts, axis=1).reshape(BATCH, len(parts), KV_B, D_B, SEQ), (0, 1, 4, 2, 3))
    return (y_prompt, y_sample, jnp.stack(st_f, axis=1), jnp.stack(st_b, axis=1), swa_cache(sk), swa_cache(sv),
            jnp.stack(ckv_out, axis=1), jnp.swapaxes(jnp.stack(ckr_out, axis=1), 2, 3))
```

```python
import functools

import jax
import jax.numpy as jnp
from jax import lax
from jax.experimental import pallas as pl
from jax.experimental.pallas import tpu as pltpu

F32 = jnp.float32
BF16 = jnp.bfloat16

D_MODEL = 1024
BATCH = 16
SEQ = 256
DEPTH = 4
DEC_BATCH = 2
DEC_SEQ = 1024
PAST_LEN = 512
GRID_W = 64
D_FF = 4 * D_MODEL
EPS = 1e-6
ROPE_BASE = 10000.0
H_A = 4
DK_A = 64
DV_A = 128
GK_RANK = 16
GATE_NORM = 16.0
GLA_CHUNK = 64
H_B = 8
KV_B = 2
G_B = H_B // KV_B
D_B = 64
WINDOW = 128
H_C = 16
NOPE_C = 64
ROPE_C = 32
V_C = 64
Q_LORA = 384
KV_LORA = 256

N_PROMPT = BATCH * SEQ
N_SAMPLE = DEC_BATCH * DEC_SEQ
N_TOK = N_PROMPT + N_SAMPLE
SUBLANES = 8
N_MOD_ROWS = SUBLANES
QA_W = H_A * DK_A
VA_W = H_A * DV_A
QB_W = H_B * D_B
KB_W = KV_B * D_B
LANES = 128
HEAD_PAD_C = 128
QC_W = H_C * HEAD_PAD_C
VC_W = H_C * V_C
DOWN_RAW_W = Q_LORA + KV_LORA + ROPE_C
DOWN_W = Q_LORA + KV_LORA + LANES
ROPE_HALF_C = ROPE_C // 2
NOPE_LO_C = LANES // 2 - ROPE_HALF_C

TM_IN = 512
TM_OUT = 1024
TK_FF = 512
OUT_PROLOGUE_ROWS = 256
TN_MOD = 3072
PREP_ROWS = 256
GLA_GROUP = 256
GLA_SCAN_UNROLL = 8
GLA_GROUP_UNROLL = 4
GLA_SAFE_TOTAL = 160.0
SWA_QB = 128
SWA_WIN = 3 * SWA_QB
MLA_QB = 256
HEADS_PER_DOT_C = 4
MLA_HEAD_GROUP_P = 16
MLA_HEAD_GROUP_S = 4
VMEM_LIMIT = 60 * 1024 * 1024
NEG_BIG = -1e30


def _cparams(sem):
    return pltpu.CompilerParams(dimension_semantics=sem, vmem_limit_bytes=VMEM_LIMIT)


def _dot(a, b):
    return jnp.dot(a, b, preferred_element_type=F32)


def _dot_nt(a, b):
    return lax.dot_general(a, b, (((1,), (1,)), ((), ())), preferred_element_type=F32)


def _dot_tn(a, b):
    return lax.dot_general(a, b, (((0,), (0,)), ((), ())), preferred_element_type=F32)


def _rms(x, g):
    return x * lax.rsqrt(jnp.mean(x * x, axis=-1, keepdims=True) + EPS) * g


def _silu(x):
    return x / (1.0 + jnp.exp(-x))


def _rope(x, cos, sin, half):
    lane = lax.broadcasted_iota(jnp.int32, x.shape, 1)
    first = (lane % (2 * half)) < half
    partner = jnp.where(first, pltpu.roll(x, LANES - half, 1), pltpu.roll(x, half, 1))
    return x * cos + partner * sin


def _rope_c(x, cos, sin):
    return x * cos + pltpu.roll(x, LANES // 2, 1) * sin


def _rope_split_c(r):
    q = ROPE_C // 4
    first = jnp.concatenate([r[..., 0:q], r[..., 2 * q:3 * q]], axis=-1)
    second = jnp.concatenate([r[..., q:2 * q], r[..., 3 * q:4 * q]], axis=-1)
    return first, second


def _split3(x):
    x1 = x.astype(BF16)
    r1 = x - x1.astype(F32)
    x2 = r1.astype(BF16)
    x3 = (r1 - x2.astype(F32)).astype(BF16)
    return x1, x2, x3


def _dot3(t, parts):
    return _dot(t, parts[2]) + _dot(t, parts[1]) + _dot(t, parts[0])


def _ones_where(cond):
    return jnp.where(cond, 1.0, 0.0).astype(BF16)


def _mod_row(tile, tm):
    n_prompt_tiles = N_PROMPT // tm
    tiles_per_seq = DEC_SEQ // tm
    return jnp.where(tile < n_prompt_tiles, 0, 1 + (tile - n_prompt_tiles) // tiles_per_seq)


def _layer_spec(shape, idx):
    return pl.BlockSpec((None,) + shape, lambda *_: (idx,) + (0,) * len(shape))


def _mod_spec(tm):
    return pl.BlockSpec((None, 6, D_MODEL), lambda i, *_: (_mod_row(i, tm), 0, 0))


def _split_specs(tm, width, col=0):
    n_prompt_tiles = N_PROMPT // tm
    return [pl.BlockSpec((tm, width), lambda i, *_: (jnp.minimum(i, n_prompt_tiles - 1), col)),
            pl.BlockSpec((tm, width), lambda i, *_: (jnp.maximum(i - n_prompt_tiles, 0), col))]


def _split_shapes(width, dtype):
    return [jax.ShapeDtypeStruct((N_PROMPT, width), dtype), jax.ShapeDtypeStruct((N_SAMPLE, width), dtype)]


def _is_prompt_tile(tm):
    return pl.program_id(0) < N_PROMPT // tm


def _load_split(pair, tm, rows=slice(None)):
    return jnp.where(_is_prompt_tile(tm), pair[0][rows, :], pair[1][rows, :])


def _store_split(pair, tm, value, rows=slice(None)):
    is_prompt = _is_prompt_tile(tm)

    @pl.when(is_prompt)
    def _():
        pair[0][rows, :] = value

    @pl.when(jnp.logical_not(is_prompt))
    def _():
        pair[1][rows, :] = value


_C_QA, _C_KA, _C_VA, _C_GA = 0, QA_W, 2 * QA_W, 2 * QA_W + VA_W
_C_QB = _C_GA + VA_W
_C_KB = _C_QB + QB_W
_C_VB = _C_KB + KB_W
_C_LO = _C_VB + KB_W
AB_IN = _C_LO + 2 * GK_RANK


def _prep_even_kernel(wt_ref, o_ref):
    raw_lo = _C_QB
    raw_qb = raw_lo + 2 * GK_RANK
    o_ref[:, 0:_C_QB] = wt_ref[0:raw_lo, :].T.astype(BF16)
    o_ref[:, _C_QB:_C_LO] = wt_ref[raw_qb:AB_IN, :].T.astype(BF16)
    o_ref[:, _C_LO:AB_IN] = wt_ref[raw_lo:raw_lo + LANES, :].T[:, 0:2 * GK_RANK].astype(BF16)


def _prep_even(w_in_t):
    n = w_in_t.shape[0]
    return pl.pallas_call(
        _prep_even_kernel,
        grid=(n, D_MODEL // PREP_ROWS),
        in_specs=[pl.BlockSpec((None, AB_IN, PREP_ROWS), lambda l, r: (l, 0, r))],
        out_specs=pl.BlockSpec((None, PREP_ROWS, AB_IN), lambda l, r: (l, r, 0)),
        out_shape=jax.ShapeDtypeStruct((n, D_MODEL, AB_IN), BF16),
        compiler_params=_cparams(("parallel", "parallel")),
        name="prep_even",
    )(w_in_t)


def _prep_odd_kernel(wdt_ref, wuq_ref, wukv_ref, od_ref, ouq_ref, ouk_ref, ouvt_ref):
    def head_group(nope, rope_first, rope_second):
        rows = nope.shape[0]
        zero_half = jnp.zeros((rows, ROPE_HALF_C), F32)
        return jnp.concatenate(
            [zero_half if rope_first is None else rope_first, nope[:, 0:NOPE_LO_C],
             zero_half if rope_second is None else rope_second, nope[:, NOPE_LO_C:NOPE_C],
             jnp.zeros((rows, HEAD_PAD_C - NOPE_C - ROPE_C), F32)], axis=1).astype(BF16)

    n_ckv = Q_LORA + KV_LORA
    od_ref[:, 0:n_ckv] = wdt_ref[0:n_ckv, :].T.astype(BF16)
    tail = wdt_ref[DOWN_RAW_W - LANES:DOWN_RAW_W, :].T
    od_ref[:, n_ckv:DOWN_W] = head_group(jnp.zeros((D_MODEL, NOPE_C), F32), *_rope_split_c(tail[:, LANES - ROPE_C:]))
    hd_q = NOPE_C + ROPE_C
    for h in range(H_C):
        wq = wuq_ref[:, h * hd_q:(h + 1) * hd_q]
        ouq_ref[:, h * HEAD_PAD_C:(h + 1) * HEAD_PAD_C] = head_group(wq[:, 0:NOPE_C], *_rope_split_c(wq[:, NOPE_C:]))
    wukv = wukv_ref[...]
    for h in range(H_C):
        ouk_ref[:, h * HEAD_PAD_C:(h + 1) * HEAD_PAD_C] = head_group(
            wukv[:, h * HEAD_PAD_C:h * HEAD_PAD_C + NOPE_C], None, None)
    wukv_t = wukv.T
    for h in range(H_C):
        ouvt_ref[h * V_C:(h + 1) * V_C, :] = wukv_t[h * HEAD_PAD_C + NOPE_C:(h + 1) * HEAD_PAD_C, :].astype(BF16)


def _prep_odd(w_down_t, w_uq, w_ukv):
    n = w_down_t.shape[0]
    spec = lambda r, c: pl.BlockSpec((None, r, c), lambda l: (l, 0, 0))
    return pl.pallas_call(
        _prep_odd_kernel,
        grid=(n,),
        in_specs=[spec(DOWN_RAW_W, D_MODEL), spec(Q_LORA, H_C * (NOPE_C + ROPE_C)), spec(KV_LORA, QC_W)],
        out_specs=[spec(D_MODEL, DOWN_W), spec(Q_LORA, QC_W), spec(KV_LORA, QC_W), spec(VC_W, KV_LORA)],
        out_shape=[jax.ShapeDtypeStruct((n, D_MODEL, DOWN_W), BF16), jax.ShapeDtypeStruct((n, Q_LORA, QC_W), BF16),
                   jax.ShapeDtypeStruct((n, KV_LORA, QC_W), BF16), jax.ShapeDtypeStruct((n, VC_W, KV_LORA), BF16)],
        compiler_params=_cparams(("parallel",)),
        name="prep_odd",
    )(w_down_t, w_uq, w_ukv)


def _mod_block(c_ref, w_ref, b_ref):
    s = _silu(c_ref[...])
    return _dot(s.astype(BF16), w_ref[...].astype(BF16)) + b_ref[...]


def _mod_kernel(c_ref, w_ref, b_ref, o_ref):
    o_ref[...] = _mod_block(c_ref, w_ref, b_ref)


def _modulation(cvecs, w_mod, b_mod, layer):
    return pl.pallas_call(
        _mod_kernel,
        grid=(6 * D_MODEL // TN_MOD,),
        in_specs=[
            pl.BlockSpec((N_MOD_ROWS, D_MODEL), lambda j: (0, 0)),
            pl.BlockSpec((None, D_MODEL, TN_MOD), lambda j: (layer, 0, j)),
            pl.BlockSpec((None, 1, TN_MOD), lambda j: (layer, 0, j)),
        ],
        out_specs=pl.BlockSpec((N_MOD_ROWS, TN_MOD), lambda j: (0, j)),
        out_shape=jax.ShapeDtypeStruct((N_MOD_ROWS, 6 * D_MODEL), F32),
        compiler_params=_cparams(("parallel",)),
        name="adaln_mod",
    )(cvecs, w_mod, b_mod)


def _x_specs(n_x, tm):
    if n_x == 1:
        return [pl.BlockSpec((tm, D_MODEL), lambda i, *_: (i, 0))]
    return _split_specs(tm, D_MODEL)


def _load_x(n_x, refs, tm, rows=slice(None)):
    if n_x == 1:
        return refs[0][rows, :]
    return _load_split(refs[:2], tm, rows)


def _in_even_kernel(n_x, *refs):
    (mod_ref, g_ref, w_ref, wgk_ref, bgk_ref, cos_ref, sin_ref,
     qa_ref, ka_ref, va_ref, ga_ref, ld_ref, qb_ref, ldmin_ref, kbt_ref, vbt_ref, kbs_ref, vbs_ref) = refs[n_x:]
    x = _load_x(n_x, refs, TM_IN)
    h = _rms(x, g_ref[0:1, :]) * (1.0 + mod_ref[1:2, :]) + mod_ref[0:1, :]
    hb = h.astype(BF16)
    qa_ref[...] = _dot(hb, w_ref[:, _C_QA:_C_KA]) * (DK_A ** -0.5)
    ka_ref[...] = _dot(hb, w_ref[:, _C_KA:_C_VA])
    va_ref[...] = _dot(hb, w_ref[:, _C_VA:_C_GA]).astype(BF16)
    ga_ref[...] = _dot(hb, w_ref[:, _C_GA:_C_QB])
    cos = cos_ref[...]
    sin = sin_ref[...]
    qb = _dot(hb, w_ref[:, _C_QB:_C_KB])
    for j in range(QB_W // LANES):
        qj = qb[:, j * LANES:(j + 1) * LANES]
        qb_ref[:, j * LANES:(j + 1) * LANES] = (_rope(qj, cos, sin, D_B // 4) * (D_B ** -0.5)).astype(BF16)
    kvb = _dot(hb, w_ref[:, _C_KB:_C_LO])
    kb = _rope(kvb[:, :KB_W], cos, sin, D_B // 4)
    vb = kvb[:, KB_W:]
    lo = _dot(hb, w_ref[:, _C_LO:AB_IN]).astype(BF16)
    z = _dot(lo, wgk_ref[...]) + bgk_ref[...]
    ld = (jnp.minimum(z, 0.0) - jnp.log(1.0 + jnp.exp(-jnp.abs(z)))) * (1.0 / GATE_NORM)
    ld_ref[...] = ld
    ld_min = jnp.min(jnp.min(ld, axis=0, keepdims=True), axis=1, keepdims=True)
    ldmin_ref[...] = jnp.broadcast_to(ld_min, ldmin_ref.shape)

    is_prompt = _is_prompt_tile(TM_IN)

    @pl.when(is_prompt)
    def _():
        for s in range(TM_IN // SEQ):
            kbt_ref[s] = kb[s * SEQ:(s + 1) * SEQ, :].T
            vbt_ref[s] = vb[s * SEQ:(s + 1) * SEQ, :].T

    @pl.when(jnp.logical_not(is_prompt))
    def _():
        kbs_ref[...] = kb
        vbs_ref[...] = vb


def _in_even(xs, layer, mods, g_norm, w_in, w_gk, b_gk, cos, sin):
    tm = TM_IN
    i_ab = layer // 2
    n_prompt_tiles = N_PROMPT // tm
    seq_per_tile = tm // SEQ
    row = lambda i: (i, 0)
    widths = (QA_W, QA_W, VA_W, VA_W, 2 * QA_W, QB_W)
    dtypes = (F32, F32, BF16, F32, F32, BF16)
    kv_specs = 2 * [pl.BlockSpec((seq_per_tile, KB_W, SEQ), lambda i: (jnp.minimum(i, n_prompt_tiles - 1), 0, 0))] + \
        2 * [pl.BlockSpec((tm, KB_W), lambda i: (jnp.maximum(i - n_prompt_tiles, 0), 0))]
    kv_shapes = 2 * [jax.ShapeDtypeStruct((BATCH, KB_W, SEQ), F32)] + 2 * [jax.ShapeDtypeStruct((N_SAMPLE, KB_W), F32)]
    return pl.pallas_call(
        functools.partial(_in_even_kernel, len(xs)),
        grid=(N_TOK // tm,),
        in_specs=_x_specs(len(xs), tm) + [
            _mod_spec(tm),
            _layer_spec((4, D_MODEL), layer),
            _layer_spec((D_MODEL, AB_IN), i_ab),
            _layer_spec((2 * GK_RANK, 2 * QA_W), i_ab),
            _layer_spec((1, 2 * QA_W), i_ab),
            pl.BlockSpec((tm, LANES), row),
            pl.BlockSpec((tm, LANES), row),
        ],
        out_specs=[pl.BlockSpec((tm, w), row) for w in widths] + [
            pl.BlockSpec((None, SUBLANES, LANES), lambda i: (i, 0, 0))] + kv_specs,
        out_shape=[jax.ShapeDtypeStruct((N_TOK, w), d) for w, d in zip(widths, dtypes)] + [
            jax.ShapeDtypeStruct((N_TOK // tm, SUBLANES, LANES), F32)] + kv_shapes,
        compiler_params=_cparams(("arbitrary",)),
        name="in_even",
    )(*xs, mods, g_norm, w_in, w_gk, b_gk, cos, sin)


def _gla_kernel(seq_len, has_s0, exact, *refs):
    qa_ref, ka_ref, va_ref, ga_ref, ld_ref, gg_ref = refs[:6]
    s0_refs = refs[6:8] if has_s0 else None
    o_ref, stf_ref, stb_ref = refs[8:11] if has_s0 else refs[6:9]
    st_ref, qi_ref, ki_ref, qcat_ref, ks_ref, dec_ref, stcat_ref, acc_ref = refs[-8:]
    b_ref = qi_ref if exact else None
    n_groups = seq_len // GLA_GROUP
    cpg = GLA_GROUP // GLA_CHUNK
    n_chunks = seq_len // GLA_CHUNK
    n_pairs = H_A // 2
    pair_k = 2 * DK_A
    pair_v = 2 * DV_A

    def chunk_masks():
        r_i = lax.broadcasted_iota(jnp.int32, (GLA_GROUP, GLA_GROUP), 0)
        c_i = lax.broadcasted_iota(jnp.int32, (GLA_GROUP, GLA_GROUP), 1)
        same = (r_i // GLA_CHUNK) == (c_i // GLA_CHUNK)
        return same & (c_i <= r_i), same & (c_i >= r_i)

    def scale_group(j, carry):
        mask_f, mask_b = chunk_masks()
        t_cum = (_ones_where(mask_f), _ones_where(mask_b))
        s_r = lax.broadcasted_iota(jnp.int32, (SUBLANES, GLA_GROUP), 0)
        s_c = lax.broadcasted_iota(jnp.int32, (SUBLANES, GLA_GROUP), 1)
        t_sel = _ones_where(s_r == s_c // GLA_CHUNK)
        rows = pl.ds(pl.multiple_of(j * GLA_GROUP, GLA_GROUP), GLA_GROUP)
        q = qa_ref[rows, :]
        k = ka_ref[rows, :]
        for d in range(2):
            parts = _split3(ld_ref[rows, d * QA_W:(d + 1) * QA_W])
            b = _dot3(t_cum[d], parts)
            tot8 = _dot3(t_sel, parts)
            dec8 = jnp.exp(tot8)
            tot = jnp.concatenate(
                [jnp.broadcast_to(tot8[cc:cc + 1, :], (GLA_CHUNK, QA_W)) for cc in range(cpg)], axis=0)
            ref = 0.5 * tot
            if exact:
                b_ref[d, rows, :] = b
            else:
                qi_ref[d, rows, :] = (q * jnp.exp(b - ref)).astype(BF16)
                ki_ref[d, rows, :] = (k * jnp.exp(ref - b)).astype(BF16)
            ks_ref[d, rows, :] = (k * jnp.exp(tot - b)).astype(BF16)
            q_inter = (q * jnp.exp(b)).astype(BF16)
            for p in range(n_pairs):
                qcat_ref[rows, p * 2 * pair_k + d * pair_k:p * 2 * pair_k + (d + 1) * pair_k] = (
                    q_inter[:, p * pair_k:(p + 1) * pair_k])
            for cc in range(cpg):
                dec_ref[d, j * cpg + cc] = jnp.broadcast_to(dec8[cc:cc + 1, :], (SUBLANES, QA_W))
        return carry

    group_unroll = 1 if exact else min(n_groups, GLA_GROUP_UNROLL)
    lax.fori_loop(0, n_groups, scale_group, 0, unroll=group_unroll)

    zpad = jnp.zeros((DK_A, DV_A), F32)
    for d in range(2):
        for p in range(n_pairs):
            if has_s0:
                top = jnp.concatenate([s0_refs[d][2 * p], zpad], axis=0).T
                bot = jnp.concatenate([zpad, s0_refs[d][2 * p + 1]], axis=0).T
                st_ref[d, p] = jnp.concatenate([top, bot], axis=0)
            else:
                st_ref[d, p] = jnp.zeros((pair_v, pair_k), F32)

    def scan_chunk(c, carry):
        bd_r = lax.broadcasted_iota(jnp.int32, (pair_v, pair_k), 0)
        bd_c = lax.broadcasted_iota(jnp.int32, (pair_v, pair_k), 1)
        bd_mask = (bd_r // DV_A) == (bd_c // DK_A)
        for d in range(2):
            cd = c if d == 0 else n_chunks - 1 - c
            rows = pl.ds(pl.multiple_of(cd * GLA_CHUNK, GLA_CHUNK), GLA_CHUNK)
            for p in range(n_pairs):
                st = st_ref[d, p]
                stcat_ref[cd, p, :, d * pair_k:(d + 1) * pair_k] = st.astype(BF16)
                u = _dot_tn(va_ref[rows, p * pair_v:(p + 1) * pair_v], ks_ref[d, rows, p * pair_k:(p + 1) * pair_k])
                dec = dec_ref[d, cd][0:1, p * pair_k:(p + 1) * pair_k]
                st_ref[d, p] = st * dec + jnp.where(bd_mask, u, 0.0)
        return carry

    lax.fori_loop(0, n_chunks, scan_chunk, 0, unroll=1 if exact else GLA_SCAN_UNROLL)

    for d, out_ref in ((0, stf_ref), (1, stb_ref)):
        for p in range(n_pairs):
            st = st_ref[d, p]
            out_ref[2 * p] = st[0:DV_A, :].T[0:DK_A, :]
            out_ref[2 * p + 1] = st[DV_A:pair_v, :].T[DK_A:pair_k, :]

    def exact_intra(row0, h):
        p, hh = divmod(h, 2)
        lanes = slice(p * pair_k, (p + 1) * pair_k)
        head_lanes = (lax.broadcasted_iota(jnp.int32, (GLA_CHUNK, pair_k), 1) // DK_A) == hh
        s_idx = lax.broadcasted_iota(jnp.int32, (GLA_CHUNK, GLA_CHUNK), 0)
        t_idx = lax.broadcasted_iota(jnp.int32, (GLA_CHUNK, GLA_CHUNK), 1)
        outs = []
        for cc in range(cpg):
            r0 = row0 + cc * GLA_CHUNK
            crow = pl.ds(pl.multiple_of(r0, GLA_CHUNK), GLA_CHUNK)
            kc = ka_ref[crow, lanes]
            att_t = jnp.zeros((GLA_CHUNK, GLA_CHUNK), F32)
            for d in range(2):
                bc = b_ref[d, crow, lanes]

                def row_step(t, att_t, d=d, bc=bc, kc=kc, r0=r0):
                    grp = pl.ds(pl.multiple_of(r0 + (t // SUBLANES) * SUBLANES, SUBLANES), SUBLANES)
                    pick = lax.broadcasted_iota(jnp.int32, (SUBLANES, pair_k), 0) == t % SUBLANES
                    bt = jnp.sum(jnp.where(pick, b_ref[d, grp, lanes], 0.0), axis=0, keepdims=True)
                    qt = jnp.sum(jnp.where(pick, qa_ref[grp, lanes], 0.0), axis=0, keepdims=True)
                    w = jnp.where(head_lanes, qt * kc * jnp.exp(jnp.minimum(bt - bc, 0.0)), 0.0)
                    col = jnp.sum(w, axis=1, keepdims=True)
                    allowed = (s_idx <= t) if d == 0 else (s_idx >= t)
                    return jnp.where((t_idx == t) & allowed, att_t + col, att_t)

                att_t = lax.fori_loop(0, GLA_CHUNK, row_step, att_t)
            outs.append(_dot_tn(att_t.astype(BF16), va_ref[crow, h * DV_A:(h + 1) * DV_A]))
        return jnp.concatenate(outs, axis=0)

    def out_group(j, carry):
        mask_f, mask_b = chunk_masks()
        lane = lax.broadcasted_iota(jnp.int32, (GLA_GROUP, pair_k), 1)
        rows = pl.ds(pl.multiple_of(j * GLA_GROUP, GLA_GROUP), GLA_GROUP)
        for p in range(n_pairs):
            for cc in range(cpg):
                c = j * cpg + cc
                crow = pl.ds(pl.multiple_of(c * GLA_CHUNK, GLA_CHUNK), GLA_CHUNK)
                acc_ref[crow, p * pair_v:(p + 1) * pair_v] = _dot_nt(
                    qcat_ref[crow, p * 2 * pair_k:(p + 1) * 2 * pair_k], stcat_ref[c, p])
        if exact:
            intras = [exact_intra(j * GLA_GROUP, h) for h in range(H_A)]
        else:
            atts = []
            for h in range(H_A):
                p, hh = divmod(h, 2)
                head_lanes = _ones_where((lane // DK_A) == hh)
                att = None
                for d, mask in ((0, mask_f), (1, mask_b)):
                    qm = qi_ref[d, rows, p * pair_k:(p + 1) * pair_k] * head_lanes
                    a = jnp.where(mask, _dot_nt(qm, ki_ref[d, rows, p * pair_k:(p + 1) * pair_k]), 0.0)
                    att = a if att is None else att + a
                atts.append(att.astype(BF16))
            intras = [_dot(atts[h], va_ref[rows, h * DV_A:(h + 1) * DV_A]) for h in range(H_A)]
        for h in range(H_A):
            o = acc_ref[rows, h * DV_A:(h + 1) * DV_A] + intras[h]
            gate = _silu(ga_ref[rows, h * DV_A:(h + 1) * DV_A])
            o_ref[rows, h * DV_A:(h + 1) * DV_A] = (_rms(o, gg_ref[...]) * gate).astype(BF16)
        return carry

    lax.fori_loop(0, n_groups, out_group, 0, unroll=group_unroll)


def _gla(seq_len, n_seq, row_block0, qa, ka, va, ga, ld, g_gla, i_ab, s0=None, exact=False):
    has_s0 = s0 is not None
    n_chunks = seq_len // GLA_CHUNK
    n_pairs = H_A // 2
    rows = lambda b: (row_block0 + b, 0)
    st_spec = pl.BlockSpec((None, H_A, DK_A, DV_A), lambda b: (b, 0, 0, 0))
    in_specs = [
        pl.BlockSpec((seq_len, QA_W), rows),
        pl.BlockSpec((seq_len, QA_W), rows),
        pl.BlockSpec((seq_len, VA_W), rows),
        pl.BlockSpec((seq_len, VA_W), rows),
        pl.BlockSpec((seq_len, 2 * QA_W), rows),
        _layer_spec((1, DV_A), i_ab),
    ]
    args = [qa, ka, va, ga, ld, g_gla]
    if has_s0:
        s0_spec = pl.BlockSpec((None, None, H_A, DK_A, DV_A), lambda b: (b, i_ab, 0, 0, 0))
        in_specs += [s0_spec, s0_spec]
        args += list(s0)
    st_shape = jax.ShapeDtypeStruct((n_seq, H_A, DK_A, DV_A), F32)
    return pl.pallas_call(
        functools.partial(_gla_kernel, seq_len, has_s0, exact),
        grid=(n_seq,),
        in_specs=in_specs,
        out_specs=[pl.BlockSpec((seq_len, VA_W), lambda b: (b, 0)), st_spec, st_spec],
        out_shape=[jax.ShapeDtypeStruct((n_seq * seq_len, VA_W), BF16), st_shape, st_shape],
        scratch_shapes=[
            pltpu.VMEM((2, n_pairs, 2 * DV_A, 2 * DK_A), F32),
            pltpu.VMEM((2, seq_len, QA_W), F32 if exact else BF16),
            pltpu.VMEM((2, seq_len, QA_W), BF16),
            pltpu.VMEM((seq_len, 2 * QA_W), BF16),
            pltpu.VMEM((2, seq_len, QA_W), BF16),
            pltpu.VMEM((2, n_chunks, SUBLANES, QA_W), F32),
            pltpu.VMEM((n_chunks, n_pairs, 2 * DV_A, 4 * DK_A), BF16),
            pltpu.VMEM((seq_len, VA_W), F32),
        ],
        compiler_params=_cparams(("parallel",)),
        name=("gla_s" if has_s0 else "gla_p") + ("_exact" if exact else ""),
    )(*args)


def _swa_head_softmax(pieces, sink):
    m = sink
    for s, _, _ in pieces:
        m = jnp.maximum(m, jnp.max(s, axis=-1, keepdims=True))
    den = jnp.exp(sink - m)
    acc = None
    for s, v, transposed in pieces:
        e = jnp.exp(s - m)
        den = den + jnp.sum(e, axis=-1, keepdims=True)
        pv = _dot_nt(e.astype(BF16), v) if transposed else _dot(e.astype(BF16), v)
        acc = pv if acc is None else acc + pv
    return acc / den


def _dup_groups(x):
    lane = lax.broadcasted_iota(jnp.int32, x.shape, 1)
    swapped = pltpu.roll(x, D_B, 1)
    low = lane < D_B
    return jnp.where(low, x, swapped).astype(BF16), jnp.where(low, swapped, x).astype(BF16)


def _dup_groups_t(xt):
    g0, g1 = xt[0:D_B, :], xt[D_B:, :]
    return jnp.concatenate([g0, g0], axis=0).astype(BF16), jnp.concatenate([g1, g1], axis=0).astype(BF16)


def _swa_heads(sink_ref, i_ab, q_ref, kv_pieces, o_ref, n_rows):
    lane = lax.broadcasted_iota(jnp.int32, (n_rows, LANES), 1)
    head_pieces = []
    for h in range(H_B):
        j, hh = divmod(h, 2)
        g = h // G_B
        qm = q_ref[:, j * LANES:(j + 1) * LANES] * _ones_where((lane // D_B) == hh)
        pieces = []
        for keys, vals, transposed, mask in kv_pieces:
            s = _dot(qm, keys[g]) if transposed else _dot_nt(qm, keys[g])
            if mask is not None:
                s = jnp.where(mask, s, NEG_BIG)
            pieces.append((s, vals[g], transposed))
        head_pieces.append(pieces)
    outs = [_swa_head_softmax(pieces, sink_ref[i_ab, h]) for h, pieces in enumerate(head_pieces)]
    for j in range(H_B // 2):
        o_ref[:, j * LANES:(j + 1) * LANES] = jnp.where(lane < D_B, outs[2 * j], outs[2 * j + 1]).astype(BF16)


def _swa_prompt_kernel(i_ab, sink_ref, q_ref, kt_ref, vt_ref, o_ref):
    _swa_heads(sink_ref, i_ab, q_ref, [(_dup_groups_t(kt_ref[...]), _dup_groups_t(vt_ref[...]), True, None)],
               o_ref, SEQ)


def _swa_prompt(sink, i_ab, qb, kbt, vbt):
    seq = lambda b: (b, 0)
    seq_t = pl.BlockSpec((None, KB_W, SEQ), lambda b: (b, 0, 0))
    return pl.pallas_call(
        functools.partial(_swa_prompt_kernel, i_ab),
        grid=(BATCH,),
        in_specs=[pl.BlockSpec(memory_space=pltpu.SMEM), pl.BlockSpec((SEQ, QB_W), seq), seq_t, seq_t],
        out_specs=pl.BlockSpec((SEQ, QB_W), seq),
        out_shape=jax.ShapeDtypeStruct((N_PROMPT, QB_W), BF16),
        compiler_params=_cparams(("parallel",)),
        name="swa_p",
    )(sink, qb, kbt, vbt)


def _swa_sample_kernel(i_ab, sink_ref, q_ref, k_ref, v_ref, kct_ref, vct_ref, o_ref):
    n = pl.program_id(1)
    start = pl.multiple_of(jnp.clip((n - 1) * SWA_QB, 0, DEC_SEQ - SWA_WIN), SWA_QB)
    local = (_dup_groups(k_ref[pl.ds(start, SWA_WIN), :]), _dup_groups(v_ref[pl.ds(start, SWA_WIN), :]))
    ctx = (_dup_groups_t(kct_ref[...]), _dup_groups_t(vct_ref[...]))
    qi = n * SWA_QB + lax.broadcasted_iota(jnp.int32, (SWA_QB, SWA_WIN), 0)
    ki = start + lax.broadcasted_iota(jnp.int32, (SWA_QB, SWA_WIN), 1)
    band = jnp.abs(qi - ki) <= WINDOW
    _swa_heads(sink_ref, i_ab, q_ref, [ctx + (True, None), local + (False, band)], o_ref, SWA_QB)


def _swa_sample(sink, i_ab, qb, kb, vb, kc, vc):
    nqb = DEC_SEQ // SWA_QB
    q0 = N_PROMPT // SWA_QB
    return pl.pallas_call(
        functools.partial(_swa_sample_kernel, i_ab),
        grid=(DEC_BATCH, nqb),
        in_specs=[
            pl.BlockSpec(memory_space=pltpu.SMEM),
            pl.BlockSpec((SWA_QB, QB_W), lambda b, n: (q0 + b * nqb + n, 0)),
            pl.BlockSpec((DEC_SEQ, KB_W), lambda b, n: (b, 0)),
            pl.BlockSpec((DEC_SEQ, KB_W), lambda b, n: (b, 0)),
            pl.BlockSpec((None, None, KB_W, PAST_LEN), lambda b, n: (b, i_ab, 0, 0)),
            pl.BlockSpec((None, None, KB_W, PAST_LEN), lambda b, n: (b, i_ab, 0, 0)),
        ],
        out_specs=pl.BlockSpec((SWA_QB, QB_W), lambda b, n: (b * nqb + n, 0)),
        out_shape=jax.ShapeDtypeStruct((N_SAMPLE, QB_W), BF16),
        compiler_params=_cparams(("parallel", "parallel")),
        name="swa_s",
    )(sink, qb, kb, vb, kc, vc)


def _in_odd_kernel(x_ref, mod_ref, g_ref, wd_ref, gq_ref, gkv_ref, wuq_ref, wuk_ref, wuvt_ref, cos_ref, sin_ref,
                   q_ref, k_ref, vt_ref, ckv_ref, krt_ref):
    h = _rms(x_ref[...], g_ref[0:1, :]) * (1.0 + mod_ref[1:2, :]) + mod_ref[0:1, :]
    hb = h.astype(BF16)
    cos = cos_ref[...]
    sin = sin_ref[...]
    c_q = _dot(hb, wd_ref[:, 0:Q_LORA])
    c_kv = _rms(_dot(hb, wd_ref[:, Q_LORA:Q_LORA + KV_LORA]), gkv_ref[...])
    kr = _rope_c(_dot(hb, wd_ref[:, Q_LORA + KV_LORA:DOWN_W]), cos, sin)
    cqb = _rms(c_q, gq_ref[...]).astype(BF16)
    ckvb = c_kv.astype(BF16)
    scale = (NOPE_C + ROPE_C) ** -0.5
    group_w = HEADS_PER_DOT_C * HEAD_PAD_C
    for grp in range(H_C // HEADS_PER_DOT_C):
        gsl = slice(grp * group_w, (grp + 1) * group_w)
        qg = _dot(cqb, wuq_ref[:, gsl])
        kg = _dot(ckvb, wuk_ref[:, gsl])
        for j in range(HEADS_PER_DOT_C):
            sl = slice(j * HEAD_PAD_C, (j + 1) * HEAD_PAD_C)
            osl = slice(grp * group_w + j * HEAD_PAD_C, grp * group_w + (j + 1) * HEAD_PAD_C)
            q_ref[:, osl] = (_rope_c(qg[:, sl], cos, sin) * scale).astype(BF16)
            k_ref[:, osl] = (kg[:, sl] + kr).astype(BF16)
    vt_ref[...] = _dot_nt(wuvt_ref[...], ckvb).astype(BF16)

    @pl.when(_is_prompt_tile(TM_IN))
    def _():
        ckv_ref[...] = c_kv
        q4 = ROPE_C // 4
        half = LANES // 2
        for s in range(TM_IN // SEQ):
            t = kr[s * SEQ:(s + 1) * SEQ, :].T
            krt_ref[s] = jnp.concatenate([t[0:q4], t[half:half + q4], t[q4:2 * q4], t[half + q4:half + 2 * q4]], axis=0)


def _in_odd(x, layer, mods, g_norm, w_down, g_q, g_kv, w_uq, w_uk, w_uvt, cos, sin):
    tm = TM_IN
    i_c = layer // 2
    n_prompt_tiles = N_PROMPT // tm
    row = lambda i: (i, 0)
    prompt_row = lambda i: (jnp.minimum(i, n_prompt_tiles - 1), 0)
    prompt_seq = lambda i: (jnp.minimum(i, n_prompt_tiles - 1), 0, 0)
    return pl.pallas_call(
        _in_odd_kernel,
        grid=(N_TOK // tm,),
        in_specs=[
            pl.BlockSpec((tm, D_MODEL), row),
            _mod_spec(tm),
            _layer_spec((4, D_MODEL), layer),
            _layer_spec((D_MODEL, DOWN_W), i_c),
            _layer_spec((1, Q_LORA), i_c),
            _layer_spec((1, KV_LORA), i_c),
            _layer_spec((Q_LORA, QC_W), i_c),
            _layer_spec((KV_LORA, QC_W), i_c),
            _layer_spec((VC_W, KV_LORA), i_c),
            pl.BlockSpec((tm, LANES), row),
            pl.BlockSpec((tm, LANES), row),
        ],
        out_specs=[pl.BlockSpec((tm, QC_W), row), pl.BlockSpec((tm, QC_W), row),
                   pl.BlockSpec((VC_W, tm), lambda i: (0, i)),
                   pl.BlockSpec((tm, KV_LORA), prompt_row), pl.BlockSpec((tm // SEQ, ROPE_C, SEQ), prompt_seq)],
        out_shape=[jax.ShapeDtypeStruct((N_TOK, QC_W), BF16), jax.ShapeDtypeStruct((N_TOK, QC_W), BF16),
                   jax.ShapeDtypeStruct((VC_W, N_TOK), BF16),
                   jax.ShapeDtypeStruct((N_PROMPT, KV_LORA), F32), jax.ShapeDtypeStruct((BATCH, ROPE_C, SEQ), F32)],
        compiler_params=_cparams(("arbitrary",)),
        name="in_odd",
    )(x, mods, g_norm, w_down, g_q, g_kv, w_uq, w_uk, w_uvt, cos, sin)


def _reduce_rows(x, op, reduce_fn):
    while x.shape[0] % (2 * SUBLANES) == 0:
        half = x.shape[0] // 2
        x = op(x[:half], x[half:])
    return reduce_fn(x, axis=0, keepdims=True)


def _mla_heads(q_ref, kv_pieces, o_ref, ot_ref, group):
    for h0 in range(0, H_C, group):
        heads = range(h0, h0 + group)
        scores = []
        for hd in heads:
            sl = slice(hd * HEAD_PAD_C, (hd + 1) * HEAD_PAD_C)
            scores.append([_dot_nt(k_ref[:, sl], q_ref[:, sl]) for k_ref, _ in kv_pieces])
        exps, dens = [], []
        for per_piece in scores:
            m = None
            for s in per_piece:
                sm = _reduce_rows(s, jnp.maximum, jnp.max)
                m = sm if m is None else jnp.maximum(m, sm)
            es = [jnp.exp(s - m) for s in per_piece]
            den = None
            for e in es:
                part = _reduce_rows(e, jnp.add, jnp.sum)
                den = part if den is None else den + part
            exps.append([e.astype(BF16) for e in es])
            dens.append(den)
        for hd, es, den in zip(heads, exps, dens):
            acc = None
            for e, (_, vt_ref) in zip(es, kv_pieces):
                pv = _dot(vt_ref[hd * V_C:(hd + 1) * V_C, :], e)
                acc = pv if acc is None else acc + pv
            ot_ref[hd * V_C:(hd + 1) * V_C, :] = acc / den
    o_ref[...] = ot_ref[...].T.astype(BF16)


def _mla_prompt_kernel(q_ref, k_ref, vt_ref, o_ref, ot_ref):
    _mla_heads(q_ref, [(k_ref, vt_ref)], o_ref, ot_ref, MLA_HEAD_GROUP_P)


def _mla_prompt(q, k, vt):
    seq = lambda b: (b, 0)
    return pl.pallas_call(
        _mla_prompt_kernel,
        grid=(BATCH,),
        in_specs=[pl.BlockSpec((SEQ, QC_W), seq), pl.BlockSpec((SEQ, QC_W), seq),
                  pl.BlockSpec((VC_W, SEQ), lambda b: (0, b))],
        out_specs=pl.BlockSpec((SEQ, VC_W), seq),
        out_shape=jax.ShapeDtypeStruct((N_PROMPT, VC_W), BF16),
        scratch_shapes=[pltpu.VMEM((VC_W, SEQ), F32)],
        compiler_params=_cparams(("parallel",)),
        name="mla_p",
    )(q, k, vt)


def _mla_sample_kernel(q_ref, k_ref, vt_ref, ckv_ref, krt_ref, wuk_ref, wuvt_ref, o_ref, kc_ref, vct_ref, ot_ref):
    @pl.when(pl.program_id(1) == 0)
    def _():
        cb = ckv_ref[...].astype(BF16)
        krt = krt_ref[...]
        q4 = ROPE_C // 4
        kr = jnp.concatenate(
            [krt[0:q4], krt[2 * q4:3 * q4], jnp.zeros((NOPE_LO_C, PAST_LEN), F32),
             krt[q4:2 * q4], krt[3 * q4:4 * q4], jnp.zeros((LANES // 2 - ROPE_HALF_C, PAST_LEN), F32)], axis=0).T
        group_w = HEADS_PER_DOT_C * HEAD_PAD_C
        for grp in range(H_C // HEADS_PER_DOT_C):
            kg = _dot(cb, wuk_ref[:, grp * group_w:(grp + 1) * group_w])
            for j in range(HEADS_PER_DOT_C):
                osl = slice(grp * group_w + j * HEAD_PAD_C, grp * group_w + (j + 1) * HEAD_PAD_C)
                kc_ref[:, osl] = (kg[:, j * HEAD_PAD_C:(j + 1) * HEAD_PAD_C] + kr).astype(BF16)
        vct_ref[...] = _dot_nt(wuvt_ref[...], cb).astype(BF16)

    _mla_heads(q_ref, [(kc_ref, vct_ref), (k_ref, vt_ref)], o_ref, ot_ref, MLA_HEAD_GROUP_S)


def _mla_sample(q, k, vt, ckv_ctx, kr_ctx, w_uk, w_uvt, i_c):
    nqb = DEC_SEQ // MLA_QB
    q0 = N_PROMPT // MLA_QB
    s0 = N_PROMPT // DEC_SEQ
    return pl.pallas_call(
        _mla_sample_kernel,
        grid=(DEC_BATCH, nqb),
        in_specs=[
            pl.BlockSpec((MLA_QB, QC_W), lambda b, n: (q0 + b * nqb + n, 0)),
            pl.BlockSpec((DEC_SEQ, QC_W), lambda b, n: (s0 + b, 0)),
            pl.BlockSpec((VC_W, DEC_SEQ), lambda b, n: (0, s0 + b)),
            pl.BlockSpec((None, None, PAST_LEN, KV_LORA), lambda b, n: (b, i_c, 0, 0)),
            pl.BlockSpec((None, None, ROPE_C, PAST_LEN), lambda b, n: (b, i_c, 0, 0)),
            _layer_spec((KV_LORA, QC_W), i_c),
            _layer_spec((VC_W, KV_LORA), i_c),
        ],
        out_specs=pl.BlockSpec((MLA_QB, VC_W), lambda b, n: (b * nqb + n, 0)),
        out_shape=jax.ShapeDtypeStruct((N_SAMPLE, VC_W), BF16),
        scratch_shapes=[pltpu.VMEM((PAST_LEN, QC_W), BF16), pltpu.VMEM((VC_W, PAST_LEN), BF16),
                        pltpu.VMEM((VC_W, MLA_QB), F32)],
        compiler_params=_cparams(("parallel", "arbitrary")),
        name="mla_s",
    )(q, k, vt, ckv_ctx, kr_ctx, w_uk, w_uvt)


def _out_kernel(n_x, split_out, *refs):
    it = iter(refs[n_x:])
    mod_ref, g_ref = next(it), next(it)
    a_refs = (next(it), next(it))
    b_refs = (next(it), next(it))
    wo_ref, w1_ref, w2_ref = next(it), next(it), next(it)
    if split_out:
        next_mod = None
        out_refs = (next(it), next(it))
        h2_ref, acc_ref, x1_ref = next(it), next(it), next(it)
        y_ref = acc_ref
    else:
        next_mod = (next(it), next(it), next(it))
        x1_ref = y_ref = next(it)
        modn_ref = next(it)
        h2_ref, acc_ref = next(it), next(it)

    def emit_next_mod():
        if next_mod is not None:
            modn_ref[...] = _mod_block(*next_mod)

    kk = pl.program_id(1)
    half = wo_ref.shape[0] // 2

    row_chunks = [slice(r * OUT_PROLOGUE_ROWS, (r + 1) * OUT_PROLOGUE_ROWS) for r in range(TM_OUT // OUT_PROLOGUE_ROWS)]

    @pl.when(kk == 0)
    def _():
        wo_a = wo_ref[0:half, :].astype(BF16)
        wo_b = wo_ref[half:, :].astype(BF16)
        gate_g1 = mod_ref[2:3, :] * g_ref[1:2, :]
        scale_g2 = g_ref[2:3, :] * (1.0 + mod_ref[4:5, :])
        for rows in row_chunks:
            a = _load_split(a_refs, TM_OUT, rows)
            b = _load_split(b_refs, TM_OUT, rows)
            mix = _dot(a, wo_a) + _dot(b, wo_b)
            x1 = _load_x(n_x, refs, TM_OUT, rows) + _rms(mix, gate_g1)
            x1_ref[rows, :] = x1
            h2_ref[rows, :] = (_rms(x1, scale_g2) + mod_ref[3:4, :]).astype(BF16)
        acc_ref[...] = jnp.zeros(acc_ref.shape, F32)

    def ffn(rows, w1, w2):
        hid = jnp.maximum(_dot(h2_ref[rows, :], w1), 0.0)
        return _dot((hid * hid).astype(BF16), w2)

    is_last = kk == pl.num_programs(1) - 1

    @pl.when(jnp.logical_not(is_last))
    def _():
        acc_ref[...] += ffn(slice(None), w1_ref[...].astype(BF16), w2_ref[...].astype(BF16))
        emit_next_mod()

    @pl.when(is_last)
    def _():
        w1 = w1_ref[...].astype(BF16)
        w2 = w2_ref[...].astype(BF16)
        gate_g3 = mod_ref[5:6, :] * g_ref[3:4, :]
        for rows in row_chunks:
            y_ref[rows, :] = x1_ref[rows, :] + _rms(acc_ref[rows, :] + ffn(rows, w1, w2), gate_g3)
        emit_next_mod()

    if split_out:
        @pl.when(is_last)
        def _():
            _store_split(out_refs, TM_OUT, acc_ref[...])


def _out_layer(xs, layer, mods, g_norm, mix_a, mix_b, b_col, w_o, i_o, w_ff1, w_ff2, next_mod_args):
    tm = TM_OUT
    half = D_MODEL // 2
    n_k = D_FF // TK_FF
    split_out = next_mod_args is None
    if split_out:
        out_specs, out_shape = _split_specs(tm, D_MODEL), _split_shapes(D_MODEL, F32)
        next_specs, next_args = [], []
    else:
        mod_cols = 6 * D_MODEL // (N_TOK // tm * n_k)
        step = lambda i, k: i * n_k + k
        out_specs = [pl.BlockSpec((tm, D_MODEL), lambda i, k: (i, 0)),
                     pl.BlockSpec((N_MOD_ROWS, mod_cols), lambda i, k: (0, step(i, k)))]
        out_shape = [jax.ShapeDtypeStruct((N_TOK, D_MODEL), F32),
                     jax.ShapeDtypeStruct((N_MOD_ROWS, 6 * D_MODEL), F32)]
        next_specs = [pl.BlockSpec((N_MOD_ROWS, D_MODEL), lambda i, k: (0, 0)),
                      pl.BlockSpec((None, D_MODEL, mod_cols), lambda i, k: (layer + 1, 0, step(i, k))),
                      pl.BlockSpec((None, 1, mod_cols), lambda i, k: (layer + 1, 0, step(i, k)))]
        next_args = list(next_mod_args)
    return pl.pallas_call(
        functools.partial(_out_kernel, len(xs), split_out),
        grid=(N_TOK // tm, n_k),
        in_specs=_x_specs(len(xs), tm) + [
            _mod_spec(tm),
            _layer_spec((4, D_MODEL), layer),
        ] + _split_specs(tm, half, 0) + _split_specs(tm, half, b_col) + [
            _layer_spec((D_MODEL, D_MODEL), i_o),
            pl.BlockSpec((None, D_MODEL, TK_FF), lambda i, k: (layer, 0, k)),
            pl.BlockSpec((None, TK_FF, D_MODEL), lambda i, k: (layer, k, 0)),
        ] + next_specs,
        out_specs=out_specs,
        out_shape=out_shape,
        scratch_shapes=[pltpu.VMEM((tm, D_MODEL), BF16), pltpu.VMEM((tm, D_MODEL), F32)] + (
            [pltpu.VMEM((tm, D_MODEL), F32)] if split_out else []),
        compiler_params=_cparams(("arbitrary", "arbitrary")),
        name="out_mlp",
    )(*xs, mods, g_norm, *mix_a, *mix_b, w_o, w_ff1, w_ff2, *next_args)


def _rope_angles(head_dim):
    nf = head_dim // 4
    n_rows = DEC_SEQ // GRID_W
    rows = jnp.repeat(jnp.arange(n_rows, dtype=F32), GRID_W)
    cols = jnp.tile(jnp.arange(GRID_W, dtype=F32), n_rows)
    inv = ROPE_BASE ** (-jnp.arange(nf, dtype=F32) / nf)
    return jnp.stack([rows[:, None] * inv, cols[:, None] * inv], axis=1)


def _token_tables(cos_g, sin_g):
    cos_t = jnp.concatenate([jnp.ones((N_PROMPT, LANES), F32), jnp.tile(cos_g, (DEC_BATCH, 1))], axis=0)
    sin_t = jnp.concatenate([jnp.zeros((N_PROMPT, LANES), F32), jnp.tile(sin_g, (DEC_BATCH, 1))], axis=0)
    return cos_t, sin_t


def _rope_tables_b():
    nf = D_B // 4
    ang = _rope_angles(D_B)
    cos = jnp.broadcast_to(jnp.cos(ang)[:, :, None, :], (DEC_SEQ, 2, 2, nf)).reshape(DEC_SEQ, D_B)
    sin = jnp.sin(ang)
    sin = jnp.stack([-sin, sin], axis=2).reshape(DEC_SEQ, D_B)
    reps = LANES // D_B
    return _token_tables(jnp.tile(cos, (1, reps)), jnp.tile(sin, (1, reps)))


def _rope_tables_c():
    ang = _rope_angles(ROPE_C).reshape(DEC_SEQ, ROPE_HALF_C)
    cos, sin = jnp.cos(ang), jnp.sin(ang)
    half = LANES // 2
    cos_g = jnp.ones((DEC_SEQ, LANES), F32).at[:, 0:ROPE_HALF_C].set(cos).at[:, half:half + ROPE_HALF_C].set(cos)
    sin_g = jnp.zeros((DEC_SEQ, LANES), F32).at[:, 0:ROPE_HALF_C].set(-sin).at[:, half:half + ROPE_HALF_C].set(sin)
    return _token_tables(cos_g, sin_g)


def kernel(x_prompt, x_sample, state_gla_fwd, state_gla_bwd, cache_swa_k, cache_swa_v, cache_mla_ckv, cache_mla_kr, c, c_ctx, w_mod, b_mod, g_norm, w_ff1, w_ff2, w_in_ab, w_gk_f, b_gk_f, w_gk_b, b_gk_b, g_gla, swa_sink, w_out_ab, w_mla_down, g_mla_q, g_mla_kv, w_mla_uq, w_mla_ukv, w_mla_o):
    xs = (x_prompt.reshape(N_PROMPT, D_MODEL), x_sample.reshape(N_SAMPLE, D_MODEL))
    cvecs = jnp.concatenate([c_ctx[None, :], c, jnp.zeros((N_MOD_ROWS - 1 - DEC_BATCH, D_MODEL), F32)], axis=0)
    b_mod3 = b_mod.reshape(DEPTH, 1, 6 * D_MODEL)
    mods = _modulation(cvecs, w_mod, b_mod3, 0).reshape(N_MOD_ROWS, 6, D_MODEL)

    cos_b, sin_b = _rope_tables_b()
    cos_c, sin_c = _rope_tables_c()
    p_blk = N_PROMPT // DEC_SEQ

    n_ab = w_in_ab.shape[0]
    w_in = _prep_even(jnp.swapaxes(w_in_ab, 1, 2))
    zgk = jnp.zeros((n_ab, GK_RANK, QA_W), F32)
    w_gk = jnp.concatenate([jnp.concatenate([w_gk_f, zgk], axis=2),
                            jnp.concatenate([zgk, w_gk_b], axis=2)], axis=1).astype(BF16)
    b_gk = jnp.concatenate([b_gk_f, b_gk_b], axis=1)[:, None, :]
    gg = g_gla[:, None, :]
    kc = jnp.transpose(cache_swa_k, (0, 1, 3, 4, 2)).reshape(DEC_BATCH, n_ab, KB_W, PAST_LEN)
    vc = jnp.transpose(cache_swa_v, (0, 1, 3, 4, 2)).reshape(DEC_BATCH, n_ab, KB_W, PAST_LEN)
    w_down, w_uq, w_uk, w_uvt = _prep_odd(jnp.swapaxes(w_mla_down, 1, 2), w_mla_uq, w_mla_ukv)
    g_q = g_mla_q[:, None, :]
    g_kv = g_mla_kv[:, None, :]
    kr_ctx = jnp.swapaxes(cache_mla_kr, 2, 3)

    def gla_layer(i_ab, exact):
        def run(qa, ka, va, ga, ld):
            o_p, stf, stb = _gla(SEQ, BATCH, 0, qa, ka, va, ga, ld, gg, i_ab, exact=exact)
            o_s, _, _ = _gla(DEC_SEQ, DEC_BATCH, p_blk, qa, ka, va, ga, ld, gg, i_ab,
                             (state_gla_fwd, state_gla_bwd), exact=exact)
            return o_p, o_s, stf, stb
        return run

    st_f, st_b, sk, sv, ckv_out, ckr_out = [], [], [], [], [], []
    for l in range(DEPTH):
        i = l // 2
        next_mod_args = None if l == DEPTH - 1 else (cvecs, w_mod, b_mod3)
        if l % 2 == 0:
            qa, ka, va, ga, ld, qb, ld_min, kbt, vbt, kb_s, vb_s = _in_even(
                xs, l, mods, g_norm, w_in, w_gk, b_gk, cos_b, sin_b)
            factorisable = jnp.min(ld_min) * GLA_CHUNK >= -GLA_SAFE_TOTAL
            o_gla_p, o_gla_s, stf, stb = lax.cond(factorisable, gla_layer(i, False), gla_layer(i, True),
                                                  qa, ka, va, ga, ld)
            o_swa_p = _swa_prompt(swa_sink, i, qb, kbt, vbt)
            o_swa_s = _swa_sample(swa_sink, i, qb, kb_s, vb_s, kc, vc)
            outs = _out_layer(xs, l, mods, g_norm, (o_gla_p, o_gla_s), (o_swa_p, o_swa_s), 0, w_out_ab, i,
                              w_ff1, w_ff2, next_mod_args)
            st_f.append(stf)
            st_b.append(stb)
            sk.append(kbt)
            sv.append(vbt)
        else:
            q, k, vt, ckv, krt = _in_odd(xs[0], l, mods, g_norm, w_down, g_q, g_kv, w_uq, w_uk, w_uvt, cos_c, sin_c)
            o_mla = (_mla_prompt(q, k, vt), _mla_sample(q, k, vt, cache_mla_ckv, kr_ctx, w_uk, w_uvt, i))
            outs = _out_layer(xs, l, mods, g_norm, o_mla, o_mla, 1, w_mla_o, i, w_ff1, w_ff2, next_mod_args)
            ckv_out.append(ckv.reshape(BATCH, SEQ, KV_LORA))
            ckr_out.append(krt)
        if next_mod_args is None:
            xs = outs
        else:
            xs, mods = (outs[0],), outs[1].reshape(N_MOD_ROWS, 6, D_MODEL)

    y_prompt = xs[0].reshape(BATCH, SEQ, D_MODEL)
    y_sample = xs[1].reshape(DEC_BATCH, DEC_SEQ, D_MODEL)
    swa_cache = lambda parts: jnp.transpose(
        jnp.stack(parts, axis=1).reshape(BATCH, len(parts), KV_B, D_B, SEQ), (0, 1, 4, 2, 3))
    return (y_prompt, y_sample, jnp.stack(st_f, axis=1), jnp.stack(st_b, axis=1), swa_cache(sk), swa_cache(sv),
            jnp.stack(ckv_out, axis=1), jnp.swapaxes(jnp.stack(ckr_out, axis=1), 2, 3))
```

```python
import functools

import jax
import jax.numpy as jnp
from jax import lax
from jax.experimental import pallas as pl
from jax.experimental.pallas import tpu as pltpu

F32 = jnp.float32
BF16 = jnp.bfloat16

D_MODEL = 1024
BATCH = 16
SEQ = 256
DEPTH = 4
DEC_BATCH = 2
DEC_SEQ = 1024
PAST_LEN = 512
GRID_W = 64
D_FF = 4 * D_MODEL
EPS = 1e-6
ROPE_BASE = 10000.0
H_A = 4
DK_A = 64
DV_A = 128
GK_RANK = 16
GATE_NORM = 16.0
GLA_CHUNK = 64
H_B = 8
KV_B = 2
G_B = H_B // KV_B
D_B = 64
WINDOW = 128
H_C = 16
NOPE_C = 64
ROPE_C = 32
V_C = 64
Q_LORA = 384
KV_LORA = 256

N_PROMPT = BATCH * SEQ
N_SAMPLE = DEC_BATCH * DEC_SEQ
N_TOK = N_PROMPT + N_SAMPLE
SUBLANES = 8
N_MOD_ROWS = SUBLANES
QA_W = H_A * DK_A
VA_W = H_A * DV_A
QB_W = H_B * D_B
KB_W = KV_B * D_B
LANES = 128
HEAD_PAD_C = 128
QC_W = H_C * HEAD_PAD_C
VC_W = H_C * V_C
DOWN_RAW_W = Q_LORA + KV_LORA + ROPE_C
DOWN_W = Q_LORA + KV_LORA + LANES
ROPE_HALF_C = ROPE_C // 2
NOPE_LO_C = LANES // 2 - ROPE_HALF_C

TM_IN = 512
TM_OUT = 1024
TK_FF = 512
OUT_PROLOGUE_ROWS = 256
TN_MOD = 1024
PREP_ROWS = 256
GLA_GROUP = 256
GLA_SCAN_UNROLL = 8
GLA_GROUP_UNROLL = 2
GLA_SAFE_TOTAL = 160.0
SWA_QB = 128
SWA_WIN = 3 * SWA_QB
MLA_QB = 256
HEADS_PER_DOT_C = 4
MLA_HEAD_GROUP_P = 16
MLA_HEAD_GROUP_S = 4
VMEM_LIMIT = 60 * 1024 * 1024
NEG_BIG = -1e30


def _cparams(sem):
    return pltpu.CompilerParams(dimension_semantics=sem, vmem_limit_bytes=VMEM_LIMIT)


def _dot(a, b):
    return jnp.dot(a, b, preferred_element_type=F32)


def _dot_nt(a, b):
    return lax.dot_general(a, b, (((1,), (1,)), ((), ())), preferred_element_type=F32)


def _dot_tn(a, b):
    return lax.dot_general(a, b, (((0,), (0,)), ((), ())), preferred_element_type=F32)


def _rms(x, g):
    return x * lax.rsqrt(jnp.mean(x * x, axis=-1, keepdims=True) + EPS) * g


def _silu(x):
    return x / (1.0 + jnp.exp(-x))


def _rope(x, cos, sin, half):
    lane = lax.broadcasted_iota(jnp.int32, x.shape, 1)
    first = (lane % (2 * half)) < half
    partner = jnp.where(first, pltpu.roll(x, LANES - half, 1), pltpu.roll(x, half, 1))
    return x * cos + partner * sin


def _rope_c(x, cos, sin):
    return x * cos + pltpu.roll(x, LANES // 2, 1) * sin


def _rope_split_c(r):
    q = ROPE_C // 4
    first = jnp.concatenate([r[..., 0:q], r[..., 2 * q:3 * q]], axis=-1)
    second = jnp.concatenate([r[..., q:2 * q], r[..., 3 * q:4 * q]], axis=-1)
    return first, second


def _split3(x):
    x1 = x.astype(BF16)
    r1 = x - x1.astype(F32)
    x2 = r1.astype(BF16)
    x3 = (r1 - x2.astype(F32)).astype(BF16)
    return x1, x2, x3


def _dot3(t, parts):
    return _dot(t, parts[2]) + _dot(t, parts[1]) + _dot(t, parts[0])


def _ones_where(cond):
    return jnp.where(cond, 1.0, 0.0).astype(BF16)


def _mod_row(tile, tm):
    n_prompt_tiles = N_PROMPT // tm
    tiles_per_seq = DEC_SEQ // tm
    return jnp.where(tile < n_prompt_tiles, 0, 1 + (tile - n_prompt_tiles) // tiles_per_seq)


def _layer_spec(shape, idx):
    return pl.BlockSpec((None,) + shape, lambda *_: (idx,) + (0,) * len(shape))


def _mod_spec(tm):
    return pl.BlockSpec((None, 6, D_MODEL), lambda i, *_: (_mod_row(i, tm), 0, 0))


def _split_specs(tm, width, col=0):
    n_prompt_tiles = N_PROMPT // tm
    return [pl.BlockSpec((tm, width), lambda i, *_: (jnp.minimum(i, n_prompt_tiles - 1), col)),
            pl.BlockSpec((tm, width), lambda i, *_: (jnp.maximum(i - n_prompt_tiles, 0), col))]


def _split_shapes(width, dtype):
    return [jax.ShapeDtypeStruct((N_PROMPT, width), dtype), jax.ShapeDtypeStruct((N_SAMPLE, width), dtype)]


def _is_prompt_tile(tm):
    return pl.program_id(0) < N_PROMPT // tm


def _load_split(pair, tm, rows=slice(None)):
    return jnp.where(_is_prompt_tile(tm), pair[0][rows, :], pair[1][rows, :])


def _store_split(pair, tm, value, rows=slice(None)):
    is_prompt = _is_prompt_tile(tm)

    @pl.when(is_prompt)
    def _():
        pair[0][rows, :] = value

    @pl.when(jnp.logical_not(is_prompt))
    def _():
        pair[1][rows, :] = value


_C_QA, _C_KA, _C_VA, _C_GA = 0, QA_W, 2 * QA_W, 2 * QA_W + VA_W
_C_QB = _C_GA + VA_W
_C_KB = _C_QB + QB_W
_C_VB = _C_KB + KB_W
_C_LO = _C_VB + KB_W
AB_IN = _C_LO + 2 * GK_RANK


def _prep_even_kernel(wt_ref, o_ref):
    raw_lo = _C_QB
    raw_qb = raw_lo + 2 * GK_RANK
    o_ref[:, 0:_C_QB] = wt_ref[0:raw_lo, :].T.astype(BF16)
    o_ref[:, _C_QB:_C_LO] = wt_ref[raw_qb:AB_IN, :].T.astype(BF16)
    o_ref[:, _C_LO:AB_IN] = wt_ref[raw_lo:raw_lo + LANES, :].T[:, 0:2 * GK_RANK].astype(BF16)


def _prep_even(w_in_t):
    n = w_in_t.shape[0]
    return pl.pallas_call(
        _prep_even_kernel,
        grid=(n, D_MODEL // PREP_ROWS),
        in_specs=[pl.BlockSpec((None, AB_IN, PREP_ROWS), lambda l, r: (l, 0, r))],
        out_specs=pl.BlockSpec((None, PREP_ROWS, AB_IN), lambda l, r: (l, r, 0)),
        out_shape=jax.ShapeDtypeStruct((n, D_MODEL, AB_IN), BF16),
        compiler_params=_cparams(("parallel", "parallel")),
        name="prep_even",
    )(w_in_t)


def _prep_odd_kernel(wdt_ref, wuq_ref, wukv_ref, od_ref, ouq_ref, ouk_ref, ouvt_ref):
    def head_group(nope, rope_first, rope_second):
        rows = nope.shape[0]
        zero_half = jnp.zeros((rows, ROPE_HALF_C), F32)
        return jnp.concatenate(
            [zero_half if rope_first is None else rope_first, nope[:, 0:NOPE_LO_C],
             zero_half if rope_second is None else rope_second, nope[:, NOPE_LO_C:NOPE_C],
             jnp.zeros((rows, HEAD_PAD_C - NOPE_C - ROPE_C), F32)], axis=1).astype(BF16)

    n_ckv = Q_LORA + KV_LORA
    od_ref[:, 0:n_ckv] = wdt_ref[0:n_ckv, :].T.astype(BF16)
    tail = wdt_ref[DOWN_RAW_W - LANES:DOWN_RAW_W, :].T
    od_ref[:, n_ckv:DOWN_W] = head_group(jnp.zeros((D_MODEL, NOPE_C), F32), *_rope_split_c(tail[:, LANES - ROPE_C:]))
    hd_q = NOPE_C + ROPE_C
    for h in range(H_C):
        wq = wuq_ref[:, h * hd_q:(h + 1) * hd_q]
        ouq_ref[:, h * HEAD_PAD_C:(h + 1) * HEAD_PAD_C] = head_group(wq[:, 0:NOPE_C], *_rope_split_c(wq[:, NOPE_C:]))
    wukv = wukv_ref[...]
    for h in range(H_C):
        ouk_ref[:, h * HEAD_PAD_C:(h + 1) * HEAD_PAD_C] = head_group(
            wukv[:, h * HEAD_PAD_C:h * HEAD_PAD_C + NOPE_C], None, None)
    wukv_t = wukv.T
    for h in range(H_C):
        ouvt_ref[h * V_C:(h + 1) * V_C, :] = wukv_t[h * HEAD_PAD_C + NOPE_C:(h + 1) * HEAD_PAD_C, :].astype(BF16)


def _prep_odd(w_down_t, w_uq, w_ukv):
    n = w_down_t.shape[0]
    spec = lambda r, c: pl.BlockSpec((None, r, c), lambda l: (l, 0, 0))
    return pl.pallas_call(
        _prep_odd_kernel,
        grid=(n,),
        in_specs=[spec(DOWN_RAW_W, D_MODEL), spec(Q_LORA, H_C * (NOPE_C + ROPE_C)), spec(KV_LORA, QC_W)],
        out_specs=[spec(D_MODEL, DOWN_W), spec(Q_LORA, QC_W), spec(KV_LORA, QC_W), spec(VC_W, KV_LORA)],
        out_shape=[jax.ShapeDtypeStruct((n, D_MODEL, DOWN_W), BF16), jax.ShapeDtypeStruct((n, Q_LORA, QC_W), BF16),
                   jax.ShapeDtypeStruct((n, KV_LORA, QC_W), BF16), jax.ShapeDtypeStruct((n, VC_W, KV_LORA), BF16)],
        compiler_params=_cparams(("parallel",)),
        name="prep_odd",
    )(w_down_t, w_uq, w_ukv)


def _mod_block(c_ref, w_ref, b_ref):
    s = _silu(c_ref[...])
    return _dot(s.astype(BF16), w_ref[...].astype(BF16)) + b_ref[...]


def _mod_kernel(c_ref, w_ref, b_ref, o_ref):
    o_ref[...] = _mod_block(c_ref, w_ref, b_ref)


def _modulation(cvecs, w_mod, b_mod, layer):
    return pl.pallas_call(
        _mod_kernel,
        grid=(6 * D_MODEL // TN_MOD,),
        in_specs=[
            pl.BlockSpec((N_MOD_ROWS, D_MODEL), lambda j: (0, 0)),
            pl.BlockSpec((None, D_MODEL, TN_MOD), lambda j: (layer, 0, j)),
            pl.BlockSpec((None, 1, TN_MOD), lambda j: (layer, 0, j)),
        ],
        out_specs=pl.BlockSpec((N_MOD_ROWS, TN_MOD), lambda j: (0, j)),
        out_shape=jax.ShapeDtypeStruct((N_MOD_ROWS, 6 * D_MODEL), F32),
        compiler_params=_cparams(("parallel",)),
        name="adaln_mod",
    )(cvecs, w_mod, b_mod)


def _x_specs(n_x, tm):
    if n_x == 1:
        return [pl.BlockSpec((tm, D_MODEL), lambda i, *_: (i, 0))]
    return _split_specs(tm, D_MODEL)


def _load_x(n_x, refs, tm, rows=slice(None)):
    if n_x == 1:
        return refs[0][rows, :]
    return _load_split(refs[:2], tm, rows)


def _in_even_kernel(n_x, *refs):
    (mod_ref, g_ref, w_ref, wgk_ref, bgk_ref, cos_ref, sin_ref,
     qa_ref, ka_ref, va_ref, ga_ref, ld_ref, qb_ref, ldmin_ref, kbt_ref, vbt_ref, kbs_ref, vbs_ref) = refs[n_x:]
    x = _load_x(n_x, refs, TM_IN)
    h = _rms(x, g_ref[0:1, :]) * (1.0 + mod_ref[1:2, :]) + mod_ref[0:1, :]
    hb = h.astype(BF16)
    qa_ref[...] = _dot(hb, w_ref[:, _C_QA:_C_KA]) * (DK_A ** -0.5)
    ka_ref[...] = _dot(hb, w_ref[:, _C_KA:_C_VA])
    va_ref[...] = _dot(hb, w_ref[:, _C_VA:_C_GA]).astype(BF16)
    ga_ref[...] = _dot(hb, w_ref[:, _C_GA:_C_QB])
    cos = cos_ref[...]
    sin = sin_ref[...]
    qb = _dot(hb, w_ref[:, _C_QB:_C_KB])
    for j in range(QB_W // LANES):
        qj = qb[:, j * LANES:(j + 1) * LANES]
        qb_ref[:, j * LANES:(j + 1) * LANES] = (_rope(qj, cos, sin, D_B // 4) * (D_B ** -0.5)).astype(BF16)
    kvb = _dot(hb, w_ref[:, _C_KB:_C_LO])
    kb = _rope(kvb[:, :KB_W], cos, sin, D_B // 4)
    vb = kvb[:, KB_W:]
    lo = _dot(hb, w_ref[:, _C_LO:AB_IN]).astype(BF16)
    z = _dot(lo, wgk_ref[...]) + bgk_ref[...]
    ld = (jnp.minimum(z, 0.0) - jnp.log(1.0 + jnp.exp(-jnp.abs(z)))) * (1.0 / GATE_NORM)
    ld_ref[...] = ld
    ld_min = jnp.min(jnp.min(ld, axis=0, keepdims=True), axis=1, keepdims=True)
    ldmin_ref[...] = jnp.broadcast_to(ld_min, ldmin_ref.shape)

    is_prompt = _is_prompt_tile(TM_IN)

    @pl.when(is_prompt)
    def _():
        for s in range(TM_IN // SEQ):
            kbt_ref[s] = kb[s * SEQ:(s + 1) * SEQ, :].T
            vbt_ref[s] = vb[s * SEQ:(s + 1) * SEQ, :].T

    @pl.when(jnp.logical_not(is_prompt))
    def _():
        kbs_ref[...] = kb
        vbs_ref[...] = vb


def _in_even(xs, layer, mods, g_norm, w_in, w_gk, b_gk, cos, sin):
    tm = TM_IN
    i_ab = layer // 2
    n_prompt_tiles = N_PROMPT // tm
    seq_per_tile = tm // SEQ
    row = lambda i: (i, 0)
    widths = (QA_W, QA_W, VA_W, VA_W, 2 * QA_W, QB_W)
    dtypes = (F32, F32, BF16, F32, F32, BF16)
    kv_specs = 2 * [pl.BlockSpec((seq_per_tile, KB_W, SEQ), lambda i: (jnp.minimum(i, n_prompt_tiles - 1), 0, 0))] + \
        2 * [pl.BlockSpec((tm, KB_W), lambda i: (jnp.maximum(i - n_prompt_tiles, 0), 0))]
    kv_shapes = 2 * [jax.ShapeDtypeStruct((BATCH, KB_W, SEQ), F32)] + 2 * [jax.ShapeDtypeStruct((N_SAMPLE, KB_W), F32)]
    return pl.pallas_call(
        functools.partial(_in_even_kernel, len(xs)),
        grid=(N_TOK // tm,),
        in_specs=_x_specs(len(xs), tm) + [
            _mod_spec(tm),
            _layer_spec((4, D_MODEL), layer),
            _layer_spec((D_MODEL, AB_IN), i_ab),
            _layer_spec((2 * GK_RANK, 2 * QA_W), i_ab),
            _layer_spec((1, 2 * QA_W), i_ab),
            pl.BlockSpec((tm, LANES), row),
            pl.BlockSpec((tm, LANES), row),
        ],
        out_specs=[pl.BlockSpec((tm, w), row) for w in widths] + [
            pl.BlockSpec((None, SUBLANES, LANES), lambda i: (i, 0, 0))] + kv_specs,
        out_shape=[jax.ShapeDtypeStruct((N_TOK, w), d) for w, d in zip(widths, dtypes)] + [
            jax.ShapeDtypeStruct((N_TOK // tm, SUBLANES, LANES), F32)] + kv_shapes,
        compiler_params=_cparams(("arbitrary",)),
        name="in_even",
    )(*xs, mods, g_norm, w_in, w_gk, b_gk, cos, sin)


def _gla_kernel(seq_len, has_s0, exact, *refs):
    qa_ref, ka_ref, va_ref, ga_ref, ld_ref, gg_ref = refs[:6]
    s0_refs = refs[6:8] if has_s0 else None
    o_ref, stf_ref, stb_ref = refs[8:11] if has_s0 else refs[6:9]
    st_ref, qi_ref, ki_ref, qcat_ref, ks_ref, dec_ref, stcat_ref, acc_ref = refs[-8:]
    b_ref = qi_ref if exact else None
    n_groups = seq_len // GLA_GROUP
    cpg = GLA_GROUP // GLA_CHUNK
    n_chunks = seq_len // GLA_CHUNK
    n_pairs = H_A // 2
    pair_k = 2 * DK_A
    pair_v = 2 * DV_A

    def chunk_masks():
        r_i = lax.broadcasted_iota(jnp.int32, (GLA_GROUP, GLA_GROUP), 0)
        c_i = lax.broadcasted_iota(jnp.int32, (GLA_GROUP, GLA_GROUP), 1)
        same = (r_i // GLA_CHUNK) == (c_i // GLA_CHUNK)
        return same & (c_i <= r_i), same & (c_i >= r_i)

    def scale_group(j, carry):
        mask_f, mask_b = chunk_masks()
        t_cum = (_ones_where(mask_f), _ones_where(mask_b))
        s_r = lax.broadcasted_iota(jnp.int32, (SUBLANES, GLA_GROUP), 0)
        s_c = lax.broadcasted_iota(jnp.int32, (SUBLANES, GLA_GROUP), 1)
        t_sel = _ones_where(s_r == s_c // GLA_CHUNK)
        rows = pl.ds(pl.multiple_of(j * GLA_GROUP, GLA_GROUP), GLA_GROUP)
        q = qa_ref[rows, :]
        k = ka_ref[rows, :]
        for d in range(2):
            parts = _split3(ld_ref[rows, d * QA_W:(d + 1) * QA_W])
            b = _dot3(t_cum[d], parts)
            tot8 = _dot3(t_sel, parts)
            dec8 = jnp.exp(tot8)
            tot = jnp.concatenate(
                [jnp.broadcast_to(tot8[cc:cc + 1, :], (GLA_CHUNK, QA_W)) for cc in range(cpg)], axis=0)
            ref = 0.5 * tot
            if exact:
                b_ref[d, rows, :] = b
            else:
                qi_ref[d, rows, :] = (q * jnp.exp(b - ref)).astype(BF16)
                ki_ref[d, rows, :] = (k * jnp.exp(ref - b)).astype(BF16)
            ks_ref[d, rows, :] = (k * jnp.exp(tot - b)).astype(BF16)
            q_inter = (q * jnp.exp(b)).astype(BF16)
            for p in range(n_pairs):
                qcat_ref[rows, p * 2 * pair_k + d * pair_k:p * 2 * pair_k + (d + 1) * pair_k] = (
                    q_inter[:, p * pair_k:(p + 1) * pair_k])
            for cc in range(cpg):
                dec_ref[d, j * cpg + cc] = jnp.broadcast_to(dec8[cc:cc + 1, :], (SUBLANES, QA_W))
        return carry

    group_unroll = 1 if exact else min(n_groups, GLA_GROUP_UNROLL)
    lax.fori_loop(0, n_groups, scale_group, 0, unroll=group_unroll)

    zpad = jnp.zeros((DK_A, DV_A), F32)
    for d in range(2):
        for p in range(n_pairs):
            if has_s0:
                top = jnp.concatenate([s0_refs[d][2 * p], zpad], axis=0).T
                bot = jnp.concatenate([zpad, s0_refs[d][2 * p + 1]], axis=0).T
                st_ref[d, p] = jnp.concatenate([top, bot], axis=0)
            else:
                st_ref[d, p] = jnp.zeros((pair_v, pair_k), F32)

    def scan_chunk(c, carry):
        bd_r = lax.broadcasted_iota(jnp.int32, (pair_v, pair_k), 0)
        bd_c = lax.broadcasted_iota(jnp.int32, (pair_v, pair_k), 1)
        bd_mask = (bd_r // DV_A) == (bd_c // DK_A)
        for d in range(2):
            cd = c if d == 0 else n_chunks - 1 - c
            rows = pl.ds(pl.multiple_of(cd * GLA_CHUNK, GLA_CHUNK), GLA_CHUNK)
            for p in range(n_pairs):
                st = st_ref[d, p]
                stcat_ref[cd, p, :, d * pair_k:(d + 1) * pair_k] = st.astype(BF16)
                u = _dot_tn(va_ref[rows, p * pair_v:(p + 1) * pair_v], ks_ref[d, rows, p * pair_k:(p + 1) * pair_k])
                dec = dec_ref[d, cd][0:1, p * pair_k:(p + 1) * pair_k]
                st_ref[d, p] = st * dec + jnp.where(bd_mask, u, 0.0)
        return carry

    lax.fori_loop(0, n_chunks, scan_chunk, 0, unroll=1 if exact else GLA_SCAN_UNROLL)

    for d, out_ref in ((0, stf_ref), (1, stb_ref)):
        for p in range(n_pairs):
            st = st_ref[d, p]
            out_ref[2 * p] = st[0:DV_A, :].T[0:DK_A, :]
            out_ref[2 * p + 1] = st[DV_A:pair_v, :].T[DK_A:pair_k, :]

    def exact_intra(row0, h):
        p, hh = divmod(h, 2)
        lanes = slice(p * pair_k, (p + 1) * pair_k)
        head_lanes = (lax.broadcasted_iota(jnp.int32, (GLA_CHUNK, pair_k), 1) // DK_A) == hh
        s_idx = lax.broadcasted_iota(jnp.int32, (GLA_CHUNK, GLA_CHUNK), 0)
        t_idx = lax.broadcasted_iota(jnp.int32, (GLA_CHUNK, GLA_CHUNK), 1)
        outs = []
        for cc in range(cpg):
            r0 = row0 + cc * GLA_CHUNK
            crow = pl.ds(pl.multiple_of(r0, GLA_CHUNK), GLA_CHUNK)
            kc = ka_ref[crow, lanes]
            att_t = jnp.zeros((GLA_CHUNK, GLA_CHUNK), F32)
            for d in range(2):
                bc = b_ref[d, crow, lanes]

                def row_step(t, att_t, d=d, bc=bc, kc=kc, r0=r0):
                    grp = pl.ds(pl.multiple_of(r0 + (t // SUBLANES) * SUBLANES, SUBLANES), SUBLANES)
                    pick = lax.broadcasted_iota(jnp.int32, (SUBLANES, pair_k), 0) == t % SUBLANES
                    bt = jnp.sum(jnp.where(pick, b_ref[d, grp, lanes], 0.0), axis=0, keepdims=True)
                    qt = jnp.sum(jnp.where(pick, qa_ref[grp, lanes], 0.0), axis=0, keepdims=True)
                    w = jnp.where(head_lanes, qt * kc * jnp.exp(jnp.minimum(bt - bc, 0.0)), 0.0)
                    col = jnp.sum(w, axis=1, keepdims=True)
                    allowed = (s_idx <= t) if d == 0 else (s_idx >= t)
                    return jnp.where((t_idx == t) & allowed, att_t + col, att_t)

                att_t = lax.fori_loop(0, GLA_CHUNK, row_step, att_t)
            outs.append(_dot_tn(att_t.astype(BF16), va_ref[crow, h * DV_A:(h + 1) * DV_A]))
        return jnp.concatenate(outs, axis=0)

    def out_group(j, carry):
        mask_f, mask_b = chunk_masks()
        lane = lax.broadcasted_iota(jnp.int32, (GLA_GROUP, pair_k), 1)
        rows = pl.ds(pl.multiple_of(j * GLA_GROUP, GLA_GROUP), GLA_GROUP)
        for p in range(n_pairs):
            for cc in range(cpg):
                c = j * cpg + cc
                crow = pl.ds(pl.multiple_of(c * GLA_CHUNK, GLA_CHUNK), GLA_CHUNK)
                acc_ref[crow, p * pair_v:(p + 1) * pair_v] = _dot_nt(
                    qcat_ref[crow, p * 2 * pair_k:(p + 1) * 2 * pair_k], stcat_ref[c, p])
        if exact:
            intras = [exact_intra(j * GLA_GROUP, h) for h in range(H_A)]
        else:
            atts = []
            for h in range(H_A):
                p, hh = divmod(h, 2)
                head_lanes = _ones_where((lane // DK_A) == hh)
                att = None
                for d, mask in ((0, mask_f), (1, mask_b)):
                    qm = qi_ref[d, rows, p * pair_k:(p + 1) * pair_k] * head_lanes
                    a = jnp.where(mask, _dot_nt(qm, ki_ref[d, rows, p * pair_k:(p + 1) * pair_k]), 0.0)
                    att = a if att is None else att + a
                atts.append(att.astype(BF16))
            intras = [_dot(atts[h], va_ref[rows, h * DV_A:(h + 1) * DV_A]) for h in range(H_A)]
        for h in range(H_A):
            o = acc_ref[rows, h * DV_A:(h + 1) * DV_A] + intras[h]
            gate = _silu(ga_ref[rows, h * DV_A:(h + 1) * DV_A])
            o_ref[rows, h * DV_A:(h + 1) * DV_A] = (_rms(o, gg_ref[...]) * gate).astype(BF16)
        return carry

    lax.fori_loop(0, n_groups, out_group, 0, unroll=group_unroll)


def _gla(seq_len, n_seq, row_block0, qa, ka, va, ga, ld, g_gla, i_ab, s0=None, exact=False):
    has_s0 = s0 is not None
    n_chunks = seq_len // GLA_CHUNK
    n_pairs = H_A // 2
    rows = lambda b: (row_block0 + b, 0)
    st_spec = pl.BlockSpec((None, H_A, DK_A, DV_A), lambda b: (b, 0, 0, 0))
    in_specs = [
        pl.BlockSpec((seq_len, QA_W), rows),
        pl.BlockSpec((seq_len, QA_W), rows),
        pl.BlockSpec((seq_len, VA_W), rows),
        pl.BlockSpec((seq_len, VA_W), rows),
        pl.BlockSpec((seq_len, 2 * QA_W), rows),
        _layer_spec((1, DV_A), i_ab),
    ]
    args = [qa, ka, va, ga, ld, g_gla]
    if has_s0:
        s0_spec = pl.BlockSpec((None, None, H_A, DK_A, DV_A), lambda b: (b, i_ab, 0, 0, 0))
        in_specs += [s0_spec, s0_spec]
        args += list(s0)
    st_shape = jax.ShapeDtypeStruct((n_seq, H_A, DK_A, DV_A), F32)
    return pl.pallas_call(
        functools.partial(_gla_kernel, seq_len, has_s0, exact),
        grid=(n_seq,),
        in_specs=in_specs,
        out_specs=[pl.BlockSpec((seq_len, VA_W), lambda b: (b, 0)), st_spec, st_spec],
        out_shape=[jax.ShapeDtypeStruct((n_seq * seq_len, VA_W), BF16), st_shape, st_shape],
        scratch_shapes=[
            pltpu.VMEM((2, n_pairs, 2 * DV_A, 2 * DK_A), F32),
            pltpu.VMEM((2, seq_len, QA_W), F32 if exact else BF16),
            pltpu.VMEM((2, seq_len, QA_W), BF16),
            pltpu.VMEM((seq_len, 2 * QA_W), BF16),
            pltpu.VMEM((2, seq_len, QA_W), BF16),
            pltpu.VMEM((2, n_chunks, SUBLANES, QA_W), F32),
            pltpu.VMEM((n_chunks, n_pairs, 2 * DV_A, 4 * DK_A), BF16),
            pltpu.VMEM((seq_len, VA_W), F32),
        ],
        compiler_params=_cparams(("parallel",)),
        name=("gla_s" if has_s0 else "gla_p") + ("_exact" if exact else ""),
    )(*args)


def _swa_head_softmax(pieces, sink):
    m = sink
    for s, _, _ in pieces:
        m = jnp.maximum(m, jnp.max(s, axis=-1, keepdims=True))
    den = jnp.exp(sink - m)
    acc = None
    for s, v, transposed in pieces:
        e = jnp.exp(s - m)
        den = den + jnp.sum(e, axis=-1, keepdims=True)
        pv = _dot_nt(e.astype(BF16), v) if transposed else _dot(e.astype(BF16), v)
        acc = pv if acc is None else acc + pv
    return acc / den


def _dup_groups(x):
    lane = lax.broadcasted_iota(jnp.int32, x.shape, 1)
    swapped = pltpu.roll(x, D_B, 1)
    low = lane < D_B
    return jnp.where(low, x, swapped).astype(BF16), jnp.where(low, swapped, x).astype(BF16)


def _dup_groups_t(xt):
    g0, g1 = xt[0:D_B, :], xt[D_B:, :]
    return jnp.concatenate([g0, g0], axis=0).astype(BF16), jnp.concatenate([g1, g1], axis=0).astype(BF16)


def _swa_heads(sink_ref, i_ab, q_ref, kv_pieces, o_ref, n_rows):
    lane = lax.broadcasted_iota(jnp.int32, (n_rows, LANES), 1)
    head_pieces = []
    for h in range(H_B):
        j, hh = divmod(h, 2)
        g = h // G_B
        qm = q_ref[:, j * LANES:(j + 1) * LANES] * _ones_where((lane // D_B) == hh)
        pieces = []
        for keys, vals, transposed, mask in kv_pieces:
            s = _dot(qm, keys[g]) if transposed else _dot_nt(qm, keys[g])
            if mask is not None:
                s = jnp.where(mask, s, NEG_BIG)
            pieces.append((s, vals[g], transposed))
        head_pieces.append(pieces)
    outs = [_swa_head_softmax(pieces, sink_ref[i_ab, h]) for h, pieces in enumerate(head_pieces)]
    for j in range(H_B // 2):
        o_ref[:, j * LANES:(j + 1) * LANES] = jnp.where(lane < D_B, outs[2 * j], outs[2 * j + 1]).astype(BF16)


def _swa_prompt_kernel(i_ab, sink_ref, q_ref, kt_ref, vt_ref, o_ref):
    _swa_heads(sink_ref, i_ab, q_ref, [(_dup_groups_t(kt_ref[...]), _dup_groups_t(vt_ref[...]), True, None)],
               o_ref, SEQ)


def _swa_prompt(sink, i_ab, qb, kbt, vbt):
    seq = lambda b: (b, 0)
    seq_t = pl.BlockSpec((None, KB_W, SEQ), lambda b: (b, 0, 0))
    return pl.pallas_call(
        functools.partial(_swa_prompt_kernel, i_ab),
        grid=(BATCH,),
        in_specs=[pl.BlockSpec(memory_space=pltpu.SMEM), pl.BlockSpec((SEQ, QB_W), seq), seq_t, seq_t],
        out_specs=pl.BlockSpec((SEQ, QB_W), seq),
        out_shape=jax.ShapeDtypeStruct((N_PROMPT, QB_W), BF16),
        compiler_params=_cparams(("parallel",)),
        name="swa_p",
    )(sink, qb, kbt, vbt)


def _swa_sample_kernel(i_ab, sink_ref, q_ref, k_ref, v_ref, kct_ref, vct_ref, o_ref):
    n = pl.program_id(1)
    start = pl.multiple_of(jnp.clip((n - 1) * SWA_QB, 0, DEC_SEQ - SWA_WIN), SWA_QB)
    local = (_dup_groups(k_ref[pl.ds(start, SWA_WIN), :]), _dup_groups(v_ref[pl.ds(start, SWA_WIN), :]))
    ctx = (_dup_groups_t(kct_ref[...]), _dup_groups_t(vct_ref[...]))
    qi = n * SWA_QB + lax.broadcasted_iota(jnp.int32, (SWA_QB, SWA_WIN), 0)
    ki = start + lax.broadcasted_iota(jnp.int32, (SWA_QB, SWA_WIN), 1)
    band = jnp.abs(qi - ki) <= WINDOW
    _swa_heads(sink_ref, i_ab, q_ref, [ctx + (True, None), local + (False, band)], o_ref, SWA_QB)


def _swa_sample(sink, i_ab, qb, kb, vb, kc, vc):
    nqb = DEC_SEQ // SWA_QB
    q0 = N_PROMPT // SWA_QB
    return pl.pallas_call(
        functools.partial(_swa_sample_kernel, i_ab),
        grid=(DEC_BATCH, nqb),
        in_specs=[
            pl.BlockSpec(memory_space=pltpu.SMEM),
            pl.BlockSpec((SWA_QB, QB_W), lambda b, n: (q0 + b * nqb + n, 0)),
            pl.BlockSpec((DEC_SEQ, KB_W), lambda b, n: (b, 0)),
            pl.BlockSpec((DEC_SEQ, KB_W), lambda b, n: (b, 0)),
            pl.BlockSpec((None, None, KB_W, PAST_LEN), lambda b, n: (b, i_ab, 0, 0)),
            pl.BlockSpec((None, None, KB_W, PAST_LEN), lambda b, n: (b, i_ab, 0, 0)),
        ],
        out_specs=pl.BlockSpec((SWA_QB, QB_W), lambda b, n: (b * nqb + n, 0)),
        out_shape=jax.ShapeDtypeStruct((N_SAMPLE, QB_W), BF16),
        compiler_params=_cparams(("parallel", "parallel")),
        name="swa_s",
    )(sink, qb, kb, vb, kc, vc)


def _in_odd_kernel(x_ref, mod_ref, g_ref, wd_ref, gq_ref, gkv_ref, wuq_ref, wuk_ref, wuvt_ref, cos_ref, sin_ref,
                   q_ref, k_ref, vt_ref, ckv_ref, krt_ref):
    h = _rms(x_ref[...], g_ref[0:1, :]) * (1.0 + mod_ref[1:2, :]) + mod_ref[0:1, :]
    hb = h.astype(BF16)
    cos = cos_ref[...]
    sin = sin_ref[...]
    c_q = _dot(hb, wd_ref[:, 0:Q_LORA])
    c_kv = _rms(_dot(hb, wd_ref[:, Q_LORA:Q_LORA + KV_LORA]), gkv_ref[...])
    kr = _rope_c(_dot(hb, wd_ref[:, Q_LORA + KV_LORA:DOWN_W]), cos, sin)
    cqb = _rms(c_q, gq_ref[...]).astype(BF16)
    ckvb = c_kv.astype(BF16)
    scale = (NOPE_C + ROPE_C) ** -0.5
    group_w = HEADS_PER_DOT_C * HEAD_PAD_C
    for grp in range(H_C // HEADS_PER_DOT_C):
        gsl = slice(grp * group_w, (grp + 1) * group_w)
        qg = _dot(cqb, wuq_ref[:, gsl])
        kg = _dot(ckvb, wuk_ref[:, gsl])
        for j in range(HEADS_PER_DOT_C):
            sl = slice(j * HEAD_PAD_C, (j + 1) * HEAD_PAD_C)
            osl = slice(grp * group_w + j * HEAD_PAD_C, grp * group_w + (j + 1) * HEAD_PAD_C)
            q_ref[:, osl] = (_rope_c(qg[:, sl], cos, sin) * scale).astype(BF16)
            k_ref[:, osl] = (kg[:, sl] + kr).astype(BF16)
    vt_ref[...] = _dot_nt(wuvt_ref[...], ckvb).astype(BF16)

    @pl.when(_is_prompt_tile(TM_IN))
    def _():
        ckv_ref[...] = c_kv
        q4 = ROPE_C // 4
        half = LANES // 2
        for s in range(TM_IN // SEQ):
            t = kr[s * SEQ:(s + 1) * SEQ, :].T
            krt_ref[s] = jnp.concatenate([t[0:q4], t[half:half + q4], t[q4:2 * q4], t[half + q4:half + 2 * q4]], axis=0)


def _in_odd(x, layer, mods, g_norm, w_down, g_q, g_kv, w_uq, w_uk, w_uvt, cos, sin):
    tm = TM_IN
    i_c = layer // 2
    n_prompt_tiles = N_PROMPT // tm
    row = lambda i: (i, 0)
    prompt_row = lambda i: (jnp.minimum(i, n_prompt_tiles - 1), 0)
    prompt_seq = lambda i: (jnp.minimum(i, n_prompt_tiles - 1), 0, 0)
    return pl.pallas_call(
        _in_odd_kernel,
        grid=(N_TOK // tm,),
        in_specs=[
            pl.BlockSpec((tm, D_MODEL), row),
            _mod_spec(tm),
            _layer_spec((4, D_MODEL), layer),
            _layer_spec((D_MODEL, DOWN_W), i_c),
            _layer_spec((1, Q_LORA), i_c),
            _layer_spec((1, KV_LORA), i_c),
            _layer_spec((Q_LORA, QC_W), i_c),
            _layer_spec((KV_LORA, QC_W), i_c),
            _layer_spec((VC_W, KV_LORA), i_c),
            pl.BlockSpec((tm, LANES), row),
            pl.BlockSpec((tm, LANES), row),
        ],
        out_specs=[pl.BlockSpec((tm, QC_W), row), pl.BlockSpec((tm, QC_W), row),
                   pl.BlockSpec((VC_W, tm), lambda i: (0, i)),
                   pl.BlockSpec((tm, KV_LORA), prompt_row), pl.BlockSpec((tm // SEQ, ROPE_C, SEQ), prompt_seq)],
        out_shape=[jax.ShapeDtypeStruct((N_TOK, QC_W), BF16), jax.ShapeDtypeStruct((N_TOK, QC_W), BF16),
                   jax.ShapeDtypeStruct((VC_W, N_TOK), BF16),
                   jax.ShapeDtypeStruct((N_PROMPT, KV_LORA), F32), jax.ShapeDtypeStruct((BATCH, ROPE_C, SEQ), F32)],
        compiler_params=_cparams(("arbitrary",)),
        name="in_odd",
    )(x, mods, g_norm, w_down, g_q, g_kv, w_uq, w_uk, w_uvt, cos, sin)


def _reduce_rows(x, op, reduce_fn):
    while x.shape[0] % (2 * SUBLANES) == 0:
        half = x.shape[0] // 2
        x = op(x[:half], x[half:])
    return reduce_fn(x, axis=0, keepdims=True)


def _mla_heads(q_ref, kv_pieces, o_ref, ot_ref, group):
    for h0 in range(0, H_C, group):
        heads = range(h0, h0 + group)
        scores = []
        for hd in heads:
            sl = slice(hd * HEAD_PAD_C, (hd + 1) * HEAD_PAD_C)
            scores.append([_dot_nt(k_ref[:, sl], q_ref[:, sl]) for k_ref, _ in kv_pieces])
        exps, dens = [], []
        for per_piece in scores:
            m = None
            for s in per_piece:
                sm = _reduce_rows(s, jnp.maximum, jnp.max)
                m = sm if m is None else jnp.maximum(m, sm)
            es = [jnp.exp(s - m) for s in per_piece]
            den = None
            for e in es:
                part = _reduce_rows(e, jnp.add, jnp.sum)
                den = part if den is None else den + part
            exps.append([e.astype(BF16) for e in es])
            dens.append(den)
        for hd, es, den in zip(heads, exps, dens):
            acc = None
            for e, (_, vt_ref) in zip(es, kv_pieces):
                pv = _dot(vt_ref[hd * V_C:(hd + 1) * V_C, :], e)
                acc = pv if acc is None else acc + pv
            ot_ref[hd * V_C:(hd + 1) * V_C, :] = acc / den
    o_ref[...] = ot_ref[...].T.astype(BF16)


def _mla_prompt_kernel(q_ref, k_ref, vt_ref, o_ref, ot_ref):
    _mla_heads(q_ref, [(k_ref, vt_ref)], o_ref, ot_ref, MLA_HEAD_GROUP_P)


def _mla_prompt(q, k, vt):
    seq = lambda b: (b, 0)
    return pl.pallas_call(
        _mla_prompt_kernel,
        grid=(BATCH,),
        in_specs=[pl.BlockSpec((SEQ, QC_W), seq), pl.BlockSpec((SEQ, QC_W), seq),
                  pl.BlockSpec((VC_W, SEQ), lambda b: (0, b))],
        out_specs=pl.BlockSpec((SEQ, VC_W), seq),
        out_shape=jax.ShapeDtypeStruct((N_PROMPT, VC_W), BF16),
        scratch_shapes=[pltpu.VMEM((VC_W, SEQ), F32)],
        compiler_params=_cparams(("parallel",)),
        name="mla_p",
    )(q, k, vt)


def _mla_sample_kernel(q_ref, k_ref, vt_ref, ckv_ref, krt_ref, wuk_ref, wuvt_ref, o_ref, kc_ref, vct_ref, ot_ref):
    @pl.when(pl.program_id(1) == 0)
    def _():
        cb = ckv_ref[...].astype(BF16)
        krt = krt_ref[...]
        q4 = ROPE_C // 4
        kr = jnp.concatenate(
            [krt[0:q4], krt[2 * q4:3 * q4], jnp.zeros((NOPE_LO_C, PAST_LEN), F32),
             krt[q4:2 * q4], krt[3 * q4:4 * q4], jnp.zeros((LANES // 2 - ROPE_HALF_C, PAST_LEN), F32)], axis=0).T
        group_w = HEADS_PER_DOT_C * HEAD_PAD_C
        for grp in range(H_C // HEADS_PER_DOT_C):
            kg = _dot(cb, wuk_ref[:, grp * group_w:(grp + 1) * group_w])
            for j in range(HEADS_PER_DOT_C):
                osl = slice(grp * group_w + j * HEAD_PAD_C, grp * group_w + (j + 1) * HEAD_PAD_C)
                kc_ref[:, osl] = (kg[:, j * HEAD_PAD_C:(j + 1) * HEAD_PAD_C] + kr).astype(BF16)
        vct_ref[...] = _dot_nt(wuvt_ref[...], cb).astype(BF16)

    _mla_heads(q_ref, [(kc_ref, vct_ref), (k_ref, vt_ref)], o_ref, ot_ref, MLA_HEAD_GROUP_S)


def _mla_sample(q, k, vt, ckv_ctx, kr_ctx, w_uk, w_uvt, i_c):
    nqb = DEC_SEQ // MLA_QB
    q0 = N_PROMPT // MLA_QB
    s0 = N_PROMPT // DEC_SEQ
    return pl.pallas_call(
        _mla_sample_kernel,
        grid=(DEC_BATCH, nqb),
        in_specs=[
            pl.BlockSpec((MLA_QB, QC_W), lambda b, n: (q0 + b * nqb + n, 0)),
            pl.BlockSpec((DEC_SEQ, QC_W), lambda b, n: (s0 + b, 0)),
            pl.BlockSpec((VC_W, DEC_SEQ), lambda b, n: (0, s0 + b)),
            pl.BlockSpec((None, None, PAST_LEN, KV_LORA), lambda b, n: (b, i_c, 0, 0)),
            pl.BlockSpec((None, None, ROPE_C, PAST_LEN), lambda b, n: (b, i_c, 0, 0)),
            _layer_spec((KV_LORA, QC_W), i_c),
            _layer_spec((VC_W, KV_LORA), i_c),
        ],
        out_specs=pl.BlockSpec((MLA_QB, VC_W), lambda b, n: (b * nqb + n, 0)),
        out_shape=jax.ShapeDtypeStruct((N_SAMPLE, VC_W), BF16),
        scratch_shapes=[pltpu.VMEM((PAST_LEN, QC_W), BF16), pltpu.VMEM((VC_W, PAST_LEN), BF16),
                        pltpu.VMEM((VC_W, MLA_QB), F32)],
        compiler_params=_cparams(("parallel", "arbitrary")),
        name="mla_s",
    )(q, k, vt, ckv_ctx, kr_ctx, w_uk, w_uvt)


def _out_kernel(n_x, split_out, *refs):
    it = iter(refs[n_x:])
    mod_ref, g_ref = next(it), next(it)
    a_refs = (next(it), next(it))
    b_refs = (next(it), next(it))
    wo_ref, w1_ref, w2_ref = next(it), next(it), next(it)
    if split_out:
        next_mod = None
        out_refs = (next(it), next(it))
        h2_ref, acc_ref, x1_ref = next(it), next(it), next(it)
        y_ref = acc_ref
    else:
        next_mod = (next(it), next(it), next(it))
        x1_ref = y_ref = next(it)
        modn_ref = next(it)
        h2_ref, acc_ref = next(it), next(it)

    def emit_next_mod():
        if next_mod is not None:
            modn_ref[...] = _mod_block(*next_mod)

    kk = pl.program_id(1)
    half = wo_ref.shape[0] // 2

    row_chunks = [slice(r * OUT_PROLOGUE_ROWS, (r + 1) * OUT_PROLOGUE_ROWS) for r in range(TM_OUT // OUT_PROLOGUE_ROWS)]

    @pl.when(kk == 0)
    def _():
        wo_a = wo_ref[0:half, :].astype(BF16)
        wo_b = wo_ref[half:, :].astype(BF16)
        gate_g1 = mod_ref[2:3, :] * g_ref[1:2, :]
        scale_g2 = g_ref[2:3, :] * (1.0 + mod_ref[4:5, :])
        for rows in row_chunks:
            a = _load_split(a_refs, TM_OUT, rows)
            b = _load_split(b_refs, TM_OUT, rows)
            mix = _dot(a, wo_a) + _dot(b, wo_b)
            x1 = _load_x(n_x, refs, TM_OUT, rows) + _rms(mix, gate_g1)
            x1_ref[rows, :] = x1
            h2_ref[rows, :] = (_rms(x1, scale_g2) + mod_ref[3:4, :]).astype(BF16)
        acc_ref[...] = jnp.zeros(acc_ref.shape, F32)

    def ffn(rows, w1, w2):
        hid = jnp.maximum(_dot(h2_ref[rows, :], w1), 0.0)
        return _dot((hid * hid).astype(BF16), w2)

    is_last = kk == pl.num_programs(1) - 1

    @pl.when(jnp.logical_not(is_last))
    def _():
        acc_ref[...] += ffn(slice(None), w1_ref[...].astype(BF16), w2_ref[...].astype(BF16))
        emit_next_mod()

    @pl.when(is_last)
    def _():
        w1 = w1_ref[...].astype(BF16)
        w2 = w2_ref[...].astype(BF16)
        gate_g3 = mod_ref[5:6, :] * g_ref[3:4, :]
        for rows in row_chunks:
            y_ref[rows, :] = x1_ref[rows, :] + _rms(acc_ref[rows, :] + ffn(rows, w1, w2), gate_g3)
        emit_next_mod()

    if split_out:
        @pl.when(is_last)
        def _():
            _store_split(out_refs, TM_OUT, acc_ref[...])


def _out_layer(xs, layer, mods, g_norm, mix_a, mix_b, b_col, w_o, i_o, w_ff1, w_ff2, next_mod_args):
    tm = TM_OUT
    half = D_MODEL // 2
    n_k = D_FF // TK_FF
    split_out = next_mod_args is None
    if split_out:
        out_specs, out_shape = _split_specs(tm, D_MODEL), _split_shapes(D_MODEL, F32)
        next_specs, next_args = [], []
    else:
        mod_cols = 6 * D_MODEL // (N_TOK // tm * n_k)
        step = lambda i, k: i * n_k + k
        out_specs = [pl.BlockSpec((tm, D_MODEL), lambda i, k: (i, 0)),
                     pl.BlockSpec((N_MOD_ROWS, mod_cols), lambda i, k: (0, step(i, k)))]
        out_shape = [jax.ShapeDtypeStruct((N_TOK, D_MODEL), F32),
                     jax.ShapeDtypeStruct((N_MOD_ROWS, 6 * D_MODEL), F32)]
        next_specs = [pl.BlockSpec((N_MOD_ROWS, D_MODEL), lambda i, k: (0, 0)),
                      pl.BlockSpec((None, D_MODEL, mod_cols), lambda i, k: (layer + 1, 0, step(i, k))),
                      pl.BlockSpec((None, 1, mod_cols), lambda i, k: (layer + 1, 0, step(i, k)))]
        next_args = list(next_mod_args)
    return pl.pallas_call(
        functools.partial(_out_kernel, len(xs), split_out),
        grid=(N_TOK // tm, n_k),
        in_specs=_x_specs(len(xs), tm) + [
            _mod_spec(tm),
            _layer_spec((4, D_MODEL), layer),
        ] + _split_specs(tm, half, 0) + _split_specs(tm, half, b_col) + [
            _layer_spec((D_MODEL, D_MODEL), i_o),
            pl.BlockSpec((None, D_MODEL, TK_FF), lambda i, k: (layer, 0, k)),
            pl.BlockSpec((None, TK_FF, D_MODEL), lambda i, k: (layer, k, 0)),
        ] + next_specs,
        out_specs=out_specs,
        out_shape=out_shape,
        scratch_shapes=[pltpu.VMEM((tm, D_MODEL), BF16), pltpu.VMEM((tm, D_MODEL), F32)] + (
            [pltpu.VMEM((tm, D_MODEL), F32)] if split_out else []),
        compiler_params=_cparams(("arbitrary", "arbitrary")),
        name="out_mlp",
    )(*xs, mods, g_norm, *mix_a, *mix_b, w_o, w_ff1, w_ff2, *next_args)


def _rope_angles(head_dim):
    nf = head_dim // 4
    n_rows = DEC_SEQ // GRID_W
    rows = jnp.repeat(jnp.arange(n_rows, dtype=F32), GRID_W)
    cols = jnp.tile(jnp.arange(GRID_W, dtype=F32), n_rows)
    inv = ROPE_BASE ** (-jnp.arange(nf, dtype=F32) / nf)
    return jnp.stack([rows[:, None] * inv, cols[:, None] * inv], axis=1)


def _token_tables(cos_g, sin_g):
    cos_t = jnp.concatenate([jnp.ones((N_PROMPT, LANES), F32), jnp.tile(cos_g, (DEC_BATCH, 1))], axis=0)
    sin_t = jnp.concatenate([jnp.zeros((N_PROMPT, LANES), F32), jnp.tile(sin_g, (DEC_BATCH, 1))], axis=0)
    return cos_t, sin_t


def _rope_tables_b():
    nf = D_B // 4
    ang = _rope_angles(D_B)
    cos = jnp.broadcast_to(jnp.cos(ang)[:, :, None, :], (DEC_SEQ, 2, 2, nf)).reshape(DEC_SEQ, D_B)
    sin = jnp.sin(ang)
    sin = jnp.stack([-sin, sin], axis=2).reshape(DEC_SEQ, D_B)
    reps = LANES // D_B
    return _token_tables(jnp.tile(cos, (1, reps)), jnp.tile(sin, (1, reps)))


def _rope_tables_c():
    ang = _rope_angles(ROPE_C).reshape(DEC_SEQ, ROPE_HALF_C)
    cos, sin = jnp.cos(ang), jnp.sin(ang)
    half = LANES // 2
    cos_g = jnp.ones((DEC_SEQ, LANES), F32).at[:, 0:ROPE_HALF_C].set(cos).at[:, half:half + ROPE_HALF_C].set(cos)
    sin_g = jnp.zeros((DEC_SEQ, LANES), F32).at[:, 0:ROPE_HALF_C].set(-sin).at[:, half:half + ROPE_HALF_C].set(sin)
    return _token_tables(cos_g, sin_g)


def kernel(x_prompt, x_sample, state_gla_fwd, state_gla_bwd, cache_swa_k, cache_swa_v, cache_mla_ckv, cache_mla_kr, c, c_ctx, w_mod, b_mod, g_norm, w_ff1, w_ff2, w_in_ab, w_gk_f, b_gk_f, w_gk_b, b_gk_b, g_gla, swa_sink, w_out_ab, w_mla_down, g_mla_q, g_mla_kv, w_mla_uq, w_mla_ukv, w_mla_o):
    xs = (x_prompt.reshape(N_PROMPT, D_MODEL), x_sample.reshape(N_SAMPLE, D_MODEL))
    cvecs = jnp.concatenate([c_ctx[None, :], c, jnp.zeros((N_MOD_ROWS - 1 - DEC_BATCH, D_MODEL), F32)], axis=0)
    b_mod3 = b_mod.reshape(DEPTH, 1, 6 * D_MODEL)
    mods = _modulation(cvecs, w_mod, b_mod3, 0).reshape(N_MOD_ROWS, 6, D_MODEL)

    cos_b, sin_b = _rope_tables_b()
    cos_c, sin_c = _rope_tables_c()
    p_blk = N_PROMPT // DEC_SEQ

    n_ab = w_in_ab.shape[0]
    w_in = _prep_even(jnp.swapaxes(w_in_ab, 1, 2))
    zgk = jnp.zeros((n_ab, GK_RANK, QA_W), F32)
    w_gk = jnp.concatenate([jnp.concatenate([w_gk_f, zgk], axis=2),
                            jnp.concatenate([zgk, w_gk_b], axis=2)], axis=1).astype(BF16)
    b_gk = jnp.concatenate([b_gk_f, b_gk_b], axis=1)[:, None, :]
    gg = g_gla[:, None, :]
    kc = jnp.transpose(cache_swa_k, (0, 1, 3, 4, 2)).reshape(DEC_BATCH, n_ab, KB_W, PAST_LEN)
    vc = jnp.transpose(cache_swa_v, (0, 1, 3, 4, 2)).reshape(DEC_BATCH, n_ab, KB_W, PAST_LEN)
    w_down, w_uq, w_uk, w_uvt = _prep_odd(jnp.swapaxes(w_mla_down, 1, 2), w_mla_uq, w_mla_ukv)
    g_q = g_mla_q[:, None, :]
    g_kv = g_mla_kv[:, None, :]
    kr_ctx = jnp.swapaxes(cache_mla_kr, 2, 3)

    def gla_layer(i_ab, exact):
        def run(qa, ka, va, ga, ld):
            o_p, stf, stb = _gla(SEQ, BATCH, 0, qa, ka, va, ga, ld, gg, i_ab, exact=exact)
            o_s, _, _ = _gla(DEC_SEQ, DEC_BATCH, p_blk, qa, ka, va, ga, ld, gg, i_ab,
                             (state_gla_fwd, state_gla_bwd), exact=exact)
            return o_p, o_s, stf, stb
        return run

    st_f, st_b, sk, sv, ckv_out, ckr_out = [], [], [], [], [], []
    for l in range(DEPTH):
        i = l // 2
        next_mod_args = None if l == DEPTH - 1 else (cvecs, w_mod, b_mod3)
        if l % 2 == 0:
            qa, ka, va, ga, ld, qb, ld_min, kbt, vbt, kb_s, vb_s = _in_even(
                xs, l, mods, g_norm, w_in, w_gk, b_gk, cos_b, sin_b)
            factorisable = jnp.min(ld_min) * GLA_CHUNK >= -GLA_SAFE_TOTAL
            o_gla_p, o_gla_s, stf, stb = lax.cond(factorisable, gla_layer(i, False), gla_layer(i, True),
                                                  qa, ka, va, ga, ld)
            o_swa_p = _swa_prompt(swa_sink, i, qb, kbt, vbt)
            o_swa_s = _swa_sample(swa_sink, i, qb, kb_s, vb_s, kc, vc)
            outs = _out_layer(xs, l, mods, g_norm, (o_gla_p, o_gla_s), (o_swa_p, o_swa_s), 0, w_out_ab, i,
                              w_ff1, w_ff2, next_mod_args)
            st_f.append(stf)
            st_b.append(stb)
            sk.append(kbt)
            sv.append(vbt)
        else:
            q, k, vt, ckv, krt = _in_odd(xs[0], l, mods, g_norm, w_down, g_q, g_kv, w_uq, w_uk, w_uvt, cos_c, sin_c)
            o_mla = (_mla_prompt(q, k, vt), _mla_sample(q, k, vt, cache_mla_ckv, kr_ctx, w_uk, w_uvt, i))
            outs = _out_layer(xs, l, mods, g_norm, o_mla, o_mla, 1, w_mla_o, i, w_ff1, w_ff2, next_mod_args)
            ckv_out.append(ckv.reshape(BATCH, SEQ, KV_LORA))
            ckr_out.append(krt)
        if next_mod_args is None:
            xs = outs
        else:
            xs, mods = (outs[0],), outs[1].reshape(N_MOD_ROWS, 6, D_MODEL)

    y_prompt = xs[0].reshape(BATCH, SEQ, D_MODEL)
    y_sample = xs[1].reshape(DEC_BATCH, DEC_SEQ, D_MODEL)
    swa_cache = lambda parts: jnp.transpose(
        jnp.stack(parts, axis=1).reshape(BATCH, len(parts), KV_B, D_B, SEQ), (0, 1, 4, 2, 3))
    return (y_prompt, y_sample, jnp.stack(st_f, axis=1), jnp.stack(st_b, axis=1), swa_cache(sk), swa_cache(sv),
            jnp.stack(ckv_out, axis=1), jnp.swapaxes(jnp.stack(ckr_out, axis=1), 2, 3))
```

```python
import functools

import jax
import jax.numpy as jnp
from jax import lax
from jax.experimental import pallas as pl
from jax.experimental.pallas import tpu as pltpu

F32 = jnp.float32
BF16 = jnp.bfloat16

D_MODEL = 1024
BATCH = 16
SEQ = 256
DEPTH = 4
DEC_BATCH = 2
DEC_SEQ = 1024
PAST_LEN = 512
GRID_W = 64
D_FF = 4 * D_MODEL
EPS = 1e-6
ROPE_BASE = 10000.0
H_A = 4
DK_A = 64
DV_A = 128
GK_RANK = 16
GATE_NORM = 16.0
GLA_CHUNK = 64
H_B = 8
KV_B = 2
G_B = H_B // KV_B
D_B = 64
WINDOW = 128
H_C = 16
NOPE_C = 64
ROPE_C = 32
V_C = 64
Q_LORA = 384
KV_LORA = 256

N_PROMPT = BATCH * SEQ
N_SAMPLE = DEC_BATCH * DEC_SEQ
N_TOK = N_PROMPT + N_SAMPLE
SUBLANES = 8
N_MOD_ROWS = SUBLANES
QA_W = H_A * DK_A
VA_W = H_A * DV_A
QB_W = H_B * D_B
KB_W = KV_B * D_B
LANES = 128
HEAD_PAD_C = 128
QC_W = H_C * HEAD_PAD_C
VC_W = H_C * V_C
DOWN_RAW_W = Q_LORA + KV_LORA + ROPE_C
DOWN_W = Q_LORA + KV_LORA + LANES
ROPE_HALF_C = ROPE_C // 2
NOPE_LO_C = LANES // 2 - ROPE_HALF_C

TM_IN = 512
TM_OUT = 1024
TK_FF = 512
OUT_PROLOGUE_ROWS = 256
TN_MOD = 3072
PREP_ROWS = 512
GLA_GROUP = 256
GLA_SCAN_UNROLL = 8
GLA_GROUP_UNROLL = 4
GLA_SAFE_TOTAL = 160.0
SWA_QB = 128
SWA_WIN = 3 * SWA_QB
MLA_QB = 256
HEADS_PER_DOT_C = 4
MLA_HEAD_GROUP_P = 16
MLA_HEAD_GROUP_S = 4
VMEM_LIMIT = 60 * 1024 * 1024
NEG_BIG = -1e30


def _cparams(sem):
    return pltpu.CompilerParams(dimension_semantics=sem, vmem_limit_bytes=VMEM_LIMIT)


def _dot(a, b):
    return jnp.dot(a, b, preferred_element_type=F32)


def _dot_nt(a, b):
    return lax.dot_general(a, b, (((1,), (1,)), ((), ())), preferred_element_type=F32)


def _dot_tn(a, b):
    return lax.dot_general(a, b, (((0,), (0,)), ((), ())), preferred_element_type=F32)


def _rms(x, g):
    return x * lax.rsqrt(jnp.mean(x * x, axis=-1, keepdims=True) + EPS) * g


def _silu(x):
    return x / (1.0 + jnp.exp(-x))


def _rope(x, cos, sin, half):
    lane = lax.broadcasted_iota(jnp.int32, x.shape, 1)
    first = (lane % (2 * half)) < half
    partner = jnp.where(first, pltpu.roll(x, LANES - half, 1), pltpu.roll(x, half, 1))
    return x * cos + partner * sin


def _rope_c(x, cos, sin):
    return x * cos + pltpu.roll(x, LANES // 2, 1) * sin


def _rope_split_c(r):
    q = ROPE_C // 4
    first = jnp.concatenate([r[..., 0:q], r[..., 2 * q:3 * q]], axis=-1)
    second = jnp.concatenate([r[..., q:2 * q], r[..., 3 * q:4 * q]], axis=-1)
    return first, second


def _split3(x):
    x1 = x.astype(BF16)
    r1 = x - x1.astype(F32)
    x2 = r1.astype(BF16)
    x3 = (r1 - x2.astype(F32)).astype(BF16)
    return x1, x2, x3


def _dot3(t, parts):
    return _dot(t, parts[2]) + _dot(t, parts[1]) + _dot(t, parts[0])


def _ones_where(cond):
    return jnp.where(cond, 1.0, 0.0).astype(BF16)


def _mod_row(tile, tm):
    n_prompt_tiles = N_PROMPT // tm
    tiles_per_seq = DEC_SEQ // tm
    return jnp.where(tile < n_prompt_tiles, 0, 1 + (tile - n_prompt_tiles) // tiles_per_seq)


def _layer_spec(shape, idx):
    return pl.BlockSpec((None,) + shape, lambda *_: (idx,) + (0,) * len(shape))


def _mod_spec(tm):
    return pl.BlockSpec((None, 6, D_MODEL), lambda i, *_: (_mod_row(i, tm), 0, 0))


def _split_specs(tm, width, col=0):
    n_prompt_tiles = N_PROMPT // tm
    return [pl.BlockSpec((tm, width), lambda i, *_: (jnp.minimum(i, n_prompt_tiles - 1), col)),
            pl.BlockSpec((tm, width), lambda i, *_: (jnp.maximum(i - n_prompt_tiles, 0), col))]


def _split_shapes(width, dtype):
    return [jax.ShapeDtypeStruct((N_PROMPT, width), dtype), jax.ShapeDtypeStruct((N_SAMPLE, width), dtype)]


def _is_prompt_tile(tm):
    return pl.program_id(0) < N_PROMPT // tm


def _load_split(pair, tm, rows=slice(None)):
    return jnp.where(_is_prompt_tile(tm), pair[0][rows, :], pair[1][rows, :])


def _store_split(pair, tm, value, rows=slice(None)):
    is_prompt = _is_prompt_tile(tm)

    @pl.when(is_prompt)
    def _():
        pair[0][rows, :] = value

    @pl.when(jnp.logical_not(is_prompt))
    def _():
        pair[1][rows, :] = value


_C_QA, _C_KA, _C_VA, _C_GA = 0, QA_W, 2 * QA_W, 2 * QA_W + VA_W
_C_QB = _C_GA + VA_W
_C_KB = _C_QB + QB_W
_C_VB = _C_KB + KB_W
_C_LO = _C_VB + KB_W
AB_IN = _C_LO + 2 * GK_RANK


def _prep_even_kernel(wt_ref, o_ref):
    raw_lo = _C_QB
    raw_qb = raw_lo + 2 * GK_RANK
    o_ref[:, 0:_C_QB] = wt_ref[0:raw_lo, :].T.astype(BF16)
    o_ref[:, _C_QB:_C_LO] = wt_ref[raw_qb:AB_IN, :].T.astype(BF16)
    o_ref[:, _C_LO:AB_IN] = wt_ref[raw_lo:raw_lo + LANES, :].T[:, 0:2 * GK_RANK].astype(BF16)


def _prep_even(w_in_t):
    n = w_in_t.shape[0]
    return pl.pallas_call(
        _prep_even_kernel,
        grid=(n, D_MODEL // PREP_ROWS),
        in_specs=[pl.BlockSpec((None, AB_IN, PREP_ROWS), lambda l, r: (l, 0, r))],
        out_specs=pl.BlockSpec((None, PREP_ROWS, AB_IN), lambda l, r: (l, r, 0)),
        out_shape=jax.ShapeDtypeStruct((n, D_MODEL, AB_IN), BF16),
        compiler_params=_cparams(("parallel", "parallel")),
        name="prep_even",
    )(w_in_t)


def _prep_odd_kernel(wdt_ref, wuq_ref, wukv_ref, od_ref, ouq_ref, ouk_ref, ouvt_ref):
    def head_group(nope, rope_first, rope_second):
        rows = nope.shape[0]
        zero_half = jnp.zeros((rows, ROPE_HALF_C), F32)
        return jnp.concatenate(
            [zero_half if rope_first is None else rope_first, nope[:, 0:NOPE_LO_C],
             zero_half if rope_second is None else rope_second, nope[:, NOPE_LO_C:NOPE_C],
             jnp.zeros((rows, HEAD_PAD_C - NOPE_C - ROPE_C), F32)], axis=1).astype(BF16)

    n_ckv = Q_LORA + KV_LORA
    od_ref[:, 0:n_ckv] = wdt_ref[0:n_ckv, :].T.astype(BF16)
    tail = wdt_ref[DOWN_RAW_W - LANES:DOWN_RAW_W, :].T
    od_ref[:, n_ckv:DOWN_W] = head_group(jnp.zeros((D_MODEL, NOPE_C), F32), *_rope_split_c(tail[:, LANES - ROPE_C:]))
    hd_q = NOPE_C + ROPE_C
    for h in range(H_C):
        wq = wuq_ref[:, h * hd_q:(h + 1) * hd_q]
        ouq_ref[:, h * HEAD_PAD_C:(h + 1) * HEAD_PAD_C] = head_group(wq[:, 0:NOPE_C], *_rope_split_c(wq[:, NOPE_C:]))
    wukv = wukv_ref[...]
    for h in range(H_C):
        ouk_ref[:, h * HEAD_PAD_C:(h + 1) * HEAD_PAD_C] = head_group(
            wukv[:, h * HEAD_PAD_C:h * HEAD_PAD_C + NOPE_C], None, None)
    wukv_t = wukv.T
    for h in range(H_C):
        ouvt_ref[h * V_C:(h + 1) * V_C, :] = wukv_t[h * HEAD_PAD_C + NOPE_C:(h + 1) * HEAD_PAD_C, :].astype(BF16)


def _prep_odd(w_down_t, w_uq, w_ukv):
    n = w_down_t.shape[0]
    spec = lambda r, c: pl.BlockSpec((None, r, c), lambda l: (l, 0, 0))
    return pl.pallas_call(
        _prep_odd_kernel,
        grid=(n,),
        in_specs=[spec(DOWN_RAW_W, D_MODEL), spec(Q_LORA, H_C * (NOPE_C + ROPE_C)), spec(KV_LORA, QC_W)],
        out_specs=[spec(D_MODEL, DOWN_W), spec(Q_LORA, QC_W), spec(KV_LORA, QC_W), spec(VC_W, KV_LORA)],
        out_shape=[jax.ShapeDtypeStruct((n, D_MODEL, DOWN_W), BF16), jax.ShapeDtypeStruct((n, Q_LORA, QC_W), BF16),
                   jax.ShapeDtypeStruct((n, KV_LORA, QC_W), BF16), jax.ShapeDtypeStruct((n, VC_W, KV_LORA), BF16)],
        compiler_params=_cparams(("parallel",)),
        name="prep_odd",
    )(w_down_t, w_uq, w_ukv)


def _mod_block(c_ref, w_ref, b_ref):
    s = _silu(c_ref[...])
    return _dot(s.astype(BF16), w_ref[...].astype(BF16)) + b_ref[...]


def _mod_kernel(c_ref, w_ref, b_ref, o_ref):
    o_ref[...] = _mod_block(c_ref, w_ref, b_ref)


def _modulation(cvecs, w_mod, b_mod, layer):
    return pl.pallas_call(
        _mod_kernel,
        grid=(6 * D_MODEL // TN_MOD,),
        in_specs=[
            pl.BlockSpec((N_MOD_ROWS, D_MODEL), lambda j: (0, 0)),
            pl.BlockSpec((None, D_MODEL, TN_MOD), lambda j: (layer, 0, j)),
            pl.BlockSpec((None, 1, TN_MOD), lambda j: (layer, 0, j)),
        ],
        out_specs=pl.BlockSpec((N_MOD_ROWS, TN_MOD), lambda j: (0, j)),
        out_shape=jax.ShapeDtypeStruct((N_MOD_ROWS, 6 * D_MODEL), F32),
        compiler_params=_cparams(("parallel",)),
        name="adaln_mod",
    )(cvecs, w_mod, b_mod)


def _x_specs(n_x, tm):
    if n_x == 1:
        return [pl.BlockSpec((tm, D_MODEL), lambda i, *_: (i, 0))]
    return _split_specs(tm, D_MODEL)


def _load_x(n_x, refs, tm, rows=slice(None)):
    if n_x == 1:
        return refs[0][rows, :]
    return _load_split(refs[:2], tm, rows)


def _in_even_kernel(n_x, *refs):
    (mod_ref, g_ref, w_ref, wgk_ref, bgk_ref, cos_ref, sin_ref,
     qa_ref, ka_ref, va_ref, ga_ref, ld_ref, qb_ref, ldmin_ref, kbt_ref, vbt_ref, kbs_ref, vbs_ref) = refs[n_x:]
    x = _load_x(n_x, refs, TM_IN)
    h = _rms(x, g_ref[0:1, :]) * (1.0 + mod_ref[1:2, :]) + mod_ref[0:1, :]
    hb = h.astype(BF16)
    qa_ref[...] = _dot(hb, w_ref[:, _C_QA:_C_KA]) * (DK_A ** -0.5)
    ka_ref[...] = _dot(hb, w_ref[:, _C_KA:_C_VA])
    va_ref[...] = _dot(hb, w_ref[:, _C_VA:_C_GA]).astype(BF16)
    ga_ref[...] = _dot(hb, w_ref[:, _C_GA:_C_QB])
    cos = cos_ref[...]
    sin = sin_ref[...]
    qb = _dot(hb, w_ref[:, _C_QB:_C_KB])
    for j in range(QB_W // LANES):
        qj = qb[:, j * LANES:(j + 1) * LANES]
        qb_ref[:, j * LANES:(j + 1) * LANES] = (_rope(qj, cos, sin, D_B // 4) * (D_B ** -0.5)).astype(BF16)
    kvb = _dot(hb, w_ref[:, _C_KB:_C_LO])
    kb = _rope(kvb[:, :KB_W], cos, sin, D_B // 4)
    vb = kvb[:, KB_W:]
    lo = _dot(hb, w_ref[:, _C_LO:AB_IN]).astype(BF16)
    z = _dot(lo, wgk_ref[...]) + bgk_ref[...]
    ld = (jnp.minimum(z, 0.0) - jnp.log(1.0 + jnp.exp(-jnp.abs(z)))) * (1.0 / GATE_NORM)
    ld_ref[...] = ld
    ld_min = jnp.min(jnp.min(ld, axis=0, keepdims=True), axis=1, keepdims=True)
    ldmin_ref[...] = jnp.broadcast_to(ld_min, ldmin_ref.shape)

    is_prompt = _is_prompt_tile(TM_IN)

    @pl.when(is_prompt)
    def _():
        for s in range(TM_IN // SEQ):
            kbt_ref[s] = kb[s * SEQ:(s + 1) * SEQ, :].T
            vbt_ref[s] = vb[s * SEQ:(s + 1) * SEQ, :].T

    @pl.when(jnp.logical_not(is_prompt))
    def _():
        kbs_ref[...] = kb
        vbs_ref[...] = vb


def _in_even(xs, layer, mods, g_norm, w_in, w_gk, b_gk, cos, sin):
    tm = TM_IN
    i_ab = layer // 2
    n_prompt_tiles = N_PROMPT // tm
    seq_per_tile = tm // SEQ
    row = lambda i: (i, 0)
    widths = (QA_W, QA_W, VA_W, VA_W, 2 * QA_W, QB_W)
    dtypes = (F32, F32, BF16, F32, F32, BF16)
    kv_specs = 2 * [pl.BlockSpec((seq_per_tile, KB_W, SEQ), lambda i: (jnp.minimum(i, n_prompt_tiles - 1), 0, 0))] + \
        2 * [pl.BlockSpec((tm, KB_W), lambda i: (jnp.maximum(i - n_prompt_tiles, 0), 0))]
    kv_shapes = 2 * [jax.ShapeDtypeStruct((BATCH, KB_W, SEQ), F32)] + 2 * [jax.ShapeDtypeStruct((N_SAMPLE, KB_W), F32)]
    return pl.pallas_call(
        functools.partial(_in_even_kernel, len(xs)),
        grid=(N_TOK // tm,),
        in_specs=_x_specs(len(xs), tm) + [
            _mod_spec(tm),
            _layer_spec((4, D_MODEL), layer),
            _layer_spec((D_MODEL, AB_IN), i_ab),
            _layer_spec((2 * GK_RANK, 2 * QA_W), i_ab),
            _layer_spec((1, 2 * QA_W), i_ab),
            pl.BlockSpec((tm, LANES), row),
            pl.BlockSpec((tm, LANES), row),
        ],
        out_specs=[pl.BlockSpec((tm, w), row) for w in widths] + [
            pl.BlockSpec((None, SUBLANES, LANES), lambda i: (i, 0, 0))] + kv_specs,
        out_shape=[jax.ShapeDtypeStruct((N_TOK, w), d) for w, d in zip(widths, dtypes)] + [
            jax.ShapeDtypeStruct((N_TOK // tm, SUBLANES, LANES), F32)] + kv_shapes,
        compiler_params=_cparams(("arbitrary",)),
        name="in_even",
    )(*xs, mods, g_norm, w_in, w_gk, b_gk, cos, sin)


def _gla_kernel(seq_len, has_s0, exact, *refs):
    qa_ref, ka_ref, va_ref, ga_ref, ld_ref, gg_ref = refs[:6]
    s0_refs = refs[6:8] if has_s0 else None
    o_ref, stf_ref, stb_ref = refs[8:11] if has_s0 else refs[6:9]
    st_ref, qi_ref, ki_ref, qcat_ref, ks_ref, dec_ref, stcat_ref, acc_ref = refs[-8:]
    b_ref = qi_ref if exact else None
    n_groups = seq_len // GLA_GROUP
    cpg = GLA_GROUP // GLA_CHUNK
    n_chunks = seq_len // GLA_CHUNK
    n_pairs = H_A // 2
    pair_k = 2 * DK_A
    pair_v = 2 * DV_A

    def chunk_masks():
        r_i = lax.broadcasted_iota(jnp.int32, (GLA_GROUP, GLA_GROUP), 0)
        c_i = lax.broadcasted_iota(jnp.int32, (GLA_GROUP, GLA_GROUP), 1)
        same = (r_i // GLA_CHUNK) == (c_i // GLA_CHUNK)
        return same & (c_i <= r_i), same & (c_i >= r_i)

    def scale_group(j, carry):
        mask_f, mask_b = chunk_masks()
        t_cum = (_ones_where(mask_f), _ones_where(mask_b))
        s_r = lax.broadcasted_iota(jnp.int32, (SUBLANES, GLA_GROUP), 0)
        s_c = lax.broadcasted_iota(jnp.int32, (SUBLANES, GLA_GROUP), 1)
        t_sel = _ones_where(s_r == s_c // GLA_CHUNK)
        rows = pl.ds(pl.multiple_of(j * GLA_GROUP, GLA_GROUP), GLA_GROUP)
        q = qa_ref[rows, :]
        k = ka_ref[rows, :]
        for d in range(2):
            parts = _split3(ld_ref[rows, d * QA_W:(d + 1) * QA_W])
            b = _dot3(t_cum[d], parts)
            tot8 = _dot3(t_sel, parts)
            dec8 = jnp.exp(tot8)
            tot = jnp.concatenate(
                [jnp.broadcast_to(tot8[cc:cc + 1, :], (GLA_CHUNK, QA_W)) for cc in range(cpg)], axis=0)
            ref = 0.5 * tot
            if exact:
                b_ref[d, rows, :] = b
            else:
                qi_ref[d, rows, :] = (q * jnp.exp(b - ref)).astype(BF16)
                ki_ref[d, rows, :] = (k * jnp.exp(ref - b)).astype(BF16)
            ks_ref[d, rows, :] = (k * jnp.exp(tot - b)).astype(BF16)
            q_inter = (q * jnp.exp(b)).astype(BF16)
            for p in range(n_pairs):
                qcat_ref[rows, p * 2 * pair_k + d * pair_k:p * 2 * pair_k + (d + 1) * pair_k] = (
                    q_inter[:, p * pair_k:(p + 1) * pair_k])
            for cc in range(cpg):
                dec_ref[d, j * cpg + cc] = jnp.broadcast_to(dec8[cc:cc + 1, :], (SUBLANES, QA_W))
        return carry

    group_unroll = 1 if exact else min(n_groups, GLA_GROUP_UNROLL)
    lax.fori_loop(0, n_groups, scale_group, 0, unroll=group_unroll)

    zpad = jnp.zeros((DK_A, DV_A), F32)
    for d in range(2):
        for p in range(n_pairs):
            if has_s0:
                top = jnp.concatenate([s0_refs[d][2 * p], zpad], axis=0).T
                bot = jnp.concatenate([zpad, s0_refs[d][2 * p + 1]], axis=0).T
                st_ref[d, p] = jnp.concatenate([top, bot], axis=0)
            else:
                st_ref[d, p] = jnp.zeros((pair_v, pair_k), F32)

    def scan_chunk(c, carry):
        bd_r = lax.broadcasted_iota(jnp.int32, (pair_v, pair_k), 0)
        bd_c = lax.broadcasted_iota(jnp.int32, (pair_v, pair_k), 1)
        bd_mask = (bd_r // DV_A) == (bd_c // DK_A)
        for d in range(2):
            cd = c if d == 0 else n_chunks - 1 - c
            rows = pl.ds(pl.multiple_of(cd * GLA_CHUNK, GLA_CHUNK), GLA_CHUNK)
            for p in range(n_pairs):
                st = st_ref[d, p]
                stcat_ref[cd, p, :, d * pair_k:(d + 1) * pair_k] = st.astype(BF16)
                u = _dot_tn(va_ref[rows, p * pair_v:(p + 1) * pair_v], ks_ref[d, rows, p * pair_k:(p + 1) * pair_k])
                dec = dec_ref[d, cd][0:1, p * pair_k:(p + 1) * pair_k]
                st_ref[d, p] = st * dec + jnp.where(bd_mask, u, 0.0)
        return carry

    lax.fori_loop(0, n_chunks, scan_chunk, 0, unroll=1 if exact else GLA_SCAN_UNROLL)

    for d, out_ref in ((0, stf_ref), (1, stb_ref)):
        for p in range(n_pairs):
            st = st_ref[d, p]
            out_ref[2 * p] = st[0:DV_A, :].T[0:DK_A, :]
            out_ref[2 * p + 1] = st[DV_A:pair_v, :].T[DK_A:pair_k, :]

    def exact_intra(row0, h):
        p, hh = divmod(h, 2)
        lanes = slice(p * pair_k, (p + 1) * pair_k)
        head_lanes = (lax.broadcasted_iota(jnp.int32, (GLA_CHUNK, pair_k), 1) // DK_A) == hh
        s_idx = lax.broadcasted_iota(jnp.int32, (GLA_CHUNK, GLA_CHUNK), 0)
        t_idx = lax.broadcasted_iota(jnp.int32, (GLA_CHUNK, GLA_CHUNK), 1)
        outs = []
        for cc in range(cpg):
            r0 = row0 + cc * GLA_CHUNK
            crow = pl.ds(pl.multiple_of(r0, GLA_CHUNK), GLA_CHUNK)
            kc = ka_ref[crow, lanes]
            att_t = jnp.zeros((GLA_CHUNK, GLA_CHUNK), F32)
            for d in range(2):
                bc = b_ref[d, crow, lanes]

                def row_step(t, att_t, d=d, bc=bc, kc=kc, r0=r0):
                    grp = pl.ds(pl.multiple_of(r0 + (t // SUBLANES) * SUBLANES, SUBLANES), SUBLANES)
                    pick = lax.broadcasted_iota(jnp.int32, (SUBLANES, pair_k), 0) == t % SUBLANES
                    bt = jnp.sum(jnp.where(pick, b_ref[d, grp, lanes], 0.0), axis=0, keepdims=True)
                    qt = jnp.sum(jnp.where(pick, qa_ref[grp, lanes], 0.0), axis=0, keepdims=True)
                    w = jnp.where(head_lanes, qt * kc * jnp.exp(jnp.minimum(bt - bc, 0.0)), 0.0)
                    col = jnp.sum(w, axis=1, keepdims=True)
                    allowed = (s_idx <= t) if d == 0 else (s_idx >= t)
                    return jnp.where((t_idx == t) & allowed, att_t + col, att_t)

                att_t = lax.fori_loop(0, GLA_CHUNK, row_step, att_t)
            outs.append(_dot_tn(att_t.astype(BF16), va_ref[crow, h * DV_A:(h + 1) * DV_A]))
        return jnp.concatenate(outs, axis=0)

    def out_group(j, carry):
        mask_f, mask_b = chunk_masks()
        lane = lax.broadcasted_iota(jnp.int32, (GLA_GROUP, pair_k), 1)
        rows = pl.ds(pl.multiple_of(j * GLA_GROUP, GLA_GROUP), GLA_GROUP)
        for p in range(n_pairs):
            for cc in range(cpg):
                c = j * cpg + cc
                crow = pl.ds(pl.multiple_of(c * GLA_CHUNK, GLA_CHUNK), GLA_CHUNK)
                acc_ref[crow, p * pair_v:(p + 1) * pair_v] = _dot_nt(
                    qcat_ref[crow, p * 2 * pair_k:(p + 1) * 2 * pair_k], stcat_ref[c, p])
        if exact:
            intras = [exact_intra(j * GLA_GROUP, h) for h in range(H_A)]
        else:
            atts = []
            for h in range(H_A):
                p, hh = divmod(h, 2)
                head_lanes = _ones_where((lane // DK_A) == hh)
                att = None
                for d, mask in ((0, mask_f), (1, mask_b)):
                    qm = qi_ref[d, rows, p * pair_k:(p + 1) * pair_k] * head_lanes
                    a = jnp.where(mask, _dot_nt(qm, ki_ref[d, rows, p * pair_k:(p + 1) * pair_k]), 0.0)
                    att = a if att is None else att + a
                atts.append(att.astype(BF16))
            intras = [_dot(atts[h], va_ref[rows, h * DV_A:(h + 1) * DV_A]) for h in range(H_A)]
        for h in range(H_A):
            o = acc_ref[rows, h * DV_A:(h + 1) * DV_A] + intras[h]
            gate = _silu(ga_ref[rows, h * DV_A:(h + 1) * DV_A])
            o_ref[rows, h * DV_A:(h + 1) * DV_A] = (_rms(o, gg_ref[...]) * gate).astype(BF16)
        return carry

    lax.fori_loop(0, n_groups, out_group, 0, unroll=group_unroll)


def _gla(seq_len, n_seq, row_block0, qa, ka, va, ga, ld, g_gla, i_ab, s0=None, exact=False):
    has_s0 = s0 is not None
    n_chunks = seq_len // GLA_CHUNK
    n_pairs = H_A // 2
    rows = lambda b: (row_block0 + b, 0)
    st_spec = pl.BlockSpec((None, H_A, DK_A, DV_A), lambda b: (b, 0, 0, 0))
    in_specs = [
        pl.BlockSpec((seq_len, QA_W), rows),
        pl.BlockSpec((seq_len, QA_W), rows),
        pl.BlockSpec((seq_len, VA_W), rows),
        pl.BlockSpec((seq_len, VA_W), rows),
        pl.BlockSpec((seq_len, 2 * QA_W), rows),
        _layer_spec((1, DV_A), i_ab),
    ]
    args = [qa, ka, va, ga, ld, g_gla]
    if has_s0:
        s0_spec = pl.BlockSpec((None, None, H_A, DK_A, DV_A), lambda b: (b, i_ab, 0, 0, 0))
        in_specs += [s0_spec, s0_spec]
        args += list(s0)
    st_shape = jax.ShapeDtypeStruct((n_seq, H_A, DK_A, DV_A), F32)
    return pl.pallas_call(
        functools.partial(_gla_kernel, seq_len, has_s0, exact),
        grid=(n_seq,),
        in_specs=in_specs,
        out_specs=[pl.BlockSpec((seq_len, VA_W), lambda b: (b, 0)), st_spec, st_spec],
        out_shape=[jax.ShapeDtypeStruct((n_seq * seq_len, VA_W), BF16), st_shape, st_shape],
        scratch_shapes=[
            pltpu.VMEM((2, n_pairs, 2 * DV_A, 2 * DK_A), F32),
            pltpu.VMEM((2, seq_len, QA_W), F32 if exact else BF16),
            pltpu.VMEM((2, seq_len, QA_W), BF16),
            pltpu.VMEM((seq_len, 2 * QA_W), BF16),
            pltpu.VMEM((2, seq_len, QA_W), BF16),
            pltpu.VMEM((2, n_chunks, SUBLANES, QA_W), F32),
            pltpu.VMEM((n_chunks, n_pairs, 2 * DV_A, 4 * DK_A), BF16),
            pltpu.VMEM((seq_len, VA_W), F32),
        ],
        compiler_params=_cparams(("parallel",)),
        name=("gla_s" if has_s0 else "gla_p") + ("_exact" if exact else ""),
    )(*args)


def _swa_head_softmax(pieces, sink):
    m = sink
    for s, _, _ in pieces:
        m = jnp.maximum(m, jnp.max(s, axis=-1, keepdims=True))
    den = jnp.exp(sink - m)
    acc = None
    for s, v, transposed in pieces:
        e = jnp.exp(s - m)
        den = den + jnp.sum(e, axis=-1, keepdims=True)
        pv = _dot_nt(e.astype(BF16), v) if transposed else _dot(e.astype(BF16), v)
        acc = pv if acc is None else acc + pv
    return acc / den


def _dup_groups(x):
    lane = lax.broadcasted_iota(jnp.int32, x.shape, 1)
    swapped = pltpu.roll(x, D_B, 1)
    low = lane < D_B
    return jnp.where(low, x, swapped).astype(BF16), jnp.where(low, swapped, x).astype(BF16)


def _dup_groups_t(xt):
    g0, g1 = xt[0:D_B, :], xt[D_B:, :]
    return jnp.concatenate([g0, g0], axis=0).astype(BF16), jnp.concatenate([g1, g1], axis=0).astype(BF16)


def _swa_heads(sink_ref, i_ab, q_ref, kv_pieces, o_ref, n_rows):
    lane = lax.broadcasted_iota(jnp.int32, (n_rows, LANES), 1)
    head_pieces = []
    for h in range(H_B):
        j, hh = divmod(h, 2)
        g = h // G_B
        qm = q_ref[:, j * LANES:(j + 1) * LANES] * _ones_where((lane // D_B) == hh)
        pieces = []
        for keys, vals, transposed, mask in kv_pieces:
            s = _dot(qm, keys[g]) if transposed else _dot_nt(qm, keys[g])
            if mask is not None:
                s = jnp.where(mask, s, NEG_BIG)
            pieces.append((s, vals[g], transposed))
        head_pieces.append(pieces)
    outs = [_swa_head_softmax(pieces, sink_ref[i_ab, h]) for h, pieces in enumerate(head_pieces)]
    for j in range(H_B // 2):
        o_ref[:, j * LANES:(j + 1) * LANES] = jnp.where(lane < D_B, outs[2 * j], outs[2 * j + 1]).astype(BF16)


def _swa_prompt_kernel(i_ab, sink_ref, q_ref, kt_ref, vt_ref, o_ref):
    _swa_heads(sink_ref, i_ab, q_ref, [(_dup_groups_t(kt_ref[...]), _dup_groups_t(vt_ref[...]), True, None)],
               o_ref, SEQ)


def _swa_prompt(sink, i_ab, qb, kbt, vbt):
    seq = lambda b: (b, 0)
    seq_t = pl.BlockSpec((None, KB_W, SEQ), lambda b: (b, 0, 0))
    return pl.pallas_call(
        functools.partial(_swa_prompt_kernel, i_ab),
        grid=(BATCH,),
        in_specs=[pl.BlockSpec(memory_space=pltpu.SMEM), pl.BlockSpec((SEQ, QB_W), seq), seq_t, seq_t],
        out_specs=pl.BlockSpec((SEQ, QB_W), seq),
        out_shape=jax.ShapeDtypeStruct((N_PROMPT, QB_W), BF16),
        compiler_params=_cparams(("parallel",)),
        name="swa_p",
    )(sink, qb, kbt, vbt)


def _swa_sample_kernel(i_ab, sink_ref, q_ref, k_ref, v_ref, kct_ref, vct_ref, o_ref):
    n = pl.program_id(1)
    start = pl.multiple_of(jnp.clip((n - 1) * SWA_QB, 0, DEC_SEQ - SWA_WIN), SWA_QB)
    local = (_dup_groups(k_ref[pl.ds(start, SWA_WIN), :]), _dup_groups(v_ref[pl.ds(start, SWA_WIN), :]))
    ctx = (_dup_groups_t(kct_ref[...]), _dup_groups_t(vct_ref[...]))
    qi = n * SWA_QB + lax.broadcasted_iota(jnp.int32, (SWA_QB, SWA_WIN), 0)
    ki = start + lax.broadcasted_iota(jnp.int32, (SWA_QB, SWA_WIN), 1)
    band = jnp.abs(qi - ki) <= WINDOW
    _swa_heads(sink_ref, i_ab, q_ref, [ctx + (True, None), local + (False, band)], o_ref, SWA_QB)


def _swa_sample(sink, i_ab, qb, kb, vb, kc, vc):
    nqb = DEC_SEQ // SWA_QB
    q0 = N_PROMPT // SWA_QB
    return pl.pallas_call(
        functools.partial(_swa_sample_kernel, i_ab),
        grid=(DEC_BATCH, nqb),
        in_specs=[
            pl.BlockSpec(memory_space=pltpu.SMEM),
            pl.BlockSpec((SWA_QB, QB_W), lambda b, n: (q0 + b * nqb + n, 0)),
            pl.BlockSpec((DEC_SEQ, KB_W), lambda b, n: (b, 0)),
            pl.BlockSpec((DEC_SEQ, KB_W), lambda b, n: (b, 0)),
            pl.BlockSpec((None, None, KB_W, PAST_LEN), lambda b, n: (b, i_ab, 0, 0)),
            pl.BlockSpec((None, None, KB_W, PAST_LEN), lambda b, n: (b, i_ab, 0, 0)),
        ],
        out_specs=pl.BlockSpec((SWA_QB, QB_W), lambda b, n: (b * nqb + n, 0)),
        out_shape=jax.ShapeDtypeStruct((N_SAMPLE, QB_W), BF16),
        compiler_params=_cparams(("parallel", "parallel")),
        name="swa_s",
    )(sink, qb, kb, vb, kc, vc)


def _in_odd_kernel(x_ref, mod_ref, g_ref, wd_ref, gq_ref, gkv_ref, wuq_ref, wuk_ref, wuvt_ref, cos_ref, sin_ref,
                   q_ref, k_ref, vt_ref, ckv_ref, krt_ref):
    h = _rms(x_ref[...], g_ref[0:1, :]) * (1.0 + mod_ref[1:2, :]) + mod_ref[0:1, :]
    hb = h.astype(BF16)
    cos = cos_ref[...]
    sin = sin_ref[...]
    c_q = _dot(hb, wd_ref[:, 0:Q_LORA])
    c_kv = _rms(_dot(hb, wd_ref[:, Q_LORA:Q_LORA + KV_LORA]), gkv_ref[...])
    kr = _rope_c(_dot(hb, wd_ref[:, Q_LORA + KV_LORA:DOWN_W]), cos, sin)
    cqb = _rms(c_q, gq_ref[...]).astype(BF16)
    ckvb = c_kv.astype(BF16)
    scale = (NOPE_C + ROPE_C) ** -0.5
    group_w = HEADS_PER_DOT_C * HEAD_PAD_C
    for grp in range(H_C // HEADS_PER_DOT_C):
        gsl = slice(grp * group_w, (grp + 1) * group_w)
        qg = _dot(cqb, wuq_ref[:, gsl])
        kg = _dot(ckvb, wuk_ref[:, gsl])
        for j in range(HEADS_PER_DOT_C):
            sl = slice(j * HEAD_PAD_C, (j + 1) * HEAD_PAD_C)
            osl = slice(grp * group_w + j * HEAD_PAD_C, grp * group_w + (j + 1) * HEAD_PAD_C)
            q_ref[:, osl] = (_rope_c(qg[:, sl], cos, sin) * scale).astype(BF16)
            k_ref[:, osl] = (kg[:, sl] + kr).astype(BF16)
    vt_ref[...] = _dot_nt(wuvt_ref[...], ckvb).astype(BF16)

    @pl.when(_is_prompt_tile(TM_IN))
    def _():
        ckv_ref[...] = c_kv
        q4 = ROPE_C // 4
        half = LANES // 2
        for s in range(TM_IN // SEQ):
            t = kr[s * SEQ:(s + 1) * SEQ, :].T
            krt_ref[s] = jnp.concatenate([t[0:q4], t[half:half + q4], t[q4:2 * q4], t[half + q4:half + 2 * q4]], axis=0)


def _in_odd(x, layer, mods, g_norm, w_down, g_q, g_kv, w_uq, w_uk, w_uvt, cos, sin):
    tm = TM_IN
    i_c = layer // 2
    n_prompt_tiles = N_PROMPT // tm
    row = lambda i: (i, 0)
    prompt_row = lambda i: (jnp.minimum(i, n_prompt_tiles - 1), 0)
    prompt_seq = lambda i: (jnp.minimum(i, n_prompt_tiles - 1), 0, 0)
    return pl.pallas_call(
        _in_odd_kernel,
        grid=(N_TOK // tm,),
        in_specs=[
            pl.BlockSpec((tm, D_MODEL), row),
            _mod_spec(tm),
            _layer_spec((4, D_MODEL), layer),
            _layer_spec((D_MODEL, DOWN_W), i_c),
            _layer_spec((1, Q_LORA), i_c),
            _layer_spec((1, KV_LORA), i_c),
            _layer_spec((Q_LORA, QC_W), i_c),
            _layer_spec((KV_LORA, QC_W), i_c),
            _layer_spec((VC_W, KV_LORA), i_c),
            pl.BlockSpec((tm, LANES), row),
            pl.BlockSpec((tm, LANES), row),
        ],
        out_specs=[pl.BlockSpec((tm, QC_W), row), pl.BlockSpec((tm, QC_W), row),
                   pl.BlockSpec((VC_W, tm), lambda i: (0, i)),
                   pl.BlockSpec((tm, KV_LORA), prompt_row), pl.BlockSpec((tm // SEQ, ROPE_C, SEQ), prompt_seq)],
        out_shape=[jax.ShapeDtypeStruct((N_TOK, QC_W), BF16), jax.ShapeDtypeStruct((N_TOK, QC_W), BF16),
                   jax.ShapeDtypeStruct((VC_W, N_TOK), BF16),
                   jax.ShapeDtypeStruct((N_PROMPT, KV_LORA), F32), jax.ShapeDtypeStruct((BATCH, ROPE_C, SEQ), F32)],
        compiler_params=_cparams(("arbitrary",)),
        name="in_odd",
    )(x, mods, g_norm, w_down, g_q, g_kv, w_uq, w_uk, w_uvt, cos, sin)


def _reduce_rows(x, op, reduce_fn):
    while x.shape[0] % (2 * SUBLANES) == 0:
        half = x.shape[0] // 2
        x = op(x[:half], x[half:])
    return reduce_fn(x, axis=0, keepdims=True)


def _mla_heads(q_ref, kv_pieces, o_ref, ot_ref, group):
    for h0 in range(0, H_C, group):
        heads = range(h0, h0 + group)
        scores = []
        for hd in heads:
            sl = slice(hd * HEAD_PAD_C, (hd + 1) * HEAD_PAD_C)
            scores.append([_dot_nt(k_ref[:, sl], q_ref[:, sl]) for k_ref, _ in kv_pieces])
        exps, dens = [], []
        for per_piece in scores:
            m = None
            for s in per_piece:
                sm = _reduce_rows(s, jnp.maximum, jnp.max)
                m = sm if m is None else jnp.maximum(m, sm)
            es = [jnp.exp(s - m) for s in per_piece]
            den = None
            for e in es:
                part = _reduce_rows(e, jnp.add, jnp.sum)
                den = part if den is None else den + part
            exps.append([e.astype(BF16) for e in es])
            dens.append(den)
        for hd, es, den in zip(heads, exps, dens):
            acc = None
            for e, (_, vt_ref) in zip(es, kv_pieces):
                pv = _dot(vt_ref[hd * V_C:(hd + 1) * V_C, :], e)
                acc = pv if acc is None else acc + pv
            ot_ref[hd * V_C:(hd + 1) * V_C, :] = acc / den
    o_ref[...] = ot_ref[...].T.astype(BF16)


def _mla_prompt_kernel(q_ref, k_ref, vt_ref, o_ref, ot_ref):
    _mla_heads(q_ref, [(k_ref, vt_ref)], o_ref, ot_ref, MLA_HEAD_GROUP_P)


def _mla_prompt(q, k, vt):
    seq = lambda b: (b, 0)
    return pl.pallas_call(
        _mla_prompt_kernel,
        grid=(BATCH,),
        in_specs=[pl.BlockSpec((SEQ, QC_W), seq), pl.BlockSpec((SEQ, QC_W), seq),
                  pl.BlockSpec((VC_W, SEQ), lambda b: (0, b))],
        out_specs=pl.BlockSpec((SEQ, VC_W), seq),
        out_shape=jax.ShapeDtypeStruct((N_PROMPT, VC_W), BF16),
        scratch_shapes=[pltpu.VMEM((VC_W, SEQ), F32)],
        compiler_params=_cparams(("parallel",)),
        name="mla_p",
    )(q, k, vt)


def _mla_sample_kernel(q_ref, k_ref, vt_ref, ckv_ref, krt_ref, wuk_ref, wuvt_ref, o_ref, kc_ref, vct_ref, ot_ref):
    @pl.when(pl.program_id(1) == 0)
    def _():
        cb = ckv_ref[...].astype(BF16)
        krt = krt_ref[...]
        q4 = ROPE_C // 4
        kr = jnp.concatenate(
            [krt[0:q4], krt[2 * q4:3 * q4], jnp.zeros((NOPE_LO_C, PAST_LEN), F32),
             krt[q4:2 * q4], krt[3 * q4:4 * q4], jnp.zeros((LANES // 2 - ROPE_HALF_C, PAST_LEN), F32)], axis=0).T
        group_w = HEADS_PER_DOT_C * HEAD_PAD_C
        for grp in range(H_C // HEADS_PER_DOT_C):
            kg = _dot(cb, wuk_ref[:, grp * group_w:(grp + 1) * group_w])
            for j in range(HEADS_PER_DOT_C):
                osl = slice(grp * group_w + j * HEAD_PAD_C, grp * group_w + (j + 1) * HEAD_PAD_C)
                kc_ref[:, osl] = (kg[:, j * HEAD_PAD_C:(j + 1) * HEAD_PAD_C] + kr).astype(BF16)
        vct_ref[...] = _dot_nt(wuvt_ref[...], cb).astype(BF16)

    _mla_heads(q_ref, [(kc_ref, vct_ref), (k_ref, vt_ref)], o_ref, ot_ref, MLA_HEAD_GROUP_S)


def _mla_sample(q, k, vt, ckv_ctx, kr_ctx, w_uk, w_uvt, i_c):
    nqb = DEC_SEQ // MLA_QB
    q0 = N_PROMPT // MLA_QB
    s0 = N_PROMPT // DEC_SEQ
    return pl.pallas_call(
        _mla_sample_kernel,
        grid=(DEC_BATCH, nqb),
        in_specs=[
            pl.BlockSpec((MLA_QB, QC_W), lambda b, n: (q0 + b * nqb + n, 0)),
            pl.BlockSpec((DEC_SEQ, QC_W), lambda b, n: (s0 + b, 0)),
            pl.BlockSpec((VC_W, DEC_SEQ), lambda b, n: (0, s0 + b)),
            pl.BlockSpec((None, None, PAST_LEN, KV_LORA), lambda b, n: (b, i_c, 0, 0)),
            pl.BlockSpec((None, None, ROPE_C, PAST_LEN), lambda b, n: (b, i_c, 0, 0)),
            _layer_spec((KV_LORA, QC_W), i_c),
            _layer_spec((VC_W, KV_LORA), i_c),
        ],
        out_specs=pl.BlockSpec((MLA_QB, VC_W), lambda b, n: (b * nqb + n, 0)),
        out_shape=jax.ShapeDtypeStruct((N_SAMPLE, VC_W), BF16),
        scratch_shapes=[pltpu.VMEM((PAST_LEN, QC_W), BF16), pltpu.VMEM((VC_W, PAST_LEN), BF16),
                        pltpu.VMEM((VC_W, MLA_QB), F32)],
        compiler_params=_cparams(("parallel", "arbitrary")),
        name="mla_s",
    )(q, k, vt, ckv_ctx, kr_ctx, w_uk, w_uvt)


def _out_kernel(n_x, split_out, *refs):
    it = iter(refs[n_x:])
    mod_ref, g_ref = next(it), next(it)
    a_refs = (next(it), next(it))
    b_refs = (next(it), next(it))
    wo_ref, w1_ref, w2_ref = next(it), next(it), next(it)
    if split_out:
        next_mod = None
        out_refs = (next(it), next(it))
        h2_ref, acc_ref, x1_ref = next(it), next(it), next(it)
        y_ref = acc_ref
    else:
        next_mod = (next(it), next(it), next(it))
        x1_ref = y_ref = next(it)
        modn_ref = next(it)
        h2_ref, acc_ref = next(it), next(it)

    def emit_next_mod():
        if next_mod is not None:
            modn_ref[...] = _mod_block(*next_mod)

    kk = pl.program_id(1)
    half = wo_ref.shape[0] // 2

    row_chunks = [slice(r * OUT_PROLOGUE_ROWS, (r + 1) * OUT_PROLOGUE_ROWS) for r in range(TM_OUT // OUT_PROLOGUE_ROWS)]

    @pl.when(kk == 0)
    def _():
        wo_a = wo_ref[0:half, :].astype(BF16)
        wo_b = wo_ref[half:, :].astype(BF16)
        gate_g1 = mod_ref[2:3, :] * g_ref[1:2, :]
        scale_g2 = g_ref[2:3, :] * (1.0 + mod_ref[4:5, :])
        for rows in row_chunks:
            a = _load_split(a_refs, TM_OUT, rows)
            b = _load_split(b_refs, TM_OUT, rows)
            mix = _dot(a, wo_a) + _dot(b, wo_b)
            x1 = _load_x(n_x, refs, TM_OUT, rows) + _rms(mix, gate_g1)
            x1_ref[rows, :] = x1
            h2_ref[rows, :] = (_rms(x1, scale_g2) + mod_ref[3:4, :]).astype(BF16)
        acc_ref[...] = jnp.zeros(acc_ref.shape, F32)

    def ffn(rows, w1, w2):
        hid = jnp.maximum(_dot(h2_ref[rows, :], w1), 0.0)
        return _dot((hid * hid).astype(BF16), w2)

    is_last = kk == pl.num_programs(1) - 1

    @pl.when(jnp.logical_not(is_last))
    def _():
        acc_ref[...] += ffn(slice(None), w1_ref[...].astype(BF16), w2_ref[...].astype(BF16))
        emit_next_mod()

    @pl.when(is_last)
    def _():
        w1 = w1_ref[...].astype(BF16)
        w2 = w2_ref[...].astype(BF16)
        gate_g3 = mod_ref[5:6, :] * g_ref[3:4, :]
        for rows in row_chunks:
            y_ref[rows, :] = x1_ref[rows, :] + _rms(acc_ref[rows, :] + ffn(rows, w1, w2), gate_g3)
        emit_next_mod()

    if split_out:
        @pl.when(is_last)
        def _():
            _store_split(out_refs, TM_OUT, acc_ref[...])


def _out_layer(xs, layer, mods, g_norm, mix_a, mix_b, b_col, w_o, i_o, w_ff1, w_ff2, next_mod_args):
    tm = TM_OUT
    half = D_MODEL // 2
    n_k = D_FF // TK_FF
    split_out = next_mod_args is None
    if split_out:
        out_specs, out_shape = _split_specs(tm, D_MODEL), _split_shapes(D_MODEL, F32)
        next_specs, next_args = [], []
    else:
        mod_cols = 6 * D_MODEL // (N_TOK // tm * n_k)
        step = lambda i, k: i * n_k + k
        out_specs = [pl.BlockSpec((tm, D_MODEL), lambda i, k: (i, 0)),
                     pl.BlockSpec((N_MOD_ROWS, mod_cols), lambda i, k: (0, step(i, k)))]
        out_shape = [jax.ShapeDtypeStruct((N_TOK, D_MODEL), F32),
                     jax.ShapeDtypeStruct((N_MOD_ROWS, 6 * D_MODEL), F32)]
        next_specs = [pl.BlockSpec((N_MOD_ROWS, D_MODEL), lambda i, k: (0, 0)),
                      pl.BlockSpec((None, D_MODEL, mod_cols), lambda i, k: (layer + 1, 0, step(i, k))),
                      pl.BlockSpec((None, 1, mod_cols), lambda i, k: (layer + 1, 0, step(i, k)))]
        next_args = list(next_mod_args)
    return pl.pallas_call(
        functools.partial(_out_kernel, len(xs), split_out),
        grid=(N_TOK // tm, n_k),
        in_specs=_x_specs(len(xs), tm) + [
            _mod_spec(tm),
            _layer_spec((4, D_MODEL), layer),
        ] + _split_specs(tm, half, 0) + _split_specs(tm, half, b_col) + [
            _layer_spec((D_MODEL, D_MODEL), i_o),
            pl.BlockSpec((None, D_MODEL, TK_FF), lambda i, k: (layer, 0, k)),
            pl.BlockSpec((None, TK_FF, D_MODEL), lambda i, k: (layer, k, 0)),
        ] + next_specs,
        out_specs=out_specs,
        out_shape=out_shape,
        scratch_shapes=[pltpu.VMEM((tm, D_MODEL), BF16), pltpu.VMEM((tm, D_MODEL), F32)] + (
            [pltpu.VMEM((tm, D_MODEL), F32)] if split_out else []),
        compiler_params=_cparams(("arbitrary", "arbitrary")),
        name="out_mlp",
    )(*xs, mods, g_norm, *mix_a, *mix_b, w_o, w_ff1, w_ff2, *next_args)


def _rope_angles(head_dim):
    nf = head_dim // 4
    n_rows = DEC_SEQ // GRID_W
    rows = jnp.repeat(jnp.arange(n_rows, dtype=F32), GRID_W)
    cols = jnp.tile(jnp.arange(GRID_W, dtype=F32), n_rows)
    inv = ROPE_BASE ** (-jnp.arange(nf, dtype=F32) / nf)
    return jnp.stack([rows[:, None] * inv, cols[:, None] * inv], axis=1)


def _token_tables(cos_g, sin_g):
    cos_t = jnp.concatenate([jnp.ones((N_PROMPT, LANES), F32), jnp.tile(cos_g, (DEC_BATCH, 1))], axis=0)
    sin_t = jnp.concatenate([jnp.zeros((N_PROMPT, LANES), F32), jnp.tile(sin_g, (DEC_BATCH, 1))], axis=0)
    return cos_t, sin_t


def _rope_tables_b():
    nf = D_B // 4
    ang = _rope_angles(D_B)
    cos = jnp.broadcast_to(jnp.cos(ang)[:, :, None, :], (DEC_SEQ, 2, 2, nf)).reshape(DEC_SEQ, D_B)
    sin = jnp.sin(ang)
    sin = jnp.stack([-sin, sin], axis=2).reshape(DEC_SEQ, D_B)
    reps = LANES // D_B
    return _token_tables(jnp.tile(cos, (1, reps)), jnp.tile(sin, (1, reps)))


def _rope_tables_c():
    ang = _rope_angles(ROPE_C).reshape(DEC_SEQ, ROPE_HALF_C)
    cos, sin = jnp.cos(ang), jnp.sin(ang)
    half = LANES // 2
    cos_g = jnp.ones((DEC_SEQ, LANES), F32).at[:, 0:ROPE_HALF_C].set(cos).at[:, half:half + ROPE_HALF_C].set(cos)
    sin_g = jnp.zeros((DEC_SEQ, LANES), F32).at[:, 0:ROPE_HALF_C].set(-sin).at[:, half:half + ROPE_HALF_C].set(sin)
    return _token_tables(cos_g, sin_g)


def kernel(x_prompt, x_sample, state_gla_fwd, state_gla_bwd, cache_swa_k, cache_swa_v, cache_mla_ckv, cache_mla_kr, c, c_ctx, w_mod, b_mod, g_norm, w_ff1, w_ff2, w_in_ab, w_gk_f, b_gk_f, w_gk_b, b_gk_b, g_gla, swa_sink, w_out_ab, w_mla_down, g_mla_q, g_mla_kv, w_mla_uq, w_mla_ukv, w_mla_o):
    xs = (x_prompt.reshape(N_PROMPT, D_MODEL), x_sample.reshape(N_SAMPLE, D_MODEL))
    cvecs = jnp.concatenate([c_ctx[None, :], c, jnp.zeros((N_MOD_ROWS - 1 - DEC_BATCH, D_MODEL), F32)], axis=0)
    b_mod3 = b_mod.reshape(DEPTH, 1, 6 * D_MODEL)
    mods = _modulation(cvecs, w_mod, b_mod3, 0).reshape(N_MOD_ROWS, 6, D_MODEL)

    cos_b, sin_b = _rope_tables_b()
    cos_c, sin_c = _rope_tables_c()
    p_blk = N_PROMPT // DEC_SEQ

    n_ab = w_in_ab.shape[0]
    w_in = _prep_even(jnp.swapaxes(w_in_ab, 1, 2))
    zgk = jnp.zeros((n_ab, GK_RANK, QA_W), F32)
    w_gk = jnp.concatenate([jnp.concatenate([w_gk_f, zgk], axis=2),
                            jnp.concatenate([zgk, w_gk_b], axis=2)], axis=1).astype(BF16)
    b_gk = jnp.concatenate([b_gk_f, b_gk_b], axis=1)[:, None, :]
    gg = g_gla[:, None, :]
    kc = jnp.transpose(cache_swa_k, (0, 1, 3, 4, 2)).reshape(DEC_BATCH, n_ab, KB_W, PAST_LEN)
    vc = jnp.transpose(cache_swa_v, (0, 1, 3, 4, 2)).reshape(DEC_BATCH, n_ab, KB_W, PAST_LEN)
    w_down, w_uq, w_uk, w_uvt = _prep_odd(jnp.swapaxes(w_mla_down, 1, 2), w_mla_uq, w_mla_ukv)
    g_q = g_mla_q[:, None, :]
    g_kv = g_mla_kv[:, None, :]
    kr_ctx = jnp.swapaxes(cache_mla_kr, 2, 3)

    def gla_layer(i_ab, exact):
        def run(qa, ka, va, ga, ld):
            o_p, stf, stb = _gla(SEQ, BATCH, 0, qa, ka, va, ga, ld, gg, i_ab, exact=exact)
            o_s, _, _ = _gla(DEC_SEQ, DEC_BATCH, p_blk, qa, ka, va, ga, ld, gg, i_ab,
                             (state_gla_fwd, state_gla_bwd), exact=exact)
            return o_p, o_s, stf, stb
        return run

    st_f, st_b, sk, sv, ckv_out, ckr_out = [], [], [], [], [], []
    for l in range(DEPTH):
        i = l // 2
        next_mod_args = None if l == DEPTH - 1 else (cvecs, w_mod, b_mod3)
        if l % 2 == 0:
            qa, ka, va, ga, ld, qb, ld_min, kbt, vbt, kb_s, vb_s = _in_even(
                xs, l, mods, g_norm, w_in, w_gk, b_gk, cos_b, sin_b)
            factorisable = jnp.min(ld_min) * GLA_CHUNK >= -GLA_SAFE_TOTAL
            o_gla_p, o_gla_s, stf, stb = lax.cond(factorisable, gla_layer(i, False), gla_layer(i, True),
                                                  qa, ka, va, ga, ld)
            o_swa_p = _swa_prompt(swa_sink, i, qb, kbt, vbt)
            o_swa_s = _swa_sample(swa_sink, i, qb, kb_s, vb_s, kc, vc)
            outs = _out_layer(xs, l, mods, g_norm, (o_gla_p, o_gla_s), (o_swa_p, o_swa_s), 0, w_out_ab, i,
                              w_ff1, w_ff2, next_mod_args)
            st_f.append(stf)
            st_b.append(stb)
            sk.append(kbt)
            sv.append(vbt)
        else:
            q, k, vt, ckv, krt = _in_odd(xs[0], l, mods, g_norm, w_down, g_q, g_kv, w_uq, w_uk, w_uvt, cos_c, sin_c)
            o_mla = (_mla_prompt(q, k, vt), _mla_sample(q, k, vt, cache_mla_ckv, kr_ctx, w_uk, w_uvt, i))
            outs = _out_layer(xs, l, mods, g_norm, o_mla, o_mla, 1, w_mla_o, i, w_ff1, w_ff2, next_mod_args)
            ckv_out.append(ckv.reshape(BATCH, SEQ, KV_LORA))
            ckr_out.append(krt)
        if next_mod_args is None:
            xs = outs
        else:
            xs, mods = (outs[0],), outs[1].reshape(N_MOD_ROWS, 6, D_MODEL)

    y_prompt = xs[0].reshape(BATCH, SEQ, D_MODEL)
    y_sample = xs[1].reshape(DEC_BATCH, DEC_SEQ, D_MODEL)
    swa_cache = lambda parts: jnp.transpose(
        jnp.stack(parts, axis=1).reshape(BATCH, len(parts), KV_B, D_B, SEQ), (0, 1, 4, 2, 3))
    return (y_prompt, y_sample, jnp.stack(st_f, axis=1), jnp.stack(st_b, axis=1), swa_cache(sk), swa_cache(sv),
            jnp.stack(ckv_out, axis=1), jnp.swapaxes(jnp.stack(ckr_out, axis=1), 2, 3))
```

```python
import functools

import jax
import jax.numpy as jnp
from jax import lax
from jax.experimental import pallas as pl
from jax.experimental.pallas import tpu as pltpu

F32 = jnp.float32
BF16 = jnp.bfloat16

D_MODEL = 1024
BATCH = 16
SEQ = 256
DEPTH = 4
DEC_BATCH = 2
DEC_SEQ = 1024
PAST_LEN = 512
GRID_W = 64
D_FF = 4 * D_MODEL
EPS = 1e-6
ROPE_BASE = 10000.0
H_A = 4
DK_A = 64
DV_A = 128
GK_RANK = 16
GATE_NORM = 16.0
GLA_CHUNK = 64
H_B = 8
KV_B = 2
G_B = H_B // KV_B
D_B = 64
WINDOW = 128
H_C = 16
NOPE_C = 64
ROPE_C = 32
V_C = 64
Q_LORA = 384
KV_LORA = 256

N_PROMPT = BATCH * SEQ
N_SAMPLE = DEC_BATCH * DEC_SEQ
N_TOK = N_PROMPT + N_SAMPLE
SUBLANES = 8
N_MOD_ROWS = SUBLANES
QA_W = H_A * DK_A
VA_W = H_A * DV_A
QB_W = H_B * D_B
KB_W = KV_B * D_B
LANES = 128
HEAD_PAD_C = 128
QC_W = H_C * HEAD_PAD_C
VC_W = H_C * V_C
DOWN_RAW_W = Q_LORA + KV_LORA + ROPE_C
DOWN_W = Q_LORA + KV_LORA + LANES
ROPE_HALF_C = ROPE_C // 2
NOPE_LO_C = LANES // 2 - ROPE_HALF_C

TM_IN = 512
TM_OUT = 1024
TK_FF = 512
OUT_PROLOGUE_ROWS = 256
TN_MOD = 3072
PREP_ROWS = 1024
GLA_GROUP = 256
GLA_SCAN_UNROLL = 8
GLA_GROUP_UNROLL = 4
GLA_SAFE_TOTAL = 160.0
SWA_QB = 128
SWA_WIN = 3 * SWA_QB
MLA_QB = 256
HEADS_PER_DOT_C = 4
MLA_HEAD_GROUP_P = 16
MLA_HEAD_GROUP_S = 4
VMEM_LIMIT = 60 * 1024 * 1024
NEG_BIG = -1e30


def _cparams(sem):
    return pltpu.CompilerParams(dimension_semantics=sem, vmem_limit_bytes=VMEM_LIMIT)


def _dot(a, b):
    return jnp.dot(a, b, preferred_element_type=F32)


def _dot_nt(a, b):
    return lax.dot_general(a, b, (((1,), (1,)), ((), ())), preferred_element_type=F32)


def _dot_tn(a, b):
    return lax.dot_general(a, b, (((0,), (0,)), ((), ())), preferred_element_type=F32)


def _rms(x, g):
    return x * lax.rsqrt(jnp.mean(x * x, axis=-1, keepdims=True) + EPS) * g


def _silu(x):
    return x / (1.0 + jnp.exp(-x))


def _rope(x, cos, sin, half):
    lane = lax.broadcasted_iota(jnp.int32, x.shape, 1)
    first = (lane % (2 * half)) < half
    partner = jnp.where(first, pltpu.roll(x, LANES - half, 1), pltpu.roll(x, half, 1))
    return x * cos + partner * sin


def _rope_c(x, cos, sin):
    return x * cos + pltpu.roll(x, LANES // 2, 1) * sin


def _rope_split_c(r):
    q = ROPE_C // 4
    first = jnp.concatenate([r[..., 0:q], r[..., 2 * q:3 * q]], axis=-1)
    second = jnp.concatenate([r[..., q:2 * q], r[..., 3 * q:4 * q]], axis=-1)
    return first, second


def _split3(x):
    x1 = x.astype(BF16)
    r1 = x - x1.astype(F32)
    x2 = r1.astype(BF16)
    x3 = (r1 - x2.astype(F32)).astype(BF16)
    return x1, x2, x3


def _dot3(t, parts):
    return _dot(t, parts[2]) + _dot(t, parts[1]) + _dot(t, parts[0])


def _ones_where(cond):
    return jnp.where(cond, 1.0, 0.0).astype(BF16)


def _mod_row(tile, tm):
    n_prompt_tiles = N_PROMPT // tm
    tiles_per_seq = DEC_SEQ // tm
    return jnp.where(tile < n_prompt_tiles, 0, 1 + (tile - n_prompt_tiles) // tiles_per_seq)


def _layer_spec(shape, idx):
    return pl.BlockSpec((None,) + shape, lambda *_: (idx,) + (0,) * len(shape))


def _mod_spec(tm):
    return pl.BlockSpec((None, 6, D_MODEL), lambda i, *_: (_mod_row(i, tm), 0, 0))


def _split_specs(tm, width, col=0):
    n_prompt_tiles = N_PROMPT // tm
    return [pl.BlockSpec((tm, width), lambda i, *_: (jnp.minimum(i, n_prompt_tiles - 1), col)),
            pl.BlockSpec((tm, width), lambda i, *_: (jnp.maximum(i - n_prompt_tiles, 0), col))]


def _split_shapes(width, dtype):
    return [jax.ShapeDtypeStruct((N_PROMPT, width), dtype), jax.ShapeDtypeStruct((N_SAMPLE, width), dtype)]


def _is_prompt_tile(tm):
    return pl.program_id(0) < N_PROMPT // tm


def _load_split(pair, tm, rows=slice(None)):
    return jnp.where(_is_prompt_tile(tm), pair[0][rows, :], pair[1][rows, :])


def _store_split(pair, tm, value, rows=slice(None)):
    is_prompt = _is_prompt_tile(tm)

    @pl.when(is_prompt)
    def _():
        pair[0][rows, :] = value

    @pl.when(jnp.logical_not(is_prompt))
    def _():
        pair[1][rows, :] = value


_C_QA, _C_KA, _C_VA, _C_GA = 0, QA_W, 2 * QA_W, 2 * QA_W + VA_W
_C_QB = _C_GA + VA_W
_C_KB = _C_QB + QB_W
_C_VB = _C_KB + KB_W
_C_LO = _C_VB + KB_W
AB_IN = _C_LO + 2 * GK_RANK


def _prep_even_kernel(wt_ref, o_ref):
    raw_lo = _C_QB
    raw_qb = raw_lo + 2 * GK_RANK
    o_ref[:, 0:_C_QB] = wt_ref[0:raw_lo, :].T.astype(BF16)
    o_ref[:, _C_QB:_C_LO] = wt_ref[raw_qb:AB_IN, :].T.astype(BF16)
    o_ref[:, _C_LO:AB_IN] = wt_ref[raw_lo:raw_lo + LANES, :].T[:, 0:2 * GK_RANK].astype(BF16)


def _prep_even(w_in_t):
    n = w_in_t.shape[0]
    return pl.pallas_call(
        _prep_even_kernel,
        grid=(n, D_MODEL // PREP_ROWS),
        in_specs=[pl.BlockSpec((None, AB_IN, PREP_ROWS), lambda l, r: (l, 0, r))],
        out_specs=pl.BlockSpec((None, PREP_ROWS, AB_IN), lambda l, r: (l, r, 0)),
        out_shape=jax.ShapeDtypeStruct((n, D_MODEL, AB_IN), BF16),
        compiler_params=_cparams(("parallel", "parallel")),
        name="prep_even",
    )(w_in_t)


def _prep_odd_kernel(wdt_ref, wuq_ref, wukv_ref, od_ref, ouq_ref, ouk_ref, ouvt_ref):
    def head_group(nope, rope_first, rope_second):
        rows = nope.shape[0]
        zero_half = jnp.zeros((rows, ROPE_HALF_C), F32)
        return jnp.concatenate(
            [zero_half if rope_first is None else rope_first, nope[:, 0:NOPE_LO_C],
             zero_half if rope_second is None else rope_second, nope[:, NOPE_LO_C:NOPE_C],
             jnp.zeros((rows, HEAD_PAD_C - NOPE_C - ROPE_C), F32)], axis=1).astype(BF16)

    n_ckv = Q_LORA + KV_LORA
    od_ref[:, 0:n_ckv] = wdt_ref[0:n_ckv, :].T.astype(BF16)
    tail = wdt_ref[DOWN_RAW_W - LANES:DOWN_RAW_W, :].T
    od_ref[:, n_ckv:DOWN_W] = head_group(jnp.zeros((D_MODEL, NOPE_C), F32), *_rope_split_c(tail[:, LANES - ROPE_C:]))
    hd_q = NOPE_C + ROPE_C
    for h in range(H_C):
        wq = wuq_ref[:, h * hd_q:(h + 1) * hd_q]
        ouq_ref[:, h * HEAD_PAD_C:(h + 1) * HEAD_PAD_C] = head_group(wq[:, 0:NOPE_C], *_rope_split_c(wq[:, NOPE_C:]))
    wukv = wukv_ref[...]
    for h in range(H_C):
        ouk_ref[:, h * HEAD_PAD_C:(h + 1) * HEAD_PAD_C] = head_group(
            wukv[:, h * HEAD_PAD_C:h * HEAD_PAD_C + NOPE_C], None, None)
    wukv_t = wukv.T
    for h in range(H_C):
        ouvt_ref[h * V_C:(h + 1) * V_C, :] = wukv_t[h * HEAD_PAD_C + NOPE_C:(h + 1) * HEAD_PAD_C, :].astype(BF16)


def _prep_odd(w_down_t, w_uq, w_ukv):
    n = w_down_t.shape[0]
    spec = lambda r, c: pl.BlockSpec((None, r, c), lambda l: (l, 0, 0))
    return pl.pallas_call(
        _prep_odd_kernel,
        grid=(n,),
        in_specs=[spec(DOWN_RAW_W, D_MODEL), spec(Q_LORA, H_C * (NOPE_C + ROPE_C)), spec(KV_LORA, QC_W)],
        out_specs=[spec(D_MODEL, DOWN_W), spec(Q_LORA, QC_W), spec(KV_LORA, QC_W), spec(VC_W, KV_LORA)],
        out_shape=[jax.ShapeDtypeStruct((n, D_MODEL, DOWN_W), BF16), jax.ShapeDtypeStruct((n, Q_LORA, QC_W), BF16),
                   jax.ShapeDtypeStruct((n, KV_LORA, QC_W), BF16), jax.ShapeDtypeStruct((n, VC_W, KV_LORA), BF16)],
        compiler_params=_cparams(("parallel",)),
        name="prep_odd",
    )(w_down_t, w_uq, w_ukv)


def _mod_block(c_ref, w_ref, b_ref):
    s = _silu(c_ref[...])
    return _dot(s.astype(BF16), w_ref[...].astype(BF16)) + b_ref[...]


def _mod_kernel(c_ref, w_ref, b_ref, o_ref):
    o_ref[...] = _mod_block(c_ref, w_ref, b_ref)


def _modulation(cvecs, w_mod, b_mod, layer):
    return pl.pallas_call(
        _mod_kernel,
        grid=(6 * D_MODEL // TN_MOD,),
        in_specs=[
            pl.BlockSpec((N_MOD_ROWS, D_MODEL), lambda j: (0, 0)),
            pl.BlockSpec((None, D_MODEL, TN_MOD), lambda j: (layer, 0, j)),
            pl.BlockSpec((None, 1, TN_MOD), lambda j: (layer, 0, j)),
        ],
        out_specs=pl.BlockSpec((N_MOD_ROWS, TN_MOD), lambda j: (0, j)),
        out_shape=jax.ShapeDtypeStruct((N_MOD_ROWS, 6 * D_MODEL), F32),
        compiler_params=_cparams(("parallel",)),
        name="adaln_mod",
    )(cvecs, w_mod, b_mod)


def _x_specs(n_x, tm):
    if n_x == 1:
        return [pl.BlockSpec((tm, D_MODEL), lambda i, *_: (i, 0))]
    return _split_specs(tm, D_MODEL)


def _load_x(n_x, refs, tm, rows=slice(None)):
    if n_x == 1:
        return refs[0][rows, :]
    return _load_split(refs[:2], tm, rows)


def _in_even_kernel(n_x, *refs):
    (mod_ref, g_ref, w_ref, wgk_ref, bgk_ref, cos_ref, sin_ref,
     qa_ref, ka_ref, va_ref, ga_ref, ld_ref, qb_ref, ldmin_ref, kbt_ref, vbt_ref, kbs_ref, vbs_ref) = refs[n_x:]
    x = _load_x(n_x, refs, TM_IN)
    h = _rms(x, g_ref[0:1, :]) * (1.0 + mod_ref[1:2, :]) + mod_ref[0:1, :]
    hb = h.astype(BF16)
    qa_ref[...] = _dot(hb, w_ref[:, _C_QA:_C_KA]) * (DK_A ** -0.5)
    ka_ref[...] = _dot(hb, w_ref[:, _C_KA:_C_VA])
    va_ref[...] = _dot(hb, w_ref[:, _C_VA:_C_GA]).astype(BF16)
    ga_ref[...] = _dot(hb, w_ref[:, _C_GA:_C_QB])
    cos = cos_ref[...]
    sin = sin_ref[...]
    qb = _dot(hb, w_ref[:, _C_QB:_C_KB])
    for j in range(QB_W // LANES):
        qj = qb[:, j * LANES:(j + 1) * LANES]
        qb_ref[:, j * LANES:(j + 1) * LANES] = (_rope(qj, cos, sin, D_B // 4) * (D_B ** -0.5)).astype(BF16)
    kvb = _dot(hb, w_ref[:, _C_KB:_C_LO])
    kb = _rope(kvb[:, :KB_W], cos, sin, D_B // 4)
    vb = kvb[:, KB_W:]
    lo = _dot(hb, w_ref[:, _C_LO:AB_IN]).astype(BF16)
    z = _dot(lo, wgk_ref[...]) + bgk_ref[...]
    ld = (jnp.minimum(z, 0.0) - jnp.log(1.0 + jnp.exp(-jnp.abs(z)))) * (1.0 / GATE_NORM)
    ld_ref[...] = ld
    ld_min = jnp.min(jnp.min(ld, axis=0, keepdims=True), axis=1, keepdims=True)
    ldmin_ref[...] = jnp.broadcast_to(ld_min, ldmin_ref.shape)

    is_prompt = _is_prompt_tile(TM_IN)

    @pl.when(is_prompt)
    def _():
        for s in range(TM_IN // SEQ):
            kbt_ref[s] = kb[s * SEQ:(s + 1) * SEQ, :].T
            vbt_ref[s] = vb[s * SEQ:(s + 1) * SEQ, :].T

    @pl.when(jnp.logical_not(is_prompt))
    def _():
        kbs_ref[...] = kb
        vbs_ref[...] = vb


def _in_even(xs, layer, mods, g_norm, w_in, w_gk, b_gk, cos, sin):
    tm = TM_IN
    i_ab = layer // 2
    n_prompt_tiles = N_PROMPT // tm
    seq_per_tile = tm // SEQ
    row = lambda i: (i, 0)
    widths = (QA_W, QA_W, VA_W, VA_W, 2 * QA_W, QB_W)
    dtypes = (F32, F32, BF16, F32, F32, BF16)
    kv_specs = 2 * [pl.BlockSpec((seq_per_tile, KB_W, SEQ), lambda i: (jnp.minimum(i, n_prompt_tiles - 1), 0, 0))] + \
        2 * [pl.BlockSpec((tm, KB_W), lambda i: (jnp.maximum(i - n_prompt_tiles, 0), 0))]
    kv_shapes = 2 * [jax.ShapeDtypeStruct((BATCH, KB_W, SEQ), F32)] + 2 * [jax.ShapeDtypeStruct((N_SAMPLE, KB_W), F32)]
    return pl.pallas_call(
        functools.partial(_in_even_kernel, len(xs)),
        grid=(N_TOK // tm,),
        in_specs=_x_specs(len(xs), tm) + [
            _mod_spec(tm),
            _layer_spec((4, D_MODEL), layer),
            _layer_spec((D_MODEL, AB_IN), i_ab),
            _layer_spec((2 * GK_RANK, 2 * QA_W), i_ab),
            _layer_spec((1, 2 * QA_W), i_ab),
            pl.BlockSpec((tm, LANES), row),
            pl.BlockSpec((tm, LANES), row),
        ],
        out_specs=[pl.BlockSpec((tm, w), row) for w in widths] + [
            pl.BlockSpec((None, SUBLANES, LANES), lambda i: (i, 0, 0))] + kv_specs,
        out_shape=[jax.ShapeDtypeStruct((N_TOK, w), d) for w, d in zip(widths, dtypes)] + [
            jax.ShapeDtypeStruct((N_TOK // tm, SUBLANES, LANES), F32)] + kv_shapes,
        compiler_params=_cparams(("arbitrary",)),
        name="in_even",
    )(*xs, mods, g_norm, w_in, w_gk, b_gk, cos, sin)


def _gla_kernel(seq_len, has_s0, exact, *refs):
    qa_ref, ka_ref, va_ref, ga_ref, ld_ref, gg_ref = refs[:6]
    s0_refs = refs[6:8] if has_s0 else None
    o_ref, stf_ref, stb_ref = refs[8:11] if has_s0 else refs[6:9]
    st_ref, qi_ref, ki_ref, qcat_ref, ks_ref, dec_ref, stcat_ref, acc_ref = refs[-8:]
    b_ref = qi_ref if exact else None
    n_groups = seq_len // GLA_GROUP
    cpg = GLA_GROUP // GLA_CHUNK
    n_chunks = seq_len // GLA_CHUNK
    n_pairs = H_A // 2
    pair_k = 2 * DK_A
    pair_v = 2 * DV_A

    def chunk_masks():
        r_i = lax.broadcasted_iota(jnp.int32, (GLA_GROUP, GLA_GROUP), 0)
        c_i = lax.broadcasted_iota(jnp.int32, (GLA_GROUP, GLA_GROUP), 1)
        same = (r_i // GLA_CHUNK) == (c_i // GLA_CHUNK)
        return same & (c_i <= r_i), same & (c_i >= r_i)

    def scale_group(j, carry):
        mask_f, mask_b = chunk_masks()
        t_cum = (_ones_where(mask_f), _ones_where(mask_b))
        s_r = lax.broadcasted_iota(jnp.int32, (SUBLANES, GLA_GROUP), 0)
        s_c = lax.broadcasted_iota(jnp.int32, (SUBLANES, GLA_GROUP), 1)
        t_sel = _ones_where(s_r == s_c // GLA_CHUNK)
        rows = pl.ds(pl.multiple_of(j * GLA_GROUP, GLA_GROUP), GLA_GROUP)
        q = qa_ref[rows, :]
        k = ka_ref[rows, :]
        for d in range(2):
            parts = _split3(ld_ref[rows, d * QA_W:(d + 1) * QA_W])
            b = _dot3(t_cum[d], parts)
            tot8 = _dot3(t_sel, parts)
            dec8 = jnp.exp(tot8)
            tot = jnp.concatenate(
                [jnp.broadcast_to(tot8[cc:cc + 1, :], (GLA_CHUNK, QA_W)) for cc in range(cpg)], axis=0)
            ref = 0.5 * tot
            if exact:
                b_ref[d, rows, :] = b
            else:
                qi_ref[d, rows, :] = (q * jnp.exp(b - ref)).astype(BF16)
                ki_ref[d, rows, :] = (k * jnp.exp(ref - b)).astype(BF16)
            ks_ref[d, rows, :] = (k * jnp.exp(tot - b)).astype(BF16)
            q_inter = (q * jnp.exp(b)).astype(BF16)
            for p in range(n_pairs):
                qcat_ref[rows, p * 2 * pair_k + d * pair_k:p * 2 * pair_k + (d + 1) * pair_k] = (
                    q_inter[:, p * pair_k:(p + 1) * pair_k])
            for cc in range(cpg):
                dec_ref[d, j * cpg + cc] = jnp.broadcast_to(dec8[cc:cc + 1, :], (SUBLANES, QA_W))
        return carry

    group_unroll = 1 if exact else min(n_groups, GLA_GROUP_UNROLL)
    lax.fori_loop(0, n_groups, scale_group, 0, unroll=group_unroll)

    zpad = jnp.zeros((DK_A, DV_A), F32)
    for d in range(2):
        for p in range(n_pairs):
            if has_s0:
                top = jnp.concatenate([s0_refs[d][2 * p], zpad], axis=0).T
                bot = jnp.concatenate([zpad, s0_refs[d][2 * p + 1]], axis=0).T
                st_ref[d, p] = jnp.concatenate([top, bot], axis=0)
            else:
                st_ref[d, p] = jnp.zeros((pair_v, pair_k), F32)

    def scan_chunk(c, carry):
        bd_r = lax.broadcasted_iota(jnp.int32, (pair_v, pair_k), 0)
        bd_c = lax.broadcasted_iota(jnp.int32, (pair_v, pair_k), 1)
        bd_mask = (bd_r // DV_A) == (bd_c // DK_A)
        for d in range(2):
            cd = c if d == 0 else n_chunks - 1 - c
            rows = pl.ds(pl.multiple_of(cd * GLA_CHUNK, GLA_CHUNK), GLA_CHUNK)
            for p in range(n_pairs):
                st = st_ref[d, p]
                stcat_ref[cd, p, :, d * pair_k:(d + 1) * pair_k] = st.astype(BF16)
                u = _dot_tn(va_ref[rows, p * pair_v:(p + 1) * pair_v], ks_ref[d, rows, p * pair_k:(p + 1) * pair_k])
                dec = dec_ref[d, cd][0:1, p * pair_k:(p + 1) * pair_k]
                st_ref[d, p] = st * dec + jnp.where(bd_mask, u, 0.0)
        return carry

    lax.fori_loop(0, n_chunks, scan_chunk, 0, unroll=1 if exact else GLA_SCAN_UNROLL)

    for d, out_ref in ((0, stf_ref), (1, stb_ref)):
        for p in range(n_pairs):
            st = st_ref[d, p]
            out_ref[2 * p] = st[0:DV_A, :].T[0:DK_A, :]
            out_ref[2 * p + 1] = st[DV_A:pair_v, :].T[DK_A:pair_k, :]

    def exact_intra(row0, h):
        p, hh = divmod(h, 2)
        lanes = slice(p * pair_k, (p + 1) * pair_k)
        head_lanes = (lax.broadcasted_iota(jnp.int32, (GLA_CHUNK, pair_k), 1) // DK_A) == hh
        s_idx = lax.broadcasted_iota(jnp.int32, (GLA_CHUNK, GLA_CHUNK), 0)
        t_idx = lax.broadcasted_iota(jnp.int32, (GLA_CHUNK, GLA_CHUNK), 1)
        outs = []
        for cc in range(cpg):
            r0 = row0 + cc * GLA_CHUNK
            crow = pl.ds(pl.multiple_of(r0, GLA_CHUNK), GLA_CHUNK)
            kc = ka_ref[crow, lanes]
            att_t = jnp.zeros((GLA_CHUNK, GLA_CHUNK), F32)
            for d in range(2):
                bc = b_ref[d, crow, lanes]

                def row_step(t, att_t, d=d, bc=bc, kc=kc, r0=r0):
                    grp = pl.ds(pl.multiple_of(r0 + (t // SUBLANES) * SUBLANES, SUBLANES), SUBLANES)
                    pick = lax.broadcasted_iota(jnp.int32, (SUBLANES, pair_k), 0) == t % SUBLANES
                    bt = jnp.sum(jnp.where(pick, b_ref[d, grp, lanes], 0.0), axis=0, keepdims=True)
                    qt = jnp.sum(jnp.where(pick, qa_ref[grp, lanes], 0.0), axis=0, keepdims=True)
                    w = jnp.where(head_lanes, qt * kc * jnp.exp(jnp.minimum(bt - bc, 0.0)), 0.0)
                    col = jnp.sum(w, axis=1, keepdims=True)
                    allowed = (s_idx <= t) if d == 0 else (s_idx >= t)
                    return jnp.where((t_idx == t) & allowed, att_t + col, att_t)

                att_t = lax.fori_loop(0, GLA_CHUNK, row_step, att_t)
            outs.append(_dot_tn(att_t.astype(BF16), va_ref[crow, h * DV_A:(h + 1) * DV_A]))
        return jnp.concatenate(outs, axis=0)

    def out_group(j, carry):
        mask_f, mask_b = chunk_masks()
        lane = lax.broadcasted_iota(jnp.int32, (GLA_GROUP, pair_k), 1)
        rows = pl.ds(pl.multiple_of(j * GLA_GROUP, GLA_GROUP), GLA_GROUP)
        for p in range(n_pairs):
            for cc in range(cpg):
                c = j * cpg + cc
                crow = pl.ds(pl.multiple_of(c * GLA_CHUNK, GLA_CHUNK), GLA_CHUNK)
                acc_ref[crow, p * pair_v:(p + 1) * pair_v] = _dot_nt(
                    qcat_ref[crow, p * 2 * pair_k:(p + 1) * 2 * pair_k], stcat_ref[c, p])
        if exact:
            intras = [exact_intra(j * GLA_GROUP, h) for h in range(H_A)]
        else:
            atts = []
            for h in range(H_A):
                p, hh = divmod(h, 2)
                head_lanes = _ones_where((lane // DK_A) == hh)
                att = None
                for d, mask in ((0, mask_f), (1, mask_b)):
                    qm = qi_ref[d, rows, p * pair_k:(p + 1) * pair_k] * head_lanes
                    a = jnp.where(mask, _dot_nt(qm, ki_ref[d, rows, p * pair_k:(p + 1) * pair_k]), 0.0)
                    att = a if att is None else att + a
                atts.append(att.astype(BF16))
            intras = [_dot(atts[h], va_ref[rows, h * DV_A:(h + 1) * DV_A]) for h in range(H_A)]
        for h in range(H_A):
            o = acc_ref[rows, h * DV_A:(h + 1) * DV_A] + intras[h]
            gate = _silu(ga_ref[rows, h * DV_A:(h + 1) * DV_A])
            o_ref[rows, h * DV_A:(h + 1) * DV_A] = (_rms(o, gg_ref[...]) * gate).astype(BF16)
        return carry

    lax.fori_loop(0, n_groups, out_group, 0, unroll=group_unroll)


def _gla(seq_len, n_seq, row_block0, qa, ka, va, ga, ld, g_gla, i_ab, s0=None, exact=False):
    has_s0 = s0 is not None
    n_chunks = seq_len // GLA_CHUNK
    n_pairs = H_A // 2
    rows = lambda b: (row_block0 + b, 0)
    st_spec = pl.BlockSpec((None, H_A, DK_A, DV_A), lambda b: (b, 0, 0, 0))
    in_specs = [
        pl.BlockSpec((seq_len, QA_W), rows),
        pl.BlockSpec((seq_len, QA_W), rows),
        pl.BlockSpec((seq_len, VA_W), rows),
        pl.BlockSpec((seq_len, VA_W), rows),
        pl.BlockSpec((seq_len, 2 * QA_W), rows),
        _layer_spec((1, DV_A), i_ab),
    ]
    args = [qa, ka, va, ga, ld, g_gla]
    if has_s0:
        s0_spec = pl.BlockSpec((None, None, H_A, DK_A, DV_A), lambda b: (b, i_ab, 0, 0, 0))
        in_specs += [s0_spec, s0_spec]
        args += list(s0)
    st_shape = jax.ShapeDtypeStruct((n_seq, H_A, DK_A, DV_A), F32)
    return pl.pallas_call(
        functools.partial(_gla_kernel, seq_len, has_s0, exact),
        grid=(n_seq,),
        in_specs=in_specs,
        out_specs=[pl.BlockSpec((seq_len, VA_W), lambda b: (b, 0)), st_spec, st_spec],
        out_shape=[jax.ShapeDtypeStruct((n_seq * seq_len, VA_W), BF16), st_shape, st_shape],
        scratch_shapes=[
            pltpu.VMEM((2, n_pairs, 2 * DV_A, 2 * DK_A), F32),
            pltpu.VMEM((2, seq_len, QA_W), F32 if exact else BF16),
            pltpu.VMEM((2, seq_len, QA_W), BF16),
            pltpu.VMEM((seq_len, 2 * QA_W), BF16),
            pltpu.VMEM((2, seq_len, QA_W), BF16),
            pltpu.VMEM((2, n_chunks, SUBLANES, QA_W), F32),
            pltpu.VMEM((n_chunks, n_pairs, 2 * DV_A, 4 * DK_A), BF16),
            pltpu.VMEM((seq_len, VA_W), F32),
        ],
        compiler_params=_cparams(("parallel",)),
        name=("gla_s" if has_s0 else "gla_p") + ("_exact" if exact else ""),
    )(*args)


def _swa_head_softmax(pieces, sink):
    m = sink
    for s, _, _ in pieces:
        m = jnp.maximum(m, jnp.max(s, axis=-1, keepdims=True))
    den = jnp.exp(sink - m)
    acc = None
    for s, v, transposed in pieces:
        e = jnp.exp(s - m)
        den = den + jnp.sum(e, axis=-1, keepdims=True)
        pv = _dot_nt(e.astype(BF16), v) if transposed else _dot(e.astype(BF16), v)
        acc = pv if acc is None else acc + pv
    return acc / den


def _dup_groups(x):
    lane = lax.broadcasted_iota(jnp.int32, x.shape, 1)
    swapped = pltpu.roll(x, D_B, 1)
    low = lane < D_B
    return jnp.where(low, x, swapped).astype(BF16), jnp.where(low, swapped, x).astype(BF16)


def _dup_groups_t(xt):
    g0, g1 = xt[0:D_B, :], xt[D_B:, :]
    return jnp.concatenate([g0, g0], axis=0).astype(BF16), jnp.concatenate([g1, g1], axis=0).astype(BF16)


def _swa_heads(sink_ref, i_ab, q_ref, kv_pieces, o_ref, n_rows):
    lane = lax.broadcasted_iota(jnp.int32, (n_rows, LANES), 1)
    head_pieces = []
    for h in range(H_B):
        j, hh = divmod(h, 2)
        g = h // G_B
        qm = q_ref[:, j * LANES:(j + 1) * LANES] * _ones_where((lane // D_B) == hh)
        pieces = []
        for keys, vals, transposed, mask in kv_pieces:
            s = _dot(qm, keys[g]) if transposed else _dot_nt(qm, keys[g])
            if mask is not None:
                s = jnp.where(mask, s, NEG_BIG)
            pieces.append((s, vals[g], transposed))
        head_pieces.append(pieces)
    outs = [_swa_head_softmax(pieces, sink_ref[i_ab, h]) for h, pieces in enumerate(head_pieces)]
    for j in range(H_B // 2):
        o_ref[:, j * LANES:(j + 1) * LANES] = jnp.where(lane < D_B, outs[2 * j], outs[2 * j + 1]).astype(BF16)


def _swa_prompt_kernel(i_ab, sink_ref, q_ref, kt_ref, vt_ref, o_ref):
    _swa_heads(sink_ref, i_ab, q_ref, [(_dup_groups_t(kt_ref[...]), _dup_groups_t(vt_ref[...]), True, None)],
               o_ref, SEQ)


def _swa_prompt(sink, i_ab, qb, kbt, vbt):
    seq = lambda b: (b, 0)
    seq_t = pl.BlockSpec((None, KB_W, SEQ), lambda b: (b, 0, 0))
    return pl.pallas_call(
        functools.partial(_swa_prompt_kernel, i_ab),
        grid=(BATCH,),
        in_specs=[pl.BlockSpec(memory_space=pltpu.SMEM), pl.BlockSpec((SEQ, QB_W), seq), seq_t, seq_t],
        out_specs=pl.BlockSpec((SEQ, QB_W), seq),
        out_shape=jax.ShapeDtypeStruct((N_PROMPT, QB_W), BF16),
        compiler_params=_cparams(("parallel",)),
        name="swa_p",
    )(sink, qb, kbt, vbt)


def _swa_sample_kernel(i_ab, sink_ref, q_ref, k_ref, v_ref, kct_ref, vct_ref, o_ref):
    n = pl.program_id(1)
    start = pl.multiple_of(jnp.clip((n - 1) * SWA_QB, 0, DEC_SEQ - SWA_WIN), SWA_QB)
    local = (_dup_groups(k_ref[pl.ds(start, SWA_WIN), :]), _dup_groups(v_ref[pl.ds(start, SWA_WIN), :]))
    ctx = (_dup_groups_t(kct_ref[...]), _dup_groups_t(vct_ref[...]))
    qi = n * SWA_QB + lax.broadcasted_iota(jnp.int32, (SWA_QB, SWA_WIN), 0)
    ki = start + lax.broadcasted_iota(jnp.int32, (SWA_QB, SWA_WIN), 1)
    band = jnp.abs(qi - ki) <= WINDOW
    _swa_heads(sink_ref, i_ab, q_ref, [ctx + (True, None), local + (False, band)], o_ref, SWA_QB)


def _swa_sample(sink, i_ab, qb, kb, vb, kc, vc):
    nqb = DEC_SEQ // SWA_QB
    q0 = N_PROMPT // SWA_QB
    return pl.pallas_call(
        functools.partial(_swa_sample_kernel, i_ab),
        grid=(DEC_BATCH, nqb),
        in_specs=[
            pl.BlockSpec(memory_space=pltpu.SMEM),
            pl.BlockSpec((SWA_QB, QB_W), lambda b, n: (q0 + b * nqb + n, 0)),
            pl.BlockSpec((DEC_SEQ, KB_W), lambda b, n: (b, 0)),
            pl.BlockSpec((DEC_SEQ, KB_W), lambda b, n: (b, 0)),
            pl.BlockSpec((None, None, KB_W, PAST_LEN), lambda b, n: (b, i_ab, 0, 0)),
            pl.BlockSpec((None, None, KB_W, PAST_LEN), lambda b, n: (b, i_ab, 0, 0)),
        ],
        out_specs=pl.BlockSpec((SWA_QB, QB_W), lambda b, n: (b * nqb + n, 0)),
        out_shape=jax.ShapeDtypeStruct((N_SAMPLE, QB_W), BF16),
        compiler_params=_cparams(("parallel", "parallel")),
        name="swa_s",
    )(sink, qb, kb, vb, kc, vc)


def _in_odd_kernel(x_ref, mod_ref, g_ref, wd_ref, gq_ref, gkv_ref, wuq_ref, wuk_ref, wuvt_ref, cos_ref, sin_ref,
                   q_ref, k_ref, vt_ref, ckv_ref, krt_ref):
    h = _rms(x_ref[...], g_ref[0:1, :]) * (1.0 + mod_ref[1:2, :]) + mod_ref[0:1, :]
    hb = h.astype(BF16)
    cos = cos_ref[...]
    sin = sin_ref[...]
    c_q = _dot(hb, wd_ref[:, 0:Q_LORA])
    c_kv = _rms(_dot(hb, wd_ref[:, Q_LORA:Q_LORA + KV_LORA]), gkv_ref[...])
    kr = _rope_c(_dot(hb, wd_ref[:, Q_LORA + KV_LORA:DOWN_W]), cos, sin)
    cqb = _rms(c_q, gq_ref[...]).astype(BF16)
    ckvb = c_kv.astype(BF16)
    scale = (NOPE_C + ROPE_C) ** -0.5
    group_w = HEADS_PER_DOT_C * HEAD_PAD_C
    for grp in range(H_C // HEADS_PER_DOT_C):
        gsl = slice(grp * group_w, (grp + 1) * group_w)
        qg = _dot(cqb, wuq_ref[:, gsl])
        kg = _dot(ckvb, wuk_ref[:, gsl])
        for j in range(HEADS_PER_DOT_C):
            sl = slice(j * HEAD_PAD_C, (j + 1) * HEAD_PAD_C)
            osl = slice(grp * group_w + j * HEAD_PAD_C, grp * group_w + (j + 1) * HEAD_PAD_C)
            q_ref[:, osl] = (_rope_c(qg[:, sl], cos, sin) * scale).astype(BF16)
            k_ref[:, osl] = (kg[:, sl] + kr).astype(BF16)
    vt_ref[...] = _dot_nt(wuvt_ref[...], ckvb).astype(BF16)

    @pl.when(_is_prompt_tile(TM_IN))
    def _():
        ckv_ref[...] = c_kv
        q4 = ROPE_C // 4
        half = LANES // 2
        for s in range(TM_IN // SEQ):
            t = kr[s * SEQ:(s + 1) * SEQ, :].T
            krt_ref[s] = jnp.concatenate([t[0:q4], t[half:half + q4], t[q4:2 * q4], t[half + q4:half + 2 * q4]], axis=0)


def _in_odd(x, layer, mods, g_norm, w_down, g_q, g_kv, w_uq, w_uk, w_uvt, cos, sin):
    tm = TM_IN
    i_c = layer // 2
    n_prompt_tiles = N_PROMPT // tm
    row = lambda i: (i, 0)
    prompt_row = lambda i: (jnp.minimum(i, n_prompt_tiles - 1), 0)
    prompt_seq = lambda i: (jnp.minimum(i, n_prompt_tiles - 1), 0, 0)
    return pl.pallas_call(
        _in_odd_kernel,
        grid=(N_TOK // tm,),
        in_specs=[
            pl.BlockSpec((tm, D_MODEL), row),
            _mod_spec(tm),
            _layer_spec((4, D_MODEL), layer),
            _layer_spec((D_MODEL, DOWN_W), i_c),
            _layer_spec((1, Q_LORA), i_c),
            _layer_spec((1, KV_LORA), i_c),
            _layer_spec((Q_LORA, QC_W), i_c),
            _layer_spec((KV_LORA, QC_W), i_c),
            _layer_spec((VC_W, KV_LORA), i_c),
            pl.BlockSpec((tm, LANES), row),
            pl.BlockSpec((tm, LANES), row),
        ],
        out_specs=[pl.BlockSpec((tm, QC_W), row), pl.BlockSpec((tm, QC_W), row),
                   pl.BlockSpec((VC_W, tm), lambda i: (0, i)),
                   pl.BlockSpec((tm, KV_LORA), prompt_row), pl.BlockSpec((tm // SEQ, ROPE_C, SEQ), prompt_seq)],
        out_shape=[jax.ShapeDtypeStruct((N_TOK, QC_W), BF16), jax.ShapeDtypeStruct((N_TOK, QC_W), BF16),
                   jax.ShapeDtypeStruct((VC_W, N_TOK), BF16),
                   jax.ShapeDtypeStruct((N_PROMPT, KV_LORA), F32), jax.ShapeDtypeStruct((BATCH, ROPE_C, SEQ), F32)],
        compiler_params=_cparams(("arbitrary",)),
        name="in_odd",
    )(x, mods, g_norm, w_down, g_q, g_kv, w_uq, w_uk, w_uvt, cos, sin)


def _reduce_rows(x, op, reduce_fn):
    while x.shape[0] % (2 * SUBLANES) == 0:
        half = x.shape[0] // 2
        x = op(x[:half], x[half:])
    return reduce_fn(x, axis=0, keepdims=True)


def _mla_heads(q_ref, kv_pieces, o_ref, ot_ref, group):
    for h0 in range(0, H_C, group):
        heads = range(h0, h0 + group)
        scores = []
        for hd in heads:
            sl = slice(hd * HEAD_PAD_C, (hd + 1) * HEAD_PAD_C)
            scores.append([_dot_nt(k_ref[:, sl], q_ref[:, sl]) for k_ref, _ in kv_pieces])
        exps, dens = [], []
        for per_piece in scores:
            m = None
            for s in per_piece:
                sm = _reduce_rows(s, jnp.maximum, jnp.max)
                m = sm if m is None else jnp.maximum(m, sm)
            es = [jnp.exp(s - m) for s in per_piece]
            den = None
            for e in es:
                part = _reduce_rows(e, jnp.add, jnp.sum)
                den = part if den is None else den + part
            exps.append([e.astype(BF16) for e in es])
            dens.append(den)
        for hd, es, den in zip(heads, exps, dens):
            acc = None
            for e, (_, vt_ref) in zip(es, kv_pieces):
                pv = _dot(vt_ref[hd * V_C:(hd + 1) * V_C, :], e)
                acc = pv if acc is None else acc + pv
            ot_ref[hd * V_C:(hd + 1) * V_C, :] = acc / den
    o_ref[...] = ot_ref[...].T.astype(BF16)


def _mla_prompt_kernel(q_ref, k_ref, vt_ref, o_ref, ot_ref):
    _mla_heads(q_ref, [(k_ref, vt_ref)], o_ref, ot_ref, MLA_HEAD_GROUP_P)


def _mla_prompt(q, k, vt):
    seq = lambda b: (b, 0)
    return pl.pallas_call(
        _mla_prompt_kernel,
        grid=(BATCH,),
        in_specs=[pl.BlockSpec((SEQ, QC_W), seq), pl.BlockSpec((SEQ, QC_W), seq),
                  pl.BlockSpec((VC_W, SEQ), lambda b: (0, b))],
        out_specs=pl.BlockSpec((SEQ, VC_W), seq),
        out_shape=jax.ShapeDtypeStruct((N_PROMPT, VC_W), BF16),
        scratch_shapes=[pltpu.VMEM((VC_W, SEQ), F32)],
        compiler_params=_cparams(("parallel",)),
        name="mla_p",
    )(q, k, vt)


def _mla_sample_kernel(q_ref, k_ref, vt_ref, ckv_ref, krt_ref, wuk_ref, wuvt_ref, o_ref, kc_ref, vct_ref, ot_ref):
    @pl.when(pl.program_id(1) == 0)
    def _():
        cb = ckv_ref[...].astype(BF16)
        krt = krt_ref[...]
        q4 = ROPE_C // 4
        kr = jnp.concatenate(
            [krt[0:q4], krt[2 * q4:3 * q4], jnp.zeros((NOPE_LO_C, PAST_LEN), F32),
             krt[q4:2 * q4], krt[3 * q4:4 * q4], jnp.zeros((LANES // 2 - ROPE_HALF_C, PAST_LEN), F32)], axis=0).T
        group_w = HEADS_PER_DOT_C * HEAD_PAD_C
        for grp in range(H_C // HEADS_PER_DOT_C):
            kg = _dot(cb, wuk_ref[:, grp * group_w:(grp + 1) * group_w])
            for j in range(HEADS_PER_DOT_C):
                osl = slice(grp * group_w + j * HEAD_PAD_C, grp * group_w + (j + 1) * HEAD_PAD_C)
                kc_ref[:, osl] = (kg[:, j * HEAD_PAD_C:(j + 1) * HEAD_PAD_C] + kr).astype(BF16)
        vct_ref[...] = _dot_nt(wuvt_ref[...], cb).astype(BF16)

    _mla_heads(q_ref, [(kc_ref, vct_ref), (k_ref, vt_ref)], o_ref, ot_ref, MLA_HEAD_GROUP_S)


def _mla_sample(q, k, vt, ckv_ctx, kr_ctx, w_uk, w_uvt, i_c):
    nqb = DEC_SEQ // MLA_QB
    q0 = N_PROMPT // MLA_QB
    s0 = N_PROMPT // DEC_SEQ
    return pl.pallas_call(
        _mla_sample_kernel,
        grid=(DEC_BATCH, nqb),
        in_specs=[
            pl.BlockSpec((MLA_QB, QC_W), lambda b, n: (q0 + b * nqb + n, 0)),
            pl.BlockSpec((DEC_SEQ, QC_W), lambda b, n: (s0 + b, 0)),
            pl.BlockSpec((VC_W, DEC_SEQ), lambda b, n: (0, s0 + b)),
            pl.BlockSpec((None, None, PAST_LEN, KV_LORA), lambda b, n: (b, i_c, 0, 0)),
            pl.BlockSpec((None, None, ROPE_C, PAST_LEN), lambda b, n: (b, i_c, 0, 0)),
            _layer_spec((KV_LORA, QC_W), i_c),
            _layer_spec((VC_W, KV_LORA), i_c),
        ],
        out_specs=pl.BlockSpec((MLA_QB, VC_W), lambda b, n: (b * nqb + n, 0)),
        out_shape=jax.ShapeDtypeStruct((N_SAMPLE, VC_W), BF16),
        scratch_shapes=[pltpu.VMEM((PAST_LEN, QC_W), BF16), pltpu.VMEM((VC_W, PAST_LEN), BF16),
                        pltpu.VMEM((VC_W, MLA_QB), F32)],
        compiler_params=_cparams(("parallel", "arbitrary")),
        name="mla_s",
    )(q, k, vt, ckv_ctx, kr_ctx, w_uk, w_uvt)


def _out_kernel(n_x, split_out, *refs):
    it = iter(refs[n_x:])
    mod_ref, g_ref = next(it), next(it)
    a_refs = (next(it), next(it))
    b_refs = (next(it), next(it))
    wo_ref, w1_ref, w2_ref = next(it), next(it), next(it)
    if split_out:
        next_mod = None
        out_refs = (next(it), next(it))
        h2_ref, acc_ref, x1_ref = next(it), next(it), next(it)
        y_ref = acc_ref
    else:
        next_mod = (next(it), next(it), next(it))
        x1_ref = y_ref = next(it)
        modn_ref = next(it)
        h2_ref, acc_ref = next(it), next(it)

    def emit_next_mod():
        if next_mod is not None:
            modn_ref[...] = _mod_block(*next_mod)

    kk = pl.program_id(1)
    half = wo_ref.shape[0] // 2

    row_chunks = [slice(r * OUT_PROLOGUE_ROWS, (r + 1) * OUT_PROLOGUE_ROWS) for r in range(TM_OUT // OUT_PROLOGUE_ROWS)]

    @pl.when(kk == 0)
    def _():
        wo_a = wo_ref[0:half, :].astype(BF16)
        wo_b = wo_ref[half:, :].astype(BF16)
        gate_g1 = mod_ref[2:3, :] * g_ref[1:2, :]
        scale_g2 = g_ref[2:3, :] * (1.0 + mod_ref[4:5, :])
        for rows in row_chunks:
            a = _load_split(a_refs, TM_OUT, rows)
            b = _load_split(b_refs, TM_OUT, rows)
            mix = _dot(a, wo_a) + _dot(b, wo_b)
            x1 = _load_x(n_x, refs, TM_OUT, rows) + _rms(mix, gate_g1)
            x1_ref[rows, :] = x1
            h2_ref[rows, :] = (_rms(x1, scale_g2) + mod_ref[3:4, :]).astype(BF16)
        acc_ref[...] = jnp.zeros(acc_ref.shape, F32)

    def ffn(rows, w1, w2):
        hid = jnp.maximum(_dot(h2_ref[rows, :], w1), 0.0)
        return _dot((hid * hid).astype(BF16), w2)

    is_last = kk == pl.num_programs(1) - 1

    @pl.when(jnp.logical_not(is_last))
    def _():
        acc_ref[...] += ffn(slice(None), w1_ref[...].astype(BF16), w2_ref[...].astype(BF16))
        emit_next_mod()

    @pl.when(is_last)
    def _():
        w1 = w1_ref[...].astype(BF16)
        w2 = w2_ref[...].astype(BF16)
        gate_g3 = mod_ref[5:6, :] * g_ref[3:4, :]
        for rows in row_chunks:
            y_ref[rows, :] = x1_ref[rows, :] + _rms(acc_ref[rows, :] + ffn(rows, w1, w2), gate_g3)
        emit_next_mod()

    if split_out:
        @pl.when(is_last)
        def _():
            _store_split(out_refs, TM_OUT, acc_ref[...])


def _out_layer(xs, layer, mods, g_norm, mix_a, mix_b, b_col, w_o, i_o, w_ff1, w_ff2, next_mod_args):
    tm = TM_OUT
    half = D_MODEL // 2
    n_k = D_FF // TK_FF
    split_out = next_mod_args is None
    if split_out:
        out_specs, out_shape = _split_specs(tm, D_MODEL), _split_shapes(D_MODEL, F32)
        next_specs, next_args = [], []
    else:
        mod_cols = 6 * D_MODEL // (N_TOK // tm * n_k)
        step = lambda i, k: i * n_k + k
        out_specs = [pl.BlockSpec((tm, D_MODEL), lambda i, k: (i, 0)),
                     pl.BlockSpec((N_MOD_ROWS, mod_cols), lambda i, k: (0, step(i, k)))]
        out_shape = [jax.ShapeDtypeStruct((N_TOK, D_MODEL), F32),
                     jax.ShapeDtypeStruct((N_MOD_ROWS, 6 * D_MODEL), F32)]
        next_specs = [pl.BlockSpec((N_MOD_ROWS, D_MODEL), lambda i, k: (0, 0)),
                      pl.BlockSpec((None, D_MODEL, mod_cols), lambda i, k: (layer + 1, 0, step(i, k))),
                      pl.BlockSpec((None, 1, mod_cols), lambda i, k: (layer + 1, 0, step(i, k)))]
        next_args = list(next_mod_args)
    return pl.pallas_call(
        functools.partial(_out_kernel, len(xs), split_out),
        grid=(N_TOK // tm, n_k),
        in_specs=_x_specs(len(xs), tm) + [
            _mod_spec(tm),
            _layer_spec((4, D_MODEL), layer),
        ] + _split_specs(tm, half, 0) + _split_specs(tm, half, b_col) + [
            _layer_spec((D_MODEL, D_MODEL), i_o),
            pl.BlockSpec((None, D_MODEL, TK_FF), lambda i, k: (layer, 0, k)),
            pl.BlockSpec((None, TK_FF, D_MODEL), lambda i, k: (layer, k, 0)),
        ] + next_specs,
        out_specs=out_specs,
        out_shape=out_shape,
        scratch_shapes=[pltpu.VMEM((tm, D_MODEL), BF16), pltpu.VMEM((tm, D_MODEL), F32)] + (
            [pltpu.VMEM((tm, D_MODEL), F32)] if split_out else []),
        compiler_params=_cparams(("arbitrary", "arbitrary")),
        name="out_mlp",
    )(*xs, mods, g_norm, *mix_a, *mix_b, w_o, w_ff1, w_ff2, *next_args)


def _rope_angles(head_dim):
    nf = head_dim // 4
    n_rows = DEC_SEQ // GRID_W
    rows = jnp.repeat(jnp.arange(n_rows, dtype=F32), GRID_W)
    cols = jnp.tile(jnp.arange(GRID_W, dtype=F32), n_rows)
    inv = ROPE_BASE ** (-jnp.arange(nf, dtype=F32) / nf)
    return jnp.stack([rows[:, None] * inv, cols[:, None] * inv], axis=1)


def _token_tables(cos_g, sin_g):
    cos_t = jnp.concatenate([jnp.ones((N_PROMPT, LANES), F32), jnp.tile(cos_g, (DEC_BATCH, 1))], axis=0)
    sin_t = jnp.concatenate([jnp.zeros((N_PROMPT, LANES), F32), jnp.tile(sin_g, (DEC_BATCH, 1))], axis=0)
    return cos_t, sin_t


def _rope_tables_b():
    nf = D_B // 4
    ang = _rope_angles(D_B)
    cos = jnp.broadcast_to(jnp.cos(ang)[:, :, None, :], (DEC_SEQ, 2, 2, nf)).reshape(DEC_SEQ, D_B)
    sin = jnp.sin(ang)
    sin = jnp.stack([-sin, sin], axis=2).reshape(DEC_SEQ, D_B)
    reps = LANES // D_B
    return _token_tables(jnp.tile(cos, (1, reps)), jnp.tile(sin, (1, reps)))


def _rope_tables_c():
    ang = _rope_angles(ROPE_C).reshape(DEC_SEQ, ROPE_HALF_C)
    cos, sin = jnp.cos(ang), jnp.sin(ang)
    half = LANES // 2
    cos_g = jnp.ones((DEC_SEQ, LANES), F32).at[:, 0:ROPE_HALF_C].set(cos).at[:, half:half + ROPE_HALF_C].set(cos)
    sin_g = jnp.zeros((DEC_SEQ, LANES), F32).at[:, 0:ROPE_HALF_C].set(-sin).at[:, half:half + ROPE_HALF_C].set(sin)
    return _token_tables(cos_g, sin_g)


def kernel(x_prompt, x_sample, state_gla_fwd, state_gla_bwd, cache_swa_k, cache_swa_v, cache_mla_ckv, cache_mla_kr, c, c_ctx, w_mod, b_mod, g_norm, w_ff1, w_ff2, w_in_ab, w_gk_f, b_gk_f, w_gk_b, b_gk_b, g_gla, swa_sink, w_out_ab, w_mla_down, g_mla_q, g_mla_kv, w_mla_uq, w_mla_ukv, w_mla_o):
    xs = (x_prompt.reshape(N_PROMPT, D_MODEL), x_sample.reshape(N_SAMPLE, D_MODEL))
    cvecs = jnp.concatenate([c_ctx[None, :], c, jnp.zeros((N_MOD_ROWS - 1 - DEC_BATCH, D_MODEL), F32)], axis=0)
    b_mod3 = b_mod.reshape(DEPTH, 1, 6 * D_MODEL)
    mods = _modulation(cvecs, w_mod, b_mod3, 0).reshape(N_MOD_ROWS, 6, D_MODEL)

    cos_b, sin_b = _rope_tables_b()
    cos_c, sin_c = _rope_tables_c()
    p_blk = N_PROMPT // DEC_SEQ

    n_ab = w_in_ab.shape[0]
    w_in = _prep_even(jnp.swapaxes(w_in_ab, 1, 2))
    zgk = jnp.zeros((n_ab, GK_RANK, QA_W), F32)
    w_gk = jnp.concatenate([jnp.concatenate([w_gk_f, zgk], axis=2),
                            jnp.concatenate([zgk, w_gk_b], axis=2)], axis=1).astype(BF16)
    b_gk = jnp.concatenate([b_gk_f, b_gk_b], axis=1)[:, None, :]
    gg = g_gla[:, None, :]
    kc = jnp.transpose(cache_swa_k, (0, 1, 3, 4, 2)).reshape(DEC_BATCH, n_ab, KB_W, PAST_LEN)
    vc = jnp.transpose(cache_swa_v, (0, 1, 3, 4, 2)).reshape(DEC_BATCH, n_ab, KB_W, PAST_LEN)
    w_down, w_uq, w_uk, w_uvt = _prep_odd(jnp.swapaxes(w_mla_down, 1, 2), w_mla_uq, w_mla_ukv)
    g_q = g_mla_q[:, None, :]
    g_kv = g_mla_kv[:, None, :]
    kr_ctx = jnp.swapaxes(cache_mla_kr, 2, 3)

    def gla_layer(i_ab, exact):
        def run(qa, ka, va, ga, ld):
            o_p, stf, stb = _gla(SEQ, BATCH, 0, qa, ka, va, ga, ld, gg, i_ab, exact=exact)
            o_s, _, _ = _gla(DEC_SEQ, DEC_BATCH, p_blk, qa, ka, va, ga, ld, gg, i_ab,
                             (state_gla_fwd, state_gla_bwd), exact=exact)
            return o_p, o_s, stf, stb
        return run

    st_f, st_b, sk, sv, ckv_out, ckr_out = [], [], [], [], [], []
    for l in range(DEPTH):
        i = l // 2
        next_mod_args = None if l == DEPTH - 1 else (cvecs, w_mod, b_mod3)
        if l % 2 == 0:
            qa, ka, va, ga, ld, qb, ld_min, kbt, vbt, kb_s, vb_s = _in_even(
                xs, l, mods, g_norm, w_in, w_gk, b_gk, cos_b, sin_b)
            factorisable = jnp.min(ld_min) * GLA_CHUNK >= -GLA_SAFE_TOTAL
            o_gla_p, o_gla_s, stf, stb = lax.cond(factorisable, gla_layer(i, False), gla_layer(i, True),
                                                  qa, ka, va, ga, ld)
            o_swa_p = _swa_prompt(swa_sink, i, qb, kbt, vbt)
            o_swa_s = _swa_sample(swa_sink, i, qb, kb_s, vb_s, kc, vc)
            outs = _out_layer(xs, l, mods, g_norm, (o_gla_p, o_gla_s), (o_swa_p, o_swa_s), 0, w_out_ab, i,
                              w_ff1, w_ff2, next_mod_args)
            st_f.append(stf)
            st_b.append(stb)
            sk.append(kbt)
            sv.append(vbt)
        else:
            q, k, vt, ckv, krt = _in_odd(xs[0], l, mods, g_norm, w_down, g_q, g_kv, w_uq, w_uk, w_uvt, cos_c, sin_c)
            o_mla = (_mla_prompt(q, k, vt), _mla_sample(q, k, vt, cache_mla_ckv, kr_ctx, w_uk, w_uvt, i))
            outs = _out_layer(xs, l, mods, g_norm, o_mla, o_mla, 1, w_mla_o, i, w_ff1, w_ff2, next_mod_args)
            ckv_out.append(ckv.reshape(BATCH, SEQ, KV_LORA))
            ckr_out.append(krt)
        if next_mod_args is None:
            xs = outs
        else:
            xs, mods = (outs[0],), outs[1].reshape(N_MOD_ROWS, 6, D_MODEL)

    y_prompt = xs[0].reshape(BATCH, SEQ, D_MODEL)
    y_sample = xs[1].reshape(DEC_BATCH, DEC_SEQ, D_MODEL)
    swa_cache = lambda parts: jnp.transpose(
        jnp.stack(parts, axis=1).reshape(BATCH, len(parts), KV_B, D_B, SEQ), (0, 1, 4, 2, 3))
    return (y_prompt, y_sample, jnp.stack(st_f, axis=1), jnp.stack(st_b, axis=1), swa_cache(sk), swa_cache(sv),
            jnp.stack(ckv_out, axis=1), jnp.swapaxes(jnp.stack(ckr_out, axis=1), 2, 3))
```
